```python
import math
import jax
import jax.numpy as jnp
from jax import lax
import numpy as np

D_MODEL = 4096
BATCH = 4
SEQ = 4096
DEPTH = 2

MIX_WIDTH = D_MODEL
S5_WIDTH = MIX_WIDTH // 4
SGU_WIDTH = MIX_WIDTH // 4
HG_WIDTH = MIX_WIDTH - S5_WIDTH - SGU_WIDTH
S5_GROUP_CH = 16
S5_GROUPS = S5_WIDTH // S5_GROUP_CH
S5_STATE = 64
SGU_CHUNK = 128
SGU_HEAD_DIM = 128
SGU_HEADS = SGU_WIDTH // SGU_HEAD_DIM
HG_HEAD_DIM = 128
HG_HEADS = HG_WIDTH // HG_HEAD_DIM
HG_CHUNK = 64
D_FF = -(-8 * D_MODEL // (3 * 256)) * 256
PLE_DIM = 256
LN_EPS = 1e-5
RMS_EPS = 1e-6
PROJ_WIDTH = S5_WIDTH + 2 * SGU_WIDTH + 4 * HG_WIDTH
IN_SPLITS = (S5_WIDTH,
             S5_WIDTH + SGU_WIDTH,
             S5_WIDTH + 2 * SGU_WIDTH,
             S5_WIDTH + 2 * SGU_WIDTH + HG_WIDTH,
             S5_WIDTH + 2 * SGU_WIDTH + 2 * HG_WIDTH,
             S5_WIDTH + 2 * SGU_WIDTH + 3 * HG_WIDTH)

kernel_name = "hybrid_s5_sgu_hgrn2_deepnorm"


def layer_norm(x, w, b):
    xf = x.astype(jnp.float32)
    mu = jnp.mean(xf, axis=-1, keepdims=True)
    var = jnp.mean(jnp.square(xf - mu), axis=-1, keepdims=True)
    return ((xf - mu) * lax.rsqrt(var + LN_EPS)).astype(x.dtype) * w + b


def rms_norm(x, w):
    xf = x.astype(jnp.float32)
    y = xf * lax.rsqrt(jnp.mean(jnp.square(xf), axis=-1, keepdims=True) + RMS_EPS)
    return y.astype(x.dtype) * w


def _complex_linear_combine(e1, e2):
    a1r, a1i, b1r, b1i = e1
    a2r, a2i, b2r, b2i = e2
    return (a2r * a1r - a2i * a1i,
            a2r * a1i + a2i * a1r,
            a2r * b1r - a2i * b1i + b2r,
            a2r * b1i + a2i * b1r + b2i)


def s5_mixer(u, lam_re, lam_im, log_step, b_re, b_im, c_re, c_im, d, w_glu):
    bsz, seq, _ = u.shape
    uf = u.astype(jnp.float32).reshape(bsz, seq, S5_GROUPS, S5_GROUP_CH)
    lr = jnp.minimum(lam_re.astype(jnp.float32), -1e-4)
    li = lam_im.astype(jnp.float32)
    dt = jnp.exp(log_step.astype(jnp.float32))[:, None]
    mag = jnp.exp(lr * dt)
    ab_re = mag * jnp.cos(li * dt)
    ab_im = mag * jnp.sin(li * dt)
    den = lr * lr + li * li
    nr = ab_re - 1.0
    g_re = (nr * lr + ab_im * li) / den
    g_im = (ab_im * lr - nr * li) / den
    br = b_re.astype(jnp.float32)
    bi = b_im.astype(jnp.float32)
    bb_re = g_re[..., None] * br - g_im[..., None] * bi
    bb_im = g_re[..., None] * bi + g_im[..., None] * br
    bu_re = jnp.einsum('gnc,blgc->lbgn', bb_re, uf)
    bu_im = jnp.einsum('gnc,blgc->lbgn', bb_im, uf)
    a_re = jnp.broadcast_to(ab_re[None, None], (seq, 1, S5_GROUPS, S5_STATE))
    a_im = jnp.broadcast_to(ab_im[None, None], (seq, 1, S5_GROUPS, S5_STATE))
    _, _, s_re, s_im = lax.associative_scan(_complex_linear_combine,
                                            (a_re, a_im, bu_re, bu_im), axis=0)
    y = (jnp.einsum('gcn,lbgn->blgc', c_re.astype(jnp.float32), s_re)
         - jnp.einsum('gcn,lbgn->blgc', c_im.astype(jnp.float32), s_im)
         + d.astype(jnp.float32).reshape(S5_GROUPS, S5_GROUP_CH) * uf)
    z = jax.nn.gelu(y.reshape(bsz, seq, S5_WIDTH)).astype(u.dtype)
    return z * jax.nn.sigmoid(z @ w_glu)


def sgu_mixer(u, v, ln_w, ln_b, w_s, b_s):
    bsz, seq, _ = u.shape
    u = jax.nn.gelu(u)
    v = layer_norm(jax.nn.gelu(v), ln_w, ln_b)
    vc = v.reshape(bsz, seq // SGU_CHUNK, SGU_CHUNK, SGU_HEADS, SGU_HEAD_DIM)
    causal = jnp.tril(jnp.ones((SGU_CHUNK, SGU_CHUNK), dtype=bool))
    ws = jnp.where(causal[None], w_s, jnp.zeros_like(w_s))
    z = jnp.einsum('hts,bcshd->bcthd', ws, vc) + b_s.T[None, None, :, :, None]
    return u * z.reshape(bsz, seq, SGU_WIDTH)


def gated_recurrence_chunked(q, k, v, log_f):
    bsz, seq, nh, dk = q.shape
    dv = v.shape[-1]
    n_chunks = seq // HG_CHUNK

    def chunks(t):
        return t.reshape(bsz, n_chunks, HG_CHUNK, nh, t.shape[-1]).transpose(1, 0, 3, 2, 4)

    qc, kc, vc = chunks(q), chunks(k), chunks(v)
    bc = jnp.cumsum(chunks(log_f), axis=3)
    causal = jnp.tril(jnp.ones((HG_CHUNK, HG_CHUNK), dtype=bool))[:, :, None]

    def step(state, inp):
        q_, k_, v_, b_ = inp
        o_inter = jnp.einsum('bhtk,bhkv->bhtv', q_ * jnp.exp(b_), state)
        rel = jnp.where(causal, b_[:, :, :, None, :] - b_[:, :, None, :, :], -jnp.inf)
        scores = jnp.einsum('bhtsk,bhsk->bhts', q_[:, :, :, None, :] * jnp.exp(rel), k_)
        o_intra = jnp.einsum('bhts,bhsv->bhtv', scores, v_)
        b_end = b_[:, :, -1:, :]
        state = (jnp.exp(b_end[:, :, 0, :, None]) * state
                 + jnp.einsum('bhsk,bhsv->bhkv', k_ * jnp.exp(b_end - b_), v_))
        return state, o_inter + o_intra

    s0 = jnp.zeros((bsz, nh, dk, dv), q.dtype)
    _, o = lax.scan(step, s0, (qc, kc, vc, bc))
    return o.transpose(1, 0, 3, 2, 4).reshape(bsz, seq, nh, dv)


def hgrn2_mixer(q, f, i, g, lb, norm_w):
    bsz, seq, _ = q.shape

    def heads(t):
        return t.astype(jnp.float32).reshape(bsz, seq, HG_HEADS, HG_HEAD_DIM)

    qh = jax.nn.silu(heads(q))
    fr = heads(f)
    vh = heads(i)
    lbh = lb.astype(jnp.float32).reshape(HG_HEADS, HG_HEAD_DIM)
    log_f = jnp.logaddexp(jnp.log(lbh), jnp.log1p(-lbh) + jax.nn.log_sigmoid(fr))
    kh = (1.0 - lbh) * jax.nn.sigmoid(-fr)
    o = gated_recurrence_chunked(qh, kh, vh, log_f)
    o = rms_norm(o, norm_w.astype(jnp.float32)) * jax.nn.silu(heads(g))
    return o.reshape(bsz, seq, HG_WIDTH).astype(q.dtype)


def setup_inputs(seed: int = 0) -> dict:
    key = jax.random.key(seed)
    ks = iter(jax.random.split(key, 40))
    f32 = jnp.float32

    def nrm(shape, scale=1.0):
        return scale * jax.random.normal(next(ks), shape, f32)

    def gain(shape):
        return 1.0 + 0.01 * jax.random.normal(next(ks), shape, f32)

    beta = (8.0 * DEPTH) ** -0.25
    n_idx = jnp.arange(S5_STATE, dtype=f32)
    return {
        "x": nrm((BATCH, SEQ, D_MODEL)),
        "p": nrm((DEPTH, BATCH, SEQ, PLE_DIM)),
        "w_in": nrm((DEPTH, D_MODEL, PROJ_WIDTH), D_MODEL ** -0.5),
        "s5_lam_re": -0.5 + nrm((DEPTH, S5_GROUPS, S5_STATE), 0.01),
        "s5_lam_im": math.pi * n_idx + nrm((DEPTH, S5_GROUPS, S5_STATE), 0.01),
        "s5_log_step": jax.random.uniform(next(ks), (DEPTH, S5_GROUPS), f32,
                                          math.log(1e-3), math.log(1e-1)),
        "s5_b_re": nrm((DEPTH, S5_GROUPS, S5_STATE, S5_GROUP_CH), (2.0 * S5_GROUP_CH) ** -0.5),
        "s5_b_im": nrm((DEPTH, S5_GROUPS, S5_STATE, S5_GROUP_CH), (2.0 * S5_GROUP_CH) ** -0.5),
        "s5_c_re": nrm((DEPTH, S5_GROUPS, S5_GROUP_CH, S5_STATE), (2.0 * S5_STATE) ** -0.5),
        "s5_c_im": nrm((DEPTH, S5_GROUPS, S5_GROUP_CH, S5_STATE), (2.0 * S5_STATE) ** -0.5),
        "s5_d": nrm((DEPTH, S5_WIDTH)),
        "s5_w_glu": nrm((DEPTH, S5_WIDTH, S5_WIDTH), S5_WIDTH ** -0.5),
        "sgu_ln_w": gain((DEPTH, SGU_WIDTH)),
        "sgu_ln_b": nrm((DEPTH, SGU_WIDTH), 0.01),
        "sgu_w": nrm((DEPTH, SGU_HEADS, SGU_CHUNK, SGU_CHUNK), SGU_CHUNK ** -0.5),
        "sgu_b": gain((DEPTH, SGU_HEADS, SGU_CHUNK)),
        "hg_lb_logits": nrm((DEPTH, HG_WIDTH), 0.1),
        "hg_norm_w": gain((DEPTH, HG_HEAD_DIM)),
        "norm_a_w": gain((DEPTH, S5_WIDTH)),
        "norm_b_w": gain((DEPTH, SGU_WIDTH)),
        "w_out": nrm((DEPTH, MIX_WIDTH, D_MODEL), beta * MIX_WIDTH ** -0.5),
        "ln1_w": gain((DEPTH, D_MODEL)),
        "ln1_b": nrm((DEPTH, D_MODEL), 0.01),
        "w_ffn_in": nrm((DEPTH, D_MODEL, 2 * D_FF), D_MODEL ** -0.5),
        "w_ffn_out": nrm((DEPTH, D_FF, D_MODEL), beta * D_FF ** -0.5),
        "ln2_w": gain((DEPTH, D_MODEL)),
        "ln2_b": nrm((DEPTH, D_MODEL), 0.01),
        "w_ple_in": nrm((DEPTH, PLE_DIM, D_MODEL), beta * PLE_DIM ** -0.5),
        "w_ple_gate": nrm((DEPTH, D_MODEL, D_MODEL), D_MODEL ** -0.5),
        "ln3_w": gain((DEPTH, D_MODEL)),
        "ln3_b": nrm((DEPTH, D_MODEL), 0.01),
    }


def reference(x, p, w_in, s5_lam_re, s5_lam_im, s5_log_step, s5_b_re, s5_b_im, s5_c_re, s5_c_im,
              s5_d, s5_w_glu, sgu_ln_w, sgu_ln_b, sgu_w, sgu_b, hg_lb_logits, hg_norm_w,
              norm_a_w, norm_b_w, w_out, ln1_w, ln1_b, w_ffn_in, w_ffn_out, ln2_w, ln2_b,
              w_ple_in, w_ple_gate, ln3_w, ln3_b):
    alpha = (2.0 * DEPTH) ** 0.25
    lbs = jnp.cumsum(jax.nn.softmax(hg_lb_logits.astype(jnp.float32), axis=0), axis=0)
    lbs = lbs - lbs[0:1]
    h = x
    for l in range(DEPTH):
        proj = h @ w_in[l]
        xa, u, v, q, f, ig, g = jnp.split(proj, IN_SPLITS, axis=-1)
        ya = rms_norm(s5_mixer(xa, s5_lam_re[l], s5_lam_im[l], s5_log_step[l], s5_b_re[l],
                               s5_b_im[l], s5_c_re[l], s5_c_im[l], s5_d[l], s5_w_glu[l]),
                      norm_a_w[l])
        yb = rms_norm(sgu_mixer(u, v, sgu_ln_w[l], sgu_ln_b[l], sgu_w[l], sgu_b[l]), norm_b_w[l])
        yc = hgrn2_mixer(q, f, ig, g, lbs[l], hg_norm_w[l])
        mix = jnp.concatenate([ya, yb, yc], axis=-1) @ w_out[l]
        h = layer_norm(alpha * h + mix, ln1_w[l], ln1_b[l])
        gate, up = jnp.split(h @ w_ffn_in[l], 2, axis=-1)
        h = layer_norm(alpha * h + (jax.nn.silu(gate) * up) @ w_ffn_out[l], ln2_w[l], ln2_b[l])
        ple = (p[l] @ w_ple_in[l]) * jax.nn.sigmoid(h @ w_ple_gate[l])
        h = layer_norm(alpha * h + ple, ln3_w[l], ln3_b[l])
    return h
```

```python
import functools
import math

import jax
import jax.numpy as jnp
from jax import lax
from jax.experimental import pallas as pl
from jax.experimental.pallas import tpu as pltpu

F32 = jnp.float32
BF16 = jnp.bfloat16

V7X_VMEM_LIMIT_BYTES = 56 * 1024 * 1024
LANES = 128

LN_EPS = 1e-5
RMS_EPS = 1e-6

S5_WIDTH = 1024
S5_GROUP_CH = 16
S5_STATE = 64
S5_CHUNK = 16
S5_TILE_GROUPS = LANES // S5_GROUP_CH
S5_TILE_STATE = S5_TILE_GROUPS * S5_STATE

SGU_WIDTH = 1024
SGU_CHUNK = 128
SGU_HEADS = 8

HG_WIDTH = 2048
HG_HEAD_DIM = 128
HG_CHUNK = 128
HG_LEVELS = 7


def _cparams(sem):
    return pltpu.CompilerParams(dimension_semantics=sem,
                                vmem_limit_bytes=V7X_VMEM_LIMIT_BYTES)


def _gelu_tanh(x):
    c = math.sqrt(2.0 / math.pi)
    return x * (0.5 * (1.0 + jnp.tanh(c * (x + 0.044715 * (x * x * x)))))


def _sigmoid(x):
    return 1.0 / (1.0 + jnp.exp(-x))


def _dot(a, b):
    return jnp.dot(a, b, preferred_element_type=F32)


def _mm_body(*refs, n_dot, n_extra, epilogue):
    a_refs = refs[:n_dot]
    w_refs = refs[n_dot:2 * n_dot]
    extra = refs[2 * n_dot:2 * n_dot + n_extra]
    out_refs = refs[2 * n_dot + n_extra:]
    accs = [_dot(a[...], w[...]) for a, w in zip(a_refs, w_refs)]
    outs = epilogue(accs, [e[...] for e in extra])
    for o_ref, o in zip(out_refs, outs):
        o_ref[...] = o.astype(o_ref.dtype)


def _fused_mm(dots, extras, epilogue, out_dtypes, n_cols, tm, tn, name):
    t_rows = dots[0][0].shape[0]
    grid = (t_rows // tm, n_cols // tn)
    in_specs, args = [], []
    for a, _, _, _, _ in dots:
        in_specs.append(pl.BlockSpec((tm, a.shape[1]), lambda i, j: (i, 0)))
        args.append(a)
    for _, w, rb, ri, co in dots:
        in_specs.append(pl.BlockSpec((rb, tn), functools.partial(
            lambda i, j, ri, co: (ri, j + co), ri=ri, co=co)))
        args.append(w)
    for arr, kind in extras:
        if kind == "tile":
            in_specs.append(pl.BlockSpec((tm, tn), lambda i, j: (i, j)))
        else:
            in_specs.append(pl.BlockSpec((1, tn), lambda i, j: (0, j)))
        args.append(arr)
    out_shape = [jax.ShapeDtypeStruct((t_rows, n_cols), dt) for dt in out_dtypes]
    out_specs = [pl.BlockSpec((tm, tn), lambda i, j: (i, j)) for _ in out_dtypes]
    body = functools.partial(_mm_body, n_dot=len(dots), n_extra=len(extras),
                             epilogue=epilogue)
    return pl.pallas_call(
        body, out_shape=out_shape, grid=grid, in_specs=in_specs, out_specs=out_specs,
        compiler_params=_cparams(("parallel", "arbitrary")), name=name)(*args)


def _ln_body(x_ref, w_ref, b_ref, h_ref, hb_ref):
    x = x_ref[...]
    mu = jnp.mean(x, axis=-1, keepdims=True)
    xc = x - mu
    var = jnp.mean(xc * xc, axis=-1, keepdims=True)
    y = xc * lax.rsqrt(var + LN_EPS) * w_ref[...] + b_ref[...]
    h_ref[...] = y
    hb_ref[...] = y.astype(BF16)


def _layer_norm(x, w, b, tm):
    t_rows, d = x.shape
    return pl.pallas_call(
        _ln_body,
        out_shape=[jax.ShapeDtypeStruct((t_rows, d), F32),
                   jax.ShapeDtypeStruct((t_rows, d), BF16)],
        grid=(t_rows // tm,),
        in_specs=[pl.BlockSpec((tm, d), lambda i: (i, 0)),
                  pl.BlockSpec((1, d), lambda i: (0, 0)),
                  pl.BlockSpec((1, d), lambda i: (0, 0))],
        out_specs=[pl.BlockSpec((tm, d), lambda i: (i, 0)),
                   pl.BlockSpec((tm, d), lambda i: (i, 0))],
        compiler_params=_cparams(("parallel",)), name="layer_norm")(
            x, w.reshape(1, d), b.reshape(1, d))


def _s5_prepare(lam_re, lam_im, log_step, b_re, b_im, c_re, c_im, d):
    hp = lax.Precision.HIGHEST
    n_tiles = S5_WIDTH // LANES
    lr = jnp.minimum(lam_re.astype(F32), -1e-4)
    li = lam_im.astype(F32)
    dt = jnp.exp(log_step.astype(F32))[:, None]
    mag = jnp.exp(lr * dt)
    ab_re = mag * jnp.cos(li * dt)
    ab_im = mag * jnp.sin(li * dt)
    den = lr * lr + li * li
    nr = ab_re - 1.0
    g_re = (nr * lr + ab_im * li) / den
    g_im = (ab_im * lr - nr * li) / den
    br = b_re.astype(F32)
    bi = b_im.astype(F32)
    bb_re = g_re[..., None] * br - g_im[..., None] * bi
    bb_im = g_re[..., None] * bi + g_im[..., None] * br
    pr, pi = [jnp.ones_like(ab_re)], [jnp.zeros_like(ab_re)]
    for _ in range(S5_CHUNK):
        pr_new = pr[-1] * ab_re - pi[-1] * ab_im
        pi_new = pr[-1] * ab_im + pi[-1] * ab_re
        pr.append(pr_new)
        pi.append(pi_new)
    p_re = jnp.stack(pr)
    p_im = jnp.stack(pi)
    pb_re = (p_re[:S5_CHUNK, :, :, None] * bb_re[None]
             - p_im[:S5_CHUNK, :, :, None] * bb_im[None])
    pb_im = (p_re[:S5_CHUNK, :, :, None] * bb_im[None]
             + p_im[:S5_CHUNK, :, :, None] * bb_re[None])
    cr = c_re.astype(F32)
    ci = c_im.astype(F32)
    kern = (jnp.einsum("gon,jgni->jgoi", cr, pb_re, precision=hp)
            - jnp.einsum("gon,jgni->jgoi", ci, pb_im, precision=hp))
    eye = jnp.eye(S5_TILE_GROUPS, dtype=F32)
    tg, gc, ns = S5_TILE_GROUPS, S5_GROUP_CH, S5_STATE

    k5 = kern.reshape(S5_CHUNK, n_tiles, tg, gc, gc)
    kcat = jnp.einsum("jtgoi,gh->tjgiho", k5, eye).reshape(
        n_tiles, S5_CHUNK * LANES, LANES)

    def w1_half(pb):
        pb5 = pb[::-1].reshape(S5_CHUNK, n_tiles, tg, ns, gc)
        return jnp.einsum("jtgni,gh->tjgihn", pb5, eye).reshape(
            n_tiles, S5_CHUNK * LANES, S5_TILE_STATE)
    w1 = jnp.concatenate([w1_half(pb_re), w1_half(pb_im)], axis=-1)

    ca_re = cr[None] * p_re[1:, :, None, :] - ci[None] * p_im[1:, :, None, :]
    ca_im = cr[None] * p_im[1:, :, None, :] + ci[None] * p_re[1:, :, None, :]

    def p_half(ca):
        ca5 = ca.reshape(S5_CHUNK, n_tiles, tg, gc, ns)
        return jnp.einsum("jtgon,gh->tgnjho", ca5, eye).reshape(
            n_tiles, S5_TILE_STATE, S5_CHUNK * LANES)
    pcat = jnp.concatenate([p_half(ca_re), -p_half(ca_im)], axis=1)

    a16 = jnp.concatenate([p_re[S5_CHUNK].reshape(n_tiles, 1, S5_TILE_STATE),
                           p_im[S5_CHUNK].reshape(n_tiles, 1, S5_TILE_STATE)], axis=-1)
    dd = d.astype(F32).reshape(n_tiles, 1, LANES)
    return kcat.astype(BF16), w1.astype(BF16), pcat.astype(BF16), a16, dd


def _s5_body(x_ref, kcat_ref, w1_ref, pcat_ref, a16_ref, d_ref, z_ref,
             state_ref, xcat_ref, xr_ref, q_ref, sin_ref, yc_ref, yint_ref, *, nb, tl):
    t = pl.program_id(1)
    nk = tl // S5_CHUNK
    n = nb * tl
    ts = S5_TILE_STATE

    @pl.when(t == 0)
    def _():
        state_ref[...] = jnp.zeros_like(state_ref)

    for b in range(nb):
        for tp in range(S5_CHUNK):
            xr_ref[tp, pl.ds(b, nk, stride=nb), :] = x_ref[b, pl.ds(tp, nk, stride=S5_CHUNK), :]
    xr = jnp.concatenate([xr_ref[tp].astype(BF16) for tp in range(S5_CHUNK)], axis=1)
    q_ref[...] = _dot(xr, w1_ref[...])

    a_re = a16_ref[:, :ts]
    a_im = a16_ref[:, ts:]
    s = state_ref[...]
    for k in range(nk):
        sin_ref[k * nb:(k + 1) * nb, :] = s
        s_re = s[:, :ts]
        s_im = s[:, ts:]
        s = jnp.concatenate([a_re * s_re - a_im * s_im, a_re * s_im + a_im * s_re],
                            axis=1) + q_ref[k * nb:(k + 1) * nb, :]
    state_ref[...] = s

    yc = _dot(sin_ref[...].astype(BF16), pcat_ref[...])
    for tp in range(S5_CHUNK):
        yc_ref[tp] = yc[:, tp * LANES:(tp + 1) * LANES]
    for b in range(nb):
        for tp in range(S5_CHUNK):
            yint_ref[pl.ds(b * tl + tp, nk, stride=S5_CHUNK), :] = (
                yc_ref[tp, pl.ds(b, nk, stride=nb), :])

    x = x_ref[...].reshape(n, LANES)
    pos = lax.broadcasted_iota(jnp.int32, (n, LANES), 0) % S5_CHUNK
    for j in range(S5_CHUNK):
        xs = x if j == 0 else jnp.where(pos >= j, pltpu.roll(x, j, axis=0), 0.0)
        xcat_ref[:, j * LANES:(j + 1) * LANES] = xs.astype(BF16)
    y = _dot(xcat_ref[...], kcat_ref[...]) + yint_ref[...] + d_ref[...] * x
    z_ref[...] = _gelu_tanh(y).reshape(nb, tl, LANES)


def _s5_mixer(proj3, prep, tl):
    kcat, w1, pcat, a16, dd = prep
    nb, seq, _ = proj3.shape
    n_tiles = S5_WIDTH // LANES
    nk = tl // S5_CHUNK
    ts = S5_TILE_STATE
    body = functools.partial(_s5_body, nb=nb, tl=tl)
    wspec = lambda shp: pl.BlockSpec((None,) + shp, lambda j, t: (j, 0, 0))
    return pl.pallas_call(
        body,
        out_shape=jax.ShapeDtypeStruct((nb, seq, S5_WIDTH), F32),
        grid=(n_tiles, seq // tl),
        in_specs=[pl.BlockSpec((nb, tl, LANES), lambda j, t: (0, t, j)),
                  wspec((S5_CHUNK * LANES, LANES)),
                  wspec((S5_CHUNK * LANES, 2 * ts)),
                  wspec((2 * ts, S5_CHUNK * LANES)),
                  wspec((1, 2 * ts)),
                  wspec((1, LANES))],
        out_specs=pl.BlockSpec((nb, tl, LANES), lambda j, t: (0, t, j)),
        scratch_shapes=[pltpu.VMEM((nb, 2 * ts), F32),
                        pltpu.VMEM((nb * tl, S5_CHUNK * LANES), BF16),
                        pltpu.VMEM((S5_CHUNK, nb * nk, LANES), F32),
                        pltpu.VMEM((nb * nk, 2 * ts), F32),
                        pltpu.VMEM((nb * nk, 2 * ts), F32),
                        pltpu.VMEM((S5_CHUNK, nb * nk, LANES), F32),
                        pltpu.VMEM((nb * tl, LANES), F32)],
        compiler_params=_cparams(("parallel", "arbitrary")), name="s5_mixer")(
            proj3, kcat, w1, pcat, a16, dd)


def _glu_body(z_ref, w_ref, nw_ref, o_ref):
    z = z_ref[...]
    y = z * _sigmoid(_dot(z.astype(BF16), w_ref[...]))
    ms = jnp.mean(y * y, axis=-1, keepdims=True)
    o_ref[...] = (y * lax.rsqrt(ms + RMS_EPS) * nw_ref[...]).astype(BF16)


def _s5_glu_norm(z, w_glu, norm_w, tm):
    t_rows, d = z.shape
    return pl.pallas_call(
        _glu_body,
        out_shape=jax.ShapeDtypeStruct((t_rows, d), BF16),
        grid=(t_rows // tm,),
        in_specs=[pl.BlockSpec((tm, d), lambda i: (i, 0)),
                  pl.BlockSpec((d, d), lambda i: (0, 0)),
                  pl.BlockSpec((1, d), lambda i: (0, 0))],
        out_specs=pl.BlockSpec((tm, d), lambda i: (i, 0)),
        compiler_params=_cparams(("parallel",)), name="s5_glu_norm")(
            z, w_glu, norm_w.reshape(1, d))


def _sgu_body(u_ref, v_ref, lnw_ref, lnb_ref, w_ref, bs_ref, nw_ref, o_ref,
              vb_ref, x_ref, *, nc):
    v = _gelu_tanh(v_ref[...])
    mu = jnp.mean(v, axis=-1, keepdims=True)
    vc = v - mu
    var = jnp.mean(vc * vc, axis=-1, keepdims=True)
    vb_ref[...] = (vc * lax.rsqrt(var + LN_EPS) * lnw_ref[...] + lnb_ref[...]).astype(BF16)
    hd = SGU_WIDTH // SGU_HEADS
    for h in range(SGU_HEADS):
        cs = slice(h * hd, (h + 1) * hd)
        rhs = jnp.concatenate(
            [vb_ref[c * SGU_CHUNK:(c + 1) * SGU_CHUNK, cs] for c in range(nc)], axis=1)
        zz = _dot(w_ref[h], rhs)
        for c in range(nc):
            rs = slice(c * SGU_CHUNK, (c + 1) * SGU_CHUNK)
            z = zz[:, c * hd:(c + 1) * hd] + bs_ref[:, cs]
            x_ref[rs, cs] = _gelu_tanh(u_ref[rs, cs]) * z
    x = x_ref[...]
    ms = jnp.mean(x * x, axis=-1, keepdims=True)
    o_ref[...] = (x * lax.rsqrt(ms + RMS_EPS) * nw_ref[...]).astype(BF16)


def _sgu_mixer(proj3, ln_w, ln_b, w_s, b_s, norm_w, tl):
    nb, seq, _ = proj3.shape
    nc = tl // SGU_CHUNK
    hd = SGU_WIDTH // SGU_HEADS
    causal = jnp.tril(jnp.ones((SGU_CHUNK, SGU_CHUNK), dtype=bool))
    w_causal = jnp.where(causal[None], w_s, jnp.zeros_like(w_s)).astype(BF16)
    bias = jnp.repeat(b_s.astype(F32).T, hd, axis=1)
    u_blk = S5_WIDTH // SGU_WIDTH
    row = lambda a: a.astype(F32).reshape(1, SGU_WIDTH)
    const = lambda shp: pl.BlockSpec(shp, lambda b, t: (0,) * len(shp))
    body = functools.partial(_sgu_body, nc=nc)
    return pl.pallas_call(
        body,
        out_shape=jax.ShapeDtypeStruct((nb, seq, SGU_WIDTH), BF16),
        grid=(nb, seq // tl),
        in_specs=[pl.BlockSpec((None, tl, SGU_WIDTH), lambda b, t: (b, t, u_blk)),
                  pl.BlockSpec((None, tl, SGU_WIDTH), lambda b, t: (b, t, u_blk + 1)),
                  const((1, SGU_WIDTH)), const((1, SGU_WIDTH)),
                  const((SGU_HEADS, SGU_CHUNK, SGU_CHUNK)),
                  const((SGU_CHUNK, SGU_WIDTH)), const((1, SGU_WIDTH))],
        out_specs=pl.BlockSpec((None, tl, SGU_WIDTH), lambda b, t: (b, t, 0)),
        scratch_shapes=[pltpu.VMEM((tl, SGU_WIDTH), BF16),
                        pltpu.VMEM((tl, SGU_WIDTH), F32)],
        compiler_params=_cparams(("parallel", "parallel")), name="sgu_mixer")(
            proj3, proj3, row(ln_w), row(ln_b), w_causal, bias, row(norm_w))


def _hgrn_tables():
    c = HG_CHUNK
    t = jnp.arange(c)[:, None]
    r = jnp.arange(c)[None, :]
    mats, masks = [], []
    for lev in range(HG_LEVELS):
        m = 1 << lev
        mid = (t // (2 * m)) * (2 * m) + m
        later = t >= mid
        mats.append(jnp.where(later, (r >= mid) & (r <= t), (r > t) & (r < mid)))
        same = (t // (2 * m)) == (r // (2 * m))
        masks.append(same & later & (r < mid))
    mats.append(r <= t)
    masks.append(t == r)
    return (jnp.concatenate(mats, axis=0).astype(BF16),
            jnp.stack(masks).astype(F32))


def _hgrn_body(q_ref, f_ref, i_ref, g_ref, par_ref, nw_ref, mst_ref, mask_ref, o_ref,
               s_ref, *, nc, hb):
    c = HG_CHUNK
    hd = HG_HEAD_DIM
    t = pl.program_id(2)

    @pl.when(t == 0)
    def _():
        s_ref[...] = jnp.zeros_like(s_ref)

    log_lb = par_ref[0:1, :]
    log_1m_lb = par_ref[1:2, :]
    one_m_lb = par_ref[2:3, :]
    nw = nw_ref[...]
    mst = mst_ref[...]
    nt_dims = (((1,), (1,)), ((), ()))
    tn_dims = (((0,), (0,)), ((), ()))
    for ch in range(nc):
        rs = slice(ch * c, (ch + 1) * c)
        fr = f_ref[rs, :]
        e = jnp.exp(-jnp.abs(fr))
        r = 1.0 / (1.0 + e)
        sig_neg = jnp.where(fr >= 0, e * r, r)
        log_sig = jnp.minimum(fr, 0.0) - jnp.log(1.0 + e)
        y = log_1m_lb + log_sig
        log_f = jnp.maximum(log_lb, y) + jnp.log(1.0 + jnp.exp(-jnp.abs(log_lb - y)))
        kk = one_m_lb * sig_neg
        hi = log_f.astype(BF16)
        lo = (log_f - hi.astype(F32)).astype(BF16)
        ee = _dot(mst, hi) + _dot(mst, lo)
        qv = q_ref[rs, :]
        qs = qv * _sigmoid(qv)
        vv = i_ref[rs, :]
        gv = g_ref[rs, :]
        bcum = ee[HG_LEVELS * c:(HG_LEVELS + 1) * c, :]
        for h in range(hb):
            cs = slice(h * hd, (h + 1) * hd)
            qh = qs[:, cs]
            kh = kk[:, cs]
            vh = vv[:, cs].astype(BF16)
            diag = jnp.sum(qh * kh, axis=-1, keepdims=True)
            scores = mask_ref[HG_LEVELS] * diag
            for lev in range(HG_LEVELS):
                a = jnp.exp(ee[lev * c:(lev + 1) * c, cs])
                sc = lax.dot_general((qh * a).astype(BF16), (kh * a).astype(BF16),
                                     nt_dims, preferred_element_type=F32)
                scores = scores + mask_ref[lev] * sc
            bh = bcum[:, cs]
            state = s_ref[h]
            o = (_dot((qh * jnp.exp(bh)).astype(BF16), state.astype(BF16))
                 + _dot(scores.astype(BF16), vh))
            bend = bh[c - 1:c, :]
            kd = (kh * jnp.exp(bend - bh)).astype(BF16)
            dec = jnp.transpose(jnp.broadcast_to(jnp.exp(bend), (hd, hd)))
            s_ref[h] = dec * state + lax.dot_general(kd, vh, tn_dims,
                                                     preferred_element_type=F32)
            ms = jnp.mean(o * o, axis=-1, keepdims=True)
            gg = gv[:, cs]
            o_ref[rs, cs] = (o * lax.rsqrt(ms + RMS_EPS) * nw
                             * (gg * _sigmoid(gg))).astype(BF16)


def _hgrn_mixer(proj3, lb, norm_w, tl, hb):
    nb, seq, _ = proj3.shape
    nc = tl // HG_CHUNK
    wb = hb * HG_HEAD_DIM
    q0 = (S5_WIDTH + 2 * SGU_WIDTH) // wb
    nblk = HG_WIDTH // wb
    lbf = lb.astype(F32)
    par = jnp.stack([jnp.log(lbf), jnp.log1p(-lbf), 1.0 - lbf])
    mst, masks = _hgrn_tables()
    sec = lambda k: pl.BlockSpec((None, tl, wb), functools.partial(
        lambda b, h, t, k: (b, t, q0 + k * nblk + h), k=k))
    body = functools.partial(_hgrn_body, nc=nc, hb=hb)
    return pl.pallas_call(
        body,
        out_shape=jax.ShapeDtypeStruct((nb, seq, HG_WIDTH), BF16),
        grid=(nb, nblk, seq // tl),
        in_specs=[sec(0), sec(1), sec(2), sec(3),
                  pl.BlockSpec((3, wb), lambda b, h, t: (0, h)),
                  pl.BlockSpec((1, HG_HEAD_DIM), lambda b, h, t: (0, 0)),
                  pl.BlockSpec(mst.shape, lambda b, h, t: (0, 0)),
                  pl.BlockSpec(masks.shape, lambda b, h, t: (0, 0, 0))],
        out_specs=pl.BlockSpec((None, tl, wb), lambda b, h, t: (b, t, h)),
        scratch_shapes=[pltpu.VMEM((hb, HG_HEAD_DIM, HG_HEAD_DIM), F32)],
        compiler_params=_cparams(("parallel", "parallel", "arbitrary")),
        name="hgrn2_mixer")(
            proj3, proj3, proj3, proj3, par,
            norm_w.astype(F32).reshape(1, HG_HEAD_DIM), mst, masks)


def _pick(n, prefs):
    for p in prefs:
        if n % p == 0:
            return p
    return n


def kernel(x, p, w_in, s5_lam_re, s5_lam_im, s5_log_step, s5_b_re, s5_b_im, s5_c_re, s5_c_im, s5_d, s5_w_glu, sgu_ln_w, sgu_ln_b, sgu_w, sgu_b, hg_lb_logits, hg_norm_w, norm_a_w, norm_b_w, w_out, ln1_w, ln1_b, w_ffn_in, w_ffn_out, ln2_w, ln2_b, w_ple_in, w_ple_gate, ln3_w, ln3_b):
    nb, seq, d_model = x.shape
    depth = w_in.shape[0]
    t_rows = nb * seq
    proj_w = w_in.shape[2]
    d_ff = w_ffn_out.shape[1]
    alpha = (2.0 * depth) ** 0.25

    tm = _pick(t_rows, (1024, 512, 256, 128))
    tm_ln = _pick(t_rows, (256, 128))
    tl_s5 = _pick(seq, (512, 256, 128))
    tl_sgu = _pick(seq, (512, 256, 128))
    tl_hg = _pick(seq, (256, 128))

    lbs = jnp.cumsum(jax.nn.softmax(hg_lb_logits.astype(F32), axis=0), axis=0)
    lbs = lbs - lbs[0:1]

    h32 = x.reshape(t_rows, d_model).astype(F32)
    h16 = h32.astype(BF16)
    row = lambda a: a.astype(F32).reshape(1, -1)

    for l in range(depth):
        (proj,) = _fused_mm([(h16, w_in[l].astype(BF16), d_model, 0, 0)], [],
                            lambda accs, ex: [accs[0]], [F32], proj_w, tm, 512, "in_proj")
        proj3 = proj.reshape(nb, seq, proj_w)
        prep = _s5_prepare(s5_lam_re[l], s5_lam_im[l], s5_log_step[l], s5_b_re[l],
                           s5_b_im[l], s5_c_re[l], s5_c_im[l], s5_d[l])
        z = _s5_mixer(proj3, prep, tl_s5).reshape(t_rows, S5_WIDTH)
        ya = _s5_glu_norm(z, s5_w_glu[l].astype(BF16), norm_a_w[l], _pick(t_rows, (512, 256, 128)))
        yb = _sgu_mixer(proj3, sgu_ln_w[l], sgu_ln_b[l], sgu_w[l], sgu_b[l],
                        norm_b_w[l], tl_sgu).reshape(t_rows, SGU_WIDTH)
        yc = _hgrn_mixer(proj3, lbs[l], hg_norm_w[l], tl_hg, 2).reshape(t_rows, HG_WIDTH)
        wo = w_out[l].astype(BF16)
        (xres,) = _fused_mm(
            [(ya, wo, S5_WIDTH, 0, 0), (yb, wo, SGU_WIDTH, 1, 0), (yc, wo, HG_WIDTH, 1, 0)],
            [(h32, "tile")],
            lambda accs, ex: [alpha * ex[0] + (accs[0] + accs[1] + accs[2])],
            [F32], d_model, tm, 512, "out_proj")
        h32, h16 = _layer_norm(xres, ln1_w[l], ln1_b[l], tm_ln)

        wf = w_ffn_in[l].astype(BF16)
        (hid,) = _fused_mm(
            [(h16, wf, d_model, 0, 0), (h16, wf, d_model, 0, d_ff // 256)], [],
            lambda accs, ex: [accs[0] * _sigmoid(accs[0]) * accs[1]],
            [BF16], d_ff, tm, 256, "ffn_in")
        (xres,) = _fused_mm(
            [(hid, w_ffn_out[l].astype(BF16), d_ff, 0, 0)], [(h32, "tile")],
            lambda accs, ex: [alpha * ex[0] + accs[0]],
            [F32], d_model, _pick(t_rows, (512, 256, 128)), 256, "ffn_out")
        h32, h16 = _layer_norm(xres, ln2_w[l], ln2_b[l], tm_ln)

        p16 = p[l].reshape(t_rows, -1).astype(BF16)
        (xres,) = _fused_mm(
            [(h16, w_ple_gate[l].astype(BF16), d_model, 0, 0),
             (p16, w_ple_in[l].astype(BF16), p16.shape[1], 0, 0)], [(h32, "tile")],
            lambda accs, ex: [alpha * ex[0] + accs[1] * _sigmoid(accs[0])],
            [F32], d_model, tm, 512, "ple")
        h32, h16 = _layer_norm(xres, ln3_w[l], ln3_b[l], tm_ln)

    return h32.reshape(nb, seq, d_model).astype(x.dtype)
```

```python
import functools
import math

import jax
import jax.numpy as jnp
from jax import lax
from jax.experimental import pallas as pl
from jax.experimental.pallas import tpu as pltpu

F32 = jnp.float32
BF16 = jnp.bfloat16

V7X_VMEM_LIMIT_BYTES = 56 * 1024 * 1024
LANES = 128

LN_EPS = 1e-5
RMS_EPS = 1e-6

S5_WIDTH = 1024
S5_GROUP_CH = 16
S5_STATE = 64
S5_CHUNK = 16
S5_TILE_GROUPS = LANES // S5_GROUP_CH
S5_TILE_STATE = S5_TILE_GROUPS * S5_STATE

SGU_WIDTH = 1024
SGU_CHUNK = 128
SGU_HEADS = 8

HG_WIDTH = 2048
HG_HEAD_DIM = 128
HG_CHUNK = 128
HG_LEVELS = 7


def _cparams(sem):
    return pltpu.CompilerParams(dimension_semantics=sem,
                                vmem_limit_bytes=V7X_VMEM_LIMIT_BYTES)


def _gelu_tanh(x):
    c = math.sqrt(2.0 / math.pi)
    return x * (0.5 * (1.0 + jnp.tanh(c * (x + 0.044715 * (x * x * x)))))


def _sigmoid(x):
    return 1.0 / (1.0 + jnp.exp(-x))


def _dot(a, b):
    return jnp.dot(a, b, preferred_element_type=F32)


def _pick(n, prefs):
    for p in prefs:
        if n % p == 0:
            return p
    return n


CAST_BLOCK_BYTES = 8 * 1024 * 1024


def _cast_body(w_ref, o_ref):
    o_ref[...] = w_ref[...].astype(BF16)


def _layer_to_bf16(w, layer):
    _, rows, cols = w.shape
    fits = [tr for tr in (8192, 4096, 2048, 1024, 512, 256, 128, 64, 32, 16)
            if rows % tr == 0 and tr * cols * 4 <= CAST_BLOCK_BYTES]
    tr = fits[0]
    return pl.pallas_call(
        _cast_body,
        out_shape=jax.ShapeDtypeStruct((rows, cols), BF16),
        grid=(rows // tr,),
        in_specs=[pl.BlockSpec((None, tr, cols), lambda i: (layer, i, 0))],
        out_specs=pl.BlockSpec((tr, cols), lambda i: (i, 0)),
        compiler_params=_cparams(("parallel",)), name="cast_bf16")(w)


def _mm_body(*refs, n_dot, n_extra, epilogue):
    a_refs = refs[:n_dot]
    w_refs = refs[n_dot:2 * n_dot]
    extra = refs[2 * n_dot:2 * n_dot + n_extra]
    out_refs = refs[2 * n_dot + n_extra:]
    accs = [_dot(a[...], w[...]) for a, w in zip(a_refs, w_refs)]
    outs = epilogue(accs, [e[...] for e in extra])
    for o_ref, o in zip(out_refs, outs):
        o_ref[...] = o.astype(o_ref.dtype)


def _fused_mm(dots, extras, epilogue, out_dtypes, n_cols, tm, tn, name):
    t_rows = dots[0][0].shape[0]
    grid = (t_rows // tm, n_cols // tn)
    in_specs, args = [], []
    for a, _, _, _, _ in dots:
        in_specs.append(pl.BlockSpec((tm, a.shape[1]), lambda i, j: (i, 0)))
        args.append(a)
    for _, w, rb, ri, co in dots:
        in_specs.append(pl.BlockSpec((rb, tn), functools.partial(
            lambda i, j, ri, co: (ri, j + co), ri=ri, co=co)))
        args.append(w)
    for arr, kind in extras:
        if kind == "tile":
            in_specs.append(pl.BlockSpec((tm, tn), lambda i, j: (i, j)))
        else:
            in_specs.append(pl.BlockSpec((1, tn), lambda i, j: (0, j)))
        args.append(arr)
    out_shape = [jax.ShapeDtypeStruct((t_rows, n_cols), dt) for dt in out_dtypes]
    out_specs = [pl.BlockSpec((tm, tn), lambda i, j: (i, j)) for _ in out_dtypes]
    body = functools.partial(_mm_body, n_dot=len(dots), n_extra=len(extras),
                             epilogue=epilogue)
    return pl.pallas_call(
        body, out_shape=out_shape, grid=grid, in_specs=in_specs, out_specs=out_specs,
        compiler_params=_cparams(("parallel", "arbitrary")), name=name)(*args)


def _ln_body(x_ref, w_ref, b_ref, h_ref, hb_ref):
    x = x_ref[...]
    mu = jnp.mean(x, axis=-1, keepdims=True)
    xc = x - mu
    var = jnp.mean(xc * xc, axis=-1, keepdims=True)
    y = xc * lax.rsqrt(var + LN_EPS) * w_ref[...] + b_ref[...]
    h_ref[...] = y
    hb_ref[...] = y.astype(BF16)


def _layer_norm(x, w, b, tm):
    t_rows, d = x.shape
    return pl.pallas_call(
        _ln_body,
        out_shape=[jax.ShapeDtypeStruct((t_rows, d), F32),
                   jax.ShapeDtypeStruct((t_rows, d), BF16)],
        grid=(t_rows // tm,),
        in_specs=[pl.BlockSpec((tm, d), lambda i: (i, 0)),
                  pl.BlockSpec((1, d), lambda i: (0, 0)),
                  pl.BlockSpec((1, d), lambda i: (0, 0))],
        out_specs=[pl.BlockSpec((tm, d), lambda i: (i, 0)),
                   pl.BlockSpec((tm, d), lambda i: (i, 0))],
        compiler_params=_cparams(("parallel",)), name="layer_norm")(
            x, w.reshape(1, d), b.reshape(1, d))


def _s5_prepare(lam_re, lam_im, log_step, b_re, b_im, c_re, c_im, d):
    hp = lax.Precision.HIGHEST
    n_tiles = S5_WIDTH // LANES
    lr = jnp.minimum(lam_re.astype(F32), -1e-4)
    li = lam_im.astype(F32)
    dt = jnp.exp(log_step.astype(F32))[:, None]
    mag = jnp.exp(lr * dt)
    ab_re = mag * jnp.cos(li * dt)
    ab_im = mag * jnp.sin(li * dt)
    den = lr * lr + li * li
    nr = ab_re - 1.0
    g_re = (nr * lr + ab_im * li) / den
    g_im = (ab_im * lr - nr * li) / den
    br = b_re.astype(F32)
    bi = b_im.astype(F32)
    bb_re = g_re[..., None] * br - g_im[..., None] * bi
    bb_im = g_re[..., None] * bi + g_im[..., None] * br
    pr, pi = [jnp.ones_like(ab_re)], [jnp.zeros_like(ab_re)]
    for _ in range(S5_CHUNK):
        pr_new = pr[-1] * ab_re - pi[-1] * ab_im
        pi_new = pr[-1] * ab_im + pi[-1] * ab_re
        pr.append(pr_new)
        pi.append(pi_new)
    p_re = jnp.stack(pr)
    p_im = jnp.stack(pi)
    pb_re = (p_re[:S5_CHUNK, :, :, None] * bb_re[None]
             - p_im[:S5_CHUNK, :, :, None] * bb_im[None])
    pb_im = (p_re[:S5_CHUNK, :, :, None] * bb_im[None]
             + p_im[:S5_CHUNK, :, :, None] * bb_re[None])
    cr = c_re.astype(F32)
    ci = c_im.astype(F32)
    kern = (jnp.einsum("gon,jgni->jgoi", cr, pb_re, precision=hp)
            - jnp.einsum("gon,jgni->jgoi", ci, pb_im, precision=hp))
    tg, gc, ns = S5_TILE_GROUPS, S5_GROUP_CH, S5_STATE

    def spread(x2d, rep, row_group, col_group):
        rows, width = x2d.shape
        sel = jnp.tile(jnp.eye(width, dtype=BF16), (1, rep))
        out = jnp.dot(x2d.astype(BF16), sel)
        rg = row_group(jnp.arange(rows))[:, None]
        cg = col_group(jnp.arange(width * rep))[None, :]
        return jnp.where(rg == cg, out, jnp.zeros_like(out))

    k5 = kern.reshape(S5_CHUNK, n_tiles, tg, gc, gc).transpose(1, 0, 2, 4, 3)
    kcat = spread(k5.reshape(-1, gc), tg, lambda r: (r // gc) % tg,
                  lambda c: c // gc).reshape(n_tiles, S5_CHUNK * LANES, LANES)

    def w1_half(pb):
        pb5 = pb[::-1].reshape(S5_CHUNK, n_tiles, tg, ns, gc).transpose(1, 0, 2, 4, 3)
        return spread(pb5.reshape(-1, ns), tg, lambda r: (r // gc) % tg,
                      lambda c: c // ns).reshape(n_tiles, S5_CHUNK * LANES, S5_TILE_STATE)
    w1 = jnp.concatenate([w1_half(pb_re), w1_half(pb_im)], axis=-1)

    ca_re = cr[None] * p_re[1:, :, None, :] - ci[None] * p_im[1:, :, None, :]
    ca_im = cr[None] * p_im[1:, :, None, :] + ci[None] * p_re[1:, :, None, :]

    def p_half(ca):
        ca5 = ca.reshape(S5_CHUNK, n_tiles, tg, gc, ns).transpose(1, 2, 4, 0, 3)
        x2d = ca5.reshape(-1, S5_CHUNK * gc)
        rows, width = x2d.shape
        cols = jnp.arange(S5_CHUNK * LANES)
        src = jnp.arange(width)
        sel = ((src[:, None] // gc == cols[None, :] // LANES)
               & (src[:, None] % gc == cols[None, :] % gc)).astype(BF16)
        out = jnp.dot(x2d.astype(BF16), sel)
        rg = ((jnp.arange(rows) // ns) % tg)[:, None]
        cg = ((cols // gc) % tg)[None, :]
        return jnp.where(rg == cg, out, jnp.zeros_like(out)).reshape(
            n_tiles, S5_TILE_STATE, S5_CHUNK * LANES)
    pcat = jnp.concatenate([p_half(ca_re), -p_half(ca_im)], axis=1)

    a16 = jnp.concatenate([p_re[S5_CHUNK].reshape(n_tiles, 1, S5_TILE_STATE),
                           p_im[S5_CHUNK].reshape(n_tiles, 1, S5_TILE_STATE)], axis=-1)
    dd = d.astype(F32).reshape(n_tiles, 1, LANES)
    return kcat.astype(BF16), w1.astype(BF16), pcat.astype(BF16), a16, dd


def _s5_body(x_ref, kcat_ref, w1_ref, pcat_ref, a16_ref, d_ref, z_ref,
             state_ref, xcat_ref, xr_ref, q_ref, sin_ref, yc_ref, yint_ref, *, nb, tl):
    t = pl.program_id(1)
    nk = tl // S5_CHUNK
    n = nb * tl
    ts = S5_TILE_STATE

    @pl.when(t == 0)
    def _():
        state_ref[...] = jnp.zeros_like(state_ref)

    for b in range(nb):
        for tp in range(S5_CHUNK):
            xr_ref[tp, pl.ds(b, nk, stride=nb), :] = x_ref[b, pl.ds(tp, nk, stride=S5_CHUNK), :]
    xr = jnp.concatenate([xr_ref[tp].astype(BF16) for tp in range(S5_CHUNK)], axis=1)
    q_ref[...] = _dot(xr, w1_ref[...])

    a_re = a16_ref[:, :ts]
    a_im = a16_ref[:, ts:]
    s = state_ref[...]
    for k in range(nk):
        sin_ref[k * nb:(k + 1) * nb, :] = s
        s_re = s[:, :ts]
        s_im = s[:, ts:]
        s = jnp.concatenate([a_re * s_re - a_im * s_im, a_re * s_im + a_im * s_re],
                            axis=1) + q_ref[k * nb:(k + 1) * nb, :]
    state_ref[...] = s

    yc = _dot(sin_ref[...].astype(BF16), pcat_ref[...])
    for tp in range(S5_CHUNK):
        yc_ref[tp] = yc[:, tp * LANES:(tp + 1) * LANES]
    for b in range(nb):
        for tp in range(S5_CHUNK):
            yint_ref[pl.ds(b * tl + tp, nk, stride=S5_CHUNK), :] = (
                yc_ref[tp, pl.ds(b, nk, stride=nb), :])

    x = x_ref[...].reshape(n, LANES)
    pos = lax.broadcasted_iota(jnp.int32, (n, LANES), 0) % S5_CHUNK
    for j in range(S5_CHUNK):
        xs = x if j == 0 else jnp.where(pos >= j, pltpu.roll(x, j, axis=0), 0.0)
        xcat_ref[:, j * LANES:(j + 1) * LANES] = xs.astype(BF16)
    y = _dot(xcat_ref[...], kcat_ref[...]) + yint_ref[...] + d_ref[...] * x
    z_ref[...] = _gelu_tanh(y).reshape(nb, tl, LANES)


def _s5_mixer(proj3, prep, tl):
    kcat, w1, pcat, a16, dd = prep
    nb, seq, _ = proj3.shape
    n_tiles = S5_WIDTH // LANES
    nk = tl // S5_CHUNK
    ts = S5_TILE_STATE
    body = functools.partial(_s5_body, nb=nb, tl=tl)
    wspec = lambda shp: pl.BlockSpec((None,) + shp, lambda j, t: (j, 0, 0))
    return pl.pallas_call(
        body,
        out_shape=jax.ShapeDtypeStruct((nb, seq, S5_WIDTH), F32),
        grid=(n_tiles, seq // tl),
        in_specs=[pl.BlockSpec((nb, tl, LANES), lambda j, t: (0, t, j)),
                  wspec((S5_CHUNK * LANES, LANES)),
                  wspec((S5_CHUNK * LANES, 2 * ts)),
                  wspec((2 * ts, S5_CHUNK * LANES)),
                  wspec((1, 2 * ts)),
                  wspec((1, LANES))],
        out_specs=pl.BlockSpec((nb, tl, LANES), lambda j, t: (0, t, j)),
        scratch_shapes=[pltpu.VMEM((nb, 2 * ts), F32),
                        pltpu.VMEM((nb * tl, S5_CHUNK * LANES), BF16),
                        pltpu.VMEM((S5_CHUNK, nb * nk, LANES), F32),
                        pltpu.VMEM((nb * nk, 2 * ts), F32),
                        pltpu.VMEM((nb * nk, 2 * ts), F32),
                        pltpu.VMEM((S5_CHUNK, nb * nk, LANES), F32),
                        pltpu.VMEM((nb * tl, LANES), F32)],
        compiler_params=_cparams(("parallel", "arbitrary")), name="s5_mixer")(
            proj3, kcat, w1, pcat, a16, dd)


def _glu_body(z_ref, w_ref, nw_ref, o_ref):
    z = z_ref[...]
    y = z * _sigmoid(_dot(z.astype(BF16), w_ref[...]))
    ms = jnp.mean(y * y, axis=-1, keepdims=True)
    o_ref[...] = (y * lax.rsqrt(ms + RMS_EPS) * nw_ref[...]).astype(BF16)


def _s5_glu_norm(z, w_glu, norm_w, tm):
    t_rows, d = z.shape
    return pl.pallas_call(
        _glu_body,
        out_shape=jax.ShapeDtypeStruct((t_rows, d), BF16),
        grid=(t_rows // tm,),
        in_specs=[pl.BlockSpec((tm, d), lambda i: (i, 0)),
                  pl.BlockSpec((d, d), lambda i: (0, 0)),
                  pl.BlockSpec((1, d), lambda i: (0, 0))],
        out_specs=pl.BlockSpec((tm, d), lambda i: (i, 0)),
        compiler_params=_cparams(("parallel",)), name="s5_glu_norm")(
            z, w_glu, norm_w.reshape(1, d))


def _sgu_body(u_ref, v_ref, lnw_ref, lnb_ref, w_ref, bs_ref, nw_ref, o_ref,
              vb_ref, x_ref, *, nc):
    v = _gelu_tanh(v_ref[...])
    mu = jnp.mean(v, axis=-1, keepdims=True)
    vc = v - mu
    var = jnp.mean(vc * vc, axis=-1, keepdims=True)
    vb_ref[...] = (vc * lax.rsqrt(var + LN_EPS) * lnw_ref[...] + lnb_ref[...]).astype(BF16)
    hd = SGU_WIDTH // SGU_HEADS
    for h in range(SGU_HEADS):
        cs = slice(h * hd, (h + 1) * hd)
        rhs = jnp.concatenate(
            [vb_ref[c * SGU_CHUNK:(c + 1) * SGU_CHUNK, cs] for c in range(nc)], axis=1)
        zz = _dot(w_ref[h], rhs)
        for c in range(nc):
            rs = slice(c * SGU_CHUNK, (c + 1) * SGU_CHUNK)
            z = zz[:, c * hd:(c + 1) * hd] + bs_ref[:, cs]
            x_ref[rs, cs] = _gelu_tanh(u_ref[rs, cs]) * z
    x = x_ref[...]
    ms = jnp.mean(x * x, axis=-1, keepdims=True)
    o_ref[...] = (x * lax.rsqrt(ms + RMS_EPS) * nw_ref[...]).astype(BF16)


def _sgu_mixer(proj3, ln_w, ln_b, w_s, b_s, norm_w, tl):
    nb, seq, _ = proj3.shape
    nc = tl // SGU_CHUNK
    hd = SGU_WIDTH // SGU_HEADS
    causal = jnp.tril(jnp.ones((SGU_CHUNK, SGU_CHUNK), dtype=bool))
    w_causal = jnp.where(causal[None], w_s, jnp.zeros_like(w_s)).astype(BF16)
    bias = jnp.repeat(b_s.astype(F32).T, hd, axis=1)
    u_blk = S5_WIDTH // SGU_WIDTH
    row = lambda a: a.astype(F32).reshape(1, SGU_WIDTH)
    const = lambda shp: pl.BlockSpec(shp, lambda b, t: (0,) * len(shp))
    body = functools.partial(_sgu_body, nc=nc)
    return pl.pallas_call(
        body,
        out_shape=jax.ShapeDtypeStruct((nb, seq, SGU_WIDTH), BF16),
        grid=(nb, seq // tl),
        in_specs=[pl.BlockSpec((None, tl, SGU_WIDTH), lambda b, t: (b, t, u_blk)),
                  pl.BlockSpec((None, tl, SGU_WIDTH), lambda b, t: (b, t, u_blk + 1)),
                  const((1, SGU_WIDTH)), const((1, SGU_WIDTH)),
                  const((SGU_HEADS, SGU_CHUNK, SGU_CHUNK)),
                  const((SGU_CHUNK, SGU_WIDTH)), const((1, SGU_WIDTH))],
        out_specs=pl.BlockSpec((None, tl, SGU_WIDTH), lambda b, t: (b, t, 0)),
        scratch_shapes=[pltpu.VMEM((tl, SGU_WIDTH), BF16),
                        pltpu.VMEM((tl, SGU_WIDTH), F32)],
        compiler_params=_cparams(("parallel", "parallel")), name="sgu_mixer")(
            proj3, proj3, row(ln_w), row(ln_b), w_causal, bias, row(norm_w))


HG_PAIR = 2
HG_PAIR_W = HG_PAIR * HG_HEAD_DIM
HG_FAST_BLOCK = 32
HG_FAST_MIN_LOG_DECAY = -60.0
_NT = (((1,), (1,)), ((), ()))
_TN = (((0,), (0,)), ((), ()))


def _hgrn_tables():
    c = HG_CHUNK
    t = jnp.arange(c)[:, None]
    r = jnp.arange(c)[None, :]
    mats, masks = [], []
    for lev in range(HG_LEVELS):
        m = 1 << lev
        mid = (t // (2 * m)) * (2 * m) + m
        later = t >= mid
        mats.append(jnp.where(later, (r >= mid) & (r <= t), (r > t) & (r < mid)))
        same = (t // (2 * m)) == (r // (2 * m))
        masks.append(same & later & (r < mid))
    tri = r <= t
    mats.append(tri)
    masks.append(t == r)
    fb = HG_FAST_BLOCK
    fast = [masks[6], masks[5], ((t // fb) == (r // fb)) & tri]
    pair = lambda m: jnp.tile(m.astype(F32), (1, HG_PAIR))
    hh = jnp.arange(HG_PAIR_W) // HG_HEAD_DIM
    return dict(
        tri=tri.astype(BF16),
        mst=jnp.concatenate(mats, axis=0).astype(BF16),
        masks=jnp.stack(masks).astype(F32),
        fmasks=jnp.stack([pair(m) for m in fast]),
        bd=(hh[:, None] == hh[None, :]).astype(F32))


def _block_diag(x):
    z = jnp.zeros((x.shape[0], HG_HEAD_DIM), x.dtype)
    top = jnp.concatenate([x[:, :HG_HEAD_DIM], z], axis=1)
    bot = jnp.concatenate([z, x[:, HG_HEAD_DIM:]], axis=1)
    return jnp.concatenate([top, bot], axis=0)


def _rows(v, n):
    return jnp.broadcast_to(v, (n, v.shape[1]))


def _hgrn_finish(o, gv, nw, o_ref, rs):
    for h in range(HG_PAIR):
        cs = slice(h * HG_HEAD_DIM, (h + 1) * HG_HEAD_DIM)
        oh = o[:, cs]
        ms = jnp.mean(oh * oh, axis=-1, keepdims=True)
        gg = gv[:, cs]
        o_ref[rs, cs] = (oh * lax.rsqrt(ms + RMS_EPS) * nw * (gg * _sigmoid(gg))).astype(BF16)


def _hgrn_body(q_ref, f_ref, i_ref, g_ref, par_ref, nw_ref, tri_ref, mst_ref, mask_ref,
               fmask_ref, bd_ref, o_ref, st_ref, qs_ref, kk_ref, lf_ref, b_ref, *, nc):
    c = HG_CHUNK
    hd = HG_HEAD_DIM
    wb = HG_PAIR_W
    t = pl.program_id(2)

    @pl.when(t == 0)
    def _():
        st_ref[...] = jnp.zeros_like(st_ref)

    log_lb = par_ref[0:1, :]
    log_1m_lb = par_ref[1:2, :]
    one_m_lb = par_ref[2:3, :]
    nw = nw_ref[...]
    tri = tri_ref[...]
    row = lax.broadcasted_iota(jnp.int32, (c, wb), 0)

    wmin = None
    for ch in range(nc):
        rs = slice(ch * c, (ch + 1) * c)
        fr = f_ref[rs, :]
        e = jnp.exp(-jnp.abs(fr))
        r = 1.0 / (1.0 + e)
        sig_neg = jnp.where(fr >= 0, e * r, r)
        log_sig = jnp.minimum(fr, 0.0) - jnp.log(1.0 + e)
        y = log_1m_lb + log_sig
        log_f = jnp.maximum(log_lb, y) + jnp.log(1.0 + jnp.exp(-jnp.abs(log_lb - y)))
        hi = log_f.astype(BF16)
        lo = (log_f - hi.astype(F32)).astype(BF16)
        b = _dot(tri, hi) + _dot(tri, lo)
        qv = q_ref[rs, :]
        qs_ref[rs, :] = qv * _sigmoid(qv)
        kk_ref[rs, :] = one_m_lb * sig_neg
        lf_ref[rs, :] = log_f
        b_ref[rs, :] = b
        fb = HG_FAST_BLOCK
        starts = jnp.concatenate(
            [jnp.zeros((fb, wb), F32)]
            + [_rows(b[j * fb - 1:j * fb, :], fb) for j in range(1, c // fb)], axis=0)
        w = b - starts
        wmin = w if wmin is None else jnp.minimum(wmin, w)
    fast = jnp.min(wmin) >= HG_FAST_MIN_LOG_DECAY

    @pl.when(fast)
    def _():
        bd = bd_ref[...]
        for ch in range(nc):
            rs = slice(ch * c, (ch + 1) * c)
            b = b_ref[rs, :]
            qs = qs_ref[rs, :]
            kk = kk_ref[rs, :]
            vv = i_ref[rs, :].astype(BF16)
            b63 = _rows(b[63:64, :], c)
            e6 = jnp.where(row >= 64, b - b63, b63 - b)
            bm = jnp.concatenate([_rows(b[31:32, :], 64), _rows(b[95:96, :], 64)], axis=0)
            e5 = jnp.where((row % 64) >= 32, b - bm, bm - b)
            fb = HG_FAST_BLOCK
            starts = jnp.concatenate(
                [jnp.zeros((fb, wb), F32)]
                + [_rows(b[j * fb - 1:j * fb, :], fb) for j in range(1, c // fb)], axis=0)
            w = b - starts
            a6 = jnp.exp(e6)
            a5 = jnp.exp(e5)
            factors = [(a6, a6), (a5, a5), (jnp.exp(w), jnp.exp(-w))]
            scores = None
            for lev, (aq, ak) in enumerate(factors):
                sc = lax.dot_general((qs * aq).astype(BF16), _block_diag((kk * ak).astype(BF16)),
                                     _NT, preferred_element_type=F32)
                sc = fmask_ref[lev] * sc
                scores = sc if scores is None else scores + sc
            st = st_ref[...]
            o = (_dot(scores.astype(BF16), _block_diag(vv))
                 + lax.dot_general((qs * jnp.exp(b)).astype(BF16), st.astype(BF16), _NT,
                                   preferred_element_type=F32))
            bend = b[c - 1:c, :]
            kd = (kk * jnp.exp(bend - b)).astype(BF16)
            st_ref[...] = bd * (st * jnp.exp(bend)
                                + lax.dot_general(vv, kd, _TN, preferred_element_type=F32))
            _hgrn_finish(o, g_ref[rs, :], nw, o_ref, rs)

    @pl.when(jnp.logical_not(fast))
    def _():
        mst = mst_ref[...]

        def chunk(ch, carry):
            rs = pl.ds(pl.multiple_of(ch * c, c), c)
            log_f = lf_ref[rs, :]
            hi = log_f.astype(BF16)
            lo = (log_f - hi.astype(F32)).astype(BF16)
            ee = _dot(mst, hi) + _dot(mst, lo)
            qs = qs_ref[rs, :]
            kk = kk_ref[rs, :]
            vv = i_ref[rs, :].astype(BF16)
            bcum = ee[HG_LEVELS * c:(HG_LEVELS + 1) * c, :]
            outs = []
            for h in range(HG_PAIR):
                cs = slice(h * hd, (h + 1) * hd)
                qh = qs[:, cs]
                kh = kk[:, cs]
                vh = vv[:, cs]
                scores = mask_ref[HG_LEVELS] * jnp.sum(qh * kh, axis=-1, keepdims=True)
                for lev in range(HG_LEVELS):
                    a = jnp.exp(ee[lev * c:(lev + 1) * c, cs])
                    sc = lax.dot_general((qh * a).astype(BF16), (kh * a).astype(BF16), _NT,
                                         preferred_element_type=F32)
                    scores = scores + mask_ref[lev] * sc
                bh = bcum[:, cs]
                st = st_ref[cs, cs]
                outs.append(_dot(scores.astype(BF16), vh)
                            + lax.dot_general((qh * jnp.exp(bh)).astype(BF16), st.astype(BF16),
                                              _NT, preferred_element_type=F32))
                bend = bh[c - 1:c, :]
                kd = (kh * jnp.exp(bend - bh)).astype(BF16)
                st_ref[cs, cs] = st * jnp.exp(bend) + lax.dot_general(
                    vh, kd, _TN, preferred_element_type=F32)
            _hgrn_finish(jnp.concatenate(outs, axis=1), g_ref[rs, :], nw, o_ref, rs)
            return carry

        lax.fori_loop(0, nc, chunk, 0)


def _hgrn_mixer(proj3, lb, norm_w, tl):
    nb, seq, _ = proj3.shape
    nc = tl // HG_CHUNK
    wb = HG_PAIR_W
    q0 = (S5_WIDTH + 2 * SGU_WIDTH) // wb
    nblk = HG_WIDTH // wb
    lbf = lb.astype(F32)
    par = jnp.stack([jnp.log(lbf), jnp.log1p(-lbf), 1.0 - lbf])
    tb = _hgrn_tables()
    sec = lambda k: pl.BlockSpec((None, tl, wb), functools.partial(
        lambda b, h, t, k: (b, t, q0 + k * nblk + h), k=k))
    const = lambda a: pl.BlockSpec(a.shape, lambda b, h, t: (0,) * a.ndim)
    body = functools.partial(_hgrn_body, nc=nc)
    return pl.pallas_call(
        body,
        out_shape=jax.ShapeDtypeStruct((nb, seq, HG_WIDTH), BF16),
        grid=(nb, nblk, seq // tl),
        in_specs=[sec(0), sec(1), sec(2), sec(3),
                  pl.BlockSpec((3, wb), lambda b, h, t: (0, h)),
                  pl.BlockSpec((1, HG_HEAD_DIM), lambda b, h, t: (0, 0)),
                  const(tb["tri"]), const(tb["mst"]), const(tb["masks"]),
                  const(tb["fmasks"]), const(tb["bd"])],
        out_specs=pl.BlockSpec((None, tl, wb), lambda b, h, t: (b, t, h)),
        scratch_shapes=[pltpu.VMEM((wb, wb), F32)] + [pltpu.VMEM((tl, wb), F32)] * 4,
        compiler_params=_cparams(("parallel", "parallel", "arbitrary")),
        name="hgrn2_mixer")(
            proj3, proj3, proj3, proj3, par,
            norm_w.astype(F32).reshape(1, HG_HEAD_DIM),
            tb["tri"], tb["mst"], tb["masks"], tb["fmasks"], tb["bd"])


def kernel(x, p, w_in, s5_lam_re, s5_lam_im, s5_log_step, s5_b_re, s5_b_im, s5_c_re, s5_c_im, s5_d, s5_w_glu, sgu_ln_w, sgu_ln_b, sgu_w, sgu_b, hg_lb_logits, hg_norm_w, norm_a_w, norm_b_w, w_out, ln1_w, ln1_b, w_ffn_in, w_ffn_out, ln2_w, ln2_b, w_ple_in, w_ple_gate, ln3_w, ln3_b):
    nb, seq, d_model = x.shape
    depth = w_in.shape[0]
    t_rows = nb * seq
    proj_w = w_in.shape[2]
    d_ff = w_ffn_out.shape[1]
    alpha = (2.0 * depth) ** 0.25

    tm = _pick(t_rows, (1024, 512, 256, 128))
    tm_ln = _pick(t_rows, (256, 128))
    tl_s5 = _pick(seq, (512, 256, 128))
    tl_sgu = _pick(seq, (512, 256, 128))
    tl_hg = _pick(seq, (512, 256, 128))

    lbs = jnp.cumsum(jax.nn.softmax(hg_lb_logits.astype(F32), axis=0), axis=0)
    lbs = lbs - lbs[0:1]

    h32 = x.reshape(t_rows, d_model).astype(F32)
    h16 = _layer_to_bf16(h32.reshape(1, t_rows, d_model), 0)
    p_rows = p.reshape(depth, t_rows, -1)

    for l in range(depth):
        (proj,) = _fused_mm([(h16, _layer_to_bf16(w_in, l), d_model, 0, 0)], [],
                            lambda accs, ex: [accs[0]], [F32], proj_w, tm,
                            _pick(proj_w, (1024, 512)), "in_proj")
        proj3 = proj.reshape(nb, seq, proj_w)
        prep = _s5_prepare(s5_lam_re[l], s5_lam_im[l], s5_log_step[l], s5_b_re[l],
                           s5_b_im[l], s5_c_re[l], s5_c_im[l], s5_d[l])
        z = _s5_mixer(proj3, prep, tl_s5).reshape(t_rows, S5_WIDTH)
        ya = _s5_glu_norm(z, _layer_to_bf16(s5_w_glu, l), norm_a_w[l],
                          _pick(t_rows, (512, 256, 128)))
        yb = _sgu_mixer(proj3, sgu_ln_w[l], sgu_ln_b[l], sgu_w[l], sgu_b[l],
                        norm_b_w[l], tl_sgu).reshape(t_rows, SGU_WIDTH)
        yc = _hgrn_mixer(proj3, lbs[l], hg_norm_w[l], tl_hg).reshape(t_rows, HG_WIDTH)
        wo = _layer_to_bf16(w_out, l)
        (xres,) = _fused_mm(
            [(ya, wo, S5_WIDTH, 0, 0), (yb, wo, SGU_WIDTH, 1, 0), (yc, wo, HG_WIDTH, 1, 0)],
            [(h32, "tile")],
            lambda accs, ex: [alpha * ex[0] + (accs[0] + accs[1] + accs[2])],
            [F32], d_model, tm, 512, "out_proj")
        h32, h16 = _layer_norm(xres, ln1_w[l], ln1_b[l], tm_ln)

        wf = _layer_to_bf16(w_ffn_in, l)
        (hid,) = _fused_mm(
            [(h16, wf, d_model, 0, 0), (h16, wf, d_model, 0, d_ff // 256)], [],
            lambda accs, ex: [accs[0] * _sigmoid(accs[0]) * accs[1]],
            [BF16], d_ff, tm, 256, "ffn_in")
        (xres,) = _fused_mm(
            [(hid, _layer_to_bf16(w_ffn_out, l), d_ff, 0, 0)], [(h32, "tile")],
            lambda accs, ex: [alpha * ex[0] + accs[0]],
            [F32], d_model, _pick(t_rows, (512, 256, 128)), 256, "ffn_out")
        h32, h16 = _layer_norm(xres, ln2_w[l], ln2_b[l], tm_ln)

        p16 = _layer_to_bf16(p_rows, l)
        (xres,) = _fused_mm(
            [(h16, _layer_to_bf16(w_ple_gate, l), d_model, 0, 0),
             (p16, _layer_to_bf16(w_ple_in, l), p16.shape[1], 0, 0)], [(h32, "tile")],
            lambda accs, ex: [alpha * ex[0] + accs[1] * _sigmoid(accs[0])],
            [F32], d_model, tm, 512, "ple")
        h32, h16 = _layer_norm(xres, ln3_w[l], ln3_b[l], tm_ln)

    return h32.reshape(nb, seq, d_model).astype(x.dtype)
```

```python
import functools
import math

import jax
import jax.numpy as jnp
from jax import lax
from jax.experimental import pallas as pl
from jax.experimental.pallas import tpu as pltpu

F32 = jnp.float32
BF16 = jnp.bfloat16

V7X_VMEM_LIMIT_BYTES = 56 * 1024 * 1024
LANES = 128

LN_EPS = 1e-5
RMS_EPS = 1e-6

S5_WIDTH = 1024
S5_GROUP_CH = 16
S5_STATE = 64
S5_CHUNK = 16
S5_TILE_GROUPS = LANES // S5_GROUP_CH
S5_TILE_STATE = S5_TILE_GROUPS * S5_STATE

SGU_WIDTH = 1024
SGU_CHUNK = 128
SGU_HEADS = 8

HG_WIDTH = 2048
HG_HEAD_DIM = 128
HG_CHUNK = 128
HG_LEVELS = 7


def _cparams(sem):
    return pltpu.CompilerParams(dimension_semantics=sem,
                                vmem_limit_bytes=V7X_VMEM_LIMIT_BYTES)


def _gelu_tanh(x):
    c = math.sqrt(2.0 / math.pi)
    return x * (0.5 * (1.0 + jnp.tanh(c * (x + 0.044715 * (x * x * x)))))


def _sigmoid(x):
    return 1.0 / (1.0 + jnp.exp(-x))


def _dot(a, b):
    return jnp.dot(a, b, preferred_element_type=F32)


def _pick(n, prefs):
    for p in prefs:
        if n % p == 0:
            return p
    return n


CAST_BLOCK_BYTES = 8 * 1024 * 1024


def _cast_body(w_ref, o_ref):
    o_ref[...] = w_ref[...].astype(BF16)


def _layer_to_bf16(w, layer):
    _, rows, cols = w.shape
    fits = [tr for tr in (8192, 4096, 2048, 1024, 512, 256, 128, 64, 32, 16)
            if rows % tr == 0 and tr * cols * 4 <= CAST_BLOCK_BYTES]
    tr = fits[0]
    return pl.pallas_call(
        _cast_body,
        out_shape=jax.ShapeDtypeStruct((rows, cols), BF16),
        grid=(rows // tr,),
        in_specs=[pl.BlockSpec((None, tr, cols), lambda i: (layer, i, 0))],
        out_specs=pl.BlockSpec((tr, cols), lambda i: (i, 0)),
        compiler_params=_cparams(("parallel",)), name="cast_bf16")(w)


MM_SUB_ROWS = 256


def _mm_body(*refs, a_of_dot, n_a, extra_kinds, epilogue, sub_rows):
    n_dot = len(a_of_dot)
    n_extra = len(extra_kinds)
    a_refs = refs[:n_a]
    w_refs = refs[n_a:n_a + n_dot]
    extra = refs[n_a + n_dot:n_a + n_dot + n_extra]
    out_refs = refs[n_a + n_dot + n_extra:]
    rows = out_refs[0].shape[0]
    sub = min(rows, sub_rows)
    for r in range(rows // sub):
        rs = slice(r * sub, (r + 1) * sub)
        accs = [_dot(a_refs[ai][rs, :], w[...]) for ai, w in zip(a_of_dot, w_refs)]
        outs = epilogue(accs, [e[...] if kind == "row" else e[rs, :]
                               for e, kind in zip(extra, extra_kinds)])
        for o_ref, o in zip(out_refs, outs):
            o_ref[rs, :] = o.astype(o_ref.dtype)


def _fused_mm(dots, extras, epilogue, out_dtypes, n_cols, tm, tn, name,
              sub_rows=MM_SUB_ROWS):
    t_rows = dots[0][0].shape[0]
    grid = (t_rows // tm, n_cols // tn)
    in_specs, args = [], []
    a_of_dot = []
    for a, _, _, _, _ in dots:
        known = [k for k, seen in enumerate(args) if seen is a]
        if known:
            a_of_dot.append(known[0])
            continue
        a_of_dot.append(len(args))
        in_specs.append(pl.BlockSpec((tm, a.shape[1]), lambda i, j: (i, 0)))
        args.append(a)
    n_a = len(args)
    for _, w, rb, ri, co in dots:
        in_specs.append(pl.BlockSpec((rb, tn), functools.partial(
            lambda i, j, ri, co: (ri, j + co), ri=ri, co=co)))
        args.append(w)
    for arr, kind in extras:
        if kind == "tile":
            in_specs.append(pl.BlockSpec((tm, tn), lambda i, j: (i, j)))
        elif kind == "stat":
            in_specs.append(pl.BlockSpec((tm, LANES), lambda i, j: (i, 0)))
        else:
            in_specs.append(pl.BlockSpec((1, tn), lambda i, j: (0, j)))
        args.append(arr)
    out_shape = [jax.ShapeDtypeStruct((t_rows, n_cols), dt) for dt in out_dtypes]
    out_specs = [pl.BlockSpec((tm, tn), lambda i, j: (i, j)) for _ in out_dtypes]
    body = functools.partial(_mm_body, a_of_dot=tuple(a_of_dot), n_a=n_a,
                             extra_kinds=tuple(kind for _, kind in extras),
                             epilogue=epilogue, sub_rows=sub_rows)
    return pl.pallas_call(
        body, out_shape=out_shape, grid=grid, in_specs=in_specs, out_specs=out_specs,
        compiler_params=_cparams(("parallel", "arbitrary")), name=name)(*args)


def _ln_rows(x):
    mu = jnp.mean(x, axis=-1, keepdims=True)
    xc = x - mu
    var = jnp.mean(xc * xc, axis=-1, keepdims=True)
    return xc, mu, lax.rsqrt(var + LN_EPS)


def _ln_stats_body(x_ref, w_ref, b_ref, hb_ref, mu_ref, rs_ref):
    xc, mu, rstd = _ln_rows(x_ref[...])
    hb_ref[...] = (xc * rstd * w_ref[...] + b_ref[...]).astype(BF16)
    mu_ref[...] = jnp.broadcast_to(mu, mu_ref.shape)
    rs_ref[...] = jnp.broadcast_to(rstd, rs_ref.shape)


def _ln_final_body(x_ref, w_ref, b_ref, h_ref):
    xc, _, rstd = _ln_rows(x_ref[...])
    h_ref[...] = xc * rstd * w_ref[...] + b_ref[...]


def _layer_norm(x, w, b, tm, final):
    t_rows, d = x.shape
    rows = pl.BlockSpec((tm, d), lambda i: (i, 0))
    stat = pl.BlockSpec((tm, LANES), lambda i: (i, 0))
    vec = pl.BlockSpec((1, d), lambda i: (0, 0))
    if final:
        body, out_shape, out_specs = (
            _ln_final_body, jax.ShapeDtypeStruct((t_rows, d), F32), rows)
    else:
        body = _ln_stats_body
        out_shape = [jax.ShapeDtypeStruct((t_rows, d), BF16),
                     jax.ShapeDtypeStruct((t_rows, LANES), F32),
                     jax.ShapeDtypeStruct((t_rows, LANES), F32)]
        out_specs = [rows, stat, stat]
    return pl.pallas_call(
        body, out_shape=out_shape, grid=(t_rows // tm,),
        in_specs=[rows, vec, vec], out_specs=out_specs,
        compiler_params=_cparams(("parallel",)), name="layer_norm")(
            x, w.reshape(1, d), b.reshape(1, d))


class _Residual:
    def __init__(self, alpha, h=None, x=None, mu=None, rstd=None, w=None, b=None):
        self.alpha = alpha
        if h is not None:
            self.extras = [(h, "tile")]
        else:
            self.extras = [(x, "tile"), (mu, "stat"), (rstd, "stat"),
                           (w.astype(F32).reshape(1, -1), "row"),
                           (b.astype(F32).reshape(1, -1), "row")]

    def __call__(self, ex):
        if len(ex) == 1:
            return self.alpha * ex[0]
        x, mu, rstd, w, b = ex
        rep = x.shape[1] // LANES
        wide = lambda s: jnp.concatenate([s] * rep, axis=1)
        return self.alpha * ((x - wide(mu)) * wide(rstd) * w + b)


def _s5_prepare(lam_re, lam_im, log_step, b_re, b_im, c_re, c_im, d):
    hp = lax.Precision.HIGHEST
    n_tiles = S5_WIDTH // LANES
    lr = jnp.minimum(lam_re.astype(F32), -1e-4)
    li = lam_im.astype(F32)
    dt = jnp.exp(log_step.astype(F32))[:, None]
    mag = jnp.exp(lr * dt)
    ab_re = mag * jnp.cos(li * dt)
    ab_im = mag * jnp.sin(li * dt)
    den = lr * lr + li * li
    nr = ab_re - 1.0
    g_re = (nr * lr + ab_im * li) / den
    g_im = (ab_im * lr - nr * li) / den
    br = b_re.astype(F32)
    bi = b_im.astype(F32)
    bb_re = g_re[..., None] * br - g_im[..., None] * bi
    bb_im = g_re[..., None] * bi + g_im[..., None] * br
    pr, pi = [jnp.ones_like(ab_re)], [jnp.zeros_like(ab_re)]
    for _ in range(S5_CHUNK):
        pr_new = pr[-1] * ab_re - pi[-1] * ab_im
        pi_new = pr[-1] * ab_im + pi[-1] * ab_re
        pr.append(pr_new)
        pi.append(pi_new)
    p_re = jnp.stack(pr)
    p_im = jnp.stack(pi)
    pb_re = (p_re[:S5_CHUNK, :, :, None] * bb_re[None]
             - p_im[:S5_CHUNK, :, :, None] * bb_im[None])
    pb_im = (p_re[:S5_CHUNK, :, :, None] * bb_im[None]
             + p_im[:S5_CHUNK, :, :, None] * bb_re[None])
    cr = c_re.astype(F32)
    ci = c_im.astype(F32)
    kern = (jnp.einsum("gon,jgni->jgoi", cr, pb_re, precision=hp)
            - jnp.einsum("gon,jgni->jgoi", ci, pb_im, precision=hp))
    tg, gc, ns = S5_TILE_GROUPS, S5_GROUP_CH, S5_STATE

    def spread(x2d, rep, row_group, col_group):
        rows, width = x2d.shape
        sel = jnp.tile(jnp.eye(width, dtype=BF16), (1, rep))
        out = jnp.dot(x2d.astype(BF16), sel)
        rg = row_group(jnp.arange(rows))[:, None]
        cg = col_group(jnp.arange(width * rep))[None, :]
        return jnp.where(rg == cg, out, jnp.zeros_like(out))

    k5 = kern.reshape(S5_CHUNK, n_tiles, tg, gc, gc).transpose(1, 0, 2, 4, 3)
    kcat = spread(k5.reshape(-1, gc), tg, lambda r: (r // gc) % tg,
                  lambda c: c // gc).reshape(n_tiles, S5_CHUNK * LANES, LANES)

    def w1_half(pb):
        pb5 = pb[::-1].reshape(S5_CHUNK, n_tiles, tg, ns, gc).transpose(1, 0, 2, 4, 3)
        return spread(pb5.reshape(-1, ns), tg, lambda r: (r // gc) % tg,
                      lambda c: c // ns).reshape(n_tiles, S5_CHUNK * LANES, S5_TILE_STATE)
    w1 = jnp.concatenate([w1_half(pb_re), w1_half(pb_im)], axis=-1)

    ca_re = cr[None] * p_re[1:, :, None, :] - ci[None] * p_im[1:, :, None, :]
    ca_im = cr[None] * p_im[1:, :, None, :] + ci[None] * p_re[1:, :, None, :]

    def p_half(ca):
        ca5 = ca.reshape(S5_CHUNK, n_tiles, tg, gc, ns).transpose(1, 2, 4, 0, 3)
        x2d = ca5.reshape(-1, S5_CHUNK * gc)
        rows, width = x2d.shape
        cols = jnp.arange(S5_CHUNK * LANES)
        src = jnp.arange(width)
        sel = ((src[:, None] // gc == cols[None, :] // LANES)
               & (src[:, None] % gc == cols[None, :] % gc)).astype(BF16)
        out = jnp.dot(x2d.astype(BF16), sel)
        rg = ((jnp.arange(rows) // ns) % tg)[:, None]
        cg = ((cols // gc) % tg)[None, :]
        return jnp.where(rg == cg, out, jnp.zeros_like(out)).reshape(
            n_tiles, S5_TILE_STATE, S5_CHUNK * LANES)
    pcat = jnp.concatenate([p_half(ca_re), -p_half(ca_im)], axis=1)

    a16 = jnp.concatenate([p_re[S5_CHUNK].reshape(n_tiles, 1, S5_TILE_STATE),
                           p_im[S5_CHUNK].reshape(n_tiles, 1, S5_TILE_STATE)], axis=-1)
    dd = d.astype(F32).reshape(n_tiles, 1, LANES)
    return kcat.astype(BF16), w1.astype(BF16), pcat.astype(BF16), a16, dd


def _s5_body(x_ref, kcat_ref, w1_ref, pcat_ref, a16_ref, d_ref, z_ref,
             state_ref, xcat_ref, xr_ref, q_ref, sin_ref, yc_ref, yint_ref, *, nb, tl):
    t = pl.program_id(1)
    nk = tl // S5_CHUNK
    n = nb * tl
    ts = S5_TILE_STATE

    @pl.when(t == 0)
    def _():
        state_ref[...] = jnp.zeros_like(state_ref)

    for b in range(nb):
        for tp in range(S5_CHUNK):
            xr_ref[tp, pl.ds(b, nk, stride=nb), :] = x_ref[b, pl.ds(tp, nk, stride=S5_CHUNK), :]
    xr = jnp.concatenate([xr_ref[tp].astype(BF16) for tp in range(S5_CHUNK)], axis=1)
    q_ref[...] = _dot(xr, w1_ref[...])

    a_re = a16_ref[:, :ts]
    a_im = a16_ref[:, ts:]
    s = state_ref[...]
    for k in range(nk):
        sin_ref[k * nb:(k + 1) * nb, :] = s
        s_re = s[:, :ts]
        s_im = s[:, ts:]
        s = jnp.concatenate([a_re * s_re - a_im * s_im, a_re * s_im + a_im * s_re],
                            axis=1) + q_ref[k * nb:(k + 1) * nb, :]
    state_ref[...] = s

    yc = _dot(sin_ref[...].astype(BF16), pcat_ref[...])
    for tp in range(S5_CHUNK):
        yc_ref[tp] = yc[:, tp * LANES:(tp + 1) * LANES]
    for b in range(nb):
        for tp in range(S5_CHUNK):
            yint_ref[pl.ds(b * tl + tp, nk, stride=S5_CHUNK), :] = (
                yc_ref[tp, pl.ds(b, nk, stride=nb), :])

    x = x_ref[...].reshape(n, LANES)
    pos = lax.broadcasted_iota(jnp.int32, (n, LANES), 0) % S5_CHUNK
    for j in range(S5_CHUNK):
        xs = x if j == 0 else jnp.where(pos >= j, pltpu.roll(x, j, axis=0), 0.0)
        xcat_ref[:, j * LANES:(j + 1) * LANES] = xs.astype(BF16)
    y = _dot(xcat_ref[...], kcat_ref[...]) + yint_ref[...] + d_ref[...] * x
    z_ref[...] = _gelu_tanh(y).reshape(nb, tl, LANES)


def _s5_mixer(proj3, prep, tl):
    kcat, w1, pcat, a16, dd = prep
    nb, seq, _ = proj3.shape
    n_tiles = S5_WIDTH // LANES
    nk = tl // S5_CHUNK
    ts = S5_TILE_STATE
    body = functools.partial(_s5_body, nb=nb, tl=tl)
    wspec = lambda shp: pl.BlockSpec((None,) + shp, lambda j, t: (j, 0, 0))
    return pl.pallas_call(
        body,
        out_shape=jax.ShapeDtypeStruct((nb, seq, S5_WIDTH), F32),
        grid=(n_tiles, seq // tl),
        in_specs=[pl.BlockSpec((nb, tl, LANES), lambda j, t: (0, t, j)),
                  wspec((S5_CHUNK * LANES, LANES)),
                  wspec((S5_CHUNK * LANES, 2 * ts)),
                  wspec((2 * ts, S5_CHUNK * LANES)),
                  wspec((1, 2 * ts)),
                  wspec((1, LANES))],
        out_specs=pl.BlockSpec((nb, tl, LANES), lambda j, t: (0, t, j)),
        scratch_shapes=[pltpu.VMEM((nb, 2 * ts), F32),
                        pltpu.VMEM((nb * tl, S5_CHUNK * LANES), BF16),
                        pltpu.VMEM((S5_CHUNK, nb * nk, LANES), F32),
                        pltpu.VMEM((nb * nk, 2 * ts), F32),
                        pltpu.VMEM((nb * nk, 2 * ts), F32),
                        pltpu.VMEM((S5_CHUNK, nb * nk, LANES), F32),
                        pltpu.VMEM((nb * tl, LANES), F32)],
        compiler_params=_cparams(("parallel", "arbitrary")), name="s5_mixer")(
            proj3, kcat, w1, pcat, a16, dd)


def _glu_body(z_ref, w_ref, nw_ref, o_ref):
    z = z_ref[...]
    y = z * _sigmoid(_dot(z.astype(BF16), w_ref[...]))
    ms = jnp.mean(y * y, axis=-1, keepdims=True)
    o_ref[...] = (y * lax.rsqrt(ms + RMS_EPS) * nw_ref[...]).astype(BF16)


def _s5_glu_norm(z, w_glu, norm_w, tm):
    t_rows, d = z.shape
    return pl.pallas_call(
        _glu_body,
        out_shape=jax.ShapeDtypeStruct((t_rows, d), BF16),
        grid=(t_rows // tm,),
        in_specs=[pl.BlockSpec((tm, d), lambda i: (i, 0)),
                  pl.BlockSpec((d, d), lambda i: (0, 0)),
                  pl.BlockSpec((1, d), lambda i: (0, 0))],
        out_specs=pl.BlockSpec((tm, d), lambda i: (i, 0)),
        compiler_params=_cparams(("parallel",)), name="s5_glu_norm")(
            z, w_glu, norm_w.reshape(1, d))


def _sgu_body(u_ref, v_ref, lnw_ref, lnb_ref, w_ref, bs_ref, nw_ref, o_ref,
              vb_ref, x_ref, *, nc):
    v = _gelu_tanh(v_ref[...])
    mu = jnp.mean(v, axis=-1, keepdims=True)
    vc = v - mu
    var = jnp.mean(vc * vc, axis=-1, keepdims=True)
    vb_ref[...] = (vc * lax.rsqrt(var + LN_EPS) * lnw_ref[...] + lnb_ref[...]).astype(BF16)
    hd = SGU_WIDTH // SGU_HEADS
    for h in range(SGU_HEADS):
        cs = slice(h * hd, (h + 1) * hd)
        rhs = jnp.concatenate(
            [vb_ref[c * SGU_CHUNK:(c + 1) * SGU_CHUNK, cs] for c in range(nc)], axis=1)
        zz = _dot(w_ref[h], rhs)
        for c in range(nc):
            rs = slice(c * SGU_CHUNK, (c + 1) * SGU_CHUNK)
            z = zz[:, c * hd:(c + 1) * hd] + bs_ref[:, cs]
            x_ref[rs, cs] = _gelu_tanh(u_ref[rs, cs]) * z
    x = x_ref[...]
    ms = jnp.mean(x * x, axis=-1, keepdims=True)
    o_ref[...] = (x * lax.rsqrt(ms + RMS_EPS) * nw_ref[...]).astype(BF16)


def _sgu_mixer(proj3, ln_w, ln_b, w_s, b_s, norm_w, tl):
    nb, seq, _ = proj3.shape
    nc = tl // SGU_CHUNK
    hd = SGU_WIDTH // SGU_HEADS
    causal = jnp.tril(jnp.ones((SGU_CHUNK, SGU_CHUNK), dtype=bool))
    w_causal = jnp.where(causal[None], w_s, jnp.zeros_like(w_s)).astype(BF16)
    bias = jnp.repeat(b_s.astype(F32).T, hd, axis=1)
    u_blk = S5_WIDTH // SGU_WIDTH
    row = lambda a: a.astype(F32).reshape(1, SGU_WIDTH)
    const = lambda shp: pl.BlockSpec(shp, lambda b, t: (0,) * len(shp))
    body = functools.partial(_sgu_body, nc=nc)
    return pl.pallas_call(
        body,
        out_shape=jax.ShapeDtypeStruct((nb, seq, SGU_WIDTH), BF16),
        grid=(nb, seq // tl),
        in_specs=[pl.BlockSpec((None, tl, SGU_WIDTH), lambda b, t: (b, t, u_blk)),
                  pl.BlockSpec((None, tl, SGU_WIDTH), lambda b, t: (b, t, u_blk + 1)),
                  const((1, SGU_WIDTH)), const((1, SGU_WIDTH)),
                  const((SGU_HEADS, SGU_CHUNK, SGU_CHUNK)),
                  const((SGU_CHUNK, SGU_WIDTH)), const((1, SGU_WIDTH))],
        out_specs=pl.BlockSpec((None, tl, SGU_WIDTH), lambda b, t: (b, t, 0)),
        scratch_shapes=[pltpu.VMEM((tl, SGU_WIDTH), BF16),
                        pltpu.VMEM((tl, SGU_WIDTH), F32)],
        compiler_params=_cparams(("parallel", "parallel")), name="sgu_mixer")(
            proj3, proj3, row(ln_w), row(ln_b), w_causal, bias, row(norm_w))


HG_PAIR = 2
HG_PAIR_W = HG_PAIR * HG_HEAD_DIM
HG_FAST_BLOCK = 32
HG_FAST_MIN_LOG_DECAY = -60.0
_NT = (((1,), (1,)), ((), ()))
_TN = (((0,), (0,)), ((), ()))


def _hgrn_tables():
    c = HG_CHUNK
    t = jnp.arange(c)[:, None]
    r = jnp.arange(c)[None, :]
    mats, masks = [], []
    for lev in range(HG_LEVELS):
        m = 1 << lev
        mid = (t // (2 * m)) * (2 * m) + m
        later = t >= mid
        mats.append(jnp.where(later, (r >= mid) & (r <= t), (r > t) & (r < mid)))
        same = (t // (2 * m)) == (r // (2 * m))
        masks.append(same & later & (r < mid))
    tri = r <= t
    mats.append(tri)
    masks.append(t == r)
    fb = HG_FAST_BLOCK
    fast = [masks[6], masks[5], ((t // fb) == (r // fb)) & tri]
    pair = lambda m: jnp.tile(m.astype(F32), (1, HG_PAIR))
    hh = jnp.arange(HG_PAIR_W) // HG_HEAD_DIM
    return dict(
        tri=tri.astype(BF16),
        mst=jnp.concatenate(mats, axis=0).astype(BF16),
        masks=jnp.stack(masks).astype(F32),
        fmasks=jnp.stack([pair(m) for m in fast]),
        bd=(hh[:, None] == hh[None, :]).astype(F32))


def _block_diag(x):
    z = jnp.zeros((x.shape[0], HG_HEAD_DIM), x.dtype)
    top = jnp.concatenate([x[:, :HG_HEAD_DIM], z], axis=1)
    bot = jnp.concatenate([z, x[:, HG_HEAD_DIM:]], axis=1)
    return jnp.concatenate([top, bot], axis=0)


def _rows(v, n):
    return jnp.broadcast_to(v, (n, v.shape[1]))


def _hgrn_finish(o, gv, nw, o_ref, rs):
    for h in range(HG_PAIR):
        cs = slice(h * HG_HEAD_DIM, (h + 1) * HG_HEAD_DIM)
        oh = o[:, cs]
        ms = jnp.mean(oh * oh, axis=-1, keepdims=True)
        gg = gv[:, cs]
        o_ref[rs, cs] = (oh * lax.rsqrt(ms + RMS_EPS) * nw * (gg * _sigmoid(gg))).astype(BF16)


def _hgrn_body(q_ref, f_ref, i_ref, g_ref, par_ref, nw_ref, tri_ref, mst_ref, mask_ref,
               fmask_ref, bd_ref, o_ref, st_ref, qs_ref, kk_ref, lf_ref, b_ref, *, nc):
    c = HG_CHUNK
    hd = HG_HEAD_DIM
    wb = HG_PAIR_W
    t = pl.program_id(2)

    @pl.when(t == 0)
    def _():
        st_ref[...] = jnp.zeros_like(st_ref)

    log_lb = par_ref[0:1, :]
    log_1m_lb = par_ref[1:2, :]
    one_m_lb = par_ref[2:3, :]
    nw = nw_ref[...]
    tri = tri_ref[...]
    row = lax.broadcasted_iota(jnp.int32, (c, wb), 0)

    wmin = None
    for ch in range(nc):
        rs = slice(ch * c, (ch + 1) * c)
        fr = f_ref[rs, :]
        e = jnp.exp(-jnp.abs(fr))
        r = 1.0 / (1.0 + e)
        sig_neg = jnp.where(fr >= 0, e * r, r)
        log_sig = jnp.minimum(fr, 0.0) - jnp.log(1.0 + e)
        y = log_1m_lb + log_sig
        log_f = jnp.maximum(log_lb, y) + jnp.log(1.0 + jnp.exp(-jnp.abs(log_lb - y)))
        hi = log_f.astype(BF16)
        lo = (log_f - hi.astype(F32)).astype(BF16)
        b = _dot(tri, hi) + _dot(tri, lo)
        qv = q_ref[rs, :]
        qs_ref[rs, :] = qv * _sigmoid(qv)
        kk_ref[rs, :] = one_m_lb * sig_neg
        lf_ref[rs, :] = log_f
        b_ref[rs, :] = b
        fb = HG_FAST_BLOCK
        starts = jnp.concatenate(
            [jnp.zeros((fb, wb), F32)]
            + [_rows(b[j * fb - 1:j * fb, :], fb) for j in range(1, c // fb)], axis=0)
        w = b - starts
        wmin = w if wmin is None else jnp.minimum(wmin, w)
    fast = jnp.min(wmin) >= HG_FAST_MIN_LOG_DECAY

    @pl.when(fast)
    def _():
        bd = bd_ref[...]
        for ch in range(nc):
            rs = slice(ch * c, (ch + 1) * c)
            b = b_ref[rs, :]
            qs = qs_ref[rs, :]
            kk = kk_ref[rs, :]
            vv = i_ref[rs, :].astype(BF16)
            b63 = _rows(b[63:64, :], c)
            e6 = jnp.where(row >= 64, b - b63, b63 - b)
            bm = jnp.concatenate([_rows(b[31:32, :], 64), _rows(b[95:96, :], 64)], axis=0)
            e5 = jnp.where((row % 64) >= 32, b - bm, bm - b)
            fb = HG_FAST_BLOCK
            starts = jnp.concatenate(
                [jnp.zeros((fb, wb), F32)]
                + [_rows(b[j * fb - 1:j * fb, :], fb) for j in range(1, c // fb)], axis=0)
            w = b - starts
            a6 = jnp.exp(e6)
            a5 = jnp.exp(e5)
            factors = [(a6, a6), (a5, a5), (jnp.exp(w), jnp.exp(-w))]
            scores = None
            for lev, (aq, ak) in enumerate(factors):
                sc = lax.dot_general((qs * aq).astype(BF16), _block_diag((kk * ak).astype(BF16)),
                                     _NT, preferred_element_type=F32)
                sc = fmask_ref[lev] * sc
                scores = sc if scores is None else scores + sc
            st = st_ref[...]
            o = (_dot(scores.astype(BF16), _block_diag(vv))
                 + lax.dot_general((qs * jnp.exp(b)).astype(BF16), st.astype(BF16), _NT,
                                   preferred_element_type=F32))
            bend = b[c - 1:c, :]
            kd = (kk * jnp.exp(bend - b)).astype(BF16)
            st_ref[...] = bd * (st * jnp.exp(bend)
                                + lax.dot_general(vv, kd, _TN, preferred_element_type=F32))
            _hgrn_finish(o, g_ref[rs, :], nw, o_ref, rs)

    @pl.when(jnp.logical_not(fast))
    def _():
        mst = mst_ref[...]

        def chunk(ch, carry):
            rs = pl.ds(pl.multiple_of(ch * c, c), c)
            log_f = lf_ref[rs, :]
            hi = log_f.astype(BF16)
            lo = (log_f - hi.astype(F32)).astype(BF16)
            ee = _dot(mst, hi) + _dot(mst, lo)
            qs = qs_ref[rs, :]
            kk = kk_ref[rs, :]
            vv = i_ref[rs, :].astype(BF16)
            bcum = ee[HG_LEVELS * c:(HG_LEVELS + 1) * c, :]
            outs = []
            for h in range(HG_PAIR):
                cs = slice(h * hd, (h + 1) * hd)
                qh = qs[:, cs]
                kh = kk[:, cs]
                vh = vv[:, cs]
                scores = mask_ref[HG_LEVELS] * jnp.sum(qh * kh, axis=-1, keepdims=True)
                for lev in range(HG_LEVELS):
                    a = jnp.exp(ee[lev * c:(lev + 1) * c, cs])
                    sc = lax.dot_general((qh * a).astype(BF16), (kh * a).astype(BF16), _NT,
                                         preferred_element_type=F32)
                    scores = scores + mask_ref[lev] * sc
                bh = bcum[:, cs]
                st = st_ref[cs, cs]
                outs.append(_dot(scores.astype(BF16), vh)
                            + lax.dot_general((qh * jnp.exp(bh)).astype(BF16), st.astype(BF16),
                                              _NT, preferred_element_type=F32))
                bend = bh[c - 1:c, :]
                kd = (kh * jnp.exp(bend - bh)).astype(BF16)
                st_ref[cs, cs] = st * jnp.exp(bend) + lax.dot_general(
                    vh, kd, _TN, preferred_element_type=F32)
            _hgrn_finish(jnp.concatenate(outs, axis=1), g_ref[rs, :], nw, o_ref, rs)
            return carry

        lax.fori_loop(0, nc, chunk, 0)


def _hgrn_mixer(proj3, lb, norm_w, tl):
    nb, seq, _ = proj3.shape
    nc = tl // HG_CHUNK
    wb = HG_PAIR_W
    q0 = (S5_WIDTH + 2 * SGU_WIDTH) // wb
    nblk = HG_WIDTH // wb
    lbf = lb.astype(F32)
    par = jnp.stack([jnp.log(lbf), jnp.log1p(-lbf), 1.0 - lbf])
    tb = _hgrn_tables()
    sec = lambda k: pl.BlockSpec((None, tl, wb), functools.partial(
        lambda b, h, t, k: (b, t, q0 + k * nblk + h), k=k))
    const = lambda a: pl.BlockSpec(a.shape, lambda b, h, t: (0,) * a.ndim)
    body = functools.partial(_hgrn_body, nc=nc)
    return pl.pallas_call(
        body,
        out_shape=jax.ShapeDtypeStruct((nb, seq, HG_WIDTH), BF16),
        grid=(nb, nblk, seq // tl),
        in_specs=[sec(0), sec(1), sec(2), sec(3),
                  pl.BlockSpec((3, wb), lambda b, h, t: (0, h)),
                  pl.BlockSpec((1, HG_HEAD_DIM), lambda b, h, t: (0, 0)),
                  const(tb["tri"]), const(tb["mst"]), const(tb["masks"]),
                  const(tb["fmasks"]), const(tb["bd"])],
        out_specs=pl.BlockSpec((None, tl, wb), lambda b, h, t: (b, t, h)),
        scratch_shapes=[pltpu.VMEM((wb, wb), F32)] + [pltpu.VMEM((tl, wb), F32)] * 4,
        compiler_params=_cparams(("parallel", "parallel", "arbitrary")),
        name="hgrn2_mixer")(
            proj3, proj3, proj3, proj3, par,
            norm_w.astype(F32).reshape(1, HG_HEAD_DIM),
            tb["tri"], tb["mst"], tb["masks"], tb["fmasks"], tb["bd"])


def kernel(x, p, w_in, s5_lam_re, s5_lam_im, s5_log_step, s5_b_re, s5_b_im, s5_c_re, s5_c_im, s5_d, s5_w_glu, sgu_ln_w, sgu_ln_b, sgu_w, sgu_b, hg_lb_logits, hg_norm_w, norm_a_w, norm_b_w, w_out, ln1_w, ln1_b, w_ffn_in, w_ffn_out, ln2_w, ln2_b, w_ple_in, w_ple_gate, ln3_w, ln3_b):
    nb, seq, d_model = x.shape
    depth = w_in.shape[0]
    t_rows = nb * seq
    proj_w = w_in.shape[2]
    d_ff = w_ffn_out.shape[1]
    alpha = (2.0 * depth) ** 0.25

    tm = _pick(t_rows, (1024, 512, 256, 128))
    tm_wide = _pick(t_rows, (2048, 1024, 512, 256, 128))
    tm_half = _pick(t_rows, (512, 256, 128))
    tm_ln = _pick(t_rows, (256, 128))
    tl_s5 = _pick(seq, (512, 256, 128))
    tl_sgu = _pick(seq, (512, 256, 128))
    tl_hg = _pick(seq, (512, 256, 128))

    lbs = jnp.cumsum(jax.nn.softmax(hg_lb_logits.astype(F32), axis=0), axis=0)
    lbs = lbs - lbs[0:1]

    x_rows = x.reshape(t_rows, d_model).astype(F32)
    res = _Residual(alpha, h=x_rows)
    h16 = _layer_to_bf16(x_rows.reshape(1, t_rows, d_model), 0)
    p_rows = p.reshape(depth, t_rows, -1)

    def post_norm(xres, w, b):
        h16, mu, rstd = _layer_norm(xres, w, b, tm_ln, final=False)
        return h16, _Residual(alpha, x=xres, mu=mu, rstd=rstd, w=w, b=b)

    for l in range(depth):
        (proj,) = _fused_mm([(h16, _layer_to_bf16(w_in, l), d_model, 0, 0)], [],
                            lambda accs, ex: [accs[0]], [F32], proj_w, tm,
                            _pick(proj_w, (1024, 512)), "in_proj")
        proj3 = proj.reshape(nb, seq, proj_w)
        prep = _s5_prepare(s5_lam_re[l], s5_lam_im[l], s5_log_step[l], s5_b_re[l],
                           s5_b_im[l], s5_c_re[l], s5_c_im[l], s5_d[l])
        z = _s5_mixer(proj3, prep, tl_s5).reshape(t_rows, S5_WIDTH)
        ya = _s5_glu_norm(z, _layer_to_bf16(s5_w_glu, l), norm_a_w[l], tm_half)
        yb = _sgu_mixer(proj3, sgu_ln_w[l], sgu_ln_b[l], sgu_w[l], sgu_b[l],
                        norm_b_w[l], tl_sgu).reshape(t_rows, SGU_WIDTH)
        yc = _hgrn_mixer(proj3, lbs[l], hg_norm_w[l], tl_hg).reshape(t_rows, HG_WIDTH)
        wo = _layer_to_bf16(w_out, l)
        (xres,) = _fused_mm(
            [(ya, wo, S5_WIDTH, 0, 0), (yb, wo, SGU_WIDTH, 1, 0), (yc, wo, HG_WIDTH, 1, 0)],
            res.extras,
            functools.partial(lambda accs, ex, res: [res(ex) + (accs[0] + accs[1] + accs[2])],
                              res=res),
            [F32], d_model, tm, 512, "out_proj")
        h16, res = post_norm(xres, ln1_w[l], ln1_b[l])

        wf = _layer_to_bf16(w_ffn_in, l)
        (hid,) = _fused_mm(
            [(h16, wf, d_model, 0, 0), (h16, wf, d_model, 0, d_ff // 256)], [],
            lambda accs, ex: [accs[0] * _sigmoid(accs[0]) * accs[1]],
            [BF16], d_ff, tm_wide, 256, "ffn_in")
        (xres,) = _fused_mm(
            [(hid, _layer_to_bf16(w_ffn_out, l), d_ff, 0, 0)], res.extras,
            functools.partial(lambda accs, ex, res: [res(ex) + accs[0]], res=res),
            [F32], d_model, tm_half, 512, "ffn_out")
        h16, res = post_norm(xres, ln2_w[l], ln2_b[l])

        p16 = _layer_to_bf16(p_rows, l)
        (xres,) = _fused_mm(
            [(h16, _layer_to_bf16(w_ple_gate, l), d_model, 0, 0),
             (p16, _layer_to_bf16(w_ple_in, l), p16.shape[1], 0, 0)], res.extras,
            functools.partial(lambda accs, ex, res: [res(ex) + accs[1] * _sigmoid(accs[0])],
                              res=res),
            [F32], d_model, tm, 512, "ple", sub_rows=128)
        if l + 1 < depth:
            h16, res = post_norm(xres, ln3_w[l], ln3_b[l])

    out = _layer_norm(xres, ln3_w[depth - 1], ln3_b[depth - 1], tm_ln, final=True)
    return out.reshape(nb, seq, d_model).astype(x.dtype)
```

```python
import functools
import math

import jax
import jax.numpy as jnp
from jax import lax
from jax.experimental import pallas as pl
from jax.experimental.pallas import tpu as pltpu

F32 = jnp.float32
BF16 = jnp.bfloat16

V7X_VMEM_LIMIT_BYTES = 56 * 1024 * 1024
LANES = 128

LN_EPS = 1e-5
RMS_EPS = 1e-6

S5_WIDTH = 1024
S5_GROUP_CH = 16
S5_STATE = 64
S5_CHUNK = 8
S5_TILE_GROUPS = LANES // S5_GROUP_CH
S5_TILE_STATE = S5_TILE_GROUPS * S5_STATE

SGU_WIDTH = 1024
SGU_CHUNK = 128
SGU_HEADS = 8

HG_WIDTH = 2048
HG_HEAD_DIM = 128
HG_CHUNK = 128
HG_LEVELS = 7


def _cparams(sem):
    return pltpu.CompilerParams(dimension_semantics=sem,
                                vmem_limit_bytes=V7X_VMEM_LIMIT_BYTES)


def _gelu_tanh(x):
    c = math.sqrt(2.0 / math.pi)
    return x * (0.5 * (1.0 + jnp.tanh(c * (x + 0.044715 * (x * x * x)))))


def _sigmoid(x):
    return 1.0 / (1.0 + jnp.exp(-x))


def _dot(a, b):
    return jnp.dot(a, b, preferred_element_type=F32)


def _pick(n, prefs):
    for p in prefs:
        if n % p == 0:
            return p
    return n


CAST_BLOCK_BYTES = 8 * 1024 * 1024


def _cast_body(w_ref, o_ref):
    o_ref[...] = w_ref[...].astype(BF16)


def _layer_to_bf16(w, layer):
    _, rows, cols = w.shape
    fits = [tr for tr in (8192, 4096, 2048, 1024, 512, 256, 128, 64, 32, 16)
            if rows % tr == 0 and tr * cols * 4 <= CAST_BLOCK_BYTES]
    tr = fits[0]
    return pl.pallas_call(
        _cast_body,
        out_shape=jax.ShapeDtypeStruct((rows, cols), BF16),
        grid=(rows // tr,),
        in_specs=[pl.BlockSpec((None, tr, cols), lambda i: (layer, i, 0))],
        out_specs=pl.BlockSpec((tr, cols), lambda i: (i, 0)),
        compiler_params=_cparams(("parallel",)), name="cast_bf16")(w)


MM_SUB_ROWS = 256


def _mm_body(*refs, a_of_dot, n_a, extra_kinds, epilogue, sub_rows):
    n_dot = len(a_of_dot)
    n_extra = len(extra_kinds)
    a_refs = refs[:n_a]
    w_refs = refs[n_a:n_a + n_dot]
    extra = refs[n_a + n_dot:n_a + n_dot + n_extra]
    out_refs = refs[n_a + n_dot + n_extra:]
    rows = out_refs[0].shape[0]
    sub = min(rows, sub_rows)
    for r in range(rows // sub):
        rs = slice(r * sub, (r + 1) * sub)
        accs = [_dot(a_refs[ai][rs, :], w[...]) for ai, w in zip(a_of_dot, w_refs)]
        outs = epilogue(accs, [e[...] if kind == "row" else e[rs, :]
                               for e, kind in zip(extra, extra_kinds)])
        for o_ref, o in zip(out_refs, outs):
            o_ref[rs, :] = o.astype(o_ref.dtype)


def _fused_mm(dots, extras, epilogue, out_dtypes, n_cols, tm, tn, name,
              sub_rows=MM_SUB_ROWS):
    t_rows = dots[0][0].shape[0]
    grid = (t_rows // tm, n_cols // tn)
    in_specs, args = [], []
    a_of_dot = []
    for a, _, _, _, _ in dots:
        known = [k for k, seen in enumerate(args) if seen is a]
        if known:
            a_of_dot.append(known[0])
            continue
        a_of_dot.append(len(args))
        in_specs.append(pl.BlockSpec((tm, a.shape[1]), lambda i, j: (i, 0)))
        args.append(a)
    n_a = len(args)
    for _, w, rb, ri, co in dots:
        in_specs.append(pl.BlockSpec((rb, tn), functools.partial(
            lambda i, j, ri, co: (ri, j + co), ri=ri, co=co)))
        args.append(w)
    for arr, kind in extras:
        if kind == "tile":
            in_specs.append(pl.BlockSpec((tm, tn), lambda i, j: (i, j)))
        elif kind == "stat":
            in_specs.append(pl.BlockSpec((tm, LANES), lambda i, j: (i, 0)))
        else:
            in_specs.append(pl.BlockSpec((1, tn), lambda i, j: (0, j)))
        args.append(arr)
    out_shape = [jax.ShapeDtypeStruct((t_rows, n_cols), dt) for dt in out_dtypes]
    out_specs = [pl.BlockSpec((tm, tn), lambda i, j: (i, j)) for _ in out_dtypes]
    body = functools.partial(_mm_body, a_of_dot=tuple(a_of_dot), n_a=n_a,
                             extra_kinds=tuple(kind for _, kind in extras),
                             epilogue=epilogue, sub_rows=sub_rows)
    return pl.pallas_call(
        body, out_shape=out_shape, grid=grid, in_specs=in_specs, out_specs=out_specs,
        compiler_params=_cparams(("parallel", "arbitrary")), name=name)(*args)


def _ln_rows(x):
    mu = jnp.mean(x, axis=-1, keepdims=True)
    xc = x - mu
    var = jnp.mean(xc * xc, axis=-1, keepdims=True)
    return xc, mu, lax.rsqrt(var + LN_EPS)


def _ln_stats_body(x_ref, w_ref, b_ref, hb_ref, mu_ref, rs_ref):
    xc, mu, rstd = _ln_rows(x_ref[...])
    hb_ref[...] = (xc * rstd * w_ref[...] + b_ref[...]).astype(BF16)
    mu_ref[...] = jnp.broadcast_to(mu, mu_ref.shape)
    rs_ref[...] = jnp.broadcast_to(rstd, rs_ref.shape)


def _ln_final_body(x_ref, w_ref, b_ref, h_ref):
    xc, _, rstd = _ln_rows(x_ref[...])
    h_ref[...] = xc * rstd * w_ref[...] + b_ref[...]


def _layer_norm(x, w, b, tm, final):
    t_rows, d = x.shape
    rows = pl.BlockSpec((tm, d), lambda i: (i, 0))
    stat = pl.BlockSpec((tm, LANES), lambda i: (i, 0))
    vec = pl.BlockSpec((1, d), lambda i: (0, 0))
    if final:
        body, out_shape, out_specs = (
            _ln_final_body, jax.ShapeDtypeStruct((t_rows, d), F32), rows)
    else:
        body = _ln_stats_body
        out_shape = [jax.ShapeDtypeStruct((t_rows, d), BF16),
                     jax.ShapeDtypeStruct((t_rows, LANES), F32),
                     jax.ShapeDtypeStruct((t_rows, LANES), F32)]
        out_specs = [rows, stat, stat]
    return pl.pallas_call(
        body, out_shape=out_shape, grid=(t_rows // tm,),
        in_specs=[rows, vec, vec], out_specs=out_specs,
        compiler_params=_cparams(("parallel",)), name="layer_norm")(
            x, w.reshape(1, d), b.reshape(1, d))


class _Residual:
    def __init__(self, alpha, h=None, x=None, mu=None, rstd=None, w=None, b=None):
        self.alpha = alpha
        if h is not None:
            self.extras = [(h, "tile")]
        else:
            self.extras = [(x, "tile"), (mu, "stat"), (rstd, "stat"),
                           (w.astype(F32).reshape(1, -1), "row"),
                           (b.astype(F32).reshape(1, -1), "row")]

    def __call__(self, ex):
        if len(ex) == 1:
            return self.alpha * ex[0]
        x, mu, rstd, w, b = ex
        rep = x.shape[1] // LANES
        wide = lambda s: jnp.concatenate([s] * rep, axis=1)
        return self.alpha * ((x - wide(mu)) * wide(rstd) * w + b)


def _s5_prepare(lam_re, lam_im, log_step, b_re, b_im, c_re, c_im, d):
    hp = lax.Precision.HIGHEST
    n_tiles = S5_WIDTH // LANES
    lr = jnp.minimum(lam_re.astype(F32), -1e-4)
    li = lam_im.astype(F32)
    dt = jnp.exp(log_step.astype(F32))[:, None]
    mag = jnp.exp(lr * dt)
    ab_re = mag * jnp.cos(li * dt)
    ab_im = mag * jnp.sin(li * dt)
    den = lr * lr + li * li
    nr = ab_re - 1.0
    g_re = (nr * lr + ab_im * li) / den
    g_im = (ab_im * lr - nr * li) / den
    br = b_re.astype(F32)
    bi = b_im.astype(F32)
    bb_re = g_re[..., None] * br - g_im[..., None] * bi
    bb_im = g_re[..., None] * bi + g_im[..., None] * br
    pr, pi = [jnp.ones_like(ab_re)], [jnp.zeros_like(ab_re)]
    for _ in range(S5_CHUNK):
        pr_new = pr[-1] * ab_re - pi[-1] * ab_im
        pi_new = pr[-1] * ab_im + pi[-1] * ab_re
        pr.append(pr_new)
        pi.append(pi_new)
    p_re = jnp.stack(pr)
    p_im = jnp.stack(pi)
    pb_re = (p_re[:S5_CHUNK, :, :, None] * bb_re[None]
             - p_im[:S5_CHUNK, :, :, None] * bb_im[None])
    pb_im = (p_re[:S5_CHUNK, :, :, None] * bb_im[None]
             + p_im[:S5_CHUNK, :, :, None] * bb_re[None])
    cr = c_re.astype(F32)
    ci = c_im.astype(F32)
    kern = (jnp.einsum("gon,jgni->jgoi", cr, pb_re, precision=hp)
            - jnp.einsum("gon,jgni->jgoi", ci, pb_im, precision=hp))
    tg, gc, ns = S5_TILE_GROUPS, S5_GROUP_CH, S5_STATE

    def spread(x2d, rep, row_group, col_group):
        rows, width = x2d.shape
        sel = jnp.tile(jnp.eye(width, dtype=BF16), (1, rep))
        out = jnp.dot(x2d.astype(BF16), sel)
        rg = row_group(jnp.arange(rows))[:, None]
        cg = col_group(jnp.arange(width * rep))[None, :]
        return jnp.where(rg == cg, out, jnp.zeros_like(out))

    k5 = kern.reshape(S5_CHUNK, n_tiles, tg, gc, gc).transpose(1, 0, 2, 4, 3)
    kcat = spread(k5.reshape(-1, gc), tg, lambda r: (r // gc) % tg,
                  lambda c: c // gc).reshape(n_tiles, S5_CHUNK * LANES, LANES)

    def w1_half(pb):
        pb5 = pb[::-1].reshape(S5_CHUNK, n_tiles, tg, ns, gc).transpose(1, 0, 2, 4, 3)
        return spread(pb5.reshape(-1, ns), tg, lambda r: (r // gc) % tg,
                      lambda c: c // ns).reshape(n_tiles, S5_CHUNK * LANES, S5_TILE_STATE)
    w1 = jnp.concatenate([w1_half(pb_re), w1_half(pb_im)], axis=-1)

    ca_re = cr[None] * p_re[1:, :, None, :] - ci[None] * p_im[1:, :, None, :]
    ca_im = cr[None] * p_im[1:, :, None, :] + ci[None] * p_re[1:, :, None, :]

    def p_half(ca):
        ca5 = ca.reshape(S5_CHUNK, n_tiles, tg, gc, ns).transpose(1, 2, 4, 0, 3)
        x2d = ca5.reshape(-1, S5_CHUNK * gc)
        rows, width = x2d.shape
        cols = jnp.arange(S5_CHUNK * LANES)
        src = jnp.arange(width)
        sel = ((src[:, None] // gc == cols[None, :] // LANES)
               & (src[:, None] % gc == cols[None, :] % gc)).astype(BF16)
        out = jnp.dot(x2d.astype(BF16), sel)
        rg = ((jnp.arange(rows) // ns) % tg)[:, None]
        cg = ((cols // gc) % tg)[None, :]
        return jnp.where(rg == cg, out, jnp.zeros_like(out)).reshape(
            n_tiles, S5_TILE_STATE, S5_CHUNK * LANES)
    pcat = jnp.concatenate([p_half(ca_re), -p_half(ca_im)], axis=1)

    a16 = jnp.concatenate([p_re[S5_CHUNK].reshape(n_tiles, 1, S5_TILE_STATE),
                           p_im[S5_CHUNK].reshape(n_tiles, 1, S5_TILE_STATE)], axis=-1)
    dd = d.astype(F32).reshape(n_tiles, 1, LANES)
    return kcat.astype(BF16), w1.astype(BF16), pcat.astype(BF16), a16, dd


def _s5_body(x_ref, kcat_ref, w1_ref, pcat_ref, a16_ref, d_ref, z_ref,
             state_ref, xcat_ref, xr_ref, q_ref, sin_ref, yc_ref, yint_ref, *, nb, tl):
    t = pl.program_id(1)
    nk = tl // S5_CHUNK
    n = nb * tl
    ts = S5_TILE_STATE

    @pl.when(t == 0)
    def _():
        state_ref[...] = jnp.zeros_like(state_ref)

    for b in range(nb):
        for tp in range(S5_CHUNK):
            xr_ref[tp, pl.ds(b, nk, stride=nb), :] = x_ref[b, pl.ds(tp, nk, stride=S5_CHUNK), :]
    xr = jnp.concatenate([xr_ref[tp].astype(BF16) for tp in range(S5_CHUNK)], axis=1)
    q_ref[...] = _dot(xr, w1_ref[...])

    a_re = a16_ref[:, :ts]
    a_im = a16_ref[:, ts:]
    s = state_ref[...]
    for k in range(nk):
        sin_ref[k * nb:(k + 1) * nb, :] = s
        s_re = s[:, :ts]
        s_im = s[:, ts:]
        s = jnp.concatenate([a_re * s_re - a_im * s_im, a_re * s_im + a_im * s_re],
                            axis=1) + q_ref[k * nb:(k + 1) * nb, :]
    state_ref[...] = s

    yc = _dot(sin_ref[...].astype(BF16), pcat_ref[...])
    for tp in range(S5_CHUNK):
        yc_ref[tp] = yc[:, tp * LANES:(tp + 1) * LANES]
    for b in range(nb):
        for tp in range(S5_CHUNK):
            yint_ref[pl.ds(b * tl + tp, nk, stride=S5_CHUNK), :] = (
                yc_ref[tp, pl.ds(b, nk, stride=nb), :])

    x = x_ref[...].reshape(n, LANES)
    pos = lax.broadcasted_iota(jnp.int32, (n, LANES), 0) % S5_CHUNK
    for j in range(S5_CHUNK):
        xs = x if j == 0 else jnp.where(pos >= j, pltpu.roll(x, j, axis=0), 0.0)
        xcat_ref[:, j * LANES:(j + 1) * LANES] = xs.astype(BF16)
    y = _dot(xcat_ref[...], kcat_ref[...]) + yint_ref[...] + d_ref[...] * x
    z_ref[...] = _gelu_tanh(y).reshape(nb, tl, LANES)


def _s5_in_proj_body(a_ref, w_ref, o_ref):
    rows = a_ref.shape[0]
    sub = min(rows, MM_SUB_ROWS)
    for r in range(rows // sub):
        rs = slice(r * sub, (r + 1) * sub)
        acc = _dot(a_ref[rs, :], w_ref[...])
        for c in range(o_ref.shape[0]):
            o_ref[c, rs, :] = acc[:, c * LANES:(c + 1) * LANES]


def _s5_in_proj(h16, w_in16, tm):
    t_rows, d = h16.shape
    n_tiles = S5_WIDTH // LANES
    return pl.pallas_call(
        _s5_in_proj_body,
        out_shape=jax.ShapeDtypeStruct((n_tiles, t_rows, LANES), F32),
        grid=(t_rows // tm,),
        in_specs=[pl.BlockSpec((tm, d), lambda i: (i, 0)),
                  pl.BlockSpec((d, S5_WIDTH), lambda i: (0, 0))],
        out_specs=pl.BlockSpec((n_tiles, tm, LANES), lambda i: (0, i, 0)),
        compiler_params=_cparams(("parallel",)), name="in_proj_s5")(h16, w_in16)


def _s5_mixer(xa_tiles, prep, tl):
    kcat, w1, pcat, a16, dd = prep
    n_tiles, nb, seq, _ = xa_tiles.shape
    nk = tl // S5_CHUNK
    ts = S5_TILE_STATE
    body = functools.partial(_s5_body, nb=nb, tl=tl)
    wspec = lambda shp: pl.BlockSpec((None,) + shp, lambda j, t: (j, 0, 0))
    xspec = pl.BlockSpec((None, nb, tl, LANES), lambda j, t: (j, 0, t, 0))
    return pl.pallas_call(
        body,
        out_shape=jax.ShapeDtypeStruct(xa_tiles.shape, F32),
        grid=(n_tiles, seq // tl),
        in_specs=[xspec,
                  wspec((S5_CHUNK * LANES, LANES)),
                  wspec((S5_CHUNK * LANES, 2 * ts)),
                  wspec((2 * ts, S5_CHUNK * LANES)),
                  wspec((1, 2 * ts)),
                  wspec((1, LANES))],
        out_specs=xspec,
        scratch_shapes=[pltpu.VMEM((nb, 2 * ts), F32),
                        pltpu.VMEM((nb * tl, S5_CHUNK * LANES), BF16),
                        pltpu.VMEM((S5_CHUNK, nb * nk, LANES), F32),
                        pltpu.VMEM((nb * nk, 2 * ts), F32),
                        pltpu.VMEM((nb * nk, 2 * ts), F32),
                        pltpu.VMEM((S5_CHUNK, nb * nk, LANES), F32),
                        pltpu.VMEM((nb * tl, LANES), F32)],
        compiler_params=_cparams(("parallel", "arbitrary")), name="s5_mixer")(
            xa_tiles, kcat, w1, pcat, a16, dd)


def _glu_body(z_ref, w_ref, nw_ref, o_ref):
    z = jnp.concatenate([z_ref[c] for c in range(z_ref.shape[0])], axis=1)
    y = z * _sigmoid(_dot(z.astype(BF16), w_ref[...]))
    ms = jnp.mean(y * y, axis=-1, keepdims=True)
    o_ref[...] = (y * lax.rsqrt(ms + RMS_EPS) * nw_ref[...]).astype(BF16)


def _s5_glu_norm(z_tiles, w_glu, norm_w, tm):
    n_tiles, t_rows, _ = z_tiles.shape
    d = n_tiles * LANES
    return pl.pallas_call(
        _glu_body,
        out_shape=jax.ShapeDtypeStruct((t_rows, d), BF16),
        grid=(t_rows // tm,),
        in_specs=[pl.BlockSpec((n_tiles, tm, LANES), lambda i: (0, i, 0)),
                  pl.BlockSpec((d, d), lambda i: (0, 0)),
                  pl.BlockSpec((1, d), lambda i: (0, 0))],
        out_specs=pl.BlockSpec((tm, d), lambda i: (i, 0)),
        compiler_params=_cparams(("parallel",)), name="s5_glu_norm")(
            z_tiles, w_glu, norm_w.reshape(1, d))


def _sgu_body(u_ref, v_ref, lnw_ref, lnb_ref, w_ref, bs_ref, nw_ref, o_ref,
              vb_ref, x_ref, *, nc):
    v = _gelu_tanh(v_ref[...])
    mu = jnp.mean(v, axis=-1, keepdims=True)
    vc = v - mu
    var = jnp.mean(vc * vc, axis=-1, keepdims=True)
    vb_ref[...] = (vc * lax.rsqrt(var + LN_EPS) * lnw_ref[...] + lnb_ref[...]).astype(BF16)
    hd = SGU_WIDTH // SGU_HEADS
    for h in range(SGU_HEADS):
        cs = slice(h * hd, (h + 1) * hd)
        rhs = jnp.concatenate(
            [vb_ref[c * SGU_CHUNK:(c + 1) * SGU_CHUNK, cs] for c in range(nc)], axis=1)
        zz = _dot(w_ref[h], rhs)
        for c in range(nc):
            rs = slice(c * SGU_CHUNK, (c + 1) * SGU_CHUNK)
            z = zz[:, c * hd:(c + 1) * hd] + bs_ref[:, cs]
            x_ref[rs, cs] = _gelu_tanh(u_ref[rs, cs]) * z
    x = x_ref[...]
    ms = jnp.mean(x * x, axis=-1, keepdims=True)
    o_ref[...] = (x * lax.rsqrt(ms + RMS_EPS) * nw_ref[...]).astype(BF16)


def _sgu_mixer(proj3, ln_w, ln_b, w_s, b_s, norm_w, tl):
    nb, seq, _ = proj3.shape
    nc = tl // SGU_CHUNK
    hd = SGU_WIDTH // SGU_HEADS
    causal = jnp.tril(jnp.ones((SGU_CHUNK, SGU_CHUNK), dtype=bool))
    w_causal = jnp.where(causal[None], w_s, jnp.zeros_like(w_s)).astype(BF16)
    bias = jnp.repeat(b_s.astype(F32).T, hd, axis=1)
    u_blk = 0
    row = lambda a: a.astype(F32).reshape(1, SGU_WIDTH)
    const = lambda shp: pl.BlockSpec(shp, lambda b, t: (0,) * len(shp))
    body = functools.partial(_sgu_body, nc=nc)
    return pl.pallas_call(
        body,
        out_shape=jax.ShapeDtypeStruct((nb, seq, SGU_WIDTH), BF16),
        grid=(nb, seq // tl),
        in_specs=[pl.BlockSpec((None, tl, SGU_WIDTH), lambda b, t: (b, t, u_blk)),
                  pl.BlockSpec((None, tl, SGU_WIDTH), lambda b, t: (b, t, u_blk + 1)),
                  const((1, SGU_WIDTH)), const((1, SGU_WIDTH)),
                  const((SGU_HEADS, SGU_CHUNK, SGU_CHUNK)),
                  const((SGU_CHUNK, SGU_WIDTH)), const((1, SGU_WIDTH))],
        out_specs=pl.BlockSpec((None, tl, SGU_WIDTH), lambda b, t: (b, t, 0)),
        scratch_shapes=[pltpu.VMEM((tl, SGU_WIDTH), BF16),
                        pltpu.VMEM((tl, SGU_WIDTH), F32)],
        compiler_params=_cparams(("parallel", "parallel")), name="sgu_mixer")(
            proj3, proj3, row(ln_w), row(ln_b), w_causal, bias, row(norm_w))


HG_PAIR = 2
HG_PAIR_W = HG_PAIR * HG_HEAD_DIM
HG_FAST_BLOCK = 32
HG_FAST_MIN_LOG_DECAY = -60.0
_NT = (((1,), (1,)), ((), ()))
_TN = (((0,), (0,)), ((), ()))


def _hgrn_tables():
    c = HG_CHUNK
    t = jnp.arange(c)[:, None]
    r = jnp.arange(c)[None, :]
    mats, masks = [], []
    for lev in range(HG_LEVELS):
        m = 1 << lev
        mid = (t // (2 * m)) * (2 * m) + m
        later = t >= mid
        mats.append(jnp.where(later, (r >= mid) & (r <= t), (r > t) & (r < mid)))
        same = (t // (2 * m)) == (r // (2 * m))
        masks.append(same & later & (r < mid))
    tri = r <= t
    mats.append(tri)
    masks.append(t == r)
    fb = HG_FAST_BLOCK
    fast = [masks[6], masks[5], ((t // fb) == (r // fb)) & tri]
    pair = lambda m: jnp.tile(m.astype(F32), (1, HG_PAIR))
    hh = jnp.arange(HG_PAIR_W) // HG_HEAD_DIM
    return dict(
        tri=tri.astype(BF16),
        mst=jnp.concatenate(mats, axis=0).astype(BF16),
        masks=jnp.stack(masks).astype(F32),
        fmasks=jnp.stack([pair(m) for m in fast]),
        bd=(hh[:, None] == hh[None, :]).astype(F32))


def _block_diag(x):
    z = jnp.zeros((x.shape[0], HG_HEAD_DIM), x.dtype)
    top = jnp.concatenate([x[:, :HG_HEAD_DIM], z], axis=1)
    bot = jnp.concatenate([z, x[:, HG_HEAD_DIM:]], axis=1)
    return jnp.concatenate([top, bot], axis=0)


def _rows(v, n):
    return jnp.broadcast_to(v, (n, v.shape[1]))


def _hgrn_finish(o, gv, nw, o_ref, rs):
    for h in range(HG_PAIR):
        cs = slice(h * HG_HEAD_DIM, (h + 1) * HG_HEAD_DIM)
        oh = o[:, cs]
        ms = jnp.mean(oh * oh, axis=-1, keepdims=True)
        gg = gv[:, cs]
        o_ref[rs, cs] = (oh * lax.rsqrt(ms + RMS_EPS) * nw * (gg * _sigmoid(gg))).astype(BF16)


def _hgrn_body(q_ref, f_ref, i_ref, g_ref, par_ref, nw_ref, tri_ref, mst_ref, mask_ref,
               fmask_ref, bd_ref, o_ref, st_ref, qs_ref, kk_ref, lf_ref, b_ref, *, nc):
    c = HG_CHUNK
    hd = HG_HEAD_DIM
    wb = HG_PAIR_W
    t = pl.program_id(2)

    @pl.when(t == 0)
    def _():
        st_ref[...] = jnp.zeros_like(st_ref)

    log_lb = par_ref[0:1, :]
    log_1m_lb = par_ref[1:2, :]
    one_m_lb = par_ref[2:3, :]
    nw = nw_ref[...]
    tri = tri_ref[...]
    row = lax.broadcasted_iota(jnp.int32, (c, wb), 0)

    wmin = None
    for ch in range(nc):
        rs = slice(ch * c, (ch + 1) * c)
        fr = f_ref[rs, :]
        e = jnp.exp(-jnp.abs(fr))
        r = 1.0 / (1.0 + e)
        sig_neg = jnp.where(fr >= 0, e * r, r)
        log_sig = jnp.minimum(fr, 0.0) - jnp.log(1.0 + e)
        y = log_1m_lb + log_sig
        log_f = jnp.maximum(log_lb, y) + jnp.log(1.0 + jnp.exp(-jnp.abs(log_lb - y)))
        hi = log_f.astype(BF16)
        lo = (log_f - hi.astype(F32)).astype(BF16)
        b = _dot(tri, hi) + _dot(tri, lo)
        qv = q_ref[rs, :]
        qs_ref[rs, :] = qv * _sigmoid(qv)
        kk_ref[rs, :] = one_m_lb * sig_neg
        lf_ref[rs, :] = log_f
        b_ref[rs, :] = b
        fb = HG_FAST_BLOCK
        starts = jnp.concatenate(
            [jnp.zeros((fb, wb), F32)]
            + [_rows(b[j * fb - 1:j * fb, :], fb) for j in range(1, c // fb)], axis=0)
        w = b - starts
        wmin = w if wmin is None else jnp.minimum(wmin, w)
    fast = jnp.min(wmin) >= HG_FAST_MIN_LOG_DECAY

    @pl.when(fast)
    def _():
        bd = bd_ref[...]
        for ch in range(nc):
            rs = slice(ch * c, (ch + 1) * c)
            b = b_ref[rs, :]
            qs = qs_ref[rs, :]
            kk = kk_ref[rs, :]
            vv = i_ref[rs, :].astype(BF16)
            b63 = _rows(b[63:64, :], c)
            e6 = jnp.where(row >= 64, b - b63, b63 - b)
            bm = jnp.concatenate([_rows(b[31:32, :], 64), _rows(b[95:96, :], 64)], axis=0)
            e5 = jnp.where((row % 64) >= 32, b - bm, bm - b)
            fb = HG_FAST_BLOCK
            starts = jnp.concatenate(
                [jnp.zeros((fb, wb), F32)]
                + [_rows(b[j * fb - 1:j * fb, :], fb) for j in range(1, c // fb)], axis=0)
            w = b - starts
            a6 = jnp.exp(e6)
            a5 = jnp.exp(e5)
            factors = [(a6, a6), (a5, a5), (jnp.exp(w), jnp.exp(-w))]
            scores = None
            for lev, (aq, ak) in enumerate(factors):
                sc = lax.dot_general((qs * aq).astype(BF16), _block_diag((kk * ak).astype(BF16)),
                                     _NT, preferred_element_type=F32)
                sc = fmask_ref[lev] * sc
                scores = sc if scores is None else scores + sc
            st = st_ref[...]
            o = (_dot(scores.astype(BF16), _block_diag(vv))
                 + lax.dot_general((qs * jnp.exp(b)).astype(BF16), st.astype(BF16), _NT,
                                   preferred_element_type=F32))
            bend = b[c - 1:c, :]
            kd = (kk * jnp.exp(bend - b)).astype(BF16)
            st_ref[...] = bd * (st * jnp.exp(bend)
                                + lax.dot_general(vv, kd, _TN, preferred_element_type=F32))
            _hgrn_finish(o, g_ref[rs, :], nw, o_ref, rs)

    @pl.when(jnp.logical_not(fast))
    def _():
        mst = mst_ref[...]

        def chunk(ch, carry):
            rs = pl.ds(pl.multiple_of(ch * c, c), c)
            log_f = lf_ref[rs, :]
            hi = log_f.astype(BF16)
            lo = (log_f - hi.astype(F32)).astype(BF16)
            ee = _dot(mst, hi) + _dot(mst, lo)
            qs = qs_ref[rs, :]
            kk = kk_ref[rs, :]
            vv = i_ref[rs, :].astype(BF16)
            bcum = ee[HG_LEVELS * c:(HG_LEVELS + 1) * c, :]
            outs = []
            for h in range(HG_PAIR):
                cs = slice(h * hd, (h + 1) * hd)
                qh = qs[:, cs]
                kh = kk[:, cs]
                vh = vv[:, cs]
                scores = mask_ref[HG_LEVELS] * jnp.sum(qh * kh, axis=-1, keepdims=True)
                for lev in range(HG_LEVELS):
                    a = jnp.exp(ee[lev * c:(lev + 1) * c, cs])
                    sc = lax.dot_general((qh * a).astype(BF16), (kh * a).astype(BF16), _NT,
                                         preferred_element_type=F32)
                    scores = scores + mask_ref[lev] * sc
                bh = bcum[:, cs]
                st = st_ref[cs, cs]
                outs.append(_dot(scores.astype(BF16), vh)
                            + lax.dot_general((qh * jnp.exp(bh)).astype(BF16), st.astype(BF16),
                                              _NT, preferred_element_type=F32))
                bend = bh[c - 1:c, :]
                kd = (kh * jnp.exp(bend - bh)).astype(BF16)
                st_ref[cs, cs] = st * jnp.exp(bend) + lax.dot_general(
                    vh, kd, _TN, preferred_element_type=F32)
            _hgrn_finish(jnp.concatenate(outs, axis=1), g_ref[rs, :], nw, o_ref, rs)
            return carry

        lax.fori_loop(0, nc, chunk, 0)


def _hgrn_mixer(proj3, lb, norm_w, tl):
    nb, seq, _ = proj3.shape
    nc = tl // HG_CHUNK
    wb = HG_PAIR_W
    q0 = (2 * SGU_WIDTH) // wb
    nblk = HG_WIDTH // wb
    lbf = lb.astype(F32)
    par = jnp.stack([jnp.log(lbf), jnp.log1p(-lbf), 1.0 - lbf])
    tb = _hgrn_tables()
    sec = lambda k: pl.BlockSpec((None, tl, wb), functools.partial(
        lambda b, h, t, k: (b, t, q0 + k * nblk + h), k=k))
    const = lambda a: pl.BlockSpec(a.shape, lambda b, h, t: (0,) * a.ndim)
    body = functools.partial(_hgrn_body, nc=nc)
    return pl.pallas_call(
        body,
        out_shape=jax.ShapeDtypeStruct((nb, seq, HG_WIDTH), BF16),
        grid=(nb, nblk, seq // tl),
        in_specs=[sec(0), sec(1), sec(2), sec(3),
                  pl.BlockSpec((3, wb), lambda b, h, t: (0, h)),
                  pl.BlockSpec((1, HG_HEAD_DIM), lambda b, h, t: (0, 0)),
                  const(tb["tri"]), const(tb["mst"]), const(tb["masks"]),
                  const(tb["fmasks"]), const(tb["bd"])],
        out_specs=pl.BlockSpec((None, tl, wb), lambda b, h, t: (b, t, h)),
        scratch_shapes=[pltpu.VMEM((wb, wb), F32)] + [pltpu.VMEM((tl, wb), F32)] * 4,
        compiler_params=_cparams(("parallel", "parallel", "arbitrary")),
        name="hgrn2_mixer")(
            proj3, proj3, proj3, proj3, par,
            norm_w.astype(F32).reshape(1, HG_HEAD_DIM),
            tb["tri"], tb["mst"], tb["masks"], tb["fmasks"], tb["bd"])


def kernel(x, p, w_in, s5_lam_re, s5_lam_im, s5_log_step, s5_b_re, s5_b_im, s5_c_re, s5_c_im, s5_d, s5_w_glu, sgu_ln_w, sgu_ln_b, sgu_w, sgu_b, hg_lb_logits, hg_norm_w, norm_a_w, norm_b_w, w_out, ln1_w, ln1_b, w_ffn_in, w_ffn_out, ln2_w, ln2_b, w_ple_in, w_ple_gate, ln3_w, ln3_b):
    nb, seq, d_model = x.shape
    depth = w_in.shape[0]
    t_rows = nb * seq
    proj_w = w_in.shape[2]
    d_ff = w_ffn_out.shape[1]
    alpha = (2.0 * depth) ** 0.25

    tm = _pick(t_rows, (1024, 512, 256, 128))
    tm_wide = _pick(t_rows, (2048, 1024, 512, 256, 128))
    tm_half = _pick(t_rows, (512, 256, 128))
    tm_ln = _pick(t_rows, (256, 128))
    tl_s5 = _pick(seq, (512, 256, 128))
    tl_sgu = _pick(seq, (512, 256, 128))
    tl_hg = _pick(seq, (1024, 512, 256, 128))
    tn_in = _pick(proj_w - S5_WIDTH, (1024, 512))

    lbs =jnp.cumsum(jax.nn.softmax(hg_lb_logits.astype(F32), axis=0), axis=0)
    lbs = lbs - lbs[0:1]

    x_rows = x.reshape(t_rows, d_model).astype(F32)
    res = _Residual(alpha, h=x_rows)
    h16 = _layer_to_bf16(x_rows.reshape(1, t_rows, d_model), 0)
    p_rows = p.reshape(depth, t_rows, -1)

    def post_norm(xres, w, b):
        h16, mu, rstd = _layer_norm(xres, w, b, tm_ln, final=False)
        return h16, _Residual(alpha, x=xres, mu=mu, rstd=rstd, w=w, b=b)

    for l in range(depth):
        w_in16 = _layer_to_bf16(w_in, l)
        xa_tiles = _s5_in_proj(h16, w_in16, tm).reshape(-1, nb, seq, LANES)
        (proj,) = _fused_mm([(h16, w_in16, d_model, 0, S5_WIDTH // tn_in)], [],
                            lambda accs, ex: [accs[0]], [F32], proj_w - S5_WIDTH, tm,
                            tn_in, "in_proj")
        proj3 = proj.reshape(nb, seq, proj_w - S5_WIDTH)
        prep = _s5_prepare(s5_lam_re[l], s5_lam_im[l], s5_log_step[l], s5_b_re[l],
                           s5_b_im[l], s5_c_re[l], s5_c_im[l], s5_d[l])
        z_tiles = _s5_mixer(xa_tiles, prep, tl_s5).reshape(-1, t_rows, LANES)
        ya = _s5_glu_norm(z_tiles, _layer_to_bf16(s5_w_glu, l), norm_a_w[l], tm_half)
        yb = _sgu_mixer(proj3, sgu_ln_w[l], sgu_ln_b[l], sgu_w[l], sgu_b[l],
                        norm_b_w[l], tl_sgu).reshape(t_rows, SGU_WIDTH)
        yc = _hgrn_mixer(proj3, lbs[l], hg_norm_w[l], tl_hg).reshape(t_rows, HG_WIDTH)
        wo = _layer_to_bf16(w_out, l)
        (xres,) = _fused_mm(
            [(ya, wo, S5_WIDTH, 0, 0), (yb, wo, SGU_WIDTH, 1, 0), (yc, wo, HG_WIDTH, 1, 0)],
            res.extras,
            functools.partial(lambda accs, ex, res: [res(ex) + (accs[0] + accs[1] + accs[2])],
                              res=res),
            [F32], d_model, tm, 1024, "out_proj")
        h16, res = post_norm(xres, ln1_w[l], ln1_b[l])

        wf = _layer_to_bf16(w_ffn_in, l)
        (hid,) = _fused_mm(
            [(h16, wf, d_model, 0, 0), (h16, wf, d_model, 0, d_ff // 256)], [],
            lambda accs, ex: [accs[0] * _sigmoid(accs[0]) * accs[1]],
            [BF16], d_ff, tm_wide, 256, "ffn_in")
        (xres,) = _fused_mm(
            [(hid, _layer_to_bf16(w_ffn_out, l), d_ff, 0, 0)], res.extras,
            functools.partial(lambda accs, ex, res: [res(ex) + accs[0]], res=res),
            [F32], d_model, tm_half, 512, "ffn_out")
        h16, res = post_norm(xres, ln2_w[l], ln2_b[l])

        p16 = _layer_to_bf16(p_rows, l)
        (xres,) = _fused_mm(
            [(h16, _layer_to_bf16(w_ple_gate, l), d_model, 0, 0),
             (p16, _layer_to_bf16(w_ple_in, l), p16.shape[1], 0, 0)], res.extras,
            functools.partial(lambda accs, ex, res: [res(ex) + accs[1] * _sigmoid(accs[0])],
                              res=res),
            [F32], d_model, tm, 1024, "ple", sub_rows=128)
        if l + 1 < depth:
            h16, res = post_norm(xres, ln3_w[l], ln3_b[l])

    out = _layer_norm(xres, ln3_w[depth - 1], ln3_b[depth - 1], tm_ln, final=True)
    return out.reshape(nb, seq, d_model).astype(x.dtype)
```

```python
import functools
import math

import jax
import jax.numpy as jnp
from jax import lax
from jax.experimental import pallas as pl
from jax.experimental.pallas import tpu as pltpu

F32 = jnp.float32
BF16 = jnp.bfloat16

V7X_VMEM_LIMIT_BYTES = 56 * 1024 * 1024
LANES = 128

LN_EPS = 1e-5
RMS_EPS = 1e-6

S5_WIDTH = 1024
S5_GROUP_CH = 16
S5_STATE = 64
S5_CHUNK = 8
S5_TILE_GROUPS = LANES // S5_GROUP_CH
S5_TILE_STATE = S5_TILE_GROUPS * S5_STATE

SGU_WIDTH = 1024
SGU_CHUNK = 128
SGU_HEADS = 8

HG_WIDTH = 2048
HG_HEAD_DIM = 128
HG_CHUNK = 128
HG_LEVELS = 7


def _cparams(sem):
    return pltpu.CompilerParams(dimension_semantics=sem,
                                vmem_limit_bytes=V7X_VMEM_LIMIT_BYTES)


def _gelu_tanh(x):
    c = math.sqrt(2.0 / math.pi)
    return x * (0.5 * (1.0 + jnp.tanh(c * (x + 0.044715 * (x * x * x)))))


def _sigmoid(x):
    return 1.0 / (1.0 + jnp.exp(-x))


def _dot(a, b):
    return jnp.dot(a, b, preferred_element_type=F32)


def _pick(n, prefs):
    for p in prefs:
        if n % p == 0:
            return p
    return n


CAST_BLOCK_BYTES = 8 * 1024 * 1024


def _cast_body(w_ref, o_ref):
    o_ref[...] = w_ref[...].astype(BF16)


def _layer_to_bf16(w, layer):
    _, rows, cols = w.shape
    fits = [tr for tr in (8192, 4096, 2048, 1024, 512, 256, 128, 64, 32, 16)
            if rows % tr == 0 and tr * cols * 4 <= CAST_BLOCK_BYTES]
    tr = fits[0]
    return pl.pallas_call(
        _cast_body,
        out_shape=jax.ShapeDtypeStruct((rows, cols), BF16),
        grid=(rows // tr,),
        in_specs=[pl.BlockSpec((None, tr, cols), lambda i: (layer, i, 0))],
        out_specs=pl.BlockSpec((tr, cols), lambda i: (i, 0)),
        compiler_params=_cparams(("parallel",)), name="cast_bf16")(w)


MM_SUB_ROWS = 256


def _mm_body(*refs, a_of_dot, n_a, extra_kinds, n_side, epilogue, sub_rows):
    n_dot = len(a_of_dot)
    n_extra = len(extra_kinds)
    n_in = n_a + n_dot + n_extra
    a_refs = refs[:n_a]
    w_refs = refs[n_a:n_a + n_dot]
    extra = refs[n_a + n_dot:n_in]
    side_in = refs[n_in:n_in + n_side]
    out_refs = refs[n_in + n_side:len(refs) - n_side]
    side_out = refs[len(refs) - n_side:]
    for si, so in zip(side_in, side_out):
        so[...] = si[...].astype(BF16)
    rows = out_refs[0].shape[0]
    sub = min(rows, sub_rows)
    for r in range(rows // sub):
        rs = slice(r * sub, (r + 1) * sub)
        accs = [_dot(a_refs[ai][rs, :], w[...]) for ai, w in zip(a_of_dot, w_refs)]
        outs = epilogue(accs, [e[...] if kind == "row" else e[rs, :]
                               for e, kind in zip(extra, extra_kinds)])
        for o_ref, o in zip(out_refs, outs):
            o_ref[rs, :] = o.astype(o_ref.dtype)


BF16_ROW_TILE = 16


def _fused_mm(dots, extras, epilogue, out_dtypes, n_cols, tm, tn, name,
              sub_rows=MM_SUB_ROWS, side_casts=()):
    t_rows = dots[0][0].shape[0]
    grid = (t_rows // tm, n_cols // tn)
    in_specs, args = [], []
    a_of_dot = []
    for a, _, _, _, _ in dots:
        known = [k for k, seen in enumerate(args) if seen is a]
        if known:
            a_of_dot.append(known[0])
            continue
        a_of_dot.append(len(args))
        in_specs.append(pl.BlockSpec((tm, a.shape[1]), lambda i, j: (i, 0)))
        args.append(a)
    n_a = len(args)
    for _, w, rb, ri, co in dots:
        in_specs.append(pl.BlockSpec((rb, tn), functools.partial(
            lambda i, j, ri, co: (ri, j + co), ri=ri, co=co)))
        args.append(w)
    for arr, kind in extras:
        if kind == "tile":
            in_specs.append(pl.BlockSpec((tm, tn), lambda i, j: (i, j)))
        elif kind == "stat":
            in_specs.append(pl.BlockSpec((tm, LANES), lambda i, j: (i, 0)))
        else:
            in_specs.append(pl.BlockSpec((1, tn), lambda i, j: (0, j)))
        args.append(arr)
    out_shape = [jax.ShapeDtypeStruct((t_rows, n_cols), dt) for dt in out_dtypes]
    out_specs = [pl.BlockSpec((tm, tn), lambda i, j: (i, j)) for _ in out_dtypes]
    n_steps = grid[0] * grid[1]
    for stack, layer in side_casts:
        _, rows, cols = stack.shape
        tiles = rows // BF16_ROW_TILE
        n_slabs = max(d for d in range(1, min(tiles, n_steps) + 1) if tiles % d == 0)
        slab = functools.partial(lambda i, j, last: jnp.minimum(i * grid[1] + j, last),
                                 last=n_slabs - 1)
        in_specs.append(pl.BlockSpec(
            (None, rows // n_slabs, cols),
            functools.partial(lambda i, j, layer, slab: (layer, slab(i, j), 0),
                              layer=layer, slab=slab)))
        args.append(stack)
        out_shape.append(jax.ShapeDtypeStruct((rows, cols), BF16))
        out_specs.append(pl.BlockSpec(
            (rows // n_slabs, cols),
            functools.partial(lambda i, j, slab: (slab(i, j), 0), slab=slab)))
    body = functools.partial(_mm_body, a_of_dot=tuple(a_of_dot), n_a=n_a,
                             extra_kinds=tuple(kind for _, kind in extras),
                             n_side=len(side_casts), epilogue=epilogue, sub_rows=sub_rows)
    return pl.pallas_call(
        body, out_shape=out_shape, grid=grid, in_specs=in_specs, out_specs=out_specs,
        compiler_params=_cparams(("parallel", "arbitrary")), name=name)(*args)


def _ln_rows(x):
    mu = jnp.mean(x, axis=-1, keepdims=True)
    xc = x - mu
    var = jnp.mean(xc * xc, axis=-1, keepdims=True)
    return xc, mu, lax.rsqrt(var + LN_EPS)


def _ln_stats_body(x_ref, w_ref, b_ref, hb_ref, mu_ref, rs_ref):
    xc, mu, rstd = _ln_rows(x_ref[...])
    hb_ref[...] = (xc * rstd * w_ref[...] + b_ref[...]).astype(BF16)
    mu_ref[...] = jnp.broadcast_to(mu, mu_ref.shape)
    rs_ref[...] = jnp.broadcast_to(rstd, rs_ref.shape)


def _ln_final_body(x_ref, w_ref, b_ref, h_ref):
    xc, _, rstd = _ln_rows(x_ref[...])
    h_ref[...] = xc * rstd * w_ref[...] + b_ref[...]


def _layer_norm(x, w, b, tm, final):
    t_rows, d = x.shape
    rows = pl.BlockSpec((tm, d), lambda i: (i, 0))
    stat = pl.BlockSpec((tm, LANES), lambda i: (i, 0))
    vec = pl.BlockSpec((1, d), lambda i: (0, 0))
    if final:
        body, out_shape, out_specs = (
            _ln_final_body, jax.ShapeDtypeStruct((t_rows, d), F32), rows)
    else:
        body = _ln_stats_body
        out_shape = [jax.ShapeDtypeStruct((t_rows, d), BF16),
                     jax.ShapeDtypeStruct((t_rows, LANES), F32),
                     jax.ShapeDtypeStruct((t_rows, LANES), F32)]
        out_specs = [rows, stat, stat]
    return pl.pallas_call(
        body, out_shape=out_shape, grid=(t_rows // tm,),
        in_specs=[rows, vec, vec], out_specs=out_specs,
        compiler_params=_cparams(("parallel",)), name="layer_norm")(
            x, w.reshape(1, d), b.reshape(1, d))


class _Residual:
    def __init__(self, alpha, h=None, x=None, mu=None, rstd=None, w=None, b=None):
        self.alpha = alpha
        if h is not None:
            self.extras = [(h, "tile")]
        else:
            self.extras = [(x, "tile"), (mu, "stat"), (rstd, "stat"),
                           (w.astype(F32).reshape(1, -1), "row"),
                           (b.astype(F32).reshape(1, -1), "row")]

    def __call__(self, ex):
        if len(ex) == 1:
            return self.alpha * ex[0]
        x, mu, rstd, w, b = ex
        rep = x.shape[1] // LANES
        wide = lambda s: jnp.concatenate([s] * rep, axis=1)
        return self.alpha * ((x - wide(mu)) * wide(rstd) * w + b)


def _s5_prepare(lam_re, lam_im, log_step, b_re, b_im, c_re, c_im, d):
    hp = lax.Precision.HIGHEST
    n_tiles = S5_WIDTH // LANES
    lr = jnp.minimum(lam_re.astype(F32), -1e-4)
    li = lam_im.astype(F32)
    dt = jnp.exp(log_step.astype(F32))[:, None]
    mag = jnp.exp(lr * dt)
    ab_re = mag * jnp.cos(li * dt)
    ab_im = mag * jnp.sin(li * dt)
    den = lr * lr + li * li
    nr = ab_re - 1.0
    g_re = (nr * lr + ab_im * li) / den
    g_im = (ab_im * lr - nr * li) / den
    br = b_re.astype(F32)
    bi = b_im.astype(F32)
    bb_re = g_re[..., None] * br - g_im[..., None] * bi
    bb_im = g_re[..., None] * bi + g_im[..., None] * br
    pr, pi = [jnp.ones_like(ab_re)], [jnp.zeros_like(ab_re)]
    for _ in range(S5_CHUNK):
        pr_new = pr[-1] * ab_re - pi[-1] * ab_im
        pi_new = pr[-1] * ab_im + pi[-1] * ab_re
        pr.append(pr_new)
        pi.append(pi_new)
    p_re = jnp.stack(pr)
    p_im = jnp.stack(pi)
    pb_re = (p_re[:S5_CHUNK, :, :, None] * bb_re[None]
             - p_im[:S5_CHUNK, :, :, None] * bb_im[None])
    pb_im = (p_re[:S5_CHUNK, :, :, None] * bb_im[None]
             + p_im[:S5_CHUNK, :, :, None] * bb_re[None])
    cr = c_re.astype(F32)
    ci = c_im.astype(F32)
    kern = (jnp.einsum("gon,jgni->jgoi", cr, pb_re, precision=hp)
            - jnp.einsum("gon,jgni->jgoi", ci, pb_im, precision=hp))
    tg, gc, ns = S5_TILE_GROUPS, S5_GROUP_CH, S5_STATE

    def spread(x2d, rep, row_group, col_group):
        rows, width = x2d.shape
        sel = jnp.tile(jnp.eye(width, dtype=BF16), (1, rep))
        out = jnp.dot(x2d.astype(BF16), sel)
        rg = row_group(jnp.arange(rows))[:, None]
        cg = col_group(jnp.arange(width * rep))[None, :]
        return jnp.where(rg == cg, out, jnp.zeros_like(out))

    k5 = kern.reshape(S5_CHUNK, n_tiles, tg, gc, gc).transpose(1, 0, 2, 4, 3)
    kcat = spread(k5.reshape(-1, gc), tg, lambda r: (r // gc) % tg,
                  lambda c: c // gc).reshape(n_tiles, S5_CHUNK * LANES, LANES)

    def w1_half(pb):
        pb5 = pb[::-1].reshape(S5_CHUNK, n_tiles, tg, ns, gc).transpose(1, 0, 2, 4, 3)
        return spread(pb5.reshape(-1, ns), tg, lambda r: (r // gc) % tg,
                      lambda c: c // ns).reshape(n_tiles, S5_CHUNK * LANES, S5_TILE_STATE)
    w1 = jnp.concatenate([w1_half(pb_re), w1_half(pb_im)], axis=-1)

    ca_re = cr[None] * p_re[1:, :, None, :] - ci[None] * p_im[1:, :, None, :]
    ca_im = cr[None] * p_im[1:, :, None, :] + ci[None] * p_re[1:, :, None, :]

    def p_half(ca):
        ca5 = ca.reshape(S5_CHUNK, n_tiles, tg, gc, ns).transpose(1, 2, 4, 0, 3)
        x2d = ca5.reshape(-1, S5_CHUNK * gc)
        rows, width = x2d.shape
        cols = jnp.arange(S5_CHUNK * LANES)
        src = jnp.arange(width)
        sel = ((src[:, None] // gc == cols[None, :] // LANES)
               & (src[:, None] % gc == cols[None, :] % gc)).astype(BF16)
        out = jnp.dot(x2d.astype(BF16), sel)
        rg = ((jnp.arange(rows) // ns) % tg)[:, None]
        cg = ((cols // gc) % tg)[None, :]
        return jnp.where(rg == cg, out, jnp.zeros_like(out)).reshape(
            n_tiles, S5_TILE_STATE, S5_CHUNK * LANES)
    pcat = jnp.concatenate([p_half(ca_re), -p_half(ca_im)], axis=1)

    a16 = jnp.concatenate([p_re[S5_CHUNK].reshape(n_tiles, 1, S5_TILE_STATE),
                           p_im[S5_CHUNK].reshape(n_tiles, 1, S5_TILE_STATE)], axis=-1)
    dd = d.astype(F32).reshape(n_tiles, 1, LANES)
    return kcat.astype(BF16), w1.astype(BF16), pcat.astype(BF16), a16, dd


def _s5_body(x_ref, kcat_ref, w1_ref, pcat_ref, a16_ref, d_ref, z_ref,
             state_ref, xcat_ref, xr_ref, q_ref, sin_ref, yc_ref, yint_ref, *, nb, tl):
    t = pl.program_id(1)
    nk = tl // S5_CHUNK
    n = nb * tl
    ts = S5_TILE_STATE

    @pl.when(t == 0)
    def _():
        state_ref[...] = jnp.zeros_like(state_ref)

    for b in range(nb):
        for tp in range(S5_CHUNK):
            xr_ref[tp, pl.ds(b, nk, stride=nb), :] = x_ref[b, pl.ds(tp, nk, stride=S5_CHUNK), :]
    xr = jnp.concatenate([xr_ref[tp].astype(BF16) for tp in range(S5_CHUNK)], axis=1)
    q_ref[...] = _dot(xr, w1_ref[...])

    a_re = a16_ref[:, :ts]
    a_im = a16_ref[:, ts:]
    s = state_ref[...]
    for k in range(nk):
        sin_ref[k * nb:(k + 1) * nb, :] = s
        s_re = s[:, :ts]
        s_im = s[:, ts:]
        s = jnp.concatenate([a_re * s_re - a_im * s_im, a_re * s_im + a_im * s_re],
                            axis=1) + q_ref[k * nb:(k + 1) * nb, :]
    state_ref[...] = s

    yc = _dot(sin_ref[...].astype(BF16), pcat_ref[...])
    for tp in range(S5_CHUNK):
        yc_ref[tp] = yc[:, tp * LANES:(tp + 1) * LANES]
    for b in range(nb):
        for tp in range(S5_CHUNK):
            yint_ref[pl.ds(b * tl + tp, nk, stride=S5_CHUNK), :] = (
                yc_ref[tp, pl.ds(b, nk, stride=nb), :])

    x = x_ref[...].reshape(n, LANES)
    pos = lax.broadcasted_iota(jnp.int32, (n, LANES), 0) % S5_CHUNK
    for j in range(S5_CHUNK):
        xs = x if j == 0 else jnp.where(pos >= j, pltpu.roll(x, j, axis=0), 0.0)
        xcat_ref[:, j * LANES:(j + 1) * LANES] = xs.astype(BF16)
    y = _dot(xcat_ref[...], kcat_ref[...]) + yint_ref[...] + d_ref[...] * x
    z_ref[...] = _gelu_tanh(y).reshape(nb, tl, LANES)


def _s5_in_proj_body(a_ref, w_ref, o_ref):
    rows = a_ref.shape[0]
    sub = min(rows, MM_SUB_ROWS)
    for r in range(rows // sub):
        rs = slice(r * sub, (r + 1) * sub)
        acc = _dot(a_ref[rs, :], w_ref[...])
        for c in range(o_ref.shape[0]):
            o_ref[c, rs, :] = acc[:, c * LANES:(c + 1) * LANES]


def _s5_in_proj(h16, w_in16, tm):
    t_rows, d = h16.shape
    n_tiles = S5_WIDTH // LANES
    return pl.pallas_call(
        _s5_in_proj_body,
        out_shape=jax.ShapeDtypeStruct((n_tiles, t_rows, LANES), F32),
        grid=(t_rows // tm,),
        in_specs=[pl.BlockSpec((tm, d), lambda i: (i, 0)),
                  pl.BlockSpec((d, S5_WIDTH), lambda i: (0, 0))],
        out_specs=pl.BlockSpec((n_tiles, tm, LANES), lambda i: (0, i, 0)),
        compiler_params=_cparams(("parallel",)), name="in_proj_s5")(h16, w_in16)


def _s5_mixer(xa_tiles, prep, tl):
    kcat, w1, pcat, a16, dd = prep
    n_tiles, nb, seq, _ = xa_tiles.shape
    nk = tl // S5_CHUNK
    ts = S5_TILE_STATE
    body = functools.partial(_s5_body, nb=nb, tl=tl)
    wspec = lambda shp: pl.BlockSpec((None,) + shp, lambda j, t: (j, 0, 0))
    xspec = pl.BlockSpec((None, nb, tl, LANES), lambda j, t: (j, 0, t, 0))
    return pl.pallas_call(
        body,
        out_shape=jax.ShapeDtypeStruct(xa_tiles.shape, F32),
        grid=(n_tiles, seq // tl),
        in_specs=[xspec,
                  wspec((S5_CHUNK * LANES, LANES)),
                  wspec((S5_CHUNK * LANES, 2 * ts)),
                  wspec((2 * ts, S5_CHUNK * LANES)),
                  wspec((1, 2 * ts)),
                  wspec((1, LANES))],
        out_specs=xspec,
        scratch_shapes=[pltpu.VMEM((nb, 2 * ts), F32),
                        pltpu.VMEM((nb * tl, S5_CHUNK * LANES), BF16),
                        pltpu.VMEM((S5_CHUNK, nb * nk, LANES), F32),
                        pltpu.VMEM((nb * nk, 2 * ts), F32),
                        pltpu.VMEM((nb * nk, 2 * ts), F32),
                        pltpu.VMEM((S5_CHUNK, nb * nk, LANES), F32),
                        pltpu.VMEM((nb * tl, LANES), F32)],
        compiler_params=_cparams(("parallel", "arbitrary")), name="s5_mixer")(
            xa_tiles, kcat, w1, pcat, a16, dd)


def _glu_body(z_ref, w_ref, nw_ref, o_ref):
    z = jnp.concatenate([z_ref[c] for c in range(z_ref.shape[0])], axis=1)
    y = z * _sigmoid(_dot(z.astype(BF16), w_ref[...]))
    ms = jnp.mean(y * y, axis=-1, keepdims=True)
    o_ref[...] = (y * lax.rsqrt(ms + RMS_EPS) * nw_ref[...]).astype(BF16)


def _s5_glu_norm(z_tiles, w_glu, norm_w, tm):
    n_tiles, t_rows, _ = z_tiles.shape
    d = n_tiles * LANES
    return pl.pallas_call(
        _glu_body,
        out_shape=jax.ShapeDtypeStruct((t_rows, d), BF16),
        grid=(t_rows // tm,),
        in_specs=[pl.BlockSpec((n_tiles, tm, LANES), lambda i: (0, i, 0)),
                  pl.BlockSpec((d, d), lambda i: (0, 0)),
                  pl.BlockSpec((1, d), lambda i: (0, 0))],
        out_specs=pl.BlockSpec((tm, d), lambda i: (i, 0)),
        compiler_params=_cparams(("parallel",)), name="s5_glu_norm")(
            z_tiles, w_glu, norm_w.reshape(1, d))


def _sgu_body(u_ref, v_ref, lnw_ref, lnb_ref, w_ref, bs_ref, nw_ref, o_ref,
              vb_ref, x_ref, *, nc):
    v = _gelu_tanh(v_ref[...])
    mu = jnp.mean(v, axis=-1, keepdims=True)
    vc = v - mu
    var = jnp.mean(vc * vc, axis=-1, keepdims=True)
    vb_ref[...] = (vc * lax.rsqrt(var + LN_EPS) * lnw_ref[...] + lnb_ref[...]).astype(BF16)
    hd = SGU_WIDTH // SGU_HEADS
    for h in range(SGU_HEADS):
        cs = slice(h * hd, (h + 1) * hd)
        rhs = jnp.concatenate(
            [vb_ref[c * SGU_CHUNK:(c + 1) * SGU_CHUNK, cs] for c in range(nc)], axis=1)
        zz = _dot(w_ref[h], rhs)
        for c in range(nc):
            rs = slice(c * SGU_CHUNK, (c + 1) * SGU_CHUNK)
            z = zz[:, c * hd:(c + 1) * hd] + bs_ref[:, cs]
            x_ref[rs, cs] = _gelu_tanh(u_ref[rs, cs]) * z
    x = x_ref[...]
    ms = jnp.mean(x * x, axis=-1, keepdims=True)
    o_ref[...] = (x * lax.rsqrt(ms + RMS_EPS) * nw_ref[...]).astype(BF16)


def _sgu_mixer(proj3, ln_w, ln_b, w_s, b_s, norm_w, tl):
    nb, seq, _ = proj3.shape
    nc = tl // SGU_CHUNK
    hd = SGU_WIDTH // SGU_HEADS
    causal = jnp.tril(jnp.ones((SGU_CHUNK, SGU_CHUNK), dtype=bool))
    w_causal = jnp.where(causal[None], w_s, jnp.zeros_like(w_s)).astype(BF16)
    bias = jnp.repeat(b_s.astype(F32).T, hd, axis=1)
    u_blk = 0
    row = lambda a: a.astype(F32).reshape(1, SGU_WIDTH)
    const = lambda shp: pl.BlockSpec(shp, lambda b, t: (0,) * len(shp))
    body = functools.partial(_sgu_body, nc=nc)
    return pl.pallas_call(
        body,
        out_shape=jax.ShapeDtypeStruct((nb, seq, SGU_WIDTH), BF16),
        grid=(nb, seq // tl),
        in_specs=[pl.BlockSpec((None, tl, SGU_WIDTH), lambda b, t: (b, t, u_blk)),
                  pl.BlockSpec((None, tl, SGU_WIDTH), lambda b, t: (b, t, u_blk + 1)),
                  const((1, SGU_WIDTH)), const((1, SGU_WIDTH)),
                  const((SGU_HEADS, SGU_CHUNK, SGU_CHUNK)),
                  const((SGU_CHUNK, SGU_WIDTH)), const((1, SGU_WIDTH))],
        out_specs=pl.BlockSpec((None, tl, SGU_WIDTH), lambda b, t: (b, t, 0)),
        scratch_shapes=[pltpu.VMEM((tl, SGU_WIDTH), BF16),
                        pltpu.VMEM((tl, SGU_WIDTH), F32)],
        compiler_params=_cparams(("parallel", "parallel")), name="sgu_mixer")(
            proj3, proj3, row(ln_w), row(ln_b), w_causal, bias, row(norm_w))


HG_PAIR = 2
HG_PAIR_W = HG_PAIR * HG_HEAD_DIM
HG_FAST_BLOCK = 32
HG_FAST_MIN_LOG_DECAY = -60.0
_NT = (((1,), (1,)), ((), ()))
_TN = (((0,), (0,)), ((), ()))


def _hgrn_tables():
    c = HG_CHUNK
    t = jnp.arange(c)[:, None]
    r = jnp.arange(c)[None, :]
    mats, masks = [], []
    for lev in range(HG_LEVELS):
        m = 1 << lev
        mid = (t // (2 * m)) * (2 * m) + m
        later = t >= mid
        mats.append(jnp.where(later, (r >= mid) & (r <= t), (r > t) & (r < mid)))
        same = (t // (2 * m)) == (r // (2 * m))
        masks.append(same & later & (r < mid))
    tri = r <= t
    mats.append(tri)
    masks.append(t == r)
    fb = HG_FAST_BLOCK
    fast = [masks[6], masks[5], ((t // fb) == (r // fb)) & tri]
    pair = lambda m: jnp.tile(m.astype(F32), (1, HG_PAIR))
    hh = jnp.arange(HG_PAIR_W) // HG_HEAD_DIM
    return dict(
        tri=tri.astype(BF16),
        mst=jnp.concatenate(mats, axis=0).astype(BF16),
        masks=jnp.stack(masks).astype(F32),
        fmasks=jnp.stack([pair(m) for m in fast]),
        bd=(hh[:, None] == hh[None, :]).astype(F32))


def _block_diag(x):
    z = jnp.zeros((x.shape[0], HG_HEAD_DIM), x.dtype)
    top = jnp.concatenate([x[:, :HG_HEAD_DIM], z], axis=1)
    bot = jnp.concatenate([z, x[:, HG_HEAD_DIM:]], axis=1)
    return jnp.concatenate([top, bot], axis=0)


def _rows(v, n):
    return jnp.broadcast_to(v, (n, v.shape[1]))


def _hgrn_finish(o, gv, nw, o_ref, rs):
    for h in range(HG_PAIR):
        cs = slice(h * HG_HEAD_DIM, (h + 1) * HG_HEAD_DIM)
        oh = o[:, cs]
        ms = jnp.mean(oh * oh, axis=-1, keepdims=True)
        gg = gv[:, cs]
        o_ref[rs, cs] = (oh * lax.rsqrt(ms + RMS_EPS) * nw * (gg * _sigmoid(gg))).astype(BF16)


def _hgrn_body(q_ref, f_ref, i_ref, g_ref, par_ref, nw_ref, tri_ref, mst_ref, mask_ref,
               fmask_ref, bd_ref, o_ref, st_ref, qs_ref, kk_ref, lf_ref, b_ref, *, nc):
    c = HG_CHUNK
    hd = HG_HEAD_DIM
    wb = HG_PAIR_W
    t = pl.program_id(2)

    @pl.when(t == 0)
    def _():
        st_ref[...] = jnp.zeros_like(st_ref)

    log_lb = par_ref[0:1, :]
    log_1m_lb = par_ref[1:2, :]
    one_m_lb = par_ref[2:3, :]
    nw = nw_ref[...]
    tri = tri_ref[...]
    row = lax.broadcasted_iota(jnp.int32, (c, wb), 0)

    wmin = None
    for ch in range(nc):
        rs = slice(ch * c, (ch + 1) * c)
        fr = f_ref[rs, :]
        e = jnp.exp(-jnp.abs(fr))
        r = 1.0 / (1.0 + e)
        sig_neg = jnp.where(fr >= 0, e * r, r)
        log_sig = jnp.minimum(fr, 0.0) - jnp.log(1.0 + e)
        y = log_1m_lb + log_sig
        log_f = jnp.maximum(log_lb, y) + jnp.log(1.0 + jnp.exp(-jnp.abs(log_lb - y)))
        hi = log_f.astype(BF16)
        lo = (log_f - hi.astype(F32)).astype(BF16)
        b = _dot(tri, hi) + _dot(tri, lo)
        qv = q_ref[rs, :]
        qs_ref[rs, :] = qv * _sigmoid(qv)
        kk_ref[rs, :] = one_m_lb * sig_neg
        lf_ref[rs, :] = log_f
        b_ref[rs, :] = b
        fb = HG_FAST_BLOCK
        starts = jnp.concatenate(
            [jnp.zeros((fb, wb), F32)]
            + [_rows(b[j * fb - 1:j * fb, :], fb) for j in range(1, c // fb)], axis=0)
        w = b - starts
        wmin = w if wmin is None else jnp.minimum(wmin, w)
    fast = jnp.min(wmin) >= HG_FAST_MIN_LOG_DECAY

    @pl.when(fast)
    def _():
        bd = bd_ref[...]
        for ch in range(nc):
            rs = slice(ch * c, (ch + 1) * c)
            b = b_ref[rs, :]
            qs = qs_ref[rs, :]
            kk = kk_ref[rs, :]
            vv = i_ref[rs, :].astype(BF16)
            b63 = _rows(b[63:64, :], c)
            e6 = jnp.where(row >= 64, b - b63, b63 - b)
            bm = jnp.concatenate([_rows(b[31:32, :], 64), _rows(b[95:96, :], 64)], axis=0)
            e5 = jnp.where((row % 64) >= 32, b - bm, bm - b)
            fb = HG_FAST_BLOCK
            starts = jnp.concatenate(
                [jnp.zeros((fb, wb), F32)]
                + [_rows(b[j * fb - 1:j * fb, :], fb) for j in range(1, c // fb)], axis=0)
            w = b - starts
            a6 = jnp.exp(e6)
            a5 = jnp.exp(e5)
            factors = [(a6, a6), (a5, a5), (jnp.exp(w), jnp.exp(-w))]
            scores = None
            for lev, (aq, ak) in enumerate(factors):
                sc = lax.dot_general((qs * aq).astype(BF16), _block_diag((kk * ak).astype(BF16)),
                                     _NT, preferred_element_type=F32)
                sc = fmask_ref[lev] * sc
                scores = sc if scores is None else scores + sc
            st = st_ref[...]
            o = (_dot(scores.astype(BF16), _block_diag(vv))
                 + lax.dot_general((qs * jnp.exp(b)).astype(BF16), st.astype(BF16), _NT,
                                   preferred_element_type=F32))
            bend = b[c - 1:c, :]
            kd = (kk * jnp.exp(bend - b)).astype(BF16)
            st_ref[...] = bd * (st * jnp.exp(bend)
                                + lax.dot_general(vv, kd, _TN, preferred_element_type=F32))
            _hgrn_finish(o, g_ref[rs, :], nw, o_ref, rs)

    @pl.when(jnp.logical_not(fast))
    def _():
        mst = mst_ref[...]

        def chunk(ch, carry):
            rs = pl.ds(pl.multiple_of(ch * c, c), c)
            log_f = lf_ref[rs, :]
            hi = log_f.astype(BF16)
            lo = (log_f - hi.astype(F32)).astype(BF16)
            ee = _dot(mst, hi) + _dot(mst, lo)
            qs = qs_ref[rs, :]
            kk = kk_ref[rs, :]
            vv = i_ref[rs, :].astype(BF16)
            bcum = ee[HG_LEVELS * c:(HG_LEVELS + 1) * c, :]
            outs = []
            for h in range(HG_PAIR):
                cs = slice(h * hd, (h + 1) * hd)
                qh = qs[:, cs]
                kh = kk[:, cs]
                vh = vv[:, cs]
                scores = mask_ref[HG_LEVELS] * jnp.sum(qh * kh, axis=-1, keepdims=True)
                for lev in range(HG_LEVELS):
                    a = jnp.exp(ee[lev * c:(lev + 1) * c, cs])
                    sc = lax.dot_general((qh * a).astype(BF16), (kh * a).astype(BF16), _NT,
                                         preferred_element_type=F32)
                    scores = scores + mask_ref[lev] * sc
                bh = bcum[:, cs]
                st = st_ref[cs, cs]
                outs.append(_dot(scores.astype(BF16), vh)
                            + lax.dot_general((qh * jnp.exp(bh)).astype(BF16), st.astype(BF16),
                                              _NT, preferred_element_type=F32))
                bend = bh[c - 1:c, :]
                kd = (kh * jnp.exp(bend - bh)).astype(BF16)
                st_ref[cs, cs] = st * jnp.exp(bend) + lax.dot_general(
                    vh, kd, _TN, preferred_element_type=F32)
            _hgrn_finish(jnp.concatenate(outs, axis=1), g_ref[rs, :], nw, o_ref, rs)
            return carry

        lax.fori_loop(0, nc, chunk, 0)


def _hgrn_mixer(proj3, lb, norm_w, tl):
    nb, seq, _ = proj3.shape
    nc = tl // HG_CHUNK
    wb = HG_PAIR_W
    q0 = (2 * SGU_WIDTH) // wb
    nblk = HG_WIDTH // wb
    lbf = lb.astype(F32)
    par = jnp.stack([jnp.log(lbf), jnp.log1p(-lbf), 1.0 - lbf])
    tb = _hgrn_tables()
    sec = lambda k: pl.BlockSpec((None, tl, wb), functools.partial(
        lambda b, h, t, k: (b, t, q0 + k * nblk + h), k=k))
    const = lambda a: pl.BlockSpec(a.shape, lambda b, h, t: (0,) * a.ndim)
    body = functools.partial(_hgrn_body, nc=nc)
    return pl.pallas_call(
        body,
        out_shape=jax.ShapeDtypeStruct((nb, seq, HG_WIDTH), BF16),
        grid=(nb, nblk, seq // tl),
        in_specs=[sec(0), sec(1), sec(2), sec(3),
                  pl.BlockSpec((3, wb), lambda b, h, t: (0, h)),
                  pl.BlockSpec((1, HG_HEAD_DIM), lambda b, h, t: (0, 0)),
                  const(tb["tri"]), const(tb["mst"]), const(tb["masks"]),
                  const(tb["fmasks"]), const(tb["bd"])],
        out_specs=pl.BlockSpec((None, tl, wb), lambda b, h, t: (b, t, h)),
        scratch_shapes=[pltpu.VMEM((wb, wb), F32)] + [pltpu.VMEM((tl, wb), F32)] * 4,
        compiler_params=_cparams(("parallel", "parallel", "arbitrary")),
        name="hgrn2_mixer")(
            proj3, proj3, proj3, proj3, par,
            norm_w.astype(F32).reshape(1, HG_HEAD_DIM),
            tb["tri"], tb["mst"], tb["masks"], tb["fmasks"], tb["bd"])


def kernel(x, p, w_in, s5_lam_re, s5_lam_im, s5_log_step, s5_b_re, s5_b_im, s5_c_re, s5_c_im, s5_d, s5_w_glu, sgu_ln_w, sgu_ln_b, sgu_w, sgu_b, hg_lb_logits, hg_norm_w, norm_a_w, norm_b_w, w_out, ln1_w, ln1_b, w_ffn_in, w_ffn_out, ln2_w, ln2_b, w_ple_in, w_ple_gate, ln3_w, ln3_b):
    nb, seq, d_model = x.shape
    depth = w_in.shape[0]
    t_rows = nb * seq
    proj_w = w_in.shape[2]
    d_ff = w_ffn_out.shape[1]
    alpha = (2.0 * depth) ** 0.25

    tm = _pick(t_rows, (1024, 512, 256, 128))
    tm_wide = _pick(t_rows, (2048, 1024, 512, 256, 128))
    tm_half = _pick(t_rows, (512, 256, 128))
    tm_ln = _pick(t_rows, (256, 128))
    tl_s5 = _pick(seq, (512, 256, 128))
    tl_sgu = _pick(seq, (512, 256, 128))
    tl_hg = _pick(seq, (1024, 512, 256, 128))
    tn_in = _pick(proj_w - S5_WIDTH, (1024, 512))

    lbs =jnp.cumsum(jax.nn.softmax(hg_lb_logits.astype(F32), axis=0), axis=0)
    lbs = lbs - lbs[0:1]

    x_rows = x.reshape(t_rows, d_model).astype(F32)
    res = _Residual(alpha, h=x_rows)
    h16 = _layer_to_bf16(x_rows.reshape(1, t_rows, d_model), 0)
    p_rows = p.reshape(depth, t_rows, -1)

    def post_norm(xres, w, b):
        h16, mu, rstd = _layer_norm(xres, w, b, tm_ln, final=False)
        return h16, _Residual(alpha, x=xres, mu=mu, rstd=rstd, w=w, b=b)

    w_in16 = _layer_to_bf16(w_in, 0)
    for l in range(depth):
        xa_tiles = _s5_in_proj(h16, w_in16, tm).reshape(-1, nb, seq, LANES)
        proj, wf, wo = _fused_mm(
            [(h16, w_in16, d_model, 0, S5_WIDTH // tn_in)], [],
            lambda accs, ex: [accs[0]], [F32], proj_w - S5_WIDTH, tm, tn_in, "in_proj",
            side_casts=[(w_ffn_in, l), (w_out, l)])
        proj3 = proj.reshape(nb, seq, proj_w - S5_WIDTH)
        prep = _s5_prepare(s5_lam_re[l], s5_lam_im[l], s5_log_step[l], s5_b_re[l],
                           s5_b_im[l], s5_c_re[l], s5_c_im[l], s5_d[l])
        z_tiles = _s5_mixer(xa_tiles, prep, tl_s5).reshape(-1, t_rows, LANES)
        ya = _s5_glu_norm(z_tiles, _layer_to_bf16(s5_w_glu, l), norm_a_w[l], tm_half)
        yb = _sgu_mixer(proj3, sgu_ln_w[l], sgu_ln_b[l], sgu_w[l], sgu_b[l],
                        norm_b_w[l], tl_sgu).reshape(t_rows, SGU_WIDTH)
        yc = _hgrn_mixer(proj3, lbs[l], hg_norm_w[l], tl_hg).reshape(t_rows, HG_WIDTH)
        (xres,) = _fused_mm(
            [(ya, wo, S5_WIDTH, 0, 0), (yb, wo, SGU_WIDTH, 1, 0), (yc, wo, HG_WIDTH, 1, 0)],
            res.extras,
            functools.partial(lambda accs, ex, res: [res(ex) + (accs[0] + accs[1] + accs[2])],
                              res=res),
            [F32], d_model, tm, 1024, "out_proj")
        h16, res = post_norm(xres, ln1_w[l], ln1_b[l])

        hid, wfo, wpg = _fused_mm(
            [(h16, wf, d_model, 0, 0), (h16, wf, d_model, 0, d_ff // 256)], [],
            lambda accs, ex: [accs[0] * _sigmoid(accs[0]) * accs[1]],
            [BF16], d_ff, tm_wide, 256, "ffn_in",
            side_casts=[(w_ffn_out, l), (w_ple_gate, l)])
        xres, *next_w_in = _fused_mm(
            [(hid, wfo, d_ff, 0, 0)], res.extras,
            functools.partial(lambda accs, ex, res: [res(ex) + accs[0]], res=res),
            [F32], d_model, tm_half, 512, "ffn_out",
            side_casts=[(w_in, l + 1)] if l + 1 < depth else [])
        if next_w_in:
            w_in16 = next_w_in[0]
        h16, res = post_norm(xres, ln2_w[l], ln2_b[l])

        p16 = _layer_to_bf16(p_rows, l)
        (xres,) = _fused_mm(
            [(h16, wpg, d_model, 0, 0),
             (p16, _layer_to_bf16(w_ple_in, l), p16.shape[1], 0, 0)], res.extras,
            functools.partial(lambda accs, ex, res: [res(ex) + accs[1] * _sigmoid(accs[0])],
                              res=res),
            [F32], d_model, tm, 1024, "ple", sub_rows=128)
        if l + 1 < depth:
            h16, res = post_norm(xres, ln3_w[l], ln3_b[l])

    out = _layer_norm(xres, ln3_w[depth - 1], ln3_b[depth - 1], tm_ln, final=True)
    return out.reshape(nb, seq, d_model).astype(x.dtype)
```

```python
import functools
import math

import jax
import jax.numpy as jnp
from jax import lax
from jax.experimental import pallas as pl
from jax.experimental.pallas import tpu as pltpu

F32 = jnp.float32
BF16 = jnp.bfloat16

V7X_VMEM_LIMIT_BYTES = 56 * 1024 * 1024
LANES = 128

LN_EPS = 1e-5
RMS_EPS = 1e-6

S5_WIDTH = 1024
S5_GROUP_CH = 16
S5_STATE = 64
S5_CHUNK = 8
S5_TILE_GROUPS = LANES // S5_GROUP_CH
S5_TILE_STATE = S5_TILE_GROUPS * S5_STATE

SGU_WIDTH = 1024
SGU_CHUNK = 128
SGU_HEADS = 8

HG_WIDTH = 2048
HG_HEAD_DIM = 128
HG_CHUNK = 128
HG_LEVELS = 7


def _cparams(sem):
    return pltpu.CompilerParams(dimension_semantics=sem,
                                vmem_limit_bytes=V7X_VMEM_LIMIT_BYTES)


def _gelu_tanh(x):
    c = math.sqrt(2.0 / math.pi)
    return x * (0.5 * (1.0 + jnp.tanh(c * (x + 0.044715 * (x * x * x)))))


def _sigmoid(x):
    return 1.0 / (1.0 + jnp.exp(-x))


def _dot(a, b):
    return jnp.dot(a, b, preferred_element_type=F32)


def _pick(n, prefs):
    for p in prefs:
        if n % p == 0:
            return p
    return n


CAST_BLOCK_BYTES = 8 * 1024 * 1024


def _cast_body(w_ref, o_ref):
    o_ref[...] = w_ref[...].astype(BF16)


def _layer_to_bf16(w, layer):
    _, rows, cols = w.shape
    fits = [tr for tr in (8192, 4096, 2048, 1024, 512, 256, 128, 64, 32, 16)
            if rows % tr == 0 and tr * cols * 4 <= CAST_BLOCK_BYTES]
    tr = fits[0]
    return pl.pallas_call(
        _cast_body,
        out_shape=jax.ShapeDtypeStruct((rows, cols), BF16),
        grid=(rows // tr,),
        in_specs=[pl.BlockSpec((None, tr, cols), lambda i: (layer, i, 0))],
        out_specs=pl.BlockSpec((tr, cols), lambda i: (i, 0)),
        compiler_params=_cparams(("parallel",)), name="cast_bf16")(w)


MM_SUB_ROWS = 256


def _mm_body(*refs, a_of_dot, n_a, extra_kinds, n_side, epilogue, sub_rows):
    n_dot = len(a_of_dot)
    n_extra = len(extra_kinds)
    n_in = n_a + n_dot + n_extra
    a_refs = refs[:n_a]
    w_refs = refs[n_a:n_a + n_dot]
    extra = refs[n_a + n_dot:n_in]
    side_in = refs[n_in:n_in + n_side]
    out_refs = refs[n_in + n_side:len(refs) - n_side]
    side_out = refs[len(refs) - n_side:]
    for si, so in zip(side_in, side_out):
        so[...] = si[...].astype(BF16)
    rows = out_refs[0].shape[0]
    sub = min(rows, sub_rows)
    for r in range(rows // sub):
        rs = slice(r * sub, (r + 1) * sub)
        accs = [_dot(a_refs[ai][rs, :], w[...]) for ai, w in zip(a_of_dot, w_refs)]
        outs = epilogue(accs, [e[...] if kind == "row" else e[rs, :]
                               for e, kind in zip(extra, extra_kinds)])
        for o_ref, o in zip(out_refs, outs):
            o_ref[rs, :] = o.astype(o_ref.dtype)


BF16_ROW_TILE = 16


def _fused_mm(dots, extras, epilogue, out_dtypes, n_cols, tm, tn, name,
              sub_rows=MM_SUB_ROWS, side_casts=()):
    t_rows = dots[0][0].shape[0]
    grid = (t_rows // tm, n_cols // tn)
    in_specs, args = [], []
    a_of_dot = []
    for a, _, _, _, _ in dots:
        known = [k for k, seen in enumerate(args) if seen is a]
        if known:
            a_of_dot.append(known[0])
            continue
        a_of_dot.append(len(args))
        in_specs.append(pl.BlockSpec((tm, a.shape[1]), lambda i, j: (i, 0)))
        args.append(a)
    n_a = len(args)
    for _, w, rb, ri, co in dots:
        in_specs.append(pl.BlockSpec((rb, tn), functools.partial(
            lambda i, j, ri, co: (ri, j + co), ri=ri, co=co)))
        args.append(w)
    for arr, kind in extras:
        if kind == "tile":
            in_specs.append(pl.BlockSpec((tm, tn), lambda i, j: (i, j)))
        elif kind == "stat":
            in_specs.append(pl.BlockSpec((tm, LANES), lambda i, j: (i, 0)))
        else:
            in_specs.append(pl.BlockSpec((1, tn), lambda i, j: (0, j)))
        args.append(arr)
    out_shape = [jax.ShapeDtypeStruct((t_rows, n_cols), dt) for dt in out_dtypes]
    out_specs = [pl.BlockSpec((tm, tn), lambda i, j: (i, j)) for _ in out_dtypes]
    n_steps = grid[0] * grid[1]
    for stack, layer in side_casts:
        _, rows, cols = stack.shape
        tiles = rows // BF16_ROW_TILE
        n_slabs = max(d for d in range(1, min(tiles, n_steps) + 1) if tiles % d == 0)
        slab = functools.partial(lambda i, j, last: jnp.minimum(i * grid[1] + j, last),
                                 last=n_slabs - 1)
        in_specs.append(pl.BlockSpec(
            (None, rows // n_slabs, cols),
            functools.partial(lambda i, j, layer, slab: (layer, slab(i, j), 0),
                              layer=layer, slab=slab)))
        args.append(stack)
        out_shape.append(jax.ShapeDtypeStruct((rows, cols), BF16))
        out_specs.append(pl.BlockSpec(
            (rows // n_slabs, cols),
            functools.partial(lambda i, j, slab: (slab(i, j), 0), slab=slab)))
    body = functools.partial(_mm_body, a_of_dot=tuple(a_of_dot), n_a=n_a,
                             extra_kinds=tuple(kind for _, kind in extras),
                             n_side=len(side_casts), epilogue=epilogue, sub_rows=sub_rows)
    return pl.pallas_call(
        body, out_shape=out_shape, grid=grid, in_specs=in_specs, out_specs=out_specs,
        compiler_params=_cparams(("parallel", "arbitrary")), name=name)(*args)


def _ln_rows(x):
    mu = jnp.mean(x, axis=-1, keepdims=True)
    xc = x - mu
    var = jnp.mean(xc * xc, axis=-1, keepdims=True)
    return xc, mu, lax.rsqrt(var + LN_EPS)


LN_ROW_GROUP = 8


def _ln_stats_body(x_ref, w_ref, b_ref, hb_ref, mu_ref, rs_ref):
    w = w_ref[...]
    b = b_ref[...]
    for g in range(x_ref.shape[0] // LN_ROW_GROUP):
        rs = slice(g * LN_ROW_GROUP, (g + 1) * LN_ROW_GROUP)
        xc, mu, rstd = _ln_rows(x_ref[rs, :])
        hb_ref[rs, :] = (xc * rstd * w + b).astype(BF16)
        mu_ref[rs, :] = jnp.broadcast_to(mu, (LN_ROW_GROUP, LANES))
        rs_ref[rs, :] = jnp.broadcast_to(rstd, (LN_ROW_GROUP, LANES))


def _ln_final_body(x_ref, w_ref, b_ref, h_ref):
    w = w_ref[...]
    b = b_ref[...]
    for g in range(x_ref.shape[0] // LN_ROW_GROUP):
        rs = slice(g * LN_ROW_GROUP, (g + 1) * LN_ROW_GROUP)
        xc, _, rstd = _ln_rows(x_ref[rs, :])
        h_ref[rs, :] = xc * rstd * w + b


def _layer_norm(x, w, b, tm, final):
    t_rows, d = x.shape
    rows = pl.BlockSpec((tm, d), lambda i: (i, 0))
    stat = pl.BlockSpec((tm, LANES), lambda i: (i, 0))
    vec = pl.BlockSpec((1, d), lambda i: (0, 0))
    if final:
        body, out_shape, out_specs = (
            _ln_final_body, jax.ShapeDtypeStruct((t_rows, d), F32), rows)
    else:
        body = _ln_stats_body
        out_shape = [jax.ShapeDtypeStruct((t_rows, d), BF16),
                     jax.ShapeDtypeStruct((t_rows, LANES), F32),
                     jax.ShapeDtypeStruct((t_rows, LANES), F32)]
        out_specs = [rows, stat, stat]
    return pl.pallas_call(
        body, out_shape=out_shape, grid=(t_rows // tm,),
        in_specs=[rows, vec, vec], out_specs=out_specs,
        compiler_params=_cparams(("parallel",)), name="layer_norm")(
            x, w.reshape(1, d), b.reshape(1, d))


class _Residual:
    def __init__(self, alpha, h=None, x=None, mu=None, rstd=None, w=None, b=None):
        self.alpha = alpha
        if h is not None:
            self.extras = [(h, "tile")]
        else:
            self.extras = [(x, "tile"), (mu, "stat"), (rstd, "stat"),
                           (w.astype(F32).reshape(1, -1), "row"),
                           (b.astype(F32).reshape(1, -1), "row")]

    def __call__(self, ex):
        if len(ex) == 1:
            return self.alpha * ex[0]
        x, mu, rstd, w, b = ex
        rep = x.shape[1] // LANES
        wide = lambda s: jnp.concatenate([s] * rep, axis=1)
        return self.alpha * ((x - wide(mu)) * wide(rstd) * w + b)


def _s5_prepare(lam_re, lam_im, log_step, b_re, b_im, c_re, c_im, d):
    hp = lax.Precision.HIGHEST
    n_tiles = S5_WIDTH // LANES
    lr = jnp.minimum(lam_re.astype(F32), -1e-4)
    li = lam_im.astype(F32)
    dt = jnp.exp(log_step.astype(F32))[:, None]
    mag = jnp.exp(lr * dt)
    ab_re = mag * jnp.cos(li * dt)
    ab_im = mag * jnp.sin(li * dt)
    den = lr * lr + li * li
    nr = ab_re - 1.0
    g_re = (nr * lr + ab_im * li) / den
    g_im = (ab_im * lr - nr * li) / den
    br = b_re.astype(F32)
    bi = b_im.astype(F32)
    bb_re = g_re[..., None] * br - g_im[..., None] * bi
    bb_im = g_re[..., None] * bi + g_im[..., None] * br
    pr, pi = [jnp.ones_like(ab_re)], [jnp.zeros_like(ab_re)]
    for _ in range(S5_CHUNK):
        pr_new = pr[-1] * ab_re - pi[-1] * ab_im
        pi_new = pr[-1] * ab_im + pi[-1] * ab_re
        pr.append(pr_new)
        pi.append(pi_new)
    p_re = jnp.stack(pr)
    p_im = jnp.stack(pi)
    pb_re = (p_re[:S5_CHUNK, :, :, None] * bb_re[None]
             - p_im[:S5_CHUNK, :, :, None] * bb_im[None])
    pb_im = (p_re[:S5_CHUNK, :, :, None] * bb_im[None]
             + p_im[:S5_CHUNK, :, :, None] * bb_re[None])
    cr = c_re.astype(F32)
    ci = c_im.astype(F32)
    kern = (jnp.einsum("gon,jgni->jgoi", cr, pb_re, precision=hp)
            - jnp.einsum("gon,jgni->jgoi", ci, pb_im, precision=hp))
    tg, gc, ns = S5_TILE_GROUPS, S5_GROUP_CH, S5_STATE

    def spread(x2d, rep, row_group, col_group):
        rows, width = x2d.shape
        sel = jnp.tile(jnp.eye(width, dtype=BF16), (1, rep))
        out = jnp.dot(x2d.astype(BF16), sel)
        rg = row_group(jnp.arange(rows))[:, None]
        cg = col_group(jnp.arange(width * rep))[None, :]
        return jnp.where(rg == cg, out, jnp.zeros_like(out))

    k5 = kern.reshape(S5_CHUNK, n_tiles, tg, gc, gc).transpose(1, 0, 2, 4, 3)
    kcat = spread(k5.reshape(-1, gc), tg, lambda r: (r // gc) % tg,
                  lambda c: c // gc).reshape(n_tiles, S5_CHUNK * LANES, LANES)

    def w1_half(pb):
        pb5 = pb[::-1].reshape(S5_CHUNK, n_tiles, tg, ns, gc).transpose(1, 0, 2, 4, 3)
        return spread(pb5.reshape(-1, ns), tg, lambda r: (r // gc) % tg,
                      lambda c: c // ns).reshape(n_tiles, S5_CHUNK * LANES, S5_TILE_STATE)
    w1 = jnp.concatenate([w1_half(pb_re), w1_half(pb_im)], axis=-1)

    ca_re = cr[None] * p_re[1:, :, None, :] - ci[None] * p_im[1:, :, None, :]
    ca_im = cr[None] * p_im[1:, :, None, :] + ci[None] * p_re[1:, :, None, :]

    def p_half(ca):
        ca5 = ca.reshape(S5_CHUNK, n_tiles, tg, gc, ns).transpose(1, 2, 4, 0, 3)
        x2d = ca5.reshape(-1, S5_CHUNK * gc)
        rows, width = x2d.shape
        cols = jnp.arange(S5_CHUNK * LANES)
        src = jnp.arange(width)
        sel = ((src[:, None] // gc == cols[None, :] // LANES)
               & (src[:, None] % gc == cols[None, :] % gc)).astype(BF16)
        out = jnp.dot(x2d.astype(BF16), sel)
        rg = ((jnp.arange(rows) // ns) % tg)[:, None]
        cg = ((cols // gc) % tg)[None, :]
        return jnp.where(rg == cg, out, jnp.zeros_like(out)).reshape(
            n_tiles, S5_TILE_STATE, S5_CHUNK * LANES)
    pcat = jnp.concatenate([p_half(ca_re), -p_half(ca_im)], axis=1)

    a16 = jnp.concatenate([p_re[S5_CHUNK].reshape(n_tiles, 1, S5_TILE_STATE),
                           p_im[S5_CHUNK].reshape(n_tiles, 1, S5_TILE_STATE)], axis=-1)
    dd = d.astype(F32).reshape(n_tiles, 1, LANES)
    return kcat.astype(BF16), w1.astype(BF16), pcat.astype(BF16), a16, dd


def _s5_body(x_ref, kcat_ref, w1_ref, pcat_ref, a16_ref, d_ref, z_ref,
             state_ref, xcat_ref, xr_ref, q_ref, sin_ref, yc_ref, yint_ref, *, nb, tl):
    t = pl.program_id(1)
    nk = tl // S5_CHUNK
    n = nb * tl
    ts = S5_TILE_STATE

    @pl.when(t == 0)
    def _():
        state_ref[...] = jnp.zeros_like(state_ref)

    for b in range(nb):
        for tp in range(S5_CHUNK):
            xr_ref[tp, pl.ds(b, nk, stride=nb), :] = x_ref[b, pl.ds(tp, nk, stride=S5_CHUNK), :]
    xr = jnp.concatenate([xr_ref[tp].astype(BF16) for tp in range(S5_CHUNK)], axis=1)
    q_ref[...] = _dot(xr, w1_ref[...])

    a_re = a16_ref[:, :ts]
    a_im = a16_ref[:, ts:]
    s = state_ref[...]
    for k in range(nk):
        sin_ref[k * nb:(k + 1) * nb, :] = s
        s_re = s[:, :ts]
        s_im = s[:, ts:]
        s = jnp.concatenate([a_re * s_re - a_im * s_im, a_re * s_im + a_im * s_re],
                            axis=1) + q_ref[k * nb:(k + 1) * nb, :]
    state_ref[...] = s

    yc = _dot(sin_ref[...].astype(BF16), pcat_ref[...])
    for tp in range(S5_CHUNK):
        yc_ref[tp] = yc[:, tp * LANES:(tp + 1) * LANES]
    for b in range(nb):
        for tp in range(S5_CHUNK):
            yint_ref[pl.ds(b * tl + tp, nk, stride=S5_CHUNK), :] = (
                yc_ref[tp, pl.ds(b, nk, stride=nb), :])

    x = x_ref[...].reshape(n, LANES)
    pos = lax.broadcasted_iota(jnp.int32, (n, LANES), 0) % S5_CHUNK
    for j in range(S5_CHUNK):
        xs = x if j == 0 else jnp.where(pos >= j, pltpu.roll(x, j, axis=0), 0.0)
        xcat_ref[:, j * LANES:(j + 1) * LANES] = xs.astype(BF16)
    y = _dot(xcat_ref[...], kcat_ref[...]) + yint_ref[...] + d_ref[...] * x
    z_ref[...] = _gelu_tanh(y).reshape(nb, tl, LANES)


def _s5_in_proj_body(a_ref, w_ref, o_ref):
    rows = a_ref.shape[0]
    sub = min(rows, MM_SUB_ROWS)
    for r in range(rows // sub):
        rs = slice(r * sub, (r + 1) * sub)
        acc = _dot(a_ref[rs, :], w_ref[...])
        for c in range(o_ref.shape[0]):
            o_ref[c, rs, :] = acc[:, c * LANES:(c + 1) * LANES]


def _s5_in_proj(h16, w_in16, tm):
    t_rows, d = h16.shape
    n_tiles = S5_WIDTH // LANES
    return pl.pallas_call(
        _s5_in_proj_body,
        out_shape=jax.ShapeDtypeStruct((n_tiles, t_rows, LANES), F32),
        grid=(t_rows // tm,),
        in_specs=[pl.BlockSpec((tm, d), lambda i: (i, 0)),
                  pl.BlockSpec((d, S5_WIDTH), lambda i: (0, 0))],
        out_specs=pl.BlockSpec((n_tiles, tm, LANES), lambda i: (0, i, 0)),
        compiler_params=_cparams(("parallel",)), name="in_proj_s5")(h16, w_in16)


def _s5_mixer(xa_tiles, prep, tl):
    kcat, w1, pcat, a16, dd = prep
    n_tiles, nb, seq, _ = xa_tiles.shape
    nk = tl // S5_CHUNK
    ts = S5_TILE_STATE
    body = functools.partial(_s5_body, nb=nb, tl=tl)
    wspec = lambda shp: pl.BlockSpec((None,) + shp, lambda j, t: (j, 0, 0))
    xspec = pl.BlockSpec((None, nb, tl, LANES), lambda j, t: (j, 0, t, 0))
    return pl.pallas_call(
        body,
        out_shape=jax.ShapeDtypeStruct(xa_tiles.shape, F32),
        grid=(n_tiles, seq // tl),
        in_specs=[xspec,
                  wspec((S5_CHUNK * LANES, LANES)),
                  wspec((S5_CHUNK * LANES, 2 * ts)),
                  wspec((2 * ts, S5_CHUNK * LANES)),
                  wspec((1, 2 * ts)),
                  wspec((1, LANES))],
        out_specs=xspec,
        scratch_shapes=[pltpu.VMEM((nb, 2 * ts), F32),
                        pltpu.VMEM((nb * tl, S5_CHUNK * LANES), BF16),
                        pltpu.VMEM((S5_CHUNK, nb * nk, LANES), F32),
                        pltpu.VMEM((nb * nk, 2 * ts), F32),
                        pltpu.VMEM((nb * nk, 2 * ts), F32),
                        pltpu.VMEM((S5_CHUNK, nb * nk, LANES), F32),
                        pltpu.VMEM((nb * tl, LANES), F32)],
        compiler_params=_cparams(("parallel", "arbitrary")), name="s5_mixer")(
            xa_tiles, kcat, w1, pcat, a16, dd)


def _glu_body(z_ref, w_ref, nw_ref, o_ref):
    z = jnp.concatenate([z_ref[c] for c in range(z_ref.shape[0])], axis=1)
    y = z * _sigmoid(_dot(z.astype(BF16), w_ref[...]))
    ms = jnp.mean(y * y, axis=-1, keepdims=True)
    o_ref[...] = (y * lax.rsqrt(ms + RMS_EPS) * nw_ref[...]).astype(BF16)


def _s5_glu_norm(z_tiles, w_glu, norm_w, tm):
    n_tiles, t_rows, _ = z_tiles.shape
    d = n_tiles * LANES
    return pl.pallas_call(
        _glu_body,
        out_shape=jax.ShapeDtypeStruct((t_rows, d), BF16),
        grid=(t_rows // tm,),
        in_specs=[pl.BlockSpec((n_tiles, tm, LANES), lambda i: (0, i, 0)),
                  pl.BlockSpec((d, d), lambda i: (0, 0)),
                  pl.BlockSpec((1, d), lambda i: (0, 0))],
        out_specs=pl.BlockSpec((tm, d), lambda i: (i, 0)),
        compiler_params=_cparams(("parallel",)), name="s5_glu_norm")(
            z_tiles, w_glu, norm_w.reshape(1, d))


def _sgu_body(u_ref, v_ref, lnw_ref, lnb_ref, w_ref, bs_ref, nw_ref, o_ref,
              vb_ref, x_ref, *, nc):
    v = _gelu_tanh(v_ref[...])
    mu = jnp.mean(v, axis=-1, keepdims=True)
    vc = v - mu
    var = jnp.mean(vc * vc, axis=-1, keepdims=True)
    vb_ref[...] = (vc * lax.rsqrt(var + LN_EPS) * lnw_ref[...] + lnb_ref[...]).astype(BF16)
    hd = SGU_WIDTH // SGU_HEADS
    for h in range(SGU_HEADS):
        cs = slice(h * hd, (h + 1) * hd)
        rhs = jnp.concatenate(
            [vb_ref[c * SGU_CHUNK:(c + 1) * SGU_CHUNK, cs] for c in range(nc)], axis=1)
        zz = _dot(w_ref[h], rhs)
        for c in range(nc):
            rs = slice(c * SGU_CHUNK, (c + 1) * SGU_CHUNK)
            z = zz[:, c * hd:(c + 1) * hd] + bs_ref[:, cs]
            x_ref[rs, cs] = _gelu_tanh(u_ref[rs, cs]) * z
    x = x_ref[...]
    ms = jnp.mean(x * x, axis=-1, keepdims=True)
    o_ref[...] = (x * lax.rsqrt(ms + RMS_EPS) * nw_ref[...]).astype(BF16)


def _sgu_mixer(proj3, ln_w, ln_b, w_s, b_s, norm_w, tl):
    nb, seq, _ = proj3.shape
    nc = tl // SGU_CHUNK
    hd = SGU_WIDTH // SGU_HEADS
    causal = jnp.tril(jnp.ones((SGU_CHUNK, SGU_CHUNK), dtype=bool))
    w_causal = jnp.where(causal[None], w_s, jnp.zeros_like(w_s)).astype(BF16)
    bias = jnp.repeat(b_s.astype(F32).T, hd, axis=1)
    u_blk = 0
    row = lambda a: a.astype(F32).reshape(1, SGU_WIDTH)
    const = lambda shp: pl.BlockSpec(shp, lambda b, t: (0,) * len(shp))
    body = functools.partial(_sgu_body, nc=nc)
    return pl.pallas_call(
        body,
        out_shape=jax.ShapeDtypeStruct((nb, seq, SGU_WIDTH), BF16),
        grid=(nb, seq // tl),
        in_specs=[pl.BlockSpec((None, tl, SGU_WIDTH), lambda b, t: (b, t, u_blk)),
                  pl.BlockSpec((None, tl, SGU_WIDTH), lambda b, t: (b, t, u_blk + 1)),
                  const((1, SGU_WIDTH)), const((1, SGU_WIDTH)),
                  const((SGU_HEADS, SGU_CHUNK, SGU_CHUNK)),
                  const((SGU_CHUNK, SGU_WIDTH)), const((1, SGU_WIDTH))],
        out_specs=pl.BlockSpec((None, tl, SGU_WIDTH), lambda b, t: (b, t, 0)),
        scratch_shapes=[pltpu.VMEM((tl, SGU_WIDTH), BF16),
                        pltpu.VMEM((tl, SGU_WIDTH), F32)],
        compiler_params=_cparams(("parallel", "parallel")), name="sgu_mixer")(
            proj3, proj3, row(ln_w), row(ln_b), w_causal, bias, row(norm_w))


HG_PAIR = 2
HG_PAIR_W = HG_PAIR * HG_HEAD_DIM
HG_FAST_BLOCK = 32
HG_FAST_MIN_LOG_DECAY = -60.0
_NT = (((1,), (1,)), ((), ()))
_TN = (((0,), (0,)), ((), ()))


def _hgrn_tables():
    c = HG_CHUNK
    t = jnp.arange(c)[:, None]
    r = jnp.arange(c)[None, :]
    mats, masks = [], []
    for lev in range(HG_LEVELS):
        m = 1 << lev
        mid = (t // (2 * m)) * (2 * m) + m
        later = t >= mid
        mats.append(jnp.where(later, (r >= mid) & (r <= t), (r > t) & (r < mid)))
        same = (t // (2 * m)) == (r // (2 * m))
        masks.append(same & later & (r < mid))
    tri = r <= t
    mats.append(tri)
    masks.append(t == r)
    fb = HG_FAST_BLOCK
    fast = [masks[5], ((t // fb) == (r // fb)) & tri]
    pair = lambda m: jnp.tile(m.astype(F32), (1, HG_PAIR))
    return dict(
        tri=tri.astype(BF16),
        mst=jnp.concatenate(mats, axis=0).astype(BF16),
        masks=jnp.stack(masks).astype(F32),
        fmasks=jnp.stack([pair(m) for m in fast]))


def _block_diag(x):
    z = jnp.zeros((x.shape[0], HG_HEAD_DIM), x.dtype)
    top = jnp.concatenate([x[:, :HG_HEAD_DIM], z], axis=1)
    bot = jnp.concatenate([z, x[:, HG_HEAD_DIM:]], axis=1)
    return jnp.concatenate([top, bot], axis=0)


def _rows(v, n):
    return jnp.broadcast_to(v, (n, v.shape[1]))


def _hgrn_finish(o, gv, nw, o_ref, rs):
    for h in range(HG_PAIR):
        cs = slice(h * HG_HEAD_DIM, (h + 1) * HG_HEAD_DIM)
        oh = o[:, cs]
        ms = jnp.mean(oh * oh, axis=-1, keepdims=True)
        gg = gv[:, cs]
        o_ref[rs, cs] = (oh * lax.rsqrt(ms + RMS_EPS) * nw * (gg * _sigmoid(gg))).astype(BF16)


def _hgrn_body(q_ref, f_ref, i_ref, g_ref, par_ref, nw_ref, tri_ref, mst_ref, mask_ref,
               fmask_ref, o_ref, st_ref, qs_ref, kk_ref, lf_ref, b_ref, *, nc):
    c = HG_CHUNK
    hd = HG_HEAD_DIM
    wb = HG_PAIR_W
    t = pl.program_id(2)

    @pl.when(t == 0)
    def _():
        st_ref[...] = jnp.zeros_like(st_ref)

    log_lb = par_ref[0:1, :]
    log_1m_lb = par_ref[1:2, :]
    one_m_lb = par_ref[2:3, :]
    nw = nw_ref[...]
    tri = tri_ref[...]
    row = lax.broadcasted_iota(jnp.int32, (c, wb), 0)

    wmin = None
    for ch in range(nc):
        rs = slice(ch * c, (ch + 1) * c)
        fr = f_ref[rs, :]
        e = jnp.exp(-jnp.abs(fr))
        r = 1.0 / (1.0 + e)
        sig_neg = jnp.where(fr >= 0, e * r, r)
        log_sig = jnp.minimum(fr, 0.0) - jnp.log(1.0 + e)
        y = log_1m_lb + log_sig
        log_f = jnp.maximum(log_lb, y) + jnp.log(1.0 + jnp.exp(-jnp.abs(log_lb - y)))
        hi = log_f.astype(BF16)
        lo = (log_f - hi.astype(F32)).astype(BF16)
        b = _dot(tri, hi) + _dot(tri, lo)
        qv = q_ref[rs, :]
        qs_ref[rs, :] = qv * _sigmoid(qv)
        kk_ref[rs, :] = one_m_lb * sig_neg
        lf_ref[rs, :] = log_f
        b_ref[rs, :] = b
        fb = HG_FAST_BLOCK
        starts = jnp.concatenate(
            [jnp.zeros((fb, wb), F32)]
            + [_rows(b[j * fb - 1:j * fb, :], fb) for j in range(1, c // fb)], axis=0)
        w = b - starts
        wmin = w if wmin is None else jnp.minimum(wmin, w)
    fast = jnp.min(wmin) >= HG_FAST_MIN_LOG_DECAY

    @pl.when(fast)
    def _():
        for ch in range(nc):
            rs = slice(ch * c, (ch + 1) * c)
            b = b_ref[rs, :]
            qs = qs_ref[rs, :]
            kk = kk_ref[rs, :]
            vv = i_ref[rs, :].astype(BF16)
            half = c // 2
            b63 = _rows(b[half - 1:half, :], half)
            zero_half = jnp.zeros((half, wb), BF16)
            q6 = jnp.concatenate(
                [zero_half, (qs[half:, :] * jnp.exp(b[half:, :] - b63)).astype(BF16)], axis=0)
            k6 = jnp.concatenate(
                [(kk[:half, :] * jnp.exp(b63 - b[:half, :])).astype(BF16), zero_half], axis=0)
            scores = lax.dot_general(q6, _block_diag(k6), _NT, preferred_element_type=F32)
            bm = jnp.concatenate([_rows(b[31:32, :], 64), _rows(b[95:96, :], 64)], axis=0)
            a5 = jnp.exp(jnp.where((row % 64) >= 32, b - bm, bm - b))
            fb = HG_FAST_BLOCK
            starts = jnp.concatenate(
                [jnp.zeros((fb, wb), F32)]
                + [_rows(b[j * fb - 1:j * fb, :], fb) for j in range(1, c // fb)], axis=0)
            w = b - starts
            for lev, (aq, ak) in enumerate([(a5, a5), (jnp.exp(w), jnp.exp(-w))]):
                sc = lax.dot_general((qs * aq).astype(BF16), _block_diag((kk * ak).astype(BF16)),
                                     _NT, preferred_element_type=F32)
                scores = scores + fmask_ref[lev] * sc
            st = st_ref[...]
            v_bd = _block_diag(vv)
            o = (_dot(scores.astype(BF16), v_bd)
                 + lax.dot_general((qs * jnp.exp(b)).astype(BF16), st.astype(BF16), _NT,
                                   preferred_element_type=F32))
            bend = b[c - 1:c, :]
            kd = (kk * jnp.exp(bend - b)).astype(BF16)
            st_ref[...] = st * jnp.exp(bend) + lax.dot_general(
                v_bd, _block_diag(kd), _TN, preferred_element_type=F32)
            _hgrn_finish(o, g_ref[rs, :], nw, o_ref, rs)

    @pl.when(jnp.logical_not(fast))
    def _():
        mst = mst_ref[...]

        def chunk(ch, carry):
            rs = pl.ds(pl.multiple_of(ch * c, c), c)
            log_f = lf_ref[rs, :]
            hi = log_f.astype(BF16)
            lo = (log_f - hi.astype(F32)).astype(BF16)
            ee = _dot(mst, hi) + _dot(mst, lo)
            qs = qs_ref[rs, :]
            kk = kk_ref[rs, :]
            vv = i_ref[rs, :].astype(BF16)
            bcum = ee[HG_LEVELS * c:(HG_LEVELS + 1) * c, :]
            outs = []
            for h in range(HG_PAIR):
                cs = slice(h * hd, (h + 1) * hd)
                qh = qs[:, cs]
                kh = kk[:, cs]
                vh = vv[:, cs]
                scores = mask_ref[HG_LEVELS] * jnp.sum(qh * kh, axis=-1, keepdims=True)
                for lev in range(HG_LEVELS):
                    a = jnp.exp(ee[lev * c:(lev + 1) * c, cs])
                    sc = lax.dot_general((qh * a).astype(BF16), (kh * a).astype(BF16), _NT,
                                         preferred_element_type=F32)
                    scores = scores + mask_ref[lev] * sc
                bh = bcum[:, cs]
                st = st_ref[cs, cs]
                outs.append(_dot(scores.astype(BF16), vh)
                            + lax.dot_general((qh * jnp.exp(bh)).astype(BF16), st.astype(BF16),
                                              _NT, preferred_element_type=F32))
                bend = bh[c - 1:c, :]
                kd = (kh * jnp.exp(bend - bh)).astype(BF16)
                st_ref[cs, cs] = st * jnp.exp(bend) + lax.dot_general(
                    vh, kd, _TN, preferred_element_type=F32)
            _hgrn_finish(jnp.concatenate(outs, axis=1), g_ref[rs, :], nw, o_ref, rs)
            return carry

        lax.fori_loop(0, nc, chunk, 0)


def _hgrn_mixer(proj3, lb, norm_w, tl):
    nb, seq, _ = proj3.shape
    nc = tl // HG_CHUNK
    wb = HG_PAIR_W
    q0 = (2 * SGU_WIDTH) // wb
    nblk = HG_WIDTH // wb
    lbf = lb.astype(F32)
    par = jnp.stack([jnp.log(lbf), jnp.log1p(-lbf), 1.0 - lbf])
    tb = _hgrn_tables()
    sec = lambda k: pl.BlockSpec((None, tl, wb), functools.partial(
        lambda b, h, t, k: (b, t, q0 + k * nblk + h), k=k))
    const = lambda a: pl.BlockSpec(a.shape, lambda b, h, t: (0,) * a.ndim)
    body = functools.partial(_hgrn_body, nc=nc)
    return pl.pallas_call(
        body,
        out_shape=jax.ShapeDtypeStruct((nb, seq, HG_WIDTH), BF16),
        grid=(nb, nblk, seq // tl),
        in_specs=[sec(0), sec(1), sec(2), sec(3),
                  pl.BlockSpec((3, wb), lambda b, h, t: (0, h)),
                  pl.BlockSpec((1, HG_HEAD_DIM), lambda b, h, t: (0, 0)),
                  const(tb["tri"]), const(tb["mst"]), const(tb["masks"]),
                  const(tb["fmasks"])],
        out_specs=pl.BlockSpec((None, tl, wb), lambda b, h, t: (b, t, h)),
        scratch_shapes=[pltpu.VMEM((wb, wb), F32)] + [pltpu.VMEM((tl, wb), F32)] * 4,
        compiler_params=_cparams(("parallel", "parallel", "arbitrary")),
        name="hgrn2_mixer")(
            proj3, proj3, proj3, proj3, par,
            norm_w.astype(F32).reshape(1, HG_HEAD_DIM),
            tb["tri"], tb["mst"], tb["masks"], tb["fmasks"])


def kernel(x, p, w_in, s5_lam_re, s5_lam_im, s5_log_step, s5_b_re, s5_b_im, s5_c_re, s5_c_im, s5_d, s5_w_glu, sgu_ln_w, sgu_ln_b, sgu_w, sgu_b, hg_lb_logits, hg_norm_w, norm_a_w, norm_b_w, w_out, ln1_w, ln1_b, w_ffn_in, w_ffn_out, ln2_w, ln2_b, w_ple_in, w_ple_gate, ln3_w, ln3_b):
    nb, seq, d_model = x.shape
    depth = w_in.shape[0]
    t_rows = nb * seq
    proj_w = w_in.shape[2]
    d_ff = w_ffn_out.shape[1]
    alpha = (2.0 * depth) ** 0.25

    tm = _pick(t_rows, (1024, 512, 256, 128))
    tm_wide = _pick(t_rows, (2048, 1024, 512, 256, 128))
    tm_half = _pick(t_rows, (512, 256, 128))
    tm_ln = _pick(t_rows, (512, 256, 128))
    tl_s5 = _pick(seq, (1024, 512, 256, 128))
    tl_sgu = _pick(seq, (512, 256, 128))
    tl_hg = _pick(seq, (1024, 512, 256, 128))
    tn_in = _pick(proj_w - S5_WIDTH, (1024, 512))

    lbs =jnp.cumsum(jax.nn.softmax(hg_lb_logits.astype(F32), axis=0), axis=0)
    lbs = lbs - lbs[0:1]

    x_rows = x.reshape(t_rows, d_model).astype(F32)
    res = _Residual(alpha, h=x_rows)
    h16 = _layer_to_bf16(x_rows.reshape(1, t_rows, d_model), 0)
    p_rows = p.reshape(depth, t_rows, -1)

    def post_norm(xres, w, b):
        h16, mu, rstd = _layer_norm(xres, w, b, tm_ln, final=False)
        return h16, _Residual(alpha, x=xres, mu=mu, rstd=rstd, w=w, b=b)

    w_in16 = _layer_to_bf16(w_in, 0)
    for l in range(depth):
        xa_tiles = _s5_in_proj(h16, w_in16, tm).reshape(-1, nb, seq, LANES)
        proj, wf, wo = _fused_mm(
            [(h16, w_in16, d_model, 0, S5_WIDTH // tn_in)], [],
            lambda accs, ex: [accs[0]], [F32], proj_w - S5_WIDTH, tm, tn_in, "in_proj",
            side_casts=[(w_ffn_in, l), (w_out, l)])
        proj3 = proj.reshape(nb, seq, proj_w - S5_WIDTH)
        prep = _s5_prepare(s5_lam_re[l], s5_lam_im[l], s5_log_step[l], s5_b_re[l],
                           s5_b_im[l], s5_c_re[l], s5_c_im[l], s5_d[l])
        z_tiles = _s5_mixer(xa_tiles, prep, tl_s5).reshape(-1, t_rows, LANES)
        ya = _s5_glu_norm(z_tiles, _layer_to_bf16(s5_w_glu, l), norm_a_w[l], tm_half)
        yb = _sgu_mixer(proj3, sgu_ln_w[l], sgu_ln_b[l], sgu_w[l], sgu_b[l],
                        norm_b_w[l], tl_sgu).reshape(t_rows, SGU_WIDTH)
        yc = _hgrn_mixer(proj3, lbs[l], hg_norm_w[l], tl_hg).reshape(t_rows, HG_WIDTH)
        (xres,) = _fused_mm(
            [(ya, wo, S5_WIDTH, 0, 0), (yb, wo, SGU_WIDTH, 1, 0), (yc, wo, HG_WIDTH, 1, 0)],
            res.extras,
            functools.partial(lambda accs, ex, res: [res(ex) + (accs[0] + accs[1] + accs[2])],
                              res=res),
            [F32], d_model, tm, 1024, "out_proj")
        h16, res = post_norm(xres, ln1_w[l], ln1_b[l])

        hid, wfo, wpg = _fused_mm(
            [(h16, wf, d_model, 0, 0), (h16, wf, d_model, 0, d_ff // 256)], [],
            lambda accs, ex: [accs[0] * _sigmoid(accs[0]) * accs[1]],
            [BF16], d_ff, tm_wide, 256, "ffn_in",
            side_casts=[(w_ffn_out, l), (w_ple_gate, l)])
        xres, *next_w_in = _fused_mm(
            [(hid, wfo, d_ff, 0, 0)], res.extras,
            functools.partial(lambda accs, ex, res: [res(ex) + accs[0]], res=res),
            [F32], d_model, tm_half, 512, "ffn_out",
            side_casts=[(w_in, l + 1)] if l + 1 < depth else [])
        if next_w_in:
            w_in16 = next_w_in[0]
        h16, res = post_norm(xres, ln2_w[l], ln2_b[l])

        p16 = _layer_to_bf16(p_rows, l)
        (xres,) = _fused_mm(
            [(h16, wpg, d_model, 0, 0),
             (p16, _layer_to_bf16(w_ple_in, l), p16.shape[1], 0, 0)], res.extras,
            functools.partial(lambda accs, ex, res: [res(ex) + accs[1] * _sigmoid(accs[0])],
                              res=res),
            [F32], d_model, tm, 1024, "ple")
        if l + 1 < depth:
            h16, res = post_norm(xres, ln3_w[l], ln3_b[l])

    out = _layer_norm(xres, ln3_w[depth - 1], ln3_b[depth - 1], tm_ln, final=True)
    return out.reshape(nb, seq, d_model).astype(x.dtype)
```

```python
import functools
import math

import jax
import jax.numpy as jnp
from jax import lax
from jax.experimental import pallas as pl
from jax.experimental.pallas import tpu as pltpu

F32 = jnp.float32
BF16 = jnp.bfloat16

V7X_VMEM_LIMIT_BYTES = 56 * 1024 * 1024
LANES = 128

LN_EPS = 1e-5
RMS_EPS = 1e-6

S5_WIDTH = 1024
S5_GROUP_CH = 16
S5_STATE = 64
S5_CHUNK = 8
S5_TILE_GROUPS = LANES // S5_GROUP_CH
S5_TILE_STATE = S5_TILE_GROUPS * S5_STATE

SGU_WIDTH = 1024
SGU_CHUNK = 128
SGU_HEADS = 8

HG_WIDTH = 2048
HG_HEAD_DIM = 128
HG_CHUNK = 128
HG_LEVELS = 7


def _cparams(sem):
    return pltpu.CompilerParams(dimension_semantics=sem,
                                vmem_limit_bytes=V7X_VMEM_LIMIT_BYTES)


def _gelu_tanh(x):
    c = math.sqrt(2.0 / math.pi)
    return x * (0.5 * (1.0 + jnp.tanh(c * (x + 0.044715 * (x * x * x)))))


def _sigmoid(x):
    return 1.0 / (1.0 + jnp.exp(-x))


def _dot(a, b):
    return jnp.dot(a, b, preferred_element_type=F32)


def _pick(n, prefs):
    for p in prefs:
        if n % p == 0:
            return p
    return n


CAST_BLOCK_BYTES = 8 * 1024 * 1024


def _cast_body(w_ref, o_ref):
    o_ref[...] = w_ref[...].astype(BF16)


def _layer_to_bf16(w, layer):
    _, rows, cols = w.shape
    fits = [tr for tr in (8192, 4096, 2048, 1024, 512, 256, 128, 64, 32, 16)
            if rows % tr == 0 and tr * cols * 4 <= CAST_BLOCK_BYTES]
    tr = fits[0]
    return pl.pallas_call(
        _cast_body,
        out_shape=jax.ShapeDtypeStruct((rows, cols), BF16),
        grid=(rows // tr,),
        in_specs=[pl.BlockSpec((None, tr, cols), lambda i: (layer, i, 0))],
        out_specs=pl.BlockSpec((tr, cols), lambda i: (i, 0)),
        compiler_params=_cparams(("parallel",)), name="cast_bf16")(w)


MM_SUB_ROWS = 256


def _mm_body(*refs, a_of_dot, n_a, extra_kinds, n_side, epilogue, sub_rows):
    n_dot = len(a_of_dot)
    n_extra = len(extra_kinds)
    n_in = n_a + n_dot + n_extra
    a_refs = refs[:n_a]
    w_refs = refs[n_a:n_a + n_dot]
    extra = refs[n_a + n_dot:n_in]
    side_in = refs[n_in:n_in + n_side]
    out_refs = refs[n_in + n_side:len(refs) - n_side]
    side_out = refs[len(refs) - n_side:]
    for si, so in zip(side_in, side_out):
        so[...] = si[...].astype(BF16)
    rows = out_refs[0].shape[0]
    sub = min(rows, sub_rows)
    for r in range(rows // sub):
        rs = slice(r * sub, (r + 1) * sub)
        accs = [_dot(a_refs[ai][rs, :], w[...]) for ai, w in zip(a_of_dot, w_refs)]
        outs = epilogue(accs, [e[...] if kind == "row" else e[rs, :]
                               for e, kind in zip(extra, extra_kinds)])
        for o_ref, o in zip(out_refs, outs):
            o_ref[rs, :] = o.astype(o_ref.dtype)


BF16_ROW_TILE = 16


def _fused_mm(dots, extras, epilogue, out_dtypes, n_cols, tm, tn, name,
              sub_rows=MM_SUB_ROWS, side_casts=()):
    t_rows = dots[0][0].shape[0]
    grid = (t_rows // tm, n_cols // tn)
    in_specs, args = [], []
    a_of_dot = []
    for a, _, _, _, _ in dots:
        known = [k for k, seen in enumerate(args) if seen is a]
        if known:
            a_of_dot.append(known[0])
            continue
        a_of_dot.append(len(args))
        in_specs.append(pl.BlockSpec((tm, a.shape[1]), lambda i, j: (i, 0)))
        args.append(a)
    n_a = len(args)
    for _, w, rb, ri, co in dots:
        in_specs.append(pl.BlockSpec((rb, tn), functools.partial(
            lambda i, j, ri, co: (ri, j + co), ri=ri, co=co)))
        args.append(w)
    for arr, kind in extras:
        if kind == "tile":
            in_specs.append(pl.BlockSpec((tm, tn), lambda i, j: (i, j)))
        elif kind == "stat":
            in_specs.append(pl.BlockSpec((tm, LANES), lambda i, j: (i, 0)))
        else:
            in_specs.append(pl.BlockSpec((1, tn), lambda i, j: (0, j)))
        args.append(arr)
    out_shape = [jax.ShapeDtypeStruct((t_rows, n_cols), dt) for dt in out_dtypes]
    out_specs = [pl.BlockSpec((tm, tn), lambda i, j: (i, j)) for _ in out_dtypes]
    n_steps = grid[0] * grid[1]
    for stack, layer in side_casts:
        _, rows, cols = stack.shape
        tiles = rows // BF16_ROW_TILE
        n_slabs = max(d for d in range(1, min(tiles, n_steps) + 1) if tiles % d == 0)
        slab = functools.partial(lambda i, j, last: jnp.minimum(i * grid[1] + j, last),
                                 last=n_slabs - 1)
        in_specs.append(pl.BlockSpec(
            (None, rows // n_slabs, cols),
            functools.partial(lambda i, j, layer, slab: (layer, slab(i, j), 0),
                              layer=layer, slab=slab)))
        args.append(stack)
        out_shape.append(jax.ShapeDtypeStruct((rows, cols), BF16))
        out_specs.append(pl.BlockSpec(
            (rows // n_slabs, cols),
            functools.partial(lambda i, j, slab: (slab(i, j), 0), slab=slab)))
    body = functools.partial(_mm_body, a_of_dot=tuple(a_of_dot), n_a=n_a,
                             extra_kinds=tuple(kind for _, kind in extras),
                             n_side=len(side_casts), epilogue=epilogue, sub_rows=sub_rows)
    return pl.pallas_call(
        body, out_shape=out_shape, grid=grid, in_specs=in_specs, out_specs=out_specs,
        compiler_params=_cparams(("parallel", "arbitrary")), name=name)(*args)


def _ln_rows(x):
    mu = jnp.mean(x, axis=-1, keepdims=True)
    xc = x - mu
    var = jnp.mean(xc * xc, axis=-1, keepdims=True)
    return xc, mu, lax.rsqrt(var + LN_EPS)


LN_ROW_GROUP = 8


def _ln_final_body(x_ref, w_ref, b_ref, h_ref):
    w = w_ref[...]
    b = b_ref[...]
    for g in range(x_ref.shape[0] // LN_ROW_GROUP):
        rs = slice(g * LN_ROW_GROUP, (g + 1) * LN_ROW_GROUP)
        xc, _, rstd = _ln_rows(x_ref[rs, :])
        h_ref[rs, :] = xc * rstd * w + b


def _layer_norm_final(x, w, b, tm):
    t_rows, d = x.shape
    rows = pl.BlockSpec((tm, d), lambda i: (i, 0))
    vec = pl.BlockSpec((1, d), lambda i: (0, 0))
    return pl.pallas_call(
        _ln_final_body, out_shape=jax.ShapeDtypeStruct((t_rows, d), F32),
        grid=(t_rows // tm,), in_specs=[rows, vec, vec], out_specs=rows,
        compiler_params=_cparams(("parallel",)), name="layer_norm")(
            x, w.reshape(1, d), b.reshape(1, d))


def _ln_mm_body(*refs, a_of_dot, n_other_a, extra_kinds, n_side, n_out, emit_h16, out_tiles,
                self_alpha, epilogue, sub_rows, n_slab, slab_rows):
    n_dot = len(a_of_dot)
    n_extra = len(extra_kinds)
    x_ref, lnw_ref, lnb_ref = refs[:3]
    pos = 3
    other_a = refs[pos:pos + n_other_a]
    pos += n_other_a
    w_refs = refs[pos:pos + n_dot]
    pos += n_dot
    extra = refs[pos:pos + n_extra]
    pos += n_extra
    side_in = refs[pos:pos + n_side]
    pos += n_side
    out_refs = refs[pos:pos + n_out]
    pos += n_out
    mu_ref, rs_ref = refs[pos:pos + 2]
    pos += 2
    h16_ref = refs[pos] if emit_h16 else None
    pos += int(emit_h16)
    side_out = refs[pos:pos + n_side]
    pos += n_side
    a_slots, mu_slots, rs_slots = refs[pos:pos + 2], refs[pos + 2:pos + 4], refs[pos + 4:pos + 6]

    g = pl.program_id(0)
    j = pl.program_id(1)
    group = 2 * LN_ROW_GROUP

    def side_jobs(slot):
        a_sc, mu_sc, rs_sc = a_slots[slot], mu_slots[slot], rs_slots[slot]
        for si, so in zip(side_in, side_out):
            so[...] = si[...].astype(BF16)
        row0 = pl.multiple_of(jnp.minimum(j, n_slab - 1) * slab_rows, slab_rows)
        w = lnw_ref[...]
        b = lnb_ref[...]
        for grp in range(slab_rows // group):
            ys = []
            for half in range(2):
                r8 = slice(grp * group + half * LN_ROW_GROUP,
                           grp * group + (half + 1) * LN_ROW_GROUP)
                xc, mu, rstd = _ln_rows(x_ref[r8, :])
                ys.append(xc * rstd * w + b)
                mu_b = jnp.broadcast_to(mu, (LN_ROW_GROUP, LANES))
                rs_b = jnp.broadcast_to(rstd, (LN_ROW_GROUP, LANES))
                mu_ref[r8, :] = mu_b
                rs_ref[r8, :] = rs_b
                dst = pl.ds(row0 + grp * group + half * LN_ROW_GROUP, LN_ROW_GROUP)
                mu_sc[dst, :] = mu_b
                rs_sc[dst, :] = rs_b
            y16 = jnp.concatenate(ys, axis=0).astype(BF16)
            a_sc[pl.ds(row0 + grp * group, group), :] = y16
            if emit_h16:
                h16_ref[grp * group:(grp + 1) * group, :] = y16

    def matmul(slot):
        a_sc, mu_sc, rs_sc = a_slots[slot], mu_slots[slot], rs_slots[slot]
        rows = a_sc.shape[0]
        sub = min(rows, sub_rows)
        for r in range(rows // sub):
            rs = slice(r * sub, (r + 1) * sub)
            a_ln = a_sc[rs, :]
            accs = [_dot(a_ln if ai < 0 else other_a[ai][rs, :], w[...])
                    for ai, w in zip(a_of_dot, w_refs)]
            ex = [e[...] if kind == "row" else e[rs, :] for e, kind in zip(extra, extra_kinds)]
            if self_alpha is not None:
                x_tile, w_row, b_row = ex[:3]
                ex = ex[3:]
                rep = x_tile.shape[1] // LANES
                wide = lambda s: jnp.concatenate([s] * rep, axis=1)
                resid = self_alpha * ((x_tile - wide(mu_sc[rs, :]))
                                      * wide(rs_sc[rs, :]) * w_row + b_row)
                outs = epilogue(accs, ex, resid)
            else:
                outs = epilogue(accs, ex)
            for o_ref, o in zip(out_refs, outs):
                if out_tiles:
                    for c in range(o_ref.shape[0]):
                        o_ref[c, rs, :] = o[:, c * LANES:(c + 1) * LANES].astype(o_ref.dtype)
                else:
                    o_ref[rs, :] = o.astype(o_ref.dtype)

    @pl.when(g == 0)
    def _():
        side_jobs(0)

    for parity in (0, 1):
        @pl.when(jnp.logical_and(g > 0, lax.rem(g, 2) == parity))
        def _():
            side_jobs(parity)
            matmul(1 - parity)


def _ln_fused_mm(x, ln_w, ln_b, dots, extras, epilogue, out_dtypes, n_cols, tm, tn, name,
                 sub_rows=MM_SUB_ROWS, side_casts=(), emit_h16=False, out_tiles=False,
                 self_alpha=None):
    t_rows, d = x.shape
    ni, nj = t_rows // tm, n_cols // tn
    grid = (ni + 1, nj)
    n_slab = max(s for s in range(1, nj + 1)
                 if tm % s == 0 and (tm // s) % (2 * LN_ROW_GROUP) == 0)
    slab_rows = tm // n_slab
    prev = lambda g: jnp.maximum(g - 1, 0)
    first_col = lambda g, j: jnp.where(g > 0, j, 0)
    slab_idx = lambda g, j: (jnp.where(g < ni, g * n_slab + jnp.minimum(j, n_slab - 1),
                                       ni * n_slab - 1), 0)
    vec = lambda a: a.astype(F32).reshape(1, d)
    in_specs = [pl.BlockSpec((slab_rows, d), slab_idx),
                pl.BlockSpec((1, d), lambda g, j: (0, 0)),
                pl.BlockSpec((1, d), lambda g, j: (0, 0))]
    args = [x, vec(ln_w), vec(ln_b)]
    a_of_dot, other = [], []
    for a, _, _, _, _ in dots:
        if a is None:
            a_of_dot.append(-1)
            continue
        a_of_dot.append(len(other))
        other.append(a)
        in_specs.append(pl.BlockSpec((tm, a.shape[1]), lambda g, j: (prev(g), 0)))
        args.append(a)
    for _, w, rb, ri, co in dots:
        in_specs.append(pl.BlockSpec((rb, tn), functools.partial(
            lambda g, j, ri, co: (ri, j + co), ri=ri, co=co)))
        args.append(w)
    extras = list(extras)
    if self_alpha is not None:
        extras = [(x, "tile"), (vec(ln_w), "row"), (vec(ln_b), "row")] + extras
    for arr, kind in extras:
        if kind == "tile":
            in_specs.append(pl.BlockSpec((tm, tn), lambda g, j: (prev(g), j)))
        elif kind == "stat":
            in_specs.append(pl.BlockSpec((tm, LANES), lambda g, j: (prev(g), 0)))
        else:
            in_specs.append(pl.BlockSpec((1, tn), lambda g, j: (0, j)))
        args.append(arr)
    if out_tiles:
        n_tiles = n_cols // LANES
        out_shape = [jax.ShapeDtypeStruct((n_tiles, t_rows, LANES), dt) for dt in out_dtypes]
        out_specs = [pl.BlockSpec((tn // LANES, tm, LANES),
                                  lambda g, j: (first_col(g, j), prev(g), 0))
                     for _ in out_dtypes]
    else:
        out_shape = [jax.ShapeDtypeStruct((t_rows, n_cols), dt) for dt in out_dtypes]
        out_specs = [pl.BlockSpec((tm, tn), lambda g, j: (prev(g), first_col(g, j)))
                     for _ in out_dtypes]
    n_out = len(out_shape)
    for _ in range(2):
        out_shape.append(jax.ShapeDtypeStruct((t_rows, LANES), F32))
        out_specs.append(pl.BlockSpec((slab_rows, LANES), slab_idx))
    if emit_h16:
        out_shape.append(jax.ShapeDtypeStruct((t_rows, d), BF16))
        out_specs.append(pl.BlockSpec((slab_rows, d), slab_idx))
    n_steps = grid[0] * grid[1]
    for stack, layer in side_casts:
        _, rows, cols = stack.shape
        tiles = rows // BF16_ROW_TILE
        n_cast = max(s for s in range(1, min(tiles, n_steps) + 1) if tiles % s == 0)
        slab = functools.partial(lambda g, j, last: jnp.minimum(g * nj + j, last),
                                 last=n_cast - 1)
        in_specs.append(pl.BlockSpec(
            (None, rows // n_cast, cols),
            functools.partial(lambda g, j, layer, slab: (layer, slab(g, j), 0),
                              layer=layer, slab=slab)))
        args.append(stack)
        out_shape.append(jax.ShapeDtypeStruct((rows, cols), BF16))
        out_specs.append(pl.BlockSpec(
            (rows // n_cast, cols),
            functools.partial(lambda g, j, slab: (slab(g, j), 0), slab=slab)))
    body = functools.partial(
        _ln_mm_body, a_of_dot=tuple(a_of_dot), n_other_a=len(other),
        extra_kinds=tuple(kind for _, kind in extras), n_side=len(side_casts), n_out=n_out,
        emit_h16=emit_h16, out_tiles=out_tiles, self_alpha=self_alpha, epilogue=epilogue,
        sub_rows=sub_rows, n_slab=n_slab, slab_rows=slab_rows)
    return pl.pallas_call(
        body, out_shape=out_shape, grid=grid, in_specs=in_specs, out_specs=out_specs,
        scratch_shapes=([pltpu.VMEM((tm, d), BF16)] * 2
                        + [pltpu.VMEM((tm, LANES), F32)] * 4),
        compiler_params=_cparams(("arbitrary", "arbitrary")), name=name)(*args)


class _Residual:
    def __init__(self, alpha, h=None, x=None, mu=None, rstd=None, w=None, b=None):
        self.alpha = alpha
        if h is not None:
            self.extras = [(h, "tile")]
        else:
            self.extras = [(x, "tile"), (mu, "stat"), (rstd, "stat"),
                           (w.astype(F32).reshape(1, -1), "row"),
                           (b.astype(F32).reshape(1, -1), "row")]

    def __call__(self, ex):
        if len(ex) == 1:
            return self.alpha * ex[0]
        x, mu, rstd, w, b = ex
        rep = x.shape[1] // LANES
        wide = lambda s: jnp.concatenate([s] * rep, axis=1)
        return self.alpha * ((x - wide(mu)) * wide(rstd) * w + b)


def _s5_prepare(lam_re, lam_im, log_step, b_re, b_im, c_re, c_im, d):
    hp = lax.Precision.HIGHEST
    n_tiles = S5_WIDTH // LANES
    lr = jnp.minimum(lam_re.astype(F32), -1e-4)
    li = lam_im.astype(F32)
    dt = jnp.exp(log_step.astype(F32))[:, None]
    mag = jnp.exp(lr * dt)
    ab_re = mag * jnp.cos(li * dt)
    ab_im = mag * jnp.sin(li * dt)
    den = lr * lr + li * li
    nr = ab_re - 1.0
    g_re = (nr * lr + ab_im * li) / den
    g_im = (ab_im * lr - nr * li) / den
    br = b_re.astype(F32)
    bi = b_im.astype(F32)
    bb_re = g_re[..., None] * br - g_im[..., None] * bi
    bb_im = g_re[..., None] * bi + g_im[..., None] * br
    pr, pi = [jnp.ones_like(ab_re)], [jnp.zeros_like(ab_re)]
    for _ in range(S5_CHUNK):
        pr_new = pr[-1] * ab_re - pi[-1] * ab_im
        pi_new = pr[-1] * ab_im + pi[-1] * ab_re
        pr.append(pr_new)
        pi.append(pi_new)
    p_re = jnp.stack(pr)
    p_im = jnp.stack(pi)
    pb_re = (p_re[:S5_CHUNK, :, :, None] * bb_re[None]
             - p_im[:S5_CHUNK, :, :, None] * bb_im[None])
    pb_im = (p_re[:S5_CHUNK, :, :, None] * bb_im[None]
             + p_im[:S5_CHUNK, :, :, None] * bb_re[None])
    cr = c_re.astype(F32)
    ci = c_im.astype(F32)
    kern = (jnp.einsum("gon,jgni->jgoi", cr, pb_re, precision=hp)
            - jnp.einsum("gon,jgni->jgoi", ci, pb_im, precision=hp))
    tg, gc, ns = S5_TILE_GROUPS, S5_GROUP_CH, S5_STATE

    def spread(x2d, rep, row_group, col_group):
        rows, width = x2d.shape
        sel = jnp.tile(jnp.eye(width, dtype=BF16), (1, rep))
        out = jnp.dot(x2d.astype(BF16), sel)
        rg = row_group(jnp.arange(rows))[:, None]
        cg = col_group(jnp.arange(width * rep))[None, :]
        return jnp.where(rg == cg, out, jnp.zeros_like(out))

    k5 = kern.reshape(S5_CHUNK, n_tiles, tg, gc, gc).transpose(1, 0, 2, 4, 3)
    kcat = spread(k5.reshape(-1, gc), tg, lambda r: (r // gc) % tg,
                  lambda c: c // gc).reshape(n_tiles, S5_CHUNK * LANES, LANES)

    def w1_half(pb):
        pb5 = pb[::-1].reshape(S5_CHUNK, n_tiles, tg, ns, gc).transpose(1, 0, 2, 4, 3)
        return spread(pb5.reshape(-1, ns), tg, lambda r: (r // gc) % tg,
                      lambda c: c // ns).reshape(n_tiles, S5_CHUNK * LANES, S5_TILE_STATE)
    w1 = jnp.concatenate([w1_half(pb_re), w1_half(pb_im)], axis=-1)

    ca_re = cr[None] * p_re[1:, :, None, :] - ci[None] * p_im[1:, :, None, :]
    ca_im = cr[None] * p_im[1:, :, None, :] + ci[None] * p_re[1:, :, None, :]

    def p_half(ca):
        ca5 = ca.reshape(S5_CHUNK, n_tiles, tg, gc, ns).transpose(1, 2, 4, 0, 3)
        x2d = ca5.reshape(-1, S5_CHUNK * gc)
        rows, width = x2d.shape
        cols = jnp.arange(S5_CHUNK * LANES)
        src = jnp.arange(width)
        sel = ((src[:, None] // gc == cols[None, :] // LANES)
               & (src[:, None] % gc == cols[None, :] % gc)).astype(BF16)
        out = jnp.dot(x2d.astype(BF16), sel)
        rg = ((jnp.arange(rows) // ns) % tg)[:, None]
        cg = ((cols // gc) % tg)[None, :]
        return jnp.where(rg == cg, out, jnp.zeros_like(out)).reshape(
            n_tiles, S5_TILE_STATE, S5_CHUNK * LANES)
    pcat = jnp.concatenate([p_half(ca_re), -p_half(ca_im)], axis=1)

    a16 = jnp.concatenate([p_re[S5_CHUNK].reshape(n_tiles, 1, S5_TILE_STATE),
                           p_im[S5_CHUNK].reshape(n_tiles, 1, S5_TILE_STATE)], axis=-1)
    dd = d.astype(F32).reshape(n_tiles, 1, LANES)
    return kcat.astype(BF16), w1.astype(BF16), pcat.astype(BF16), a16, dd


def _s5_body(x_ref, kcat_ref, w1_ref, pcat_ref, a16_ref, d_ref, z_ref,
             state_ref, xcat_ref, xr_ref, q_ref, sin_ref, yc_ref, yint_ref, *, nb, tl):
    t = pl.program_id(1)
    nk = tl // S5_CHUNK
    n = nb * tl
    ts = S5_TILE_STATE

    @pl.when(t == 0)
    def _():
        state_ref[...] = jnp.zeros_like(state_ref)

    for b in range(nb):
        for tp in range(S5_CHUNK):
            xr_ref[tp, pl.ds(b, nk, stride=nb), :] = x_ref[b, pl.ds(tp, nk, stride=S5_CHUNK), :]
    xr = jnp.concatenate([xr_ref[tp].astype(BF16) for tp in range(S5_CHUNK)], axis=1)
    q_ref[...] = _dot(xr, w1_ref[...])

    a_re = a16_ref[:, :ts]
    a_im = a16_ref[:, ts:]
    s = state_ref[...]
    for k in range(nk):
        sin_ref[k * nb:(k + 1) * nb, :] = s
        s_re = s[:, :ts]
        s_im = s[:, ts:]
        s = jnp.concatenate([a_re * s_re - a_im * s_im, a_re * s_im + a_im * s_re],
                            axis=1) + q_ref[k * nb:(k + 1) * nb, :]
    state_ref[...] = s

    yc = _dot(sin_ref[...].astype(BF16), pcat_ref[...])
    for tp in range(S5_CHUNK):
        yc_ref[tp] = yc[:, tp * LANES:(tp + 1) * LANES]
    for b in range(nb):
        for tp in range(S5_CHUNK):
            yint_ref[pl.ds(b * tl + tp, nk, stride=S5_CHUNK), :] = (
                yc_ref[tp, pl.ds(b, nk, stride=nb), :])

    x = x_ref[...].reshape(n, LANES)
    pos = lax.broadcasted_iota(jnp.int32, (n, LANES), 0) % S5_CHUNK
    for j in range(S5_CHUNK):
        xs = x if j == 0 else jnp.where(pos >= j, pltpu.roll(x, j, axis=0), 0.0)
        xcat_ref[:, j * LANES:(j + 1) * LANES] = xs.astype(BF16)
    y = _dot(xcat_ref[...], kcat_ref[...]) + yint_ref[...] + d_ref[...] * x
    z_ref[...] = _gelu_tanh(y).reshape(nb, tl, LANES)


def _s5_in_proj_body(a_ref, w_ref, o_ref):
    rows = a_ref.shape[0]
    sub = min(rows, MM_SUB_ROWS)
    for r in range(rows // sub):
        rs = slice(r * sub, (r + 1) * sub)
        acc = _dot(a_ref[rs, :], w_ref[...])
        for c in range(o_ref.shape[0]):
            o_ref[c, rs, :] = acc[:, c * LANES:(c + 1) * LANES]


def _s5_in_proj(h16, w_in16, tm):
    t_rows, d = h16.shape
    n_tiles = S5_WIDTH // LANES
    return pl.pallas_call(
        _s5_in_proj_body,
        out_shape=jax.ShapeDtypeStruct((n_tiles, t_rows, LANES), F32),
        grid=(t_rows // tm,),
        in_specs=[pl.BlockSpec((tm, d), lambda i: (i, 0)),
                  pl.BlockSpec((d, S5_WIDTH), lambda i: (0, 0))],
        out_specs=pl.BlockSpec((n_tiles, tm, LANES), lambda i: (0, i, 0)),
        compiler_params=_cparams(("parallel",)), name="in_proj_s5")(h16, w_in16)


def _s5_mixer(xa_tiles, prep, tl):
    kcat, w1, pcat, a16, dd = prep
    n_tiles, nb, seq, _ = xa_tiles.shape
    nk = tl // S5_CHUNK
    ts = S5_TILE_STATE
    body = functools.partial(_s5_body, nb=nb, tl=tl)
    wspec = lambda shp: pl.BlockSpec((None,) + shp, lambda j, t: (j, 0, 0))
    xspec = pl.BlockSpec((None, nb, tl, LANES), lambda j, t: (j, 0, t, 0))
    return pl.pallas_call(
        body,
        out_shape=jax.ShapeDtypeStruct(xa_tiles.shape, F32),
        grid=(n_tiles, seq // tl),
        in_specs=[xspec,
                  wspec((S5_CHUNK * LANES, LANES)),
                  wspec((S5_CHUNK * LANES, 2 * ts)),
                  wspec((2 * ts, S5_CHUNK * LANES)),
                  wspec((1, 2 * ts)),
                  wspec((1, LANES))],
        out_specs=xspec,
        scratch_shapes=[pltpu.VMEM((nb, 2 * ts), F32),
                        pltpu.VMEM((nb * tl, S5_CHUNK * LANES), BF16),
                        pltpu.VMEM((S5_CHUNK, nb * nk, LANES), F32),
                        pltpu.VMEM((nb * nk, 2 * ts), F32),
                        pltpu.VMEM((nb * nk, 2 * ts), F32),
                        pltpu.VMEM((S5_CHUNK, nb * nk, LANES), F32),
                        pltpu.VMEM((nb * tl, LANES), F32)],
        compiler_params=_cparams(("parallel", "arbitrary")), name="s5_mixer")(
            xa_tiles, kcat, w1, pcat, a16, dd)


def _glu_body(z_ref, w_ref, nw_ref, o_ref):
    z = jnp.concatenate([z_ref[c] for c in range(z_ref.shape[0])], axis=1)
    y = z * _sigmoid(_dot(z.astype(BF16), w_ref[...]))
    ms = jnp.mean(y * y, axis=-1, keepdims=True)
    o_ref[...] = (y * lax.rsqrt(ms + RMS_EPS) * nw_ref[...]).astype(BF16)


def _s5_glu_norm(z_tiles, w_glu, norm_w, tm):
    n_tiles, t_rows, _ = z_tiles.shape
    d = n_tiles * LANES
    return pl.pallas_call(
        _glu_body,
        out_shape=jax.ShapeDtypeStruct((t_rows, d), BF16),
        grid=(t_rows // tm,),
        in_specs=[pl.BlockSpec((n_tiles, tm, LANES), lambda i: (0, i, 0)),
                  pl.BlockSpec((d, d), lambda i: (0, 0)),
                  pl.BlockSpec((1, d), lambda i: (0, 0))],
        out_specs=pl.BlockSpec((tm, d), lambda i: (i, 0)),
        compiler_params=_cparams(("parallel",)), name="s5_glu_norm")(
            z_tiles, w_glu, norm_w.reshape(1, d))


def _sgu_body(u_ref, v_ref, lnw_ref, lnb_ref, w_ref, bs_ref, nw_ref, o_ref,
              vb_ref, x_ref, *, nc):
    v = _gelu_tanh(v_ref[...])
    mu = jnp.mean(v, axis=-1, keepdims=True)
    vc = v - mu
    var = jnp.mean(vc * vc, axis=-1, keepdims=True)
    vb_ref[...] = (vc * lax.rsqrt(var + LN_EPS) * lnw_ref[...] + lnb_ref[...]).astype(BF16)
    hd = SGU_WIDTH // SGU_HEADS
    for h in range(SGU_HEADS):
        cs = slice(h * hd, (h + 1) * hd)
        rhs = jnp.concatenate(
            [vb_ref[c * SGU_CHUNK:(c + 1) * SGU_CHUNK, cs] for c in range(nc)], axis=1)
        zz = _dot(w_ref[h], rhs)
        for c in range(nc):
            rs = slice(c * SGU_CHUNK, (c + 1) * SGU_CHUNK)
            z = zz[:, c * hd:(c + 1) * hd] + bs_ref[:, cs]
            x_ref[rs, cs] = _gelu_tanh(u_ref[rs, cs]) * z
    x = x_ref[...]
    ms = jnp.mean(x * x, axis=-1, keepdims=True)
    o_ref[...] = (x * lax.rsqrt(ms + RMS_EPS) * nw_ref[...]).astype(BF16)


def _sgu_mixer(proj3, ln_w, ln_b, w_s, b_s, norm_w, tl):
    nb, seq, _ = proj3.shape
    nc = tl // SGU_CHUNK
    hd = SGU_WIDTH // SGU_HEADS
    causal = jnp.tril(jnp.ones((SGU_CHUNK, SGU_CHUNK), dtype=bool))
    w_causal = jnp.where(causal[None], w_s, jnp.zeros_like(w_s)).astype(BF16)
    bias = jnp.repeat(b_s.astype(F32).T, hd, axis=1)
    u_blk = 0
    row = lambda a: a.astype(F32).reshape(1, SGU_WIDTH)
    const = lambda shp: pl.BlockSpec(shp, lambda b, t: (0,) * len(shp))
    body = functools.partial(_sgu_body, nc=nc)
    return pl.pallas_call(
        body,
        out_shape=jax.ShapeDtypeStruct((nb, seq, SGU_WIDTH), BF16),
        grid=(nb, seq // tl),
        in_specs=[pl.BlockSpec((None, tl, SGU_WIDTH), lambda b, t: (b, t, u_blk)),
                  pl.BlockSpec((None, tl, SGU_WIDTH), lambda b, t: (b, t, u_blk + 1)),
                  const((1, SGU_WIDTH)), const((1, SGU_WIDTH)),
                  const((SGU_HEADS, SGU_CHUNK, SGU_CHUNK)),
                  const((SGU_CHUNK, SGU_WIDTH)), const((1, SGU_WIDTH))],
        out_specs=pl.BlockSpec((None, tl, SGU_WIDTH), lambda b, t: (b, t, 0)),
        scratch_shapes=[pltpu.VMEM((tl, SGU_WIDTH), BF16),
                        pltpu.VMEM((tl, SGU_WIDTH), F32)],
        compiler_params=_cparams(("parallel", "parallel")), name="sgu_mixer")(
            proj3, proj3, row(ln_w), row(ln_b), w_causal, bias, row(norm_w))


HG_PAIR = 2
HG_PAIR_W = HG_PAIR * HG_HEAD_DIM
HG_FAST_BLOCK = 32
HG_FAST_MIN_LOG_DECAY = -60.0
_NT = (((1,), (1,)), ((), ()))
_TN = (((0,), (0,)), ((), ()))


def _hgrn_tables():
    c = HG_CHUNK
    t = jnp.arange(c)[:, None]
    r = jnp.arange(c)[None, :]
    mats, masks = [], []
    for lev in range(HG_LEVELS):
        m = 1 << lev
        mid = (t // (2 * m)) * (2 * m) + m
        later = t >= mid
        mats.append(jnp.where(later, (r >= mid) & (r <= t), (r > t) & (r < mid)))
        same = (t // (2 * m)) == (r // (2 * m))
        masks.append(same & later & (r < mid))
    tri = r <= t
    mats.append(tri)
    masks.append(t == r)
    fb = HG_FAST_BLOCK
    fast = [masks[5], ((t // fb) == (r // fb)) & tri]
    pair = lambda m: jnp.tile(m.astype(F32), (1, HG_PAIR))
    return dict(
        tri=tri.astype(BF16),
        mst=jnp.concatenate(mats, axis=0).astype(BF16),
        masks=jnp.stack(masks).astype(F32),
        fmasks=jnp.stack([pair(m) for m in fast]))


def _block_diag(x):
    z = jnp.zeros((x.shape[0], HG_HEAD_DIM), x.dtype)
    top = jnp.concatenate([x[:, :HG_HEAD_DIM], z], axis=1)
    bot = jnp.concatenate([z, x[:, HG_HEAD_DIM:]], axis=1)
    return jnp.concatenate([top, bot], axis=0)


def _rows(v, n):
    return jnp.broadcast_to(v, (n, v.shape[1]))


def _hgrn_finish(o, gv, nw, o_ref, rs):
    for h in range(HG_PAIR):
        cs = slice(h * HG_HEAD_DIM, (h + 1) * HG_HEAD_DIM)
        oh = o[:, cs]
        ms = jnp.mean(oh * oh, axis=-1, keepdims=True)
        gg = gv[:, cs]
        o_ref[rs, cs] = (oh * lax.rsqrt(ms + RMS_EPS) * nw * (gg * _sigmoid(gg))).astype(BF16)


def _hgrn_body(q_ref, f_ref, i_ref, g_ref, par_ref, nw_ref, tri_ref, mst_ref, mask_ref,
               fmask_ref, o_ref, st_ref, qs_ref, kk_ref, lf_ref, b_ref, *, nc):
    c = HG_CHUNK
    hd = HG_HEAD_DIM
    wb = HG_PAIR_W
    t = pl.program_id(2)

    @pl.when(t == 0)
    def _():
        st_ref[...] = jnp.zeros_like(st_ref)

    log_lb = par_ref[0:1, :]
    log_1m_lb = par_ref[1:2, :]
    one_m_lb = par_ref[2:3, :]
    nw = nw_ref[...]
    tri = tri_ref[...]
    row = lax.broadcasted_iota(jnp.int32, (c, wb), 0)

    wmin = None
    for ch in range(nc):
        rs = slice(ch * c, (ch + 1) * c)
        fr = f_ref[rs, :]
        e = jnp.exp(-jnp.abs(fr))
        r = 1.0 / (1.0 + e)
        sig_neg = jnp.where(fr >= 0, e * r, r)
        log_sig = jnp.minimum(fr, 0.0) - jnp.log(1.0 + e)
        y = log_1m_lb + log_sig
        log_f = jnp.maximum(log_lb, y) + jnp.log(1.0 + jnp.exp(-jnp.abs(log_lb - y)))
        hi = log_f.astype(BF16)
        lo = (log_f - hi.astype(F32)).astype(BF16)
        b = _dot(tri, hi) + _dot(tri, lo)
        qv = q_ref[rs, :]
        qs_ref[rs, :] = qv * _sigmoid(qv)
        kk_ref[rs, :] = one_m_lb * sig_neg
        lf_ref[rs, :] = log_f
        b_ref[rs, :] = b
        fb = HG_FAST_BLOCK
        starts = jnp.concatenate(
            [jnp.zeros((fb, wb), F32)]
            + [_rows(b[j * fb - 1:j * fb, :], fb) for j in range(1, c // fb)], axis=0)
        w = b - starts
        wmin = w if wmin is None else jnp.minimum(wmin, w)
    fast = jnp.min(wmin) >= HG_FAST_MIN_LOG_DECAY

    @pl.when(fast)
    def _():
        for ch in range(nc):
            rs = slice(ch * c, (ch + 1) * c)
            b = b_ref[rs, :]
            qs = qs_ref[rs, :]
            kk = kk_ref[rs, :]
            vv = i_ref[rs, :].astype(BF16)
            half = c // 2
            b63 = _rows(b[half - 1:half, :], half)
            zero_half = jnp.zeros((half, wb), BF16)
            q6 = jnp.concatenate(
                [zero_half, (qs[half:, :] * jnp.exp(b[half:, :] - b63)).astype(BF16)], axis=0)
            k6 = jnp.concatenate(
                [(kk[:half, :] * jnp.exp(b63 - b[:half, :])).astype(BF16), zero_half], axis=0)
            scores = lax.dot_general(q6, _block_diag(k6), _NT, preferred_element_type=F32)
            bm = jnp.concatenate([_rows(b[31:32, :], 64), _rows(b[95:96, :], 64)], axis=0)
            a5 = jnp.exp(jnp.where((row % 64) >= 32, b - bm, bm - b))
            fb = HG_FAST_BLOCK
            starts = jnp.concatenate(
                [jnp.zeros((fb, wb), F32)]
                + [_rows(b[j * fb - 1:j * fb, :], fb) for j in range(1, c // fb)], axis=0)
            w = b - starts
            for lev, (aq, ak) in enumerate([(a5, a5), (jnp.exp(w), jnp.exp(-w))]):
                sc = lax.dot_general((qs * aq).astype(BF16), _block_diag((kk * ak).astype(BF16)),
                                     _NT, preferred_element_type=F32)
                scores = scores + fmask_ref[lev] * sc
            st = st_ref[...]
            v_bd = _block_diag(vv)
            o = (_dot(scores.astype(BF16), v_bd)
                 + lax.dot_general((qs * jnp.exp(b)).astype(BF16), st.astype(BF16), _NT,
                                   preferred_element_type=F32))
            bend = b[c - 1:c, :]
            kd = (kk * jnp.exp(bend - b)).astype(BF16)
            st_ref[...] = st * jnp.exp(bend) + lax.dot_general(
                v_bd, _block_diag(kd), _TN, preferred_element_type=F32)
            _hgrn_finish(o, g_ref[rs, :], nw, o_ref, rs)

    @pl.when(jnp.logical_not(fast))
    def _():
        mst = mst_ref[...]

        def chunk(ch, carry):
            rs = pl.ds(pl.multiple_of(ch * c, c), c)
            log_f = lf_ref[rs, :]
            hi = log_f.astype(BF16)
            lo = (log_f - hi.astype(F32)).astype(BF16)
            ee = _dot(mst, hi) + _dot(mst, lo)
            qs = qs_ref[rs, :]
            kk = kk_ref[rs, :]
            vv = i_ref[rs, :].astype(BF16)
            bcum = ee[HG_LEVELS * c:(HG_LEVELS + 1) * c, :]
            outs = []
            for h in range(HG_PAIR):
                cs = slice(h * hd, (h + 1) * hd)
                qh = qs[:, cs]
                kh = kk[:, cs]
                vh = vv[:, cs]
                scores = mask_ref[HG_LEVELS] * jnp.sum(qh * kh, axis=-1, keepdims=True)
                for lev in range(HG_LEVELS):
                    a = jnp.exp(ee[lev * c:(lev + 1) * c, cs])
                    sc = lax.dot_general((qh * a).astype(BF16), (kh * a).astype(BF16), _NT,
                                         preferred_element_type=F32)
                    scores = scores + mask_ref[lev] * sc
                bh = bcum[:, cs]
                st = st_ref[cs, cs]
                outs.append(_dot(scores.astype(BF16), vh)
                            + lax.dot_general((qh * jnp.exp(bh)).astype(BF16), st.astype(BF16),
                                              _NT, preferred_element_type=F32))
                bend = bh[c - 1:c, :]
                kd = (kh * jnp.exp(bend - bh)).astype(BF16)
                st_ref[cs, cs] = st * jnp.exp(bend) + lax.dot_general(
                    vh, kd, _TN, preferred_element_type=F32)
            _hgrn_finish(jnp.concatenate(outs, axis=1), g_ref[rs, :], nw, o_ref, rs)
            return carry

        lax.fori_loop(0, nc, chunk, 0)


def _hgrn_mixer(proj3, lb, norm_w, tl):
    nb, seq, _ = proj3.shape
    nc = tl // HG_CHUNK
    wb = HG_PAIR_W
    q0 = (2 * SGU_WIDTH) // wb
    nblk = HG_WIDTH // wb
    lbf = lb.astype(F32)
    par = jnp.stack([jnp.log(lbf), jnp.log1p(-lbf), 1.0 - lbf])
    tb = _hgrn_tables()
    sec = lambda k: pl.BlockSpec((None, tl, wb), functools.partial(
        lambda b, h, t, k: (b, t, q0 + k * nblk + h), k=k))
    const = lambda a: pl.BlockSpec(a.shape, lambda b, h, t: (0,) * a.ndim)
    body = functools.partial(_hgrn_body, nc=nc)
    return pl.pallas_call(
        body,
        out_shape=jax.ShapeDtypeStruct((nb, seq, HG_WIDTH), BF16),
        grid=(nb, nblk, seq // tl),
        in_specs=[sec(0), sec(1), sec(2), sec(3),
                  pl.BlockSpec((3, wb), lambda b, h, t: (0, h)),
                  pl.BlockSpec((1, HG_HEAD_DIM), lambda b, h, t: (0, 0)),
                  const(tb["tri"]), const(tb["mst"]), const(tb["masks"]),
                  const(tb["fmasks"])],
        out_specs=pl.BlockSpec((None, tl, wb), lambda b, h, t: (b, t, h)),
        scratch_shapes=[pltpu.VMEM((wb, wb), F32)] + [pltpu.VMEM((tl, wb), F32)] * 4,
        compiler_params=_cparams(("parallel", "parallel", "arbitrary")),
        name="hgrn2_mixer")(
            proj3, proj3, proj3, proj3, par,
            norm_w.astype(F32).reshape(1, HG_HEAD_DIM),
            tb["tri"], tb["mst"], tb["masks"], tb["fmasks"])


def kernel(x, p, w_in, s5_lam_re, s5_lam_im, s5_log_step, s5_b_re, s5_b_im, s5_c_re, s5_c_im, s5_d, s5_w_glu, sgu_ln_w, sgu_ln_b, sgu_w, sgu_b, hg_lb_logits, hg_norm_w, norm_a_w, norm_b_w, w_out, ln1_w, ln1_b, w_ffn_in, w_ffn_out, ln2_w, ln2_b, w_ple_in, w_ple_gate, ln3_w, ln3_b):
    nb, seq, d_model = x.shape
    depth = w_in.shape[0]
    t_rows = nb * seq
    proj_w = w_in.shape[2]
    d_ff = w_ffn_out.shape[1]
    alpha = (2.0 * depth) ** 0.25

    tm = _pick(t_rows, (1024, 512, 256, 128))
    tm_wide = _pick(t_rows, (2048, 1024, 512, 256, 128))
    tm_half = _pick(t_rows, (512, 256, 128))
    tm_ln = _pick(t_rows, (512, 256, 128))
    tl_s5 = _pick(seq, (1024, 512, 256, 128))
    tl_sgu = _pick(seq, (512, 256, 128))
    tl_hg = _pick(seq, (1024, 512, 256, 128))
    tn_in = _pick(proj_w - S5_WIDTH, (1024, 512))

    lbs =jnp.cumsum(jax.nn.softmax(hg_lb_logits.astype(F32), axis=0), axis=0)
    lbs = lbs - lbs[0:1]

    x_rows = x.reshape(t_rows, d_model).astype(F32)
    res = _Residual(alpha, h=x_rows)
    h16 = _layer_to_bf16(x_rows.reshape(1, t_rows, d_model), 0)
    p_rows = p.reshape(depth, t_rows, -1)

    w_in16 = _layer_to_bf16(w_in, 0)
    for l in range(depth):
        if l == 0:
            xa_tiles = _s5_in_proj(h16, w_in16, tm)
        else:
            lw, lb = ln3_w[l - 1], ln3_b[l - 1]
            xa_tiles, mu, rstd, h16 = _ln_fused_mm(
                xres, lw, lb, [(None, w_in16, d_model, 0, 0)], [],
                lambda accs, ex: [accs[0]], [F32], S5_WIDTH, tm, S5_WIDTH // 4, "in_proj_s5",
                emit_h16=True, out_tiles=True)
            res = _Residual(alpha, x=xres, mu=mu, rstd=rstd, w=lw, b=lb)
        xa_tiles = xa_tiles.reshape(-1, nb, seq, LANES)
        proj, wf, wo = _fused_mm(
            [(h16, w_in16, d_model, 0, S5_WIDTH // tn_in)], [],
            lambda accs, ex: [accs[0]], [F32], proj_w - S5_WIDTH, tm, tn_in, "in_proj",
            side_casts=[(w_ffn_in, l), (w_out, l)])
        proj3 = proj.reshape(nb, seq, proj_w - S5_WIDTH)
        prep = _s5_prepare(s5_lam_re[l], s5_lam_im[l], s5_log_step[l], s5_b_re[l],
                           s5_b_im[l], s5_c_re[l], s5_c_im[l], s5_d[l])
        z_tiles = _s5_mixer(xa_tiles, prep, tl_s5).reshape(-1, t_rows, LANES)
        ya = _s5_glu_norm(z_tiles, _layer_to_bf16(s5_w_glu, l), norm_a_w[l], tm_half)
        yb = _sgu_mixer(proj3, sgu_ln_w[l], sgu_ln_b[l], sgu_w[l], sgu_b[l],
                        norm_b_w[l], tl_sgu).reshape(t_rows, SGU_WIDTH)
        yc = _hgrn_mixer(proj3, lbs[l], hg_norm_w[l], tl_hg).reshape(t_rows, HG_WIDTH)
        (xres,) = _fused_mm(
            [(ya, wo, S5_WIDTH, 0, 0), (yb, wo, SGU_WIDTH, 1, 0), (yc, wo, HG_WIDTH, 1, 0)],
            res.extras,
            functools.partial(lambda accs, ex, res: [res(ex) + (accs[0] + accs[1] + accs[2])],
                              res=res),
            [F32], d_model, tm, 1024, "out_proj")

        hid, mu, rstd, wfo, wpg = _ln_fused_mm(
            xres, ln1_w[l], ln1_b[l],
            [(None, wf, d_model, 0, 0), (None, wf, d_model, 0, d_ff // 256)], [],
            lambda accs, ex: [accs[0] * _sigmoid(accs[0]) * accs[1]],
            [BF16], d_ff, tm_wide, 256, "ffn_in",
            side_casts=[(w_ffn_out, l), (w_ple_gate, l)])
        res = _Residual(alpha, x=xres, mu=mu, rstd=rstd, w=ln1_w[l], b=ln1_b[l])
        xres, *next_w_in = _fused_mm(
            [(hid, wfo, d_ff, 0, 0)], res.extras,
            functools.partial(lambda accs, ex, res: [res(ex) + accs[0]], res=res),
            [F32], d_model, tm_half, 512, "ffn_out",
            side_casts=[(w_in, l + 1)] if l + 1 < depth else [])
        if next_w_in:
            w_in16 = next_w_in[0]

        p16 = _layer_to_bf16(p_rows, l)
        xres, _, _ = _ln_fused_mm(
            xres, ln2_w[l], ln2_b[l],
            [(None, wpg, d_model, 0, 0),
             (p16, _layer_to_bf16(w_ple_in, l), p16.shape[1], 0, 0)], [],
            lambda accs, ex, resid: [resid + accs[1] * _sigmoid(accs[0])],
            [F32], d_model, tm, 512, "ple", self_alpha=alpha)

    out = _layer_norm_final(xres, ln3_w[depth - 1], ln3_b[depth - 1], tm_ln)
    return out.reshape(nb, seq, d_model).astype(x.dtype)
```

```python
import functools
import math

import jax
import jax.numpy as jnp
from jax import lax
from jax.experimental import pallas as pl
from jax.experimental.pallas import tpu as pltpu

F32 = jnp.float32
BF16 = jnp.bfloat16

V7X_VMEM_LIMIT_BYTES = 56 * 1024 * 1024
LANES = 128

LN_EPS = 1e-5
RMS_EPS = 1e-6

S5_WIDTH = 1024
S5_GROUP_CH = 16
S5_STATE = 64
S5_CHUNK = 8
S5_TILE_GROUPS = LANES // S5_GROUP_CH
S5_TILE_STATE = S5_TILE_GROUPS * S5_STATE

SGU_WIDTH = 1024
SGU_CHUNK = 128
SGU_HEADS = 8

HG_WIDTH = 2048
HG_HEAD_DIM = 128
HG_CHUNK = 128
HG_LEVELS = 7


def _cparams(sem):
    return pltpu.CompilerParams(dimension_semantics=sem,
                                vmem_limit_bytes=V7X_VMEM_LIMIT_BYTES)


def _gelu_tanh(x):
    c = math.sqrt(2.0 / math.pi)
    return x * (0.5 * (1.0 + jnp.tanh(c * (x + 0.044715 * (x * x * x)))))


def _sigmoid(x):
    return 1.0 / (1.0 + jnp.exp(-x))


def _dot(a, b):
    return jnp.dot(a, b, preferred_element_type=F32)


def _pick(n, prefs):
    for p in prefs:
        if n % p == 0:
            return p
    return n


CAST_BLOCK_BYTES = 8 * 1024 * 1024


def _cast_body(w_ref, o_ref):
    o_ref[...] = w_ref[...].astype(BF16)


def _layer_to_bf16(w, layer):
    _, rows, cols = w.shape
    fits = [tr for tr in (8192, 4096, 2048, 1024, 512, 256, 128, 64, 32, 16)
            if rows % tr == 0 and tr * cols * 4 <= CAST_BLOCK_BYTES]
    tr = fits[0]
    return pl.pallas_call(
        _cast_body,
        out_shape=jax.ShapeDtypeStruct((rows, cols), BF16),
        grid=(rows // tr,),
        in_specs=[pl.BlockSpec((None, tr, cols), lambda i: (layer, i, 0))],
        out_specs=pl.BlockSpec((tr, cols), lambda i: (i, 0)),
        compiler_params=_cparams(("parallel",)), name="cast_bf16")(w)


MM_SUB_ROWS = 256


def _mm_body(*refs, a_of_dot, n_a, extra_kinds, n_side, epilogue, sub_rows):
    n_dot = len(a_of_dot)
    n_extra = len(extra_kinds)
    n_in = n_a + n_dot + n_extra
    a_refs = refs[:n_a]
    w_refs = refs[n_a:n_a + n_dot]
    extra = refs[n_a + n_dot:n_in]
    side_in = refs[n_in:n_in + n_side]
    out_refs = refs[n_in + n_side:len(refs) - n_side]
    side_out = refs[len(refs) - n_side:]
    for si, so in zip(side_in, side_out):
        so[...] = si[...].astype(BF16)
    rows = out_refs[0].shape[0]
    sub = min(rows, sub_rows)
    for r in range(rows // sub):
        rs = slice(r * sub, (r + 1) * sub)
        accs = [_dot(a_refs[ai][rs, :], w[...]) for ai, w in zip(a_of_dot, w_refs)]
        outs = epilogue(accs, [e[...] if kind == "row" else e[rs, :]
                               for e, kind in zip(extra, extra_kinds)])
        for o_ref, o in zip(out_refs, outs):
            o_ref[rs, :] = o.astype(o_ref.dtype)


BF16_ROW_TILE = 16


def _fused_mm(dots, extras, epilogue, out_dtypes, n_cols, tm, tn, name,
              sub_rows=MM_SUB_ROWS, side_casts=(), cols_outer=False):
    t_rows = dots[0][0].shape[0]
    ni, nj = t_rows // tm, n_cols // tn
    grid = (nj, ni) if cols_outer else (ni, nj)

    def spec(shape, index_map, **kw):
        if cols_outer:
            return pl.BlockSpec(shape, lambda g0, g1: index_map(g1, g0), **kw)
        return pl.BlockSpec(shape, index_map, **kw)

    w_mode = dict(pipeline_mode=pl.Buffered(1)) if cols_outer else {}
    in_specs, args = [], []
    a_of_dot = []
    for a, _, _, _, _ in dots:
        known = [k for k, seen in enumerate(args) if seen is a]
        if known:
            a_of_dot.append(known[0])
            continue
        a_of_dot.append(len(args))
        in_specs.append(spec((tm, a.shape[1]), lambda i, j: (i, 0)))
        args.append(a)
    n_a = len(args)
    for _, w, rb, ri, co in dots:
        in_specs.append(spec((rb, tn), functools.partial(
            lambda i, j, ri, co: (ri, j + co), ri=ri, co=co), **w_mode))
        args.append(w)
    for arr, kind in extras:
        if kind == "tile":
            in_specs.append(spec((tm, tn), lambda i, j: (i, j)))
        elif kind == "stat":
            in_specs.append(spec((tm, LANES), lambda i, j: (i, 0)))
        else:
            in_specs.append(spec((1, tn), lambda i, j: (0, j)))
        args.append(arr)
    out_shape = [jax.ShapeDtypeStruct((t_rows, n_cols), dt) for dt in out_dtypes]
    out_specs = [spec((tm, tn), lambda i, j: (i, j)) for _ in out_dtypes]
    n_steps = grid[0] * grid[1]
    for stack, layer in side_casts:
        _, rows, cols = stack.shape
        tiles = rows // BF16_ROW_TILE
        n_slabs = max(d for d in range(1, min(tiles, n_steps) + 1) if tiles % d == 0)
        slab = functools.partial(lambda g0, g1, last: jnp.minimum(g0 * grid[1] + g1, last),
                                 last=n_slabs - 1)
        in_specs.append(pl.BlockSpec(
            (None, rows // n_slabs, cols),
            functools.partial(lambda g0, g1, layer, slab: (layer, slab(g0, g1), 0),
                              layer=layer, slab=slab)))
        args.append(stack)
        out_shape.append(jax.ShapeDtypeStruct((rows, cols), BF16))
        out_specs.append(pl.BlockSpec(
            (rows // n_slabs, cols),
            functools.partial(lambda g0, g1, slab: (slab(g0, g1), 0), slab=slab)))
    body = functools.partial(_mm_body, a_of_dot=tuple(a_of_dot), n_a=n_a,
                             extra_kinds=tuple(kind for _, kind in extras),
                             n_side=len(side_casts), epilogue=epilogue, sub_rows=sub_rows)
    return pl.pallas_call(
        body, out_shape=out_shape, grid=grid, in_specs=in_specs, out_specs=out_specs,
        compiler_params=_cparams(("parallel", "arbitrary")), name=name)(*args)


def _ln_rows(x):
    mu = jnp.mean(x, axis=-1, keepdims=True)
    xc = x - mu
    var = jnp.mean(xc * xc, axis=-1, keepdims=True)
    return xc, mu, lax.rsqrt(var + LN_EPS)


LN_ROW_GROUP = 8


def _ln_final_body(x_ref, w_ref, b_ref, h_ref):
    w = w_ref[...]
    b = b_ref[...]
    for g in range(x_ref.shape[0] // LN_ROW_GROUP):
        rs = slice(g * LN_ROW_GROUP, (g + 1) * LN_ROW_GROUP)
        xc, _, rstd = _ln_rows(x_ref[rs, :])
        h_ref[rs, :] = xc * rstd * w + b


def _layer_norm_final(x, w, b, tm):
    t_rows, d = x.shape
    rows = pl.BlockSpec((tm, d), lambda i: (i, 0))
    vec = pl.BlockSpec((1, d), lambda i: (0, 0))
    return pl.pallas_call(
        _ln_final_body, out_shape=jax.ShapeDtypeStruct((t_rows, d), F32),
        grid=(t_rows // tm,), in_specs=[rows, vec, vec], out_specs=rows,
        compiler_params=_cparams(("parallel",)), name="layer_norm")(
            x, w.reshape(1, d), b.reshape(1, d))


def _ln_mm_body(*refs, a_of_dot, n_other_a, extra_kinds, n_side, n_out, emit_h16, out_tiles,
                self_alpha, epilogue, sub_rows, n_slab, slab_rows):
    n_dot = len(a_of_dot)
    n_extra = len(extra_kinds)
    x_ref, lnw_ref, lnb_ref = refs[:3]
    pos = 3
    other_a = refs[pos:pos + n_other_a]
    pos += n_other_a
    w_refs = refs[pos:pos + n_dot]
    pos += n_dot
    extra = refs[pos:pos + n_extra]
    pos += n_extra
    side_in = refs[pos:pos + n_side]
    pos += n_side
    out_refs = refs[pos:pos + n_out]
    pos += n_out
    mu_ref, rs_ref = refs[pos:pos + 2]
    pos += 2
    h16_ref = refs[pos] if emit_h16 else None
    pos += int(emit_h16)
    side_out = refs[pos:pos + n_side]
    pos += n_side
    a_slots, mu_slots, rs_slots = refs[pos:pos + 2], refs[pos + 2:pos + 4], refs[pos + 4:pos + 6]

    g = pl.program_id(0)
    j = pl.program_id(1)
    group = 2 * LN_ROW_GROUP

    def side_jobs(slot):
        a_sc, mu_sc, rs_sc = a_slots[slot], mu_slots[slot], rs_slots[slot]
        for si, so in zip(side_in, side_out):
            so[...] = si[...].astype(BF16)
        row0 = pl.multiple_of(jnp.minimum(j, n_slab - 1) * slab_rows, slab_rows)
        w = lnw_ref[...]
        b = lnb_ref[...]
        for grp in range(slab_rows // group):
            ys = []
            for half in range(2):
                r8 = slice(grp * group + half * LN_ROW_GROUP,
                           grp * group + (half + 1) * LN_ROW_GROUP)
                xc, mu, rstd = _ln_rows(x_ref[r8, :])
                ys.append(xc * rstd * w + b)
                mu_b = jnp.broadcast_to(mu, (LN_ROW_GROUP, LANES))
                rs_b = jnp.broadcast_to(rstd, (LN_ROW_GROUP, LANES))
                mu_ref[r8, :] = mu_b
                rs_ref[r8, :] = rs_b
                dst = pl.ds(row0 + grp * group + half * LN_ROW_GROUP, LN_ROW_GROUP)
                mu_sc[dst, :] = mu_b
                rs_sc[dst, :] = rs_b
            y16 = jnp.concatenate(ys, axis=0).astype(BF16)
            a_sc[pl.ds(row0 + grp * group, group), :] = y16
            if emit_h16:
                h16_ref[grp * group:(grp + 1) * group, :] = y16

    def matmul(slot):
        a_sc, mu_sc, rs_sc = a_slots[slot], mu_slots[slot], rs_slots[slot]
        rows = a_sc.shape[0]
        sub = min(rows, sub_rows)
        for r in range(rows // sub):
            rs = slice(r * sub, (r + 1) * sub)
            a_ln = a_sc[rs, :]
            accs = [_dot(a_ln if ai < 0 else other_a[ai][rs, :], w[...])
                    for ai, w in zip(a_of_dot, w_refs)]
            ex = [e[...] if kind == "row" else e[rs, :] for e, kind in zip(extra, extra_kinds)]
            if self_alpha is not None:
                x_tile, w_row, b_row = ex[:3]
                ex = ex[3:]
                rep = x_tile.shape[1] // LANES
                wide = lambda s: jnp.concatenate([s] * rep, axis=1)
                resid = self_alpha * ((x_tile - wide(mu_sc[rs, :]))
                                      * wide(rs_sc[rs, :]) * w_row + b_row)
                outs = epilogue(accs, ex, resid)
            else:
                outs = epilogue(accs, ex)
            for o_ref, o in zip(out_refs, outs):
                if out_tiles:
                    for c in range(o_ref.shape[0]):
                        o_ref[c, rs, :] = o[:, c * LANES:(c + 1) * LANES].astype(o_ref.dtype)
                else:
                    o_ref[rs, :] = o.astype(o_ref.dtype)

    @pl.when(g == 0)
    def _():
        side_jobs(0)

    for parity in (0, 1):
        @pl.when(jnp.logical_and(g > 0, lax.rem(g, 2) == parity))
        def _():
            side_jobs(parity)
            matmul(1 - parity)


def _ln_fused_mm(x, ln_w, ln_b, dots, extras, epilogue, out_dtypes, n_cols, tm, tn, name,
                 sub_rows=MM_SUB_ROWS, side_casts=(), emit_h16=False, out_tiles=False,
                 self_alpha=None):
    t_rows, d = x.shape
    ni, nj = t_rows // tm, n_cols // tn
    grid = (ni + 1, nj)
    n_slab = max(s for s in range(1, nj + 1)
                 if tm % s == 0 and (tm // s) % (2 * LN_ROW_GROUP) == 0)
    slab_rows = tm // n_slab
    prev = lambda g: jnp.maximum(g - 1, 0)
    first_col = lambda g, j: jnp.where(g > 0, j, 0)
    slab_idx = lambda g, j: (jnp.where(g < ni, g * n_slab + jnp.minimum(j, n_slab - 1),
                                       ni * n_slab - 1), 0)
    vec = lambda a: a.astype(F32).reshape(1, d)
    in_specs = [pl.BlockSpec((slab_rows, d), slab_idx),
                pl.BlockSpec((1, d), lambda g, j: (0, 0)),
                pl.BlockSpec((1, d), lambda g, j: (0, 0))]
    args = [x, vec(ln_w), vec(ln_b)]
    a_of_dot, other = [], []
    for a, _, _, _, _ in dots:
        if a is None:
            a_of_dot.append(-1)
            continue
        a_of_dot.append(len(other))
        other.append(a)
        in_specs.append(pl.BlockSpec((tm, a.shape[1]), lambda g, j: (prev(g), 0)))
        args.append(a)
    for _, w, rb, ri, co in dots:
        in_specs.append(pl.BlockSpec((rb, tn), functools.partial(
            lambda g, j, ri, co: (ri, j + co), ri=ri, co=co)))
        args.append(w)
    extras = list(extras)
    if self_alpha is not None:
        extras = [(x, "tile"), (vec(ln_w), "row"), (vec(ln_b), "row")] + extras
    for arr, kind in extras:
        if kind == "tile":
            in_specs.append(pl.BlockSpec((tm, tn), lambda g, j: (prev(g), j)))
        elif kind == "stat":
            in_specs.append(pl.BlockSpec((tm, LANES), lambda g, j: (prev(g), 0)))
        else:
            in_specs.append(pl.BlockSpec((1, tn), lambda g, j: (0, j)))
        args.append(arr)
    if out_tiles:
        n_tiles = n_cols // LANES
        out_shape = [jax.ShapeDtypeStruct((n_tiles, t_rows, LANES), dt) for dt in out_dtypes]
        out_specs = [pl.BlockSpec((tn // LANES, tm, LANES),
                                  lambda g, j: (first_col(g, j), prev(g), 0))
                     for _ in out_dtypes]
    else:
        out_shape = [jax.ShapeDtypeStruct((t_rows, n_cols), dt) for dt in out_dtypes]
        out_specs = [pl.BlockSpec((tm, tn), lambda g, j: (prev(g), first_col(g, j)))
                     for _ in out_dtypes]
    n_out = len(out_shape)
    for _ in range(2):
        out_shape.append(jax.ShapeDtypeStruct((t_rows, LANES), F32))
        out_specs.append(pl.BlockSpec((slab_rows, LANES), slab_idx))
    if emit_h16:
        out_shape.append(jax.ShapeDtypeStruct((t_rows, d), BF16))
        out_specs.append(pl.BlockSpec((slab_rows, d), slab_idx))
    n_steps = grid[0] * grid[1]
    for stack, layer in side_casts:
        _, rows, cols = stack.shape
        tiles = rows // BF16_ROW_TILE
        n_cast = max(s for s in range(1, min(tiles, n_steps) + 1) if tiles % s == 0)
        slab = functools.partial(lambda g, j, last: jnp.minimum(g * nj + j, last),
                                 last=n_cast - 1)
        in_specs.append(pl.BlockSpec(
            (None, rows // n_cast, cols),
            functools.partial(lambda g, j, layer, slab: (layer, slab(g, j), 0),
                              layer=layer, slab=slab)))
        args.append(stack)
        out_shape.append(jax.ShapeDtypeStruct((rows, cols), BF16))
        out_specs.append(pl.BlockSpec(
            (rows // n_cast, cols),
            functools.partial(lambda g, j, slab: (slab(g, j), 0), slab=slab)))
    body = functools.partial(
        _ln_mm_body, a_of_dot=tuple(a_of_dot), n_other_a=len(other),
        extra_kinds=tuple(kind for _, kind in extras), n_side=len(side_casts), n_out=n_out,
        emit_h16=emit_h16, out_tiles=out_tiles, self_alpha=self_alpha, epilogue=epilogue,
        sub_rows=sub_rows, n_slab=n_slab, slab_rows=slab_rows)
    return pl.pallas_call(
        body, out_shape=out_shape, grid=grid, in_specs=in_specs, out_specs=out_specs,
        scratch_shapes=([pltpu.VMEM((tm, d), BF16)] * 2
                        + [pltpu.VMEM((tm, LANES), F32)] * 4),
        compiler_params=_cparams(("arbitrary", "arbitrary")), name=name)(*args)


class _Residual:
    def __init__(self, alpha, h=None, x=None, mu=None, rstd=None, w=None, b=None):
        self.alpha = alpha
        if h is not None:
            self.extras = [(h, "tile")]
        else:
            self.extras = [(x, "tile"), (mu, "stat"), (rstd, "stat"),
                           (w.astype(F32).reshape(1, -1), "row"),
                           (b.astype(F32).reshape(1, -1), "row")]

    def __call__(self, ex):
        if len(ex) == 1:
            return self.alpha * ex[0]
        x, mu, rstd, w, b = ex
        rep = x.shape[1] // LANES
        wide = lambda s: jnp.concatenate([s] * rep, axis=1)
        return self.alpha * ((x - wide(mu)) * wide(rstd) * w + b)


def _s5_prepare(lam_re, lam_im, log_step, b_re, b_im, c_re, c_im, d):
    hp = lax.Precision.HIGHEST
    n_tiles = S5_WIDTH // LANES
    lr = jnp.minimum(lam_re.astype(F32), -1e-4)
    li = lam_im.astype(F32)
    dt = jnp.exp(log_step.astype(F32))[:, None]
    mag = jnp.exp(lr * dt)
    ab_re = mag * jnp.cos(li * dt)
    ab_im = mag * jnp.sin(li * dt)
    den = lr * lr + li * li
    nr = ab_re - 1.0
    g_re = (nr * lr + ab_im * li) / den
    g_im = (ab_im * lr - nr * li) / den
    br = b_re.astype(F32)
    bi = b_im.astype(F32)
    bb_re = g_re[..., None] * br - g_im[..., None] * bi
    bb_im = g_re[..., None] * bi + g_im[..., None] * br
    pr, pi = [jnp.ones_like(ab_re)], [jnp.zeros_like(ab_re)]
    for _ in range(S5_CHUNK):
        pr_new = pr[-1] * ab_re - pi[-1] * ab_im
        pi_new = pr[-1] * ab_im + pi[-1] * ab_re
        pr.append(pr_new)
        pi.append(pi_new)
    p_re = jnp.stack(pr)
    p_im = jnp.stack(pi)
    pb_re = (p_re[:S5_CHUNK, :, :, None] * bb_re[None]
             - p_im[:S5_CHUNK, :, :, None] * bb_im[None])
    pb_im = (p_re[:S5_CHUNK, :, :, None] * bb_im[None]
             + p_im[:S5_CHUNK, :, :, None] * bb_re[None])
    cr = c_re.astype(F32)
    ci = c_im.astype(F32)
    kern = (jnp.einsum("gon,jgni->jgoi", cr, pb_re, precision=hp)
            - jnp.einsum("gon,jgni->jgoi", ci, pb_im, precision=hp))
    tg, gc, ns = S5_TILE_GROUPS, S5_GROUP_CH, S5_STATE

    def spread(x2d, rep, row_group, col_group):
        rows, width = x2d.shape
        sel = jnp.tile(jnp.eye(width, dtype=BF16), (1, rep))
        out = jnp.dot(x2d.astype(BF16), sel)
        rg = row_group(jnp.arange(rows))[:, None]
        cg = col_group(jnp.arange(width * rep))[None, :]
        return jnp.where(rg == cg, out, jnp.zeros_like(out))

    k5 = kern.reshape(S5_CHUNK, n_tiles, tg, gc, gc).transpose(1, 0, 2, 4, 3)
    kcat = spread(k5.reshape(-1, gc), tg, lambda r: (r // gc) % tg,
                  lambda c: c // gc).reshape(n_tiles, S5_CHUNK * LANES, LANES)

    def w1_half(pb):
        pb5 = pb[::-1].reshape(S5_CHUNK, n_tiles, tg, ns, gc).transpose(1, 0, 2, 4, 3)
        return spread(pb5.reshape(-1, ns), tg, lambda r: (r // gc) % tg,
                      lambda c: c // ns).reshape(n_tiles, S5_CHUNK * LANES, S5_TILE_STATE)
    w1 = jnp.concatenate([w1_half(pb_re), w1_half(pb_im)], axis=-1)

    ca_re = cr[None] * p_re[1:, :, None, :] - ci[None] * p_im[1:, :, None, :]
    ca_im = cr[None] * p_im[1:, :, None, :] + ci[None] * p_re[1:, :, None, :]

    def p_half(ca):
        ca5 = ca.reshape(S5_CHUNK, n_tiles, tg, gc, ns).transpose(1, 2, 4, 0, 3)
        x2d = ca5.reshape(-1, S5_CHUNK * gc)
        rows, width = x2d.shape
        cols = jnp.arange(S5_CHUNK * LANES)
        src = jnp.arange(width)
        sel = ((src[:, None] // gc == cols[None, :] // LANES)
               & (src[:, None] % gc == cols[None, :] % gc)).astype(BF16)
        out = jnp.dot(x2d.astype(BF16), sel)
        rg = ((jnp.arange(rows) // ns) % tg)[:, None]
        cg = ((cols // gc) % tg)[None, :]
        return jnp.where(rg == cg, out, jnp.zeros_like(out)).reshape(
            n_tiles, S5_TILE_STATE, S5_CHUNK * LANES)
    pcat = jnp.concatenate([p_half(ca_re), -p_half(ca_im)], axis=1)

    a16 = jnp.concatenate([p_re[S5_CHUNK].reshape(n_tiles, 1, S5_TILE_STATE),
                           p_im[S5_CHUNK].reshape(n_tiles, 1, S5_TILE_STATE)], axis=-1)
    dd = d.astype(F32).reshape(n_tiles, 1, LANES)
    return kcat.astype(BF16), w1.astype(BF16), pcat.astype(BF16), a16, dd


def _s5_body(x_ref, kcat_ref, w1_ref, pcat_ref, a16_ref, d_ref, z_ref,
             state_ref, xcat_ref, xr_ref, q_ref, sin_ref, yc_ref, yint_ref, *, nb, tl):
    t = pl.program_id(1)
    nk = tl // S5_CHUNK
    n = nb * tl
    ts = S5_TILE_STATE

    @pl.when(t == 0)
    def _():
        state_ref[...] = jnp.zeros_like(state_ref)

    for b in range(nb):
        for tp in range(S5_CHUNK):
            xr_ref[tp, pl.ds(b, nk, stride=nb), :] = x_ref[b, pl.ds(tp, nk, stride=S5_CHUNK), :]
    xr = jnp.concatenate([xr_ref[tp].astype(BF16) for tp in range(S5_CHUNK)], axis=1)
    q_ref[...] = _dot(xr, w1_ref[...])

    a_re = a16_ref[:, :ts]
    a_im = a16_ref[:, ts:]
    s = state_ref[...]
    for k in range(nk):
        sin_ref[k * nb:(k + 1) * nb, :] = s
        s_re = s[:, :ts]
        s_im = s[:, ts:]
        s = jnp.concatenate([a_re * s_re - a_im * s_im, a_re * s_im + a_im * s_re],
                            axis=1) + q_ref[k * nb:(k + 1) * nb, :]
    state_ref[...] = s

    yc = _dot(sin_ref[...].astype(BF16), pcat_ref[...])
    for tp in range(S5_CHUNK):
        yc_ref[tp] = yc[:, tp * LANES:(tp + 1) * LANES]
    for b in range(nb):
        for tp in range(S5_CHUNK):
            yint_ref[pl.ds(b * tl + tp, nk, stride=S5_CHUNK), :] = (
                yc_ref[tp, pl.ds(b, nk, stride=nb), :])

    x = x_ref[...].reshape(n, LANES)
    pos = lax.broadcasted_iota(jnp.int32, (n, LANES), 0) % S5_CHUNK
    for j in range(S5_CHUNK):
        xs = x if j == 0 else jnp.where(pos >= j, pltpu.roll(x, j, axis=0), 0.0)
        xcat_ref[:, j * LANES:(j + 1) * LANES] = xs.astype(BF16)
    y = _dot(xcat_ref[...], kcat_ref[...]) + yint_ref[...] + d_ref[...] * x
    z_ref[...] = _gelu_tanh(y).reshape(nb, tl, LANES)


def _s5_in_proj_body(a_ref, w_ref, o_ref):
    rows = a_ref.shape[0]
    sub = min(rows, MM_SUB_ROWS)
    for r in range(rows // sub):
        rs = slice(r * sub, (r + 1) * sub)
        acc = _dot(a_ref[rs, :], w_ref[...])
        for c in range(o_ref.shape[0]):
            o_ref[c, rs, :] = acc[:, c * LANES:(c + 1) * LANES]


def _s5_in_proj(h16, w_in16, tm):
    t_rows, d = h16.shape
    n_tiles = S5_WIDTH // LANES
    return pl.pallas_call(
        _s5_in_proj_body,
        out_shape=jax.ShapeDtypeStruct((n_tiles, t_rows, LANES), F32),
        grid=(t_rows // tm,),
        in_specs=[pl.BlockSpec((tm, d), lambda i: (i, 0)),
                  pl.BlockSpec((d, S5_WIDTH), lambda i: (0, 0))],
        out_specs=pl.BlockSpec((n_tiles, tm, LANES), lambda i: (0, i, 0)),
        compiler_params=_cparams(("parallel",)), name="in_proj_s5")(h16, w_in16)


def _s5_mixer(xa_tiles, prep, tl):
    kcat, w1, pcat, a16, dd = prep
    n_tiles, nb, seq, _ = xa_tiles.shape
    nk = tl // S5_CHUNK
    ts = S5_TILE_STATE
    body = functools.partial(_s5_body, nb=nb, tl=tl)
    wspec = lambda shp: pl.BlockSpec((None,) + shp, lambda j, t: (j, 0, 0))
    xspec = pl.BlockSpec((None, nb, tl, LANES), lambda j, t: (j, 0, t, 0))
    return pl.pallas_call(
        body,
        out_shape=jax.ShapeDtypeStruct(xa_tiles.shape, F32),
        grid=(n_tiles, seq // tl),
        in_specs=[xspec,
                  wspec((S5_CHUNK * LANES, LANES)),
                  wspec((S5_CHUNK * LANES, 2 * ts)),
                  wspec((2 * ts, S5_CHUNK * LANES)),
                  wspec((1, 2 * ts)),
                  wspec((1, LANES))],
        out_specs=xspec,
        scratch_shapes=[pltpu.VMEM((nb, 2 * ts), F32),
                        pltpu.VMEM((nb * tl, S5_CHUNK * LANES), BF16),
                        pltpu.VMEM((S5_CHUNK, nb * nk, LANES), F32),
                        pltpu.VMEM((nb * nk, 2 * ts), F32),
                        pltpu.VMEM((nb * nk, 2 * ts), F32),
                        pltpu.VMEM((S5_CHUNK, nb * nk, LANES), F32),
                        pltpu.VMEM((nb * tl, LANES), F32)],
        compiler_params=_cparams(("parallel", "arbitrary")), name="s5_mixer")(
            xa_tiles, kcat, w1, pcat, a16, dd)


def _glu_body(z_ref, w_ref, nw_ref, o_ref):
    z = jnp.concatenate([z_ref[c] for c in range(z_ref.shape[0])], axis=1)
    y = z * _sigmoid(_dot(z.astype(BF16), w_ref[...]))
    ms = jnp.mean(y * y, axis=-1, keepdims=True)
    o_ref[...] = (y * lax.rsqrt(ms + RMS_EPS) * nw_ref[...]).astype(BF16)


def _s5_glu_norm(z_tiles, w_glu, norm_w, tm):
    n_tiles, t_rows, _ = z_tiles.shape
    d = n_tiles * LANES
    return pl.pallas_call(
        _glu_body,
        out_shape=jax.ShapeDtypeStruct((t_rows, d), BF16),
        grid=(t_rows // tm,),
        in_specs=[pl.BlockSpec((n_tiles, tm, LANES), lambda i: (0, i, 0)),
                  pl.BlockSpec((d, d), lambda i: (0, 0)),
                  pl.BlockSpec((1, d), lambda i: (0, 0))],
        out_specs=pl.BlockSpec((tm, d), lambda i: (i, 0)),
        compiler_params=_cparams(("parallel",)), name="s5_glu_norm")(
            z_tiles, w_glu, norm_w.reshape(1, d))


def _sgu_body(u_ref, v_ref, lnw_ref, lnb_ref, w_ref, bs_ref, nw_ref, o_ref,
              vb_ref, x_ref, *, nc):
    v = _gelu_tanh(v_ref[...])
    mu = jnp.mean(v, axis=-1, keepdims=True)
    vc = v - mu
    var = jnp.mean(vc * vc, axis=-1, keepdims=True)
    vb_ref[...] = (vc * lax.rsqrt(var + LN_EPS) * lnw_ref[...] + lnb_ref[...]).astype(BF16)
    hd = SGU_WIDTH // SGU_HEADS
    for h in range(SGU_HEADS):
        cs = slice(h * hd, (h + 1) * hd)
        rhs = jnp.concatenate(
            [vb_ref[c * SGU_CHUNK:(c + 1) * SGU_CHUNK, cs] for c in range(nc)], axis=1)
        zz = _dot(w_ref[h], rhs)
        for c in range(nc):
            rs = slice(c * SGU_CHUNK, (c + 1) * SGU_CHUNK)
            z = zz[:, c * hd:(c + 1) * hd] + bs_ref[:, cs]
            x_ref[rs, cs] = _gelu_tanh(u_ref[rs, cs]) * z
    x = x_ref[...]
    ms = jnp.mean(x * x, axis=-1, keepdims=True)
    o_ref[...] = (x * lax.rsqrt(ms + RMS_EPS) * nw_ref[...]).astype(BF16)


def _sgu_mixer(proj3, ln_w, ln_b, w_s, b_s, norm_w, tl):
    nb, seq, _ = proj3.shape
    nc = tl // SGU_CHUNK
    hd = SGU_WIDTH // SGU_HEADS
    causal = jnp.tril(jnp.ones((SGU_CHUNK, SGU_CHUNK), dtype=bool))
    w_causal = jnp.where(causal[None], w_s, jnp.zeros_like(w_s)).astype(BF16)
    bias = jnp.repeat(b_s.astype(F32).T, hd, axis=1)
    u_blk = 0
    row = lambda a: a.astype(F32).reshape(1, SGU_WIDTH)
    const = lambda shp: pl.BlockSpec(shp, lambda b, t: (0,) * len(shp))
    body = functools.partial(_sgu_body, nc=nc)
    return pl.pallas_call(
        body,
        out_shape=jax.ShapeDtypeStruct((nb, seq, SGU_WIDTH), BF16),
        grid=(nb, seq // tl),
        in_specs=[pl.BlockSpec((None, tl, SGU_WIDTH), lambda b, t: (b, t, u_blk)),
                  pl.BlockSpec((None, tl, SGU_WIDTH), lambda b, t: (b, t, u_blk + 1)),
                  const((1, SGU_WIDTH)), const((1, SGU_WIDTH)),
                  const((SGU_HEADS, SGU_CHUNK, SGU_CHUNK)),
                  const((SGU_CHUNK, SGU_WIDTH)), const((1, SGU_WIDTH))],
        out_specs=pl.BlockSpec((None, tl, SGU_WIDTH), lambda b, t: (b, t, 0)),
        scratch_shapes=[pltpu.VMEM((tl, SGU_WIDTH), BF16),
                        pltpu.VMEM((tl, SGU_WIDTH), F32)],
        compiler_params=_cparams(("parallel", "parallel")), name="sgu_mixer")(
            proj3, proj3, row(ln_w), row(ln_b), w_causal, bias, row(norm_w))


HG_PAIR = 2
HG_PAIR_W = HG_PAIR * HG_HEAD_DIM
HG_FAST_BLOCK = 32
HG_FAST_MIN_LOG_DECAY = -60.0
_NT = (((1,), (1,)), ((), ()))
_TN = (((0,), (0,)), ((), ()))


def _hgrn_tables():
    c = HG_CHUNK
    t = jnp.arange(c)[:, None]
    r = jnp.arange(c)[None, :]
    mats, masks = [], []
    for lev in range(HG_LEVELS):
        m = 1 << lev
        mid = (t // (2 * m)) * (2 * m) + m
        later = t >= mid
        mats.append(jnp.where(later, (r >= mid) & (r <= t), (r > t) & (r < mid)))
        same = (t // (2 * m)) == (r // (2 * m))
        masks.append(same & later & (r < mid))
    tri = r <= t
    mats.append(tri)
    masks.append(t == r)
    fb = HG_FAST_BLOCK
    fast = [masks[5], ((t // fb) == (r // fb)) & tri]
    pair = lambda m: jnp.tile(m.astype(F32), (1, HG_PAIR))
    return dict(
        tri=tri.astype(BF16),
        mst=jnp.concatenate(mats, axis=0).astype(BF16),
        masks=jnp.stack(masks).astype(F32),
        fmasks=jnp.stack([pair(m) for m in fast]))


def _block_diag(x):
    z = jnp.zeros((x.shape[0], HG_HEAD_DIM), x.dtype)
    top = jnp.concatenate([x[:, :HG_HEAD_DIM], z], axis=1)
    bot = jnp.concatenate([z, x[:, HG_HEAD_DIM:]], axis=1)
    return jnp.concatenate([top, bot], axis=0)


def _rows(v, n):
    return jnp.broadcast_to(v, (n, v.shape[1]))


def _hgrn_finish(o, gv, nw, o_ref, rs):
    for h in range(HG_PAIR):
        cs = slice(h * HG_HEAD_DIM, (h + 1) * HG_HEAD_DIM)
        oh = o[:, cs]
        ms = jnp.mean(oh * oh, axis=-1, keepdims=True)
        gg = gv[:, cs]
        o_ref[rs, cs] = (oh * lax.rsqrt(ms + RMS_EPS) * nw * (gg * _sigmoid(gg))).astype(BF16)


def _hgrn_body(q_ref, f_ref, i_ref, g_ref, par_ref, nw_ref, tri_ref, mst_ref, mask_ref,
               fmask_ref, o_ref, st_ref, qs_ref, kk_ref, lf_ref, b_ref, *, nc):
    c = HG_CHUNK
    hd = HG_HEAD_DIM
    wb = HG_PAIR_W
    t = pl.program_id(2)

    @pl.when(t == 0)
    def _():
        st_ref[...] = jnp.zeros_like(st_ref)

    log_lb = par_ref[0:1, :]
    log_1m_lb = par_ref[1:2, :]
    one_m_lb = par_ref[2:3, :]
    nw = nw_ref[...]
    tri = tri_ref[...]
    row = lax.broadcasted_iota(jnp.int32, (c, wb), 0)

    wmin = None
    for ch in range(nc):
        rs = slice(ch * c, (ch + 1) * c)
        fr = f_ref[rs, :]
        e = jnp.exp(-jnp.abs(fr))
        r = 1.0 / (1.0 + e)
        sig_neg = jnp.where(fr >= 0, e * r, r)
        log_sig = jnp.minimum(fr, 0.0) - jnp.log(1.0 + e)
        y = log_1m_lb + log_sig
        log_f = jnp.maximum(log_lb, y) + jnp.log(1.0 + jnp.exp(-jnp.abs(log_lb - y)))
        hi = log_f.astype(BF16)
        lo = (log_f - hi.astype(F32)).astype(BF16)
        b = _dot(tri, hi) + _dot(tri, lo)
        qv = q_ref[rs, :]
        qs_ref[rs, :] = qv * _sigmoid(qv)
        kk_ref[rs, :] = one_m_lb * sig_neg
        lf_ref[rs, :] = log_f
        b_ref[rs, :] = b
        fb = HG_FAST_BLOCK
        starts = jnp.concatenate(
            [jnp.zeros((fb, wb), F32)]
            + [_rows(b[j * fb - 1:j * fb, :], fb) for j in range(1, c // fb)], axis=0)
        w = b - starts
        wmin = w if wmin is None else jnp.minimum(wmin, w)
    fast = jnp.min(wmin) >= HG_FAST_MIN_LOG_DECAY

    @pl.when(fast)
    def _():
        for ch in range(nc):
            rs = slice(ch * c, (ch + 1) * c)
            b = b_ref[rs, :]
            qs = qs_ref[rs, :]
            kk = kk_ref[rs, :]
            vv = i_ref[rs, :].astype(BF16)
            half = c // 2
            b63 = _rows(b[half - 1:half, :], half)
            zero_half = jnp.zeros((half, wb), BF16)
            q6 = jnp.concatenate(
                [zero_half, (qs[half:, :] * jnp.exp(b[half:, :] - b63)).astype(BF16)], axis=0)
            k6 = jnp.concatenate(
                [(kk[:half, :] * jnp.exp(b63 - b[:half, :])).astype(BF16), zero_half], axis=0)
            scores = lax.dot_general(q6, _block_diag(k6), _NT, preferred_element_type=F32)
            bm = jnp.concatenate([_rows(b[31:32, :], 64), _rows(b[95:96, :], 64)], axis=0)
            a5 = jnp.exp(jnp.where((row % 64) >= 32, b - bm, bm - b))
            fb = HG_FAST_BLOCK
            starts = jnp.concatenate(
                [jnp.zeros((fb, wb), F32)]
                + [_rows(b[j * fb - 1:j * fb, :], fb) for j in range(1, c // fb)], axis=0)
            w = b - starts
            for lev, (aq, ak) in enumerate([(a5, a5), (jnp.exp(w), jnp.exp(-w))]):
                sc = lax.dot_general((qs * aq).astype(BF16), _block_diag((kk * ak).astype(BF16)),
                                     _NT, preferred_element_type=F32)
                scores = scores + fmask_ref[lev] * sc
            st = st_ref[...]
            v_bd = _block_diag(vv)
            o = (_dot(scores.astype(BF16), v_bd)
                 + lax.dot_general((qs * jnp.exp(b)).astype(BF16), st.astype(BF16), _NT,
                                   preferred_element_type=F32))
            bend = b[c - 1:c, :]
            kd = (kk * jnp.exp(bend - b)).astype(BF16)
            st_ref[...] = st * jnp.exp(bend) + lax.dot_general(
                v_bd, _block_diag(kd), _TN, preferred_element_type=F32)
            _hgrn_finish(o, g_ref[rs, :], nw, o_ref, rs)

    @pl.when(jnp.logical_not(fast))
    def _():
        mst = mst_ref[...]

        def chunk(ch, carry):
            rs = pl.ds(pl.multiple_of(ch * c, c), c)
            log_f = lf_ref[rs, :]
            hi = log_f.astype(BF16)
            lo = (log_f - hi.astype(F32)).astype(BF16)
            ee = _dot(mst, hi) + _dot(mst, lo)
            qs = qs_ref[rs, :]
            kk = kk_ref[rs, :]
            vv = i_ref[rs, :].astype(BF16)
            bcum = ee[HG_LEVELS * c:(HG_LEVELS + 1) * c, :]
            outs = []
            for h in range(HG_PAIR):
                cs = slice(h * hd, (h + 1) * hd)
                qh = qs[:, cs]
                kh = kk[:, cs]
                vh = vv[:, cs]
                scores = mask_ref[HG_LEVELS] * jnp.sum(qh * kh, axis=-1, keepdims=True)
                for lev in range(HG_LEVELS):
                    a = jnp.exp(ee[lev * c:(lev + 1) * c, cs])
                    sc = lax.dot_general((qh * a).astype(BF16), (kh * a).astype(BF16), _NT,
                                         preferred_element_type=F32)
                    scores = scores + mask_ref[lev] * sc
                bh = bcum[:, cs]
                st = st_ref[cs, cs]
                outs.append(_dot(scores.astype(BF16), vh)
                            + lax.dot_general((qh * jnp.exp(bh)).astype(BF16), st.astype(BF16),
                                              _NT, preferred_element_type=F32))
                bend = bh[c - 1:c, :]
                kd = (kh * jnp.exp(bend - bh)).astype(BF16)
                st_ref[cs, cs] = st * jnp.exp(bend) + lax.dot_general(
                    vh, kd, _TN, preferred_element_type=F32)
            _hgrn_finish(jnp.concatenate(outs, axis=1), g_ref[rs, :], nw, o_ref, rs)
            return carry

        lax.fori_loop(0, nc, chunk, 0)


def _hgrn_mixer(proj3, lb, norm_w, tl):
    nb, seq, _ = proj3.shape
    nc = tl // HG_CHUNK
    wb = HG_PAIR_W
    q0 = (2 * SGU_WIDTH) // wb
    nblk = HG_WIDTH // wb
    lbf = lb.astype(F32)
    par = jnp.stack([jnp.log(lbf), jnp.log1p(-lbf), 1.0 - lbf])
    tb = _hgrn_tables()
    sec = lambda k: pl.BlockSpec((None, tl, wb), functools.partial(
        lambda b, h, t, k: (b, t, q0 + k * nblk + h), k=k))
    const = lambda a: pl.BlockSpec(a.shape, lambda b, h, t: (0,) * a.ndim)
    body = functools.partial(_hgrn_body, nc=nc)
    return pl.pallas_call(
        body,
        out_shape=jax.ShapeDtypeStruct((nb, seq, HG_WIDTH), BF16),
        grid=(nb, nblk, seq // tl),
        in_specs=[sec(0), sec(1), sec(2), sec(3),
                  pl.BlockSpec((3, wb), lambda b, h, t: (0, h)),
                  pl.BlockSpec((1, HG_HEAD_DIM), lambda b, h, t: (0, 0)),
                  const(tb["tri"]), const(tb["mst"]), const(tb["masks"]),
                  const(tb["fmasks"])],
        out_specs=pl.BlockSpec((None, tl, wb), lambda b, h, t: (b, t, h)),
        scratch_shapes=[pltpu.VMEM((wb, wb), F32)] + [pltpu.VMEM((tl, wb), F32)] * 4,
        compiler_params=_cparams(("parallel", "parallel", "arbitrary")),
        name="hgrn2_mixer")(
            proj3, proj3, proj3, proj3, par,
            norm_w.astype(F32).reshape(1, HG_HEAD_DIM),
            tb["tri"], tb["mst"], tb["masks"], tb["fmasks"])


def kernel(x, p, w_in, s5_lam_re, s5_lam_im, s5_log_step, s5_b_re, s5_b_im, s5_c_re, s5_c_im, s5_d, s5_w_glu, sgu_ln_w, sgu_ln_b, sgu_w, sgu_b, hg_lb_logits, hg_norm_w, norm_a_w, norm_b_w, w_out, ln1_w, ln1_b, w_ffn_in, w_ffn_out, ln2_w, ln2_b, w_ple_in, w_ple_gate, ln3_w, ln3_b):
    nb, seq, d_model = x.shape
    depth = w_in.shape[0]
    t_rows = nb * seq
    proj_w = w_in.shape[2]
    d_ff = w_ffn_out.shape[1]
    alpha = (2.0 * depth) ** 0.25

    tm = _pick(t_rows, (1024, 512, 256, 128))
    tm_wide = _pick(t_rows, (2048, 1024, 512, 256, 128))
    tm_half = _pick(t_rows, (512, 256, 128))
    tm_ln = _pick(t_rows, (512, 256, 128))
    tl_s5 = _pick(seq, (1024, 512, 256, 128))
    tl_sgu = _pick(seq, (512, 256, 128))
    tl_hg = _pick(seq, (1024, 512, 256, 128))
    tn_in = _pick(proj_w - S5_WIDTH, (1024, 512))

    lbs =jnp.cumsum(jax.nn.softmax(hg_lb_logits.astype(F32), axis=0), axis=0)
    lbs = lbs - lbs[0:1]

    x_rows = x.reshape(t_rows, d_model).astype(F32)
    res = _Residual(alpha, h=x_rows)
    h16 = _layer_to_bf16(x_rows.reshape(1, t_rows, d_model), 0)
    p_rows = p.reshape(depth, t_rows, -1)

    w_in16 = _layer_to_bf16(w_in, 0)
    for l in range(depth):
        if l == 0:
            xa_tiles = _s5_in_proj(h16, w_in16, tm)
        else:
            lw, lb = ln3_w[l - 1], ln3_b[l - 1]
            xa_tiles, mu, rstd, h16 = _ln_fused_mm(
                xres, lw, lb, [(None, w_in16, d_model, 0, 0)], [],
                lambda accs, ex: [accs[0]], [F32], S5_WIDTH, tm, S5_WIDTH // 4, "in_proj_s5",
                emit_h16=True, out_tiles=True)
            res = _Residual(alpha, x=xres, mu=mu, rstd=rstd, w=lw, b=lb)
        xa_tiles = xa_tiles.reshape(-1, nb, seq, LANES)
        proj, wf, wo = _fused_mm(
            [(h16, w_in16, d_model, 0, S5_WIDTH // tn_in)], [],
            lambda accs, ex: [accs[0]], [F32], proj_w - S5_WIDTH, tm, tn_in, "in_proj",
            side_casts=[(w_ffn_in, l), (w_out, l)])
        proj3 = proj.reshape(nb, seq, proj_w - S5_WIDTH)
        prep = _s5_prepare(s5_lam_re[l], s5_lam_im[l], s5_log_step[l], s5_b_re[l],
                           s5_b_im[l], s5_c_re[l], s5_c_im[l], s5_d[l])
        z_tiles = _s5_mixer(xa_tiles, prep, tl_s5).reshape(-1, t_rows, LANES)
        ya = _s5_glu_norm(z_tiles, _layer_to_bf16(s5_w_glu, l), norm_a_w[l], tm_half)
        yb = _sgu_mixer(proj3, sgu_ln_w[l], sgu_ln_b[l], sgu_w[l], sgu_b[l],
                        norm_b_w[l], tl_sgu).reshape(t_rows, SGU_WIDTH)
        yc = _hgrn_mixer(proj3, lbs[l], hg_norm_w[l], tl_hg).reshape(t_rows, HG_WIDTH)
        (xres,) = _fused_mm(
            [(ya, wo, S5_WIDTH, 0, 0), (yb, wo, SGU_WIDTH, 1, 0), (yc, wo, HG_WIDTH, 1, 0)],
            res.extras,
            functools.partial(lambda accs, ex, res: [res(ex) + (accs[0] + accs[1] + accs[2])],
                              res=res),
            [F32], d_model, tm, 1024, "out_proj")

        hid, mu, rstd, wfo, wpg = _ln_fused_mm(
            xres, ln1_w[l], ln1_b[l],
            [(None, wf, d_model, 0, 0), (None, wf, d_model, 0, d_ff // 256)], [],
            lambda accs, ex: [accs[0] * _sigmoid(accs[0]) * accs[1]],
            [BF16], d_ff, tm_wide, 256, "ffn_in",
            side_casts=[(w_ffn_out, l), (w_ple_gate, l)])
        res = _Residual(alpha, x=xres, mu=mu, rstd=rstd, w=ln1_w[l], b=ln1_b[l])
        (xres,) = _fused_mm(
            [(hid, wfo, d_ff, 0, 0)], res.extras,
            functools.partial(lambda accs, ex, res: [res(ex) + accs[0]], res=res),
            [F32], d_model, tm_half, 1024, "ffn_out", cols_outer=True)

        p16 = _layer_to_bf16(p_rows, l)
        xres, _, _, *next_w_in = _ln_fused_mm(
            xres, ln2_w[l], ln2_b[l],
            [(None, wpg, d_model, 0, 0),
             (p16, _layer_to_bf16(w_ple_in, l), p16.shape[1], 0, 0)], [],
            lambda accs, ex, resid: [resid + accs[1] * _sigmoid(accs[0])],
            [F32], d_model, tm, 512, "ple", self_alpha=alpha,
            side_casts=[(w_in, l + 1)] if l + 1 < depth else [])
        if next_w_in:
            w_in16 = next_w_in[0]

    out = _layer_norm_final(xres, ln3_w[depth - 1], ln3_b[depth - 1], tm_ln)
    return out.reshape(nb, seq, d_model).astype(x.dtype)
```

```python
import functools
import math

import jax
import jax.numpy as jnp
from jax import lax
from jax.experimental import pallas as pl
from jax.experimental.pallas import tpu as pltpu

F32 = jnp.float32
BF16 = jnp.bfloat16

V7X_VMEM_LIMIT_BYTES = 56 * 1024 * 1024
LANES = 128

LN_EPS = 1e-5
RMS_EPS = 1e-6

S5_WIDTH = 1024
S5_GROUP_CH = 16
S5_STATE = 64
S5_CHUNK = 8
S5_TILE_GROUPS = LANES // S5_GROUP_CH
S5_TILE_STATE = S5_TILE_GROUPS * S5_STATE

SGU_WIDTH = 1024
SGU_CHUNK = 128
SGU_HEADS = 8

HG_WIDTH = 2048
HG_HEAD_DIM = 128
HG_CHUNK = 128
HG_LEVELS = 7


def _cparams(sem):
    return pltpu.CompilerParams(dimension_semantics=sem,
                                vmem_limit_bytes=V7X_VMEM_LIMIT_BYTES)


def _gelu_tanh(x):
    c = math.sqrt(2.0 / math.pi)
    return x * (0.5 * (1.0 + jnp.tanh(c * (x + 0.044715 * (x * x * x)))))


def _sigmoid(x):
    return 1.0 / (1.0 + jnp.exp(-x))


def _dot(a, b):
    return jnp.dot(a, b, preferred_element_type=F32)


def _pick(n, prefs):
    for p in prefs:
        if n % p == 0:
            return p
    return n


CAST_BLOCK_BYTES = 8 * 1024 * 1024


def _cast_body(w_ref, o_ref):
    o_ref[...] = w_ref[...].astype(BF16)


def _layer_to_bf16(w, layer):
    _, rows, cols = w.shape
    fits = [tr for tr in (8192, 4096, 2048, 1024, 512, 256, 128, 64, 32, 16)
            if rows % tr == 0 and tr * cols * 4 <= CAST_BLOCK_BYTES]
    tr = fits[0]
    return pl.pallas_call(
        _cast_body,
        out_shape=jax.ShapeDtypeStruct((rows, cols), BF16),
        grid=(rows // tr,),
        in_specs=[pl.BlockSpec((None, tr, cols), lambda i: (layer, i, 0))],
        out_specs=pl.BlockSpec((tr, cols), lambda i: (i, 0)),
        compiler_params=_cparams(("parallel",)), name="cast_bf16")(w)


MM_SUB_ROWS = 256


def _mm_body(*refs, a_of_dot, n_a, extra_kinds, n_side, epilogue, sub_rows):
    n_dot = len(a_of_dot)
    n_extra = len(extra_kinds)
    n_in = n_a + n_dot + n_extra
    a_refs = refs[:n_a]
    w_refs = refs[n_a:n_a + n_dot]
    extra = refs[n_a + n_dot:n_in]
    side_in = refs[n_in:n_in + n_side]
    out_refs = refs[n_in + n_side:len(refs) - n_side]
    side_out = refs[len(refs) - n_side:]
    for si, so in zip(side_in, side_out):
        so[...] = si[...].astype(BF16)
    rows = out_refs[0].shape[0]
    sub = min(rows, sub_rows)
    for r in range(rows // sub):
        rs = slice(r * sub, (r + 1) * sub)
        accs = [_dot(a_refs[ai][rs, :], w[...]) for ai, w in zip(a_of_dot, w_refs)]
        outs = epilogue(accs, [e[...] if kind == "row" else e[rs, :]
                               for e, kind in zip(extra, extra_kinds)])
        for o_ref, o in zip(out_refs, outs):
            o_ref[rs, :] = o.astype(o_ref.dtype)


BF16_ROW_TILE = 16


def _fused_mm(dots, extras, epilogue, out_dtypes, n_cols, tm, tn, name,
              sub_rows=MM_SUB_ROWS, side_casts=(), cols_outer=False):
    t_rows = dots[0][0].shape[0]
    ni, nj = t_rows // tm, n_cols // tn
    grid = (nj, ni) if cols_outer else (ni, nj)

    def spec(shape, index_map, **kw):
        if cols_outer:
            return pl.BlockSpec(shape, lambda g0, g1: index_map(g1, g0), **kw)
        return pl.BlockSpec(shape, index_map, **kw)

    w_mode = dict(pipeline_mode=pl.Buffered(1)) if cols_outer else {}
    in_specs, args = [], []
    a_of_dot = []
    for a, _, _, _, _ in dots:
        known = [k for k, seen in enumerate(args) if seen is a]
        if known:
            a_of_dot.append(known[0])
            continue
        a_of_dot.append(len(args))
        in_specs.append(spec((tm, a.shape[1]), lambda i, j: (i, 0)))
        args.append(a)
    n_a = len(args)
    for _, w, rb, ri, co in dots:
        in_specs.append(spec((rb, tn), functools.partial(
            lambda i, j, ri, co: (ri, j + co), ri=ri, co=co), **w_mode))
        args.append(w)
    for arr, kind in extras:
        if kind == "tile":
            in_specs.append(spec((tm, tn), lambda i, j: (i, j)))
        elif kind == "stat":
            in_specs.append(spec((tm, LANES), lambda i, j: (i, 0)))
        else:
            in_specs.append(spec((1, tn), lambda i, j: (0, j)))
        args.append(arr)
    out_shape = [jax.ShapeDtypeStruct((t_rows, n_cols), dt) for dt in out_dtypes]
    out_specs = [spec((tm, tn), lambda i, j: (i, j)) for _ in out_dtypes]
    n_steps = grid[0] * grid[1]
    for stack, layer in side_casts:
        _, rows, cols = stack.shape
        tiles = rows // BF16_ROW_TILE
        n_slabs = max(d for d in range(1, min(tiles, n_steps) + 1) if tiles % d == 0)
        slab = functools.partial(lambda g0, g1, last: jnp.minimum(g0 * grid[1] + g1, last),
                                 last=n_slabs - 1)
        in_specs.append(pl.BlockSpec(
            (None, rows // n_slabs, cols),
            functools.partial(lambda g0, g1, layer, slab: (layer, slab(g0, g1), 0),
                              layer=layer, slab=slab)))
        args.append(stack)
        out_shape.append(jax.ShapeDtypeStruct((rows, cols), BF16))
        out_specs.append(pl.BlockSpec(
            (rows // n_slabs, cols),
            functools.partial(lambda g0, g1, slab: (slab(g0, g1), 0), slab=slab)))
    body = functools.partial(_mm_body, a_of_dot=tuple(a_of_dot), n_a=n_a,
                             extra_kinds=tuple(kind for _, kind in extras),
                             n_side=len(side_casts), epilogue=epilogue, sub_rows=sub_rows)
    return pl.pallas_call(
        body, out_shape=out_shape, grid=grid, in_specs=in_specs, out_specs=out_specs,
        compiler_params=_cparams(("parallel", "arbitrary")), name=name)(*args)


def _ln_rows(x):
    mu = jnp.mean(x, axis=-1, keepdims=True)
    xc = x - mu
    var = jnp.mean(xc * xc, axis=-1, keepdims=True)
    return xc, mu, lax.rsqrt(var + LN_EPS)


LN_ROW_GROUP = 8


def _ln_final_body(x_ref, w_ref, b_ref, h_ref):
    w = w_ref[...]
    b = b_ref[...]
    for g in range(x_ref.shape[0] // LN_ROW_GROUP):
        rs = slice(g * LN_ROW_GROUP, (g + 1) * LN_ROW_GROUP)
        xc, _, rstd = _ln_rows(x_ref[rs, :])
        h_ref[rs, :] = xc * rstd * w + b


def _layer_norm_final(x, w, b, tm):
    t_rows, d = x.shape
    rows = pl.BlockSpec((tm, d), lambda i: (i, 0))
    vec = pl.BlockSpec((1, d), lambda i: (0, 0))
    return pl.pallas_call(
        _ln_final_body, out_shape=jax.ShapeDtypeStruct((t_rows, d), F32),
        grid=(t_rows // tm,), in_specs=[rows, vec, vec], out_specs=rows,
        compiler_params=_cparams(("parallel",)), name="layer_norm")(
            x, w.reshape(1, d), b.reshape(1, d))


def _ln_mm_body(*refs, a_of_dot, n_other_a, extra_kinds, n_side, n_out, emit_h16, out_tiles,
                self_alpha, epilogue, sub_rows, n_slab, slab_rows):
    n_dot = len(a_of_dot)
    n_extra = len(extra_kinds)
    x_ref, lnw_ref, lnb_ref = refs[:3]
    pos = 3
    other_a = refs[pos:pos + n_other_a]
    pos += n_other_a
    w_refs = refs[pos:pos + n_dot]
    pos += n_dot
    extra = refs[pos:pos + n_extra]
    pos += n_extra
    side_in = refs[pos:pos + n_side]
    pos += n_side
    out_refs = refs[pos:pos + n_out]
    pos += n_out
    mu_ref, rs_ref = refs[pos:pos + 2]
    pos += 2
    h16_ref = refs[pos] if emit_h16 else None
    pos += int(emit_h16)
    side_out = refs[pos:pos + n_side]
    pos += n_side
    a_slots, mu_slots, rs_slots = refs[pos:pos + 2], refs[pos + 2:pos + 4], refs[pos + 4:pos + 6]

    g = pl.program_id(0)
    j = pl.program_id(1)
    group = 2 * LN_ROW_GROUP

    def side_jobs(slot):
        a_sc, mu_sc, rs_sc = a_slots[slot], mu_slots[slot], rs_slots[slot]
        for si, so in zip(side_in, side_out):
            so[...] = si[...].astype(BF16)
        row0 = pl.multiple_of(jnp.minimum(j, n_slab - 1) * slab_rows, slab_rows)
        w = lnw_ref[...]
        b = lnb_ref[...]
        for grp in range(slab_rows // group):
            ys = []
            for half in range(2):
                r8 = slice(grp * group + half * LN_ROW_GROUP,
                           grp * group + (half + 1) * LN_ROW_GROUP)
                xc, mu, rstd = _ln_rows(x_ref[r8, :])
                ys.append(xc * rstd * w + b)
                mu_b = jnp.broadcast_to(mu, (LN_ROW_GROUP, LANES))
                rs_b = jnp.broadcast_to(rstd, (LN_ROW_GROUP, LANES))
                mu_ref[r8, :] = mu_b
                rs_ref[r8, :] = rs_b
                dst = pl.ds(row0 + grp * group + half * LN_ROW_GROUP, LN_ROW_GROUP)
                mu_sc[dst, :] = mu_b
                rs_sc[dst, :] = rs_b
            y16 = jnp.concatenate(ys, axis=0).astype(BF16)
            a_sc[pl.ds(row0 + grp * group, group), :] = y16
            if emit_h16:
                h16_ref[grp * group:(grp + 1) * group, :] = y16

    def matmul(slot):
        a_sc, mu_sc, rs_sc = a_slots[slot], mu_slots[slot], rs_slots[slot]
        rows = a_sc.shape[0]
        sub = min(rows, sub_rows)
        for r in range(rows // sub):
            rs = slice(r * sub, (r + 1) * sub)
            a_ln = a_sc[rs, :]
            accs = [_dot(a_ln if ai < 0 else other_a[ai][rs, :], w[...])
                    for ai, w in zip(a_of_dot, w_refs)]
            ex = [e[...] if kind == "row" else e[rs, :] for e, kind in zip(extra, extra_kinds)]
            if self_alpha is not None:
                x_tile, w_row, b_row = ex[:3]
                ex = ex[3:]
                rep = x_tile.shape[1] // LANES
                wide = lambda s: jnp.concatenate([s] * rep, axis=1)
                resid = self_alpha * ((x_tile - wide(mu_sc[rs, :]))
                                      * wide(rs_sc[rs, :]) * w_row + b_row)
                outs = epilogue(accs, ex, resid)
            else:
                outs = epilogue(accs, ex)
            for o_ref, o in zip(out_refs, outs):
                if out_tiles:
                    for c in range(o_ref.shape[0]):
                        o_ref[c, rs, :] = o[:, c * LANES:(c + 1) * LANES].astype(o_ref.dtype)
                else:
                    o_ref[rs, :] = o.astype(o_ref.dtype)

    @pl.when(g == 0)
    def _():
        side_jobs(0)

    for parity in (0, 1):
        @pl.when(jnp.logical_and(g > 0, lax.rem(g, 2) == parity))
        def _():
            side_jobs(parity)
            matmul(1 - parity)


def _ln_fused_mm(x, ln_w, ln_b, dots, extras, epilogue, out_dtypes, n_cols, tm, tn, name,
                 sub_rows=MM_SUB_ROWS, side_casts=(), emit_h16=False, out_tiles=False,
                 self_alpha=None):
    t_rows, d = x.shape
    ni, nj = t_rows // tm, n_cols // tn
    grid = (ni + 1, nj)
    n_slab = max(s for s in range(1, nj + 1)
                 if tm % s == 0 and (tm // s) % (2 * LN_ROW_GROUP) == 0)
    slab_rows = tm // n_slab
    prev = lambda g: jnp.maximum(g - 1, 0)
    first_col = lambda g, j: jnp.where(g > 0, j, 0)
    slab_idx = lambda g, j: (jnp.where(g < ni, g * n_slab + jnp.minimum(j, n_slab - 1),
                                       ni * n_slab - 1), 0)
    vec = lambda a: a.astype(F32).reshape(1, d)
    in_specs = [pl.BlockSpec((slab_rows, d), slab_idx),
                pl.BlockSpec((1, d), lambda g, j: (0, 0)),
                pl.BlockSpec((1, d), lambda g, j: (0, 0))]
    args = [x, vec(ln_w), vec(ln_b)]
    a_of_dot, other = [], []
    for a, _, _, _, _ in dots:
        if a is None:
            a_of_dot.append(-1)
            continue
        a_of_dot.append(len(other))
        other.append(a)
        in_specs.append(pl.BlockSpec((tm, a.shape[1]), lambda g, j: (prev(g), 0)))
        args.append(a)
    for _, w, rb, ri, co in dots:
        in_specs.append(pl.BlockSpec((rb, tn), functools.partial(
            lambda g, j, ri, co: (ri, j + co), ri=ri, co=co)))
        args.append(w)
    extras = list(extras)
    if self_alpha is not None:
        extras = [(x, "tile"), (vec(ln_w), "row"), (vec(ln_b), "row")] + extras
    for arr, kind in extras:
        if kind == "tile":
            in_specs.append(pl.BlockSpec((tm, tn), lambda g, j: (prev(g), j)))
        elif kind == "stat":
            in_specs.append(pl.BlockSpec((tm, LANES), lambda g, j: (prev(g), 0)))
        else:
            in_specs.append(pl.BlockSpec((1, tn), lambda g, j: (0, j)))
        args.append(arr)
    if out_tiles:
        n_tiles = n_cols // LANES
        out_shape = [jax.ShapeDtypeStruct((n_tiles, t_rows, LANES), dt) for dt in out_dtypes]
        out_specs = [pl.BlockSpec((tn // LANES, tm, LANES),
                                  lambda g, j: (first_col(g, j), prev(g), 0))
                     for _ in out_dtypes]
    else:
        out_shape = [jax.ShapeDtypeStruct((t_rows, n_cols), dt) for dt in out_dtypes]
        out_specs = [pl.BlockSpec((tm, tn), lambda g, j: (prev(g), first_col(g, j)))
                     for _ in out_dtypes]
    n_out = len(out_shape)
    for _ in range(2):
        out_shape.append(jax.ShapeDtypeStruct((t_rows, LANES), F32))
        out_specs.append(pl.BlockSpec((slab_rows, LANES), slab_idx))
    if emit_h16:
        out_shape.append(jax.ShapeDtypeStruct((t_rows, d), BF16))
        out_specs.append(pl.BlockSpec((slab_rows, d), slab_idx))
    n_steps = grid[0] * grid[1]
    for stack, layer in side_casts:
        _, rows, cols = stack.shape
        tiles = rows // BF16_ROW_TILE
        n_cast = max(s for s in range(1, min(tiles, n_steps) + 1) if tiles % s == 0)
        slab = functools.partial(lambda g, j, last: jnp.minimum(g * nj + j, last),
                                 last=n_cast - 1)
        in_specs.append(pl.BlockSpec(
            (None, rows // n_cast, cols),
            functools.partial(lambda g, j, layer, slab: (layer, slab(g, j), 0),
                              layer=layer, slab=slab)))
        args.append(stack)
        out_shape.append(jax.ShapeDtypeStruct((rows, cols), BF16))
        out_specs.append(pl.BlockSpec(
            (rows // n_cast, cols),
            functools.partial(lambda g, j, slab: (slab(g, j), 0), slab=slab)))
    body = functools.partial(
        _ln_mm_body, a_of_dot=tuple(a_of_dot), n_other_a=len(other),
        extra_kinds=tuple(kind for _, kind in extras), n_side=len(side_casts), n_out=n_out,
        emit_h16=emit_h16, out_tiles=out_tiles, self_alpha=self_alpha, epilogue=epilogue,
        sub_rows=sub_rows, n_slab=n_slab, slab_rows=slab_rows)
    return pl.pallas_call(
        body, out_shape=out_shape, grid=grid, in_specs=in_specs, out_specs=out_specs,
        scratch_shapes=([pltpu.VMEM((tm, d), BF16)] * 2
                        + [pltpu.VMEM((tm, LANES), F32)] * 4),
        compiler_params=_cparams(("arbitrary", "arbitrary")), name=name)(*args)


class _Residual:
    def __init__(self, alpha, h=None, x=None, mu=None, rstd=None, w=None, b=None):
        self.alpha = alpha
        if h is not None:
            self.extras = [(h, "tile")]
        else:
            self.extras = [(x, "tile"), (mu, "stat"), (rstd, "stat"),
                           (w.astype(F32).reshape(1, -1), "row"),
                           (b.astype(F32).reshape(1, -1), "row")]

    def __call__(self, ex):
        if len(ex) == 1:
            return self.alpha * ex[0]
        x, mu, rstd, w, b = ex
        rep = x.shape[1] // LANES
        wide = lambda s: jnp.concatenate([s] * rep, axis=1)
        return self.alpha * ((x - wide(mu)) * wide(rstd) * w + b)


def _s5_prepare(lam_re, lam_im, log_step, b_re, b_im, c_re, c_im, d):
    hp = lax.Precision.HIGHEST
    n_tiles = S5_WIDTH // LANES
    lr = jnp.minimum(lam_re.astype(F32), -1e-4)
    li = lam_im.astype(F32)
    dt = jnp.exp(log_step.astype(F32))[:, None]
    mag = jnp.exp(lr * dt)
    ab_re = mag * jnp.cos(li * dt)
    ab_im = mag * jnp.sin(li * dt)
    den = lr * lr + li * li
    nr = ab_re - 1.0
    g_re = (nr * lr + ab_im * li) / den
    g_im = (ab_im * lr - nr * li) / den
    br = b_re.astype(F32)
    bi = b_im.astype(F32)
    bb_re = g_re[..., None] * br - g_im[..., None] * bi
    bb_im = g_re[..., None] * bi + g_im[..., None] * br
    pr, pi = [jnp.ones_like(ab_re)], [jnp.zeros_like(ab_re)]
    for _ in range(S5_CHUNK):
        pr_new = pr[-1] * ab_re - pi[-1] * ab_im
        pi_new = pr[-1] * ab_im + pi[-1] * ab_re
        pr.append(pr_new)
        pi.append(pi_new)
    p_re = jnp.stack(pr)
    p_im = jnp.stack(pi)
    pb_re = (p_re[:S5_CHUNK, :, :, None] * bb_re[None]
             - p_im[:S5_CHUNK, :, :, None] * bb_im[None])
    pb_im = (p_re[:S5_CHUNK, :, :, None] * bb_im[None]
             + p_im[:S5_CHUNK, :, :, None] * bb_re[None])
    cr = c_re.astype(F32)
    ci = c_im.astype(F32)
    kern = (jnp.einsum("gon,jgni->jgoi", cr, pb_re, precision=hp)
            - jnp.einsum("gon,jgni->jgoi", ci, pb_im, precision=hp))
    tg, gc, ns = S5_TILE_GROUPS, S5_GROUP_CH, S5_STATE

    def spread(x2d, rep, row_group, col_group):
        rows, width = x2d.shape
        sel = jnp.tile(jnp.eye(width, dtype=BF16), (1, rep))
        out = jnp.dot(x2d.astype(BF16), sel)
        rg = row_group(jnp.arange(rows))[:, None]
        cg = col_group(jnp.arange(width * rep))[None, :]
        return jnp.where(rg == cg, out, jnp.zeros_like(out))

    k5 = kern.reshape(S5_CHUNK, n_tiles, tg, gc, gc).transpose(1, 0, 2, 4, 3)
    kcat = spread(k5.reshape(-1, gc), tg, lambda r: (r // gc) % tg,
                  lambda c: c // gc).reshape(n_tiles, S5_CHUNK * LANES, LANES)

    def w1_half(pb):
        pb5 = pb[::-1].reshape(S5_CHUNK, n_tiles, tg, ns, gc).transpose(1, 0, 2, 4, 3)
        return spread(pb5.reshape(-1, ns), tg, lambda r: (r // gc) % tg,
                      lambda c: c // ns).reshape(n_tiles, S5_CHUNK * LANES, S5_TILE_STATE)
    w1 = jnp.concatenate([w1_half(pb_re), w1_half(pb_im)], axis=-1)

    ca_re = cr[None] * p_re[1:, :, None, :] - ci[None] * p_im[1:, :, None, :]
    ca_im = cr[None] * p_im[1:, :, None, :] + ci[None] * p_re[1:, :, None, :]

    def p_half(ca):
        ca5 = ca.reshape(S5_CHUNK, n_tiles, tg, gc, ns).transpose(1, 2, 4, 0, 3)
        x2d = ca5.reshape(-1, S5_CHUNK * gc)
        rows, width = x2d.shape
        cols = jnp.arange(S5_CHUNK * LANES)
        src = jnp.arange(width)
        sel = ((src[:, None] // gc == cols[None, :] // LANES)
               & (src[:, None] % gc == cols[None, :] % gc)).astype(BF16)
        out = jnp.dot(x2d.astype(BF16), sel)
        rg = ((jnp.arange(rows) // ns) % tg)[:, None]
        cg = ((cols // gc) % tg)[None, :]
        return jnp.where(rg == cg, out, jnp.zeros_like(out)).reshape(
            n_tiles, S5_TILE_STATE, S5_CHUNK * LANES)
    pcat = jnp.concatenate([p_half(ca_re), -p_half(ca_im)], axis=1)

    a16 = jnp.concatenate([p_re[S5_CHUNK].reshape(n_tiles, 1, S5_TILE_STATE),
                           p_im[S5_CHUNK].reshape(n_tiles, 1, S5_TILE_STATE)], axis=-1)
    dd = d.astype(F32).reshape(n_tiles, 1, LANES)
    return kcat, w1, pcat, a16, dd


def _s5_body(x_ref, kcat_ref, w1_ref, pcat_ref, a16_ref, d_ref, z_ref,
             state_ref, xcat_ref, xr_ref, q_ref, sin_ref, yc_ref, yint_ref, *, nb, tl):
    t = pl.program_id(1)
    nk = tl // S5_CHUNK
    n = nb * tl
    ts = S5_TILE_STATE

    @pl.when(t == 0)
    def _():
        state_ref[...] = jnp.zeros_like(state_ref)

    for b in range(nb):
        for tp in range(S5_CHUNK):
            xr_ref[tp, pl.ds(b, nk, stride=nb), :] = x_ref[b, pl.ds(tp, nk, stride=S5_CHUNK), :]
    xr = jnp.concatenate([xr_ref[tp].astype(BF16) for tp in range(S5_CHUNK)], axis=1)
    q_ref[...] = _dot(xr, w1_ref[...])

    a_re = a16_ref[:, :ts]
    a_im = a16_ref[:, ts:]
    s = state_ref[...]
    for k in range(nk):
        sin_ref[k * nb:(k + 1) * nb, :] = s
        s_re = s[:, :ts]
        s_im = s[:, ts:]
        s = jnp.concatenate([a_re * s_re - a_im * s_im, a_re * s_im + a_im * s_re],
                            axis=1) + q_ref[k * nb:(k + 1) * nb, :]
    state_ref[...] = s

    yc = _dot(sin_ref[...].astype(BF16), pcat_ref[...])
    for tp in range(S5_CHUNK):
        yc_ref[tp] = yc[:, tp * LANES:(tp + 1) * LANES]
    for b in range(nb):
        for tp in range(S5_CHUNK):
            yint_ref[pl.ds(b * tl + tp, nk, stride=S5_CHUNK), :] = (
                yc_ref[tp, pl.ds(b, nk, stride=nb), :])

    x = x_ref[...].reshape(n, LANES)
    pos = lax.broadcasted_iota(jnp.int32, (n, LANES), 0) % S5_CHUNK
    for j in range(S5_CHUNK):
        xs = x if j == 0 else jnp.where(pos >= j, pltpu.roll(x, j, axis=0), 0.0)
        xcat_ref[:, j * LANES:(j + 1) * LANES] = xs.astype(BF16)
    y = _dot(xcat_ref[...], kcat_ref[...]) + yint_ref[...] + d_ref[...] * x
    z_ref[...] = _gelu_tanh(y).reshape(nb, tl, LANES)


def _s5_in_proj_body(a_ref, w_ref, o_ref):
    rows = a_ref.shape[0]
    sub = min(rows, MM_SUB_ROWS)
    for r in range(rows // sub):
        rs = slice(r * sub, (r + 1) * sub)
        acc = _dot(a_ref[rs, :], w_ref[...])
        for c in range(o_ref.shape[0]):
            o_ref[c, rs, :] = acc[:, c * LANES:(c + 1) * LANES]


def _s5_in_proj(h16, w_in16, tm):
    t_rows, d = h16.shape
    n_tiles = S5_WIDTH // LANES
    return pl.pallas_call(
        _s5_in_proj_body,
        out_shape=jax.ShapeDtypeStruct((n_tiles, t_rows, LANES), F32),
        grid=(t_rows // tm,),
        in_specs=[pl.BlockSpec((tm, d), lambda i: (i, 0)),
                  pl.BlockSpec((d, S5_WIDTH), lambda i: (0, 0))],
        out_specs=pl.BlockSpec((n_tiles, tm, LANES), lambda i: (0, i, 0)),
        compiler_params=_cparams(("parallel",)), name="in_proj_s5")(h16, w_in16)


def _s5_mixer(xa_tiles, prep, tl):
    kcat, w1, pcat, a16, dd = prep
    n_tiles, nb, seq, _ = xa_tiles.shape
    nk = tl // S5_CHUNK
    ts = S5_TILE_STATE
    body = functools.partial(_s5_body, nb=nb, tl=tl)
    wspec = lambda shp: pl.BlockSpec((None,) + shp, lambda j, t: (j, 0, 0))
    xspec = pl.BlockSpec((None, nb, tl, LANES), lambda j, t: (j, 0, t, 0))
    return pl.pallas_call(
        body,
        out_shape=jax.ShapeDtypeStruct(xa_tiles.shape, F32),
        grid=(n_tiles, seq // tl),
        in_specs=[xspec,
                  wspec((S5_CHUNK * LANES, LANES)),
                  wspec((S5_CHUNK * LANES, 2 * ts)),
                  wspec((2 * ts, S5_CHUNK * LANES)),
                  wspec((1, 2 * ts)),
                  wspec((1, LANES))],
        out_specs=xspec,
        scratch_shapes=[pltpu.VMEM((nb, 2 * ts), F32),
                        pltpu.VMEM((nb * tl, S5_CHUNK * LANES), BF16),
                        pltpu.VMEM((S5_CHUNK, nb * nk, LANES), F32),
                        pltpu.VMEM((nb * nk, 2 * ts), F32),
                        pltpu.VMEM((nb * nk, 2 * ts), F32),
                        pltpu.VMEM((S5_CHUNK, nb * nk, LANES), F32),
                        pltpu.VMEM((nb * tl, LANES), F32)],
        compiler_params=_cparams(("parallel", "arbitrary")), name="s5_mixer")(
            xa_tiles, kcat, w1, pcat, a16, dd)


def _glu_body(z_ref, w_ref, nw_ref, o_ref):
    z = jnp.concatenate([z_ref[c] for c in range(z_ref.shape[0])], axis=1)
    y = z * _sigmoid(_dot(z.astype(BF16), w_ref[...]))
    ms = jnp.mean(y * y, axis=-1, keepdims=True)
    o_ref[...] = (y * lax.rsqrt(ms + RMS_EPS) * nw_ref[...]).astype(BF16)


def _s5_glu_norm(z_tiles, w_glu, norm_w, tm):
    n_tiles, t_rows, _ = z_tiles.shape
    d = n_tiles * LANES
    return pl.pallas_call(
        _glu_body,
        out_shape=jax.ShapeDtypeStruct((t_rows, d), BF16),
        grid=(t_rows // tm,),
        in_specs=[pl.BlockSpec((n_tiles, tm, LANES), lambda i: (0, i, 0)),
                  pl.BlockSpec((d, d), lambda i: (0, 0)),
                  pl.BlockSpec((1, d), lambda i: (0, 0))],
        out_specs=pl.BlockSpec((tm, d), lambda i: (i, 0)),
        compiler_params=_cparams(("parallel",)), name="s5_glu_norm")(
            z_tiles, w_glu, norm_w.reshape(1, d))


def _sgu_body(u_ref, v_ref, lnw_ref, lnb_ref, w_ref, bs_ref, nw_ref, o_ref,
              vb_ref, x_ref, *, nc):
    v = _gelu_tanh(v_ref[...])
    mu = jnp.mean(v, axis=-1, keepdims=True)
    vc = v - mu
    var = jnp.mean(vc * vc, axis=-1, keepdims=True)
    vb_ref[...] = (vc * lax.rsqrt(var + LN_EPS) * lnw_ref[...] + lnb_ref[...]).astype(BF16)
    hd = SGU_WIDTH // SGU_HEADS
    for h in range(SGU_HEADS):
        cs = slice(h * hd, (h + 1) * hd)
        rhs = jnp.concatenate(
            [vb_ref[c * SGU_CHUNK:(c + 1) * SGU_CHUNK, cs] for c in range(nc)], axis=1)
        zz = _dot(w_ref[h], rhs)
        for c in range(nc):
            rs = slice(c * SGU_CHUNK, (c + 1) * SGU_CHUNK)
            z = zz[:, c * hd:(c + 1) * hd] + bs_ref[:, cs]
            x_ref[rs, cs] = _gelu_tanh(u_ref[rs, cs]) * z
    x = x_ref[...]
    ms = jnp.mean(x * x, axis=-1, keepdims=True)
    o_ref[...] = (x * lax.rsqrt(ms + RMS_EPS) * nw_ref[...]).astype(BF16)


def _sgu_mixer(proj3, ln_w, ln_b, w_s, b_s, norm_w, tl):
    nb, seq, _ = proj3.shape
    nc = tl // SGU_CHUNK
    hd = SGU_WIDTH // SGU_HEADS
    causal = jnp.tril(jnp.ones((SGU_CHUNK, SGU_CHUNK), dtype=bool))
    w_causal = jnp.where(causal[None], w_s, jnp.zeros_like(w_s)).astype(BF16)
    bias = jnp.repeat(b_s.astype(F32).T, hd, axis=1)
    u_blk = 0
    row = lambda a: a.astype(F32).reshape(1, SGU_WIDTH)
    const = lambda shp: pl.BlockSpec(shp, lambda b, t: (0,) * len(shp))
    body = functools.partial(_sgu_body, nc=nc)
    return pl.pallas_call(
        body,
        out_shape=jax.ShapeDtypeStruct((nb, seq, SGU_WIDTH), BF16),
        grid=(nb, seq // tl),
        in_specs=[pl.BlockSpec((None, tl, SGU_WIDTH), lambda b, t: (b, t, u_blk)),
                  pl.BlockSpec((None, tl, SGU_WIDTH), lambda b, t: (b, t, u_blk + 1)),
                  const((1, SGU_WIDTH)), const((1, SGU_WIDTH)),
                  const((SGU_HEADS, SGU_CHUNK, SGU_CHUNK)),
                  const((SGU_CHUNK, SGU_WIDTH)), const((1, SGU_WIDTH))],
        out_specs=pl.BlockSpec((None, tl, SGU_WIDTH), lambda b, t: (b, t, 0)),
        scratch_shapes=[pltpu.VMEM((tl, SGU_WIDTH), BF16),
                        pltpu.VMEM((tl, SGU_WIDTH), F32)],
        compiler_params=_cparams(("parallel", "parallel")), name="sgu_mixer")(
            proj3, proj3, row(ln_w), row(ln_b), w_causal, bias, row(norm_w))


HG_PAIR = 2
HG_PAIR_W = HG_PAIR * HG_HEAD_DIM
HG_FAST_BLOCK = 32
HG_FAST_MIN_LOG_DECAY = -60.0
_NT = (((1,), (1,)), ((), ()))
_TN = (((0,), (0,)), ((), ()))


def _hgrn_tables():
    c = HG_CHUNK
    t = jnp.arange(c)[:, None]
    r = jnp.arange(c)[None, :]
    mats, masks = [], []
    for lev in range(HG_LEVELS):
        m = 1 << lev
        mid = (t // (2 * m)) * (2 * m) + m
        later = t >= mid
        mats.append(jnp.where(later, (r >= mid) & (r <= t), (r > t) & (r < mid)))
        same = (t // (2 * m)) == (r // (2 * m))
        masks.append(same & later & (r < mid))
    tri = r <= t
    mats.append(tri)
    masks.append(t == r)
    fb = HG_FAST_BLOCK
    fast = [masks[5], ((t // fb) == (r // fb)) & tri]
    pair = lambda m: jnp.tile(m.astype(F32), (1, HG_PAIR))
    return dict(
        tri=tri.astype(BF16),
        mst=jnp.concatenate(mats, axis=0).astype(BF16),
        masks=jnp.stack(masks).astype(F32),
        fmasks=jnp.stack([pair(m) for m in fast]))


def _block_diag(x):
    z = jnp.zeros((x.shape[0], HG_HEAD_DIM), x.dtype)
    top = jnp.concatenate([x[:, :HG_HEAD_DIM], z], axis=1)
    bot = jnp.concatenate([z, x[:, HG_HEAD_DIM:]], axis=1)
    return jnp.concatenate([top, bot], axis=0)


def _rows(v, n):
    return jnp.broadcast_to(v, (n, v.shape[1]))


def _hgrn_finish(o, gv, nw, o_ref, rs):
    for h in range(HG_PAIR):
        cs = slice(h * HG_HEAD_DIM, (h + 1) * HG_HEAD_DIM)
        oh = o[:, cs]
        ms = jnp.mean(oh * oh, axis=-1, keepdims=True)
        gg = gv[:, cs]
        o_ref[rs, cs] = (oh * lax.rsqrt(ms + RMS_EPS) * nw * (gg * _sigmoid(gg))).astype(BF16)


def _hgrn_body(q_ref, f_ref, i_ref, g_ref, par_ref, nw_ref, tri_ref, mst_ref, mask_ref,
               fmask_ref, o_ref, st_ref, qs_ref, kk_ref, lf_ref, b_ref, *, nc):
    c = HG_CHUNK
    hd = HG_HEAD_DIM
    wb = HG_PAIR_W
    t = pl.program_id(2)

    @pl.when(t == 0)
    def _():
        st_ref[...] = jnp.zeros_like(st_ref)

    log_lb = par_ref[0:1, :]
    log_1m_lb = par_ref[1:2, :]
    one_m_lb = par_ref[2:3, :]
    nw = nw_ref[...]
    tri = tri_ref[...]
    row = lax.broadcasted_iota(jnp.int32, (c, wb), 0)

    wmin = None
    for ch in range(nc):
        rs = slice(ch * c, (ch + 1) * c)
        fr = f_ref[rs, :]
        e = jnp.exp(-jnp.abs(fr))
        r = 1.0 / (1.0 + e)
        sig_neg = jnp.where(fr >= 0, e * r, r)
        log_sig = jnp.minimum(fr, 0.0) - jnp.log(1.0 + e)
        y = log_1m_lb + log_sig
        log_f = jnp.maximum(log_lb, y) + jnp.log(1.0 + jnp.exp(-jnp.abs(log_lb - y)))
        hi = log_f.astype(BF16)
        lo = (log_f - hi.astype(F32)).astype(BF16)
        b = _dot(tri, hi) + _dot(tri, lo)
        qv = q_ref[rs, :]
        qs_ref[rs, :] = qv * _sigmoid(qv)
        kk_ref[rs, :] = one_m_lb * sig_neg
        lf_ref[rs, :] = log_f
        b_ref[rs, :] = b
        fb = HG_FAST_BLOCK
        starts = jnp.concatenate(
            [jnp.zeros((fb, wb), F32)]
            + [_rows(b[j * fb - 1:j * fb, :], fb) for j in range(1, c // fb)], axis=0)
        w = b - starts
        wmin = w if wmin is None else jnp.minimum(wmin, w)
    fast = jnp.min(wmin) >= HG_FAST_MIN_LOG_DECAY

    @pl.when(fast)
    def _():
        for ch in range(nc):
            rs = slice(ch * c, (ch + 1) * c)
            b = b_ref[rs, :]
            qs = qs_ref[rs, :]
            kk = kk_ref[rs, :]
            vv = i_ref[rs, :].astype(BF16)
            half = c // 2
            b63 = _rows(b[half - 1:half, :], half)
            zero_half = jnp.zeros((half, wb), BF16)
            q6 = jnp.concatenate(
                [zero_half, (qs[half:, :] * jnp.exp(b[half:, :] - b63)).astype(BF16)], axis=0)
            k6 = jnp.concatenate(
                [(kk[:half, :] * jnp.exp(b63 - b[:half, :])).astype(BF16), zero_half], axis=0)
            scores = lax.dot_general(q6, _block_diag(k6), _NT, preferred_element_type=F32)
            bm = jnp.concatenate([_rows(b[31:32, :], 64), _rows(b[95:96, :], 64)], axis=0)
            a5 = jnp.exp(jnp.where((row % 64) >= 32, b - bm, bm - b))
            fb = HG_FAST_BLOCK
            starts = jnp.concatenate(
                [jnp.zeros((fb, wb), F32)]
                + [_rows(b[j * fb - 1:j * fb, :], fb) for j in range(1, c // fb)], axis=0)
            w = b - starts
            for lev, (aq, ak) in enumerate([(a5, a5), (jnp.exp(w), jnp.exp(-w))]):
                sc = lax.dot_general((qs * aq).astype(BF16), _block_diag((kk * ak).astype(BF16)),
                                     _NT, preferred_element_type=F32)
                scores = scores + fmask_ref[lev] * sc
            st = st_ref[...]
            v_bd = _block_diag(vv)
            o = (_dot(scores.astype(BF16), v_bd)
                 + lax.dot_general((qs * jnp.exp(b)).astype(BF16), st.astype(BF16), _NT,
                                   preferred_element_type=F32))
            bend = b[c - 1:c, :]
            kd = (kk * jnp.exp(bend - b)).astype(BF16)
            st_ref[...] = st * jnp.exp(bend) + lax.dot_general(
                v_bd, _block_diag(kd), _TN, preferred_element_type=F32)
            _hgrn_finish(o, g_ref[rs, :], nw, o_ref, rs)

    @pl.when(jnp.logical_not(fast))
    def _():
        mst = mst_ref[...]

        def chunk(ch, carry):
            rs = pl.ds(pl.multiple_of(ch * c, c), c)
            log_f = lf_ref[rs, :]
            hi = log_f.astype(BF16)
            lo = (log_f - hi.astype(F32)).astype(BF16)
            ee = _dot(mst, hi) + _dot(mst, lo)
            qs = qs_ref[rs, :]
            kk = kk_ref[rs, :]
            vv = i_ref[rs, :].astype(BF16)
            bcum = ee[HG_LEVELS * c:(HG_LEVELS + 1) * c, :]
            outs = []
            for h in range(HG_PAIR):
                cs = slice(h * hd, (h + 1) * hd)
                qh = qs[:, cs]
                kh = kk[:, cs]
                vh = vv[:, cs]
                scores = mask_ref[HG_LEVELS] * jnp.sum(qh * kh, axis=-1, keepdims=True)
                for lev in range(HG_LEVELS):
                    a = jnp.exp(ee[lev * c:(lev + 1) * c, cs])
                    sc = lax.dot_general((qh * a).astype(BF16), (kh * a).astype(BF16), _NT,
                                         preferred_element_type=F32)
                    scores = scores + mask_ref[lev] * sc
                bh = bcum[:, cs]
                st = st_ref[cs, cs]
                outs.append(_dot(scores.astype(BF16), vh)
                            + lax.dot_general((qh * jnp.exp(bh)).astype(BF16), st.astype(BF16),
                                              _NT, preferred_element_type=F32))
                bend = bh[c - 1:c, :]
                kd = (kh * jnp.exp(bend - bh)).astype(BF16)
                st_ref[cs, cs] = st * jnp.exp(bend) + lax.dot_general(
                    vh, kd, _TN, preferred_element_type=F32)
            _hgrn_finish(jnp.concatenate(outs, axis=1), g_ref[rs, :], nw, o_ref, rs)
            return carry

        lax.fori_loop(0, nc, chunk, 0)


def _hgrn_mixer(proj3, lb, norm_w, tl):
    nb, seq, _ = proj3.shape
    nc = tl // HG_CHUNK
    wb = HG_PAIR_W
    q0 = (2 * SGU_WIDTH) // wb
    nblk = HG_WIDTH // wb
    lbf = lb.astype(F32)
    par = jnp.stack([jnp.log(lbf), jnp.log1p(-lbf), 1.0 - lbf])
    tb = _hgrn_tables()
    sec = lambda k: pl.BlockSpec((None, tl, wb), functools.partial(
        lambda b, h, t, k: (b, t, q0 + k * nblk + h), k=k))
    const = lambda a: pl.BlockSpec(a.shape, lambda b, h, t: (0,) * a.ndim)
    body = functools.partial(_hgrn_body, nc=nc)
    return pl.pallas_call(
        body,
        out_shape=jax.ShapeDtypeStruct((nb, seq, HG_WIDTH), BF16),
        grid=(nb, nblk, seq // tl),
        in_specs=[sec(0), sec(1), sec(2), sec(3),
                  pl.BlockSpec((3, wb), lambda b, h, t: (0, h)),
                  pl.BlockSpec((1, HG_HEAD_DIM), lambda b, h, t: (0, 0)),
                  const(tb["tri"]), const(tb["mst"]), const(tb["masks"]),
                  const(tb["fmasks"])],
        out_specs=pl.BlockSpec((None, tl, wb), lambda b, h, t: (b, t, h)),
        scratch_shapes=[pltpu.VMEM((wb, wb), F32)] + [pltpu.VMEM((tl, wb), F32)] * 4,
        compiler_params=_cparams(("parallel", "parallel", "arbitrary")),
        name="hgrn2_mixer")(
            proj3, proj3, proj3, proj3, par,
            norm_w.astype(F32).reshape(1, HG_HEAD_DIM),
            tb["tri"], tb["mst"], tb["masks"], tb["fmasks"])


def kernel(x, p, w_in, s5_lam_re, s5_lam_im, s5_log_step, s5_b_re, s5_b_im, s5_c_re, s5_c_im, s5_d, s5_w_glu, sgu_ln_w, sgu_ln_b, sgu_w, sgu_b, hg_lb_logits, hg_norm_w, norm_a_w, norm_b_w, w_out, ln1_w, ln1_b, w_ffn_in, w_ffn_out, ln2_w, ln2_b, w_ple_in, w_ple_gate, ln3_w, ln3_b):
    nb, seq, d_model = x.shape
    depth = w_in.shape[0]
    t_rows = nb * seq
    proj_w = w_in.shape[2]
    d_ff = w_ffn_out.shape[1]
    alpha = (2.0 * depth) ** 0.25

    tm = _pick(t_rows, (1024, 512, 256, 128))
    tm_wide = _pick(t_rows, (2048, 1024, 512, 256, 128))
    tm_half = _pick(t_rows, (512, 256, 128))
    tm_ln = _pick(t_rows, (512, 256, 128))
    tl_s5 = _pick(seq, (1024, 512, 256, 128))
    tl_sgu = _pick(seq, (1024, 512, 256, 128))
    tl_hg = _pick(seq, (2048, 1024, 512, 256, 128))
    tn_in = _pick(proj_w - S5_WIDTH, (1024, 512))

    lbs =jnp.cumsum(jax.nn.softmax(hg_lb_logits.astype(F32), axis=0), axis=0)
    lbs = lbs - lbs[0:1]

    x_rows = x.reshape(t_rows, d_model).astype(F32)
    res = _Residual(alpha, h=x_rows)
    h16 = _layer_to_bf16(x_rows.reshape(1, t_rows, d_model), 0)
    p_rows = p.reshape(depth, t_rows, -1)

    w_in16 = _layer_to_bf16(w_in, 0)
    for l in range(depth):
        if l == 0:
            xa_tiles = _s5_in_proj(h16, w_in16, tm)
        else:
            lw, lb = ln3_w[l - 1], ln3_b[l - 1]
            xa_tiles, mu, rstd, h16 = _ln_fused_mm(
                xres, lw, lb, [(None, w_in16, d_model, 0, 0)], [],
                lambda accs, ex: [accs[0]], [F32], S5_WIDTH, tm, S5_WIDTH // 4, "in_proj_s5",
                emit_h16=True, out_tiles=True)
            res = _Residual(alpha, x=xres, mu=mu, rstd=rstd, w=lw, b=lb)
        xa_tiles = xa_tiles.reshape(-1, nb, seq, LANES)
        proj, wf, wo = _fused_mm(
            [(h16, w_in16, d_model, 0, S5_WIDTH // tn_in)], [],
            lambda accs, ex: [accs[0]], [F32], proj_w - S5_WIDTH, tm, tn_in, "in_proj",
            side_casts=[(w_ffn_in, l), (w_out, l)])
        proj3 = proj.reshape(nb, seq, proj_w - S5_WIDTH)
        prep = _s5_prepare(s5_lam_re[l], s5_lam_im[l], s5_log_step[l], s5_b_re[l],
                           s5_b_im[l], s5_c_re[l], s5_c_im[l], s5_d[l])
        z_tiles = _s5_mixer(xa_tiles, prep, tl_s5).reshape(-1, t_rows, LANES)
        ya = _s5_glu_norm(z_tiles, _layer_to_bf16(s5_w_glu, l), norm_a_w[l], tm_half)
        yb = _sgu_mixer(proj3, sgu_ln_w[l], sgu_ln_b[l], sgu_w[l], sgu_b[l],
                        norm_b_w[l], tl_sgu).reshape(t_rows, SGU_WIDTH)
        yc = _hgrn_mixer(proj3, lbs[l], hg_norm_w[l], tl_hg).reshape(t_rows, HG_WIDTH)
        (xres,) = _fused_mm(
            [(ya, wo, S5_WIDTH, 0, 0), (yb, wo, SGU_WIDTH, 1, 0), (yc, wo, HG_WIDTH, 1, 0)],
            res.extras,
            functools.partial(lambda accs, ex, res: [res(ex) + (accs[0] + accs[1] + accs[2])],
                              res=res),
            [F32], d_model, tm, 1024, "out_proj")

        hid, mu, rstd, wfo, wpg = _ln_fused_mm(
            xres, ln1_w[l], ln1_b[l],
            [(None, wf, d_model, 0, 0), (None, wf, d_model, 0, d_ff // 256)], [],
            lambda accs, ex: [accs[0] * _sigmoid(accs[0]) * accs[1]],
            [BF16], d_ff, tm_wide, 256, "ffn_in",
            side_casts=[(w_ffn_out, l), (w_ple_gate, l)])
        res = _Residual(alpha, x=xres, mu=mu, rstd=rstd, w=ln1_w[l], b=ln1_b[l])
        (xres,) = _fused_mm(
            [(hid, wfo, d_ff, 0, 0)], res.extras,
            functools.partial(lambda accs, ex, res: [res(ex) + accs[0]], res=res),
            [F32], d_model, tm_half, 1024, "ffn_out", cols_outer=True)

        p16 = _layer_to_bf16(p_rows, l)
        xres, _, _, *next_w_in = _ln_fused_mm(
            xres, ln2_w[l], ln2_b[l],
            [(None, wpg, d_model, 0, 0),
             (p16, _layer_to_bf16(w_ple_in, l), p16.shape[1], 0, 0)], [],
            lambda accs, ex, resid: [resid + accs[1] * _sigmoid(accs[0])],
            [F32], d_model, tm, 512, "ple", self_alpha=alpha,
            side_casts=[(w_in, l + 1)] if l + 1 < depth else [])
        if next_w_in:
            w_in16 = next_w_in[0]

    out = _layer_norm_final(xres, ln3_w[depth - 1], ln3_b[depth - 1], tm_ln)
    return out.reshape(nb, seq, d_model).astype(x.dtype)
```

```python
import functools
import math

import jax
import jax.numpy as jnp
from jax import lax
from jax.experimental import pallas as pl
from jax.experimental.pallas import tpu as pltpu

F32 = jnp.float32
BF16 = jnp.bfloat16

V7X_VMEM_LIMIT_BYTES = 56 * 1024 * 1024
LANES = 128

LN_EPS = 1e-5
RMS_EPS = 1e-6

S5_WIDTH = 1024
S5_GROUP_CH = 16
S5_STATE = 64
S5_CHUNK = 8
S5_TILE_GROUPS = LANES // S5_GROUP_CH
S5_TILE_STATE = S5_TILE_GROUPS * S5_STATE

SGU_WIDTH = 1024
SGU_CHUNK = 128
SGU_HEADS = 8

HG_WIDTH = 2048
HG_HEAD_DIM = 128
HG_CHUNK = 128
HG_LEVELS = 7


def _cparams(sem):
    return pltpu.CompilerParams(dimension_semantics=sem,
                                vmem_limit_bytes=V7X_VMEM_LIMIT_BYTES)


def _gelu_tanh(x):
    c = math.sqrt(2.0 / math.pi)
    return x * (0.5 * (1.0 + jnp.tanh(c * (x + 0.044715 * (x * x * x)))))


def _sigmoid(x):
    return 1.0 / (1.0 + jnp.exp(-x))


def _dot(a, b):
    return jnp.dot(a, b, preferred_element_type=F32)


def _pick(n, prefs):
    for p in prefs:
        if n % p == 0:
            return p
    return n


CAST_BLOCK_BYTES = 8 * 1024 * 1024


def _cast_body(w_ref, o_ref):
    o_ref[...] = w_ref[...].astype(BF16)


def _layer_to_bf16(w, layer):
    _, rows, cols = w.shape
    fits = [tr for tr in (8192, 4096, 2048, 1024, 512, 256, 128, 64, 32, 16)
            if rows % tr == 0 and tr * cols * 4 <= CAST_BLOCK_BYTES]
    tr = fits[0]
    return pl.pallas_call(
        _cast_body,
        out_shape=jax.ShapeDtypeStruct((rows, cols), BF16),
        grid=(rows // tr,),
        in_specs=[pl.BlockSpec((None, tr, cols), lambda i: (layer, i, 0))],
        out_specs=pl.BlockSpec((tr, cols), lambda i: (i, 0)),
        compiler_params=_cparams(("parallel",)), name="cast_bf16")(w)


MM_SUB_ROWS = 256


def _mm_body(*refs, a_of_dot, n_a, extra_kinds, n_side, epilogue, sub_rows):
    n_dot = len(a_of_dot)
    n_extra = len(extra_kinds)
    n_in = n_a + n_dot + n_extra
    a_refs = refs[:n_a]
    w_refs = refs[n_a:n_a + n_dot]
    extra = refs[n_a + n_dot:n_in]
    side_in = refs[n_in:n_in + n_side]
    out_refs = refs[n_in + n_side:len(refs) - n_side]
    side_out = refs[len(refs) - n_side:]
    for si, so in zip(side_in, side_out):
        so[...] = si[...].astype(BF16)
    rows = out_refs[0].shape[0]
    sub = min(rows, sub_rows)
    for r in range(rows // sub):
        rs = slice(r * sub, (r + 1) * sub)
        accs = [_dot(a_refs[ai][rs, :], w[...]) for ai, w in zip(a_of_dot, w_refs)]
        outs = epilogue(accs, [e[...] if kind == "row" else e[rs, :]
                               for e, kind in zip(extra, extra_kinds)])
        for o_ref, o in zip(out_refs, outs):
            o_ref[rs, :] = o.astype(o_ref.dtype)


BF16_ROW_TILE = 16


def _fused_mm(dots, extras, epilogue, out_dtypes, n_cols, tm, tn, name,
              sub_rows=MM_SUB_ROWS, side_casts=(), cols_outer=False):
    t_rows = dots[0][0].shape[0]
    ni, nj = t_rows // tm, n_cols // tn
    grid = (nj, ni) if cols_outer else (ni, nj)

    def spec(shape, index_map, **kw):
        if cols_outer:
            return pl.BlockSpec(shape, lambda g0, g1: index_map(g1, g0), **kw)
        return pl.BlockSpec(shape, index_map, **kw)

    w_mode = dict(pipeline_mode=pl.Buffered(1)) if cols_outer else {}
    in_specs, args = [], []
    a_of_dot = []
    for a, _, _, _, _ in dots:
        known = [k for k, seen in enumerate(args) if seen is a]
        if known:
            a_of_dot.append(known[0])
            continue
        a_of_dot.append(len(args))
        in_specs.append(spec((tm, a.shape[1]), lambda i, j: (i, 0)))
        args.append(a)
    n_a = len(args)
    for _, w, rb, ri, co in dots:
        in_specs.append(spec((rb, tn), functools.partial(
            lambda i, j, ri, co: (ri, j + co), ri=ri, co=co), **w_mode))
        args.append(w)
    for arr, kind in extras:
        if kind == "tile":
            in_specs.append(spec((tm, tn), lambda i, j: (i, j)))
        elif kind == "stat":
            in_specs.append(spec((tm, LANES), lambda i, j: (i, 0)))
        else:
            in_specs.append(spec((1, tn), lambda i, j: (0, j)))
        args.append(arr)
    out_shape = [jax.ShapeDtypeStruct((t_rows, n_cols), dt) for dt in out_dtypes]
    out_specs = [spec((tm, tn), lambda i, j: (i, j)) for _ in out_dtypes]
    n_steps = grid[0] * grid[1]
    for stack, layer in side_casts:
        _, rows, cols = stack.shape
        tiles = rows // BF16_ROW_TILE
        n_slabs = max(d for d in range(1, min(tiles, n_steps) + 1) if tiles % d == 0)
        slab = functools.partial(lambda g0, g1, last: jnp.minimum(g0 * grid[1] + g1, last),
                                 last=n_slabs - 1)
        in_specs.append(pl.BlockSpec(
            (None, rows // n_slabs, cols),
            functools.partial(lambda g0, g1, layer, slab: (layer, slab(g0, g1), 0),
                              layer=layer, slab=slab)))
        args.append(stack)
        out_shape.append(jax.ShapeDtypeStruct((rows, cols), BF16))
        out_specs.append(pl.BlockSpec(
            (rows // n_slabs, cols),
            functools.partial(lambda g0, g1, slab: (slab(g0, g1), 0), slab=slab)))
    body = functools.partial(_mm_body, a_of_dot=tuple(a_of_dot), n_a=n_a,
                             extra_kinds=tuple(kind for _, kind in extras),
                             n_side=len(side_casts), epilogue=epilogue, sub_rows=sub_rows)
    return pl.pallas_call(
        body, out_shape=out_shape, grid=grid, in_specs=in_specs, out_specs=out_specs,
        compiler_params=_cparams(("parallel", "arbitrary")), name=name)(*args)


def _ln_rows(x):
    mu = jnp.mean(x, axis=-1, keepdims=True)
    xc = x - mu
    var = jnp.mean(xc * xc, axis=-1, keepdims=True)
    return xc, mu, lax.rsqrt(var + LN_EPS)


LN_ROW_GROUP = 8


def _ln_final_body(x_ref, w_ref, b_ref, h_ref):
    w = w_ref[...]
    b = b_ref[...]
    for g in range(x_ref.shape[0] // LN_ROW_GROUP):
        rs = slice(g * LN_ROW_GROUP, (g + 1) * LN_ROW_GROUP)
        xc, _, rstd = _ln_rows(x_ref[rs, :])
        h_ref[rs, :] = xc * rstd * w + b


def _layer_norm_final(x, w, b, tm):
    t_rows, d = x.shape
    rows = pl.BlockSpec((tm, d), lambda i: (i, 0))
    vec = pl.BlockSpec((1, d), lambda i: (0, 0))
    return pl.pallas_call(
        _ln_final_body, out_shape=jax.ShapeDtypeStruct((t_rows, d), F32),
        grid=(t_rows // tm,), in_specs=[rows, vec, vec], out_specs=rows,
        compiler_params=_cparams(("parallel",)), name="layer_norm")(
            x, w.reshape(1, d), b.reshape(1, d))


def _ln_mm_body(*refs, a_of_dot, n_other_a, extra_kinds, n_side, n_out, emit_h16, out_tiles,
                self_alpha, epilogue, sub_rows, n_slab, slab_rows):
    n_dot = len(a_of_dot)
    n_extra = len(extra_kinds)
    x_ref, lnw_ref, lnb_ref = refs[:3]
    pos = 3
    other_a = refs[pos:pos + n_other_a]
    pos += n_other_a
    w_refs = refs[pos:pos + n_dot]
    pos += n_dot
    extra = refs[pos:pos + n_extra]
    pos += n_extra
    side_in = refs[pos:pos + n_side]
    pos += n_side
    out_refs = refs[pos:pos + n_out]
    pos += n_out
    mu_ref, rs_ref = refs[pos:pos + 2]
    pos += 2
    h16_ref = refs[pos] if emit_h16 else None
    pos += int(emit_h16)
    side_out = refs[pos:pos + n_side]
    pos += n_side
    a_slots, mu_slots, rs_slots = refs[pos:pos + 2], refs[pos + 2:pos + 4], refs[pos + 4:pos + 6]

    g = pl.program_id(0)
    j = pl.program_id(1)
    group = 2 * LN_ROW_GROUP

    def side_jobs(slot):
        a_sc, mu_sc, rs_sc = a_slots[slot], mu_slots[slot], rs_slots[slot]
        for si, so in zip(side_in, side_out):
            so[...] = si[...].astype(BF16)
        row0 = pl.multiple_of(jnp.minimum(j, n_slab - 1) * slab_rows, slab_rows)
        w = lnw_ref[...]
        b = lnb_ref[...]
        for grp in range(slab_rows // group):
            ys = []
            for half in range(2):
                r8 = slice(grp * group + half * LN_ROW_GROUP,
                           grp * group + (half + 1) * LN_ROW_GROUP)
                xc, mu, rstd = _ln_rows(x_ref[r8, :])
                ys.append(xc * rstd * w + b)
                mu_b = jnp.broadcast_to(mu, (LN_ROW_GROUP, LANES))
                rs_b = jnp.broadcast_to(rstd, (LN_ROW_GROUP, LANES))
                mu_ref[r8, :] = mu_b
                rs_ref[r8, :] = rs_b
                dst = pl.ds(row0 + grp * group + half * LN_ROW_GROUP, LN_ROW_GROUP)
                mu_sc[dst, :] = mu_b
                rs_sc[dst, :] = rs_b
            y16 = jnp.concatenate(ys, axis=0).astype(BF16)
            a_sc[pl.ds(row0 + grp * group, group), :] = y16
            if emit_h16:
                h16_ref[grp * group:(grp + 1) * group, :] = y16

    def matmul(slot):
        a_sc, mu_sc, rs_sc = a_slots[slot], mu_slots[slot], rs_slots[slot]
        rows = a_sc.shape[0]
        sub = min(rows, sub_rows)
        for r in range(rows // sub):
            rs = slice(r * sub, (r + 1) * sub)
            a_ln = a_sc[rs, :]
            accs = [_dot(a_ln if ai < 0 else other_a[ai][rs, :], w[...])
                    for ai, w in zip(a_of_dot, w_refs)]
            ex = [e[...] if kind == "row" else e[rs, :] for e, kind in zip(extra, extra_kinds)]
            if self_alpha is not None:
                x_tile, w_row, b_row = ex[:3]
                ex = ex[3:]
                rep = x_tile.shape[1] // LANES
                wide = lambda s: jnp.concatenate([s] * rep, axis=1)
                resid = self_alpha * ((x_tile - wide(mu_sc[rs, :]))
                                      * wide(rs_sc[rs, :]) * w_row + b_row)
                outs = epilogue(accs, ex, resid)
            else:
                outs = epilogue(accs, ex)
            for o_ref, o in zip(out_refs, outs):
                if out_tiles:
                    for c in range(o_ref.shape[0]):
                        o_ref[c, rs, :] = o[:, c * LANES:(c + 1) * LANES].astype(o_ref.dtype)
                else:
                    o_ref[rs, :] = o.astype(o_ref.dtype)

    @pl.when(g == 0)
    def _():
        side_jobs(0)

    for parity in (0, 1):
        @pl.when(jnp.logical_and(g > 0, lax.rem(g, 2) == parity))
        def _():
            side_jobs(parity)
            matmul(1 - parity)


def _ln_fused_mm(x, ln_w, ln_b, dots, extras, epilogue, out_dtypes, n_cols, tm, tn, name,
                 sub_rows=MM_SUB_ROWS, side_casts=(), emit_h16=False, out_tiles=False,
                 self_alpha=None):
    t_rows, d = x.shape
    ni, nj = t_rows // tm, n_cols // tn
    grid = (ni + 1, nj)
    n_slab = max(s for s in range(1, nj + 1)
                 if tm % s == 0 and (tm // s) % (2 * LN_ROW_GROUP) == 0)
    slab_rows = tm // n_slab
    prev = lambda g: jnp.maximum(g - 1, 0)
    first_col = lambda g, j: jnp.where(g > 0, j, 0)
    slab_idx = lambda g, j: (jnp.where(g < ni, g * n_slab + jnp.minimum(j, n_slab - 1),
                                       ni * n_slab - 1), 0)
    vec = lambda a: a.astype(F32).reshape(1, d)
    in_specs = [pl.BlockSpec((slab_rows, d), slab_idx),
                pl.BlockSpec((1, d), lambda g, j: (0, 0)),
                pl.BlockSpec((1, d), lambda g, j: (0, 0))]
    args = [x, vec(ln_w), vec(ln_b)]
    a_of_dot, other = [], []
    for a, _, _, _, _ in dots:
        if a is None:
            a_of_dot.append(-1)
            continue
        a_of_dot.append(len(other))
        other.append(a)
        in_specs.append(pl.BlockSpec((tm, a.shape[1]), lambda g, j: (prev(g), 0)))
        args.append(a)
    for _, w, rb, ri, co in dots:
        in_specs.append(pl.BlockSpec((rb, tn), functools.partial(
            lambda g, j, ri, co: (ri, j + co), ri=ri, co=co)))
        args.append(w)
    extras = list(extras)
    if self_alpha is not None:
        extras = [(x, "tile"), (vec(ln_w), "row"), (vec(ln_b), "row")] + extras
    for arr, kind in extras:
        if kind == "tile":
            in_specs.append(pl.BlockSpec((tm, tn), lambda g, j: (prev(g), j)))
        elif kind == "stat":
            in_specs.append(pl.BlockSpec((tm, LANES), lambda g, j: (prev(g), 0)))
        else:
            in_specs.append(pl.BlockSpec((1, tn), lambda g, j: (0, j)))
        args.append(arr)
    if out_tiles:
        n_tiles = n_cols // LANES
        out_shape = [jax.ShapeDtypeStruct((n_tiles, t_rows, LANES), dt) for dt in out_dtypes]
        out_specs = [pl.BlockSpec((tn // LANES, tm, LANES),
                                  lambda g, j: (first_col(g, j), prev(g), 0))
                     for _ in out_dtypes]
    else:
        out_shape = [jax.ShapeDtypeStruct((t_rows, n_cols), dt) for dt in out_dtypes]
        out_specs = [pl.BlockSpec((tm, tn), lambda g, j: (prev(g), first_col(g, j)))
                     for _ in out_dtypes]
    n_out = len(out_shape)
    for _ in range(2):
        out_shape.append(jax.ShapeDtypeStruct((t_rows, LANES), F32))
        out_specs.append(pl.BlockSpec((slab_rows, LANES), slab_idx))
    if emit_h16:
        out_shape.append(jax.ShapeDtypeStruct((t_rows, d), BF16))
        out_specs.append(pl.BlockSpec((slab_rows, d), slab_idx))
    n_steps = grid[0] * grid[1]
    for stack, layer in side_casts:
        _, rows, cols = stack.shape
        tiles = rows // BF16_ROW_TILE
        n_cast = max(s for s in range(1, min(tiles, n_steps) + 1) if tiles % s == 0)
        slab = functools.partial(lambda g, j, last: jnp.minimum(g * nj + j, last),
                                 last=n_cast - 1)
        in_specs.append(pl.BlockSpec(
            (None, rows // n_cast, cols),
            functools.partial(lambda g, j, layer, slab: (layer, slab(g, j), 0),
                              layer=layer, slab=slab)))
        args.append(stack)
        out_shape.append(jax.ShapeDtypeStruct((rows, cols), BF16))
        out_specs.append(pl.BlockSpec(
            (rows // n_cast, cols),
            functools.partial(lambda g, j, slab: (slab(g, j), 0), slab=slab)))
    body = functools.partial(
        _ln_mm_body, a_of_dot=tuple(a_of_dot), n_other_a=len(other),
        extra_kinds=tuple(kind for _, kind in extras), n_side=len(side_casts), n_out=n_out,
        emit_h16=emit_h16, out_tiles=out_tiles, self_alpha=self_alpha, epilogue=epilogue,
        sub_rows=sub_rows, n_slab=n_slab, slab_rows=slab_rows)
    return pl.pallas_call(
        body, out_shape=out_shape, grid=grid, in_specs=in_specs, out_specs=out_specs,
        scratch_shapes=([pltpu.VMEM((tm, d), BF16)] * 2
                        + [pltpu.VMEM((tm, LANES), F32)] * 4),
        compiler_params=_cparams(("arbitrary", "arbitrary")), name=name)(*args)


class _Residual:
    def __init__(self, alpha, h=None, x=None, mu=None, rstd=None, w=None, b=None):
        self.alpha = alpha
        if h is not None:
            self.extras = [(h, "tile")]
        else:
            self.extras = [(x, "tile"), (mu, "stat"), (rstd, "stat"),
                           (w.astype(F32).reshape(1, -1), "row"),
                           (b.astype(F32).reshape(1, -1), "row")]

    def __call__(self, ex):
        if len(ex) == 1:
            return self.alpha * ex[0]
        x, mu, rstd, w, b = ex
        rep = x.shape[1] // LANES
        wide = lambda s: jnp.concatenate([s] * rep, axis=1)
        return self.alpha * ((x - wide(mu)) * wide(rstd) * w + b)


def _s5_prepare(lam_re, lam_im, log_step, b_re, b_im, c_re, c_im, d):
    hp = lax.Precision.HIGHEST
    n_tiles = S5_WIDTH // LANES
    lr = jnp.minimum(lam_re.astype(F32), -1e-4)
    li = lam_im.astype(F32)
    dt = jnp.exp(log_step.astype(F32))[:, None]
    mag = jnp.exp(lr * dt)
    ab_re = mag * jnp.cos(li * dt)
    ab_im = mag * jnp.sin(li * dt)
    den = lr * lr + li * li
    nr = ab_re - 1.0
    g_re = (nr * lr + ab_im * li) / den
    g_im = (ab_im * lr - nr * li) / den
    br = b_re.astype(F32)
    bi = b_im.astype(F32)
    bb_re = g_re[..., None] * br - g_im[..., None] * bi
    bb_im = g_re[..., None] * bi + g_im[..., None] * br
    pr, pi = [jnp.ones_like(ab_re)], [jnp.zeros_like(ab_re)]
    for _ in range(S5_CHUNK):
        pr_new = pr[-1] * ab_re - pi[-1] * ab_im
        pi_new = pr[-1] * ab_im + pi[-1] * ab_re
        pr.append(pr_new)
        pi.append(pi_new)
    p_re = jnp.stack(pr)
    p_im = jnp.stack(pi)
    pb_re = (p_re[:S5_CHUNK, :, :, None] * bb_re[None]
             - p_im[:S5_CHUNK, :, :, None] * bb_im[None])
    pb_im = (p_re[:S5_CHUNK, :, :, None] * bb_im[None]
             + p_im[:S5_CHUNK, :, :, None] * bb_re[None])
    cr = c_re.astype(F32)
    ci = c_im.astype(F32)
    kern = (jnp.einsum("gon,jgni->jgoi", cr, pb_re, precision=hp)
            - jnp.einsum("gon,jgni->jgoi", ci, pb_im, precision=hp))
    tg, gc, ns = S5_TILE_GROUPS, S5_GROUP_CH, S5_STATE

    def spread(x2d, rep, row_group, col_group):
        rows, width = x2d.shape
        sel = jnp.tile(jnp.eye(width, dtype=BF16), (1, rep))
        out = jnp.dot(x2d.astype(BF16), sel)
        rg = row_group(jnp.arange(rows))[:, None]
        cg = col_group(jnp.arange(width * rep))[None, :]
        return jnp.where(rg == cg, out, jnp.zeros_like(out))

    def spread_steps(x2d, row_group):
        rows, width = x2d.shape
        cols = jnp.arange(S5_CHUNK * LANES)
        src = jnp.arange(width)
        sel = ((src[:, None] // gc == cols[None, :] // LANES)
               & (src[:, None] % gc == cols[None, :] % gc)).astype(BF16)
        out = jnp.dot(x2d.astype(BF16), sel)
        rg = row_group(jnp.arange(rows))[:, None]
        cg = ((cols // gc) % tg)[None, :]
        return jnp.where(rg == cg, out, jnp.zeros_like(out))

    lag = jnp.arange(S5_CHUNK)[None, :] - jnp.arange(S5_CHUNK)[:, None]
    k_lag = jnp.where((lag >= 0)[:, :, None, None, None],
                      kern[jnp.clip(lag, 0, S5_CHUNK - 1)], 0.0)
    k_lag = k_lag.reshape(S5_CHUNK, S5_CHUNK, n_tiles, tg, gc, gc).transpose(2, 0, 3, 5, 1, 4)
    toep = spread_steps(k_lag.reshape(-1, S5_CHUNK * gc), lambda r: (r // gc) % tg).reshape(
        n_tiles, S5_CHUNK * LANES, S5_CHUNK * LANES)

    def w1_half(pb):
        pb5 = pb[::-1].reshape(S5_CHUNK, n_tiles, tg, ns, gc).transpose(1, 0, 2, 4, 3)
        return spread(pb5.reshape(-1, ns), tg, lambda r: (r // gc) % tg,
                      lambda c: c // ns).reshape(n_tiles, S5_CHUNK * LANES, S5_TILE_STATE)
    w1 = jnp.concatenate([w1_half(pb_re), w1_half(pb_im), toep], axis=-1)

    ca_re = cr[None] * p_re[1:, :, None, :] - ci[None] * p_im[1:, :, None, :]
    ca_im = cr[None] * p_im[1:, :, None, :] + ci[None] * p_re[1:, :, None, :]

    def p_half(ca):
        ca5 = ca.reshape(S5_CHUNK, n_tiles, tg, gc, ns).transpose(1, 2, 4, 0, 3)
        return spread_steps(ca5.reshape(-1, S5_CHUNK * gc), lambda r: (r // ns) % tg).reshape(
            n_tiles, S5_TILE_STATE, S5_CHUNK * LANES)
    pcat = jnp.concatenate([p_half(ca_re), -p_half(ca_im)], axis=1)

    a16 = jnp.concatenate([p_re[S5_CHUNK].reshape(n_tiles, 1, S5_TILE_STATE),
                           p_im[S5_CHUNK].reshape(n_tiles, 1, S5_TILE_STATE)], axis=-1)
    dd = d.astype(F32).reshape(n_tiles, 1, LANES)
    return w1, pcat, a16, dd


def _s5_body(x_ref, w1_ref, pcat_ref, a16_ref, d_ref, z_ref,
             state_ref, xr_ref, q_ref, yi_ref, sin_ref, yc_ref, yint_ref, *, nb, tl):
    t = pl.program_id(1)
    nk = tl // S5_CHUNK
    n = nb * tl
    ts = S5_TILE_STATE

    @pl.when(t == 0)
    def _():
        state_ref[...] = jnp.zeros_like(state_ref)

    for b in range(nb):
        for tp in range(S5_CHUNK):
            xr_ref[tp, pl.ds(b, nk, stride=nb), :] = x_ref[b, pl.ds(tp, nk, stride=S5_CHUNK), :]
    xr = jnp.concatenate([xr_ref[tp].astype(BF16) for tp in range(S5_CHUNK)], axis=1)
    local = _dot(xr, w1_ref[...])
    q_ref[...] = local[:, :2 * ts]
    yi_ref[...] = local[:, 2 * ts:]

    a_re = a16_ref[:, :ts]
    a_im = a16_ref[:, ts:]
    s = state_ref[...]
    for k in range(nk):
        sin_ref[k * nb:(k + 1) * nb, :] = s
        s_re = s[:, :ts]
        s_im = s[:, ts:]
        s = jnp.concatenate([a_re * s_re - a_im * s_im, a_re * s_im + a_im * s_re],
                            axis=1) + q_ref[k * nb:(k + 1) * nb, :]
    state_ref[...] = s

    yc = yi_ref[...] + _dot(sin_ref[...].astype(BF16), pcat_ref[...])
    for tp in range(S5_CHUNK):
        yc_ref[tp] = yc[:, tp * LANES:(tp + 1) * LANES]
    for b in range(nb):
        for tp in range(S5_CHUNK):
            yint_ref[pl.ds(b * tl + tp, nk, stride=S5_CHUNK), :] = (
                yc_ref[tp, pl.ds(b, nk, stride=nb), :])

    x = x_ref[...].reshape(n, LANES)
    z_ref[...] = _gelu_tanh(yint_ref[...] + d_ref[...] * x).reshape(nb, tl, LANES)


def _s5_in_proj_body(a_ref, w_ref, o_ref):
    rows = a_ref.shape[0]
    sub = min(rows, MM_SUB_ROWS)
    for r in range(rows // sub):
        rs = slice(r * sub, (r + 1) * sub)
        acc = _dot(a_ref[rs, :], w_ref[...])
        for c in range(o_ref.shape[0]):
            o_ref[c, rs, :] = acc[:, c * LANES:(c + 1) * LANES]


def _s5_in_proj(h16, w_in16, tm):
    t_rows, d = h16.shape
    n_tiles = S5_WIDTH // LANES
    return pl.pallas_call(
        _s5_in_proj_body,
        out_shape=jax.ShapeDtypeStruct((n_tiles, t_rows, LANES), F32),
        grid=(t_rows // tm,),
        in_specs=[pl.BlockSpec((tm, d), lambda i: (i, 0)),
                  pl.BlockSpec((d, S5_WIDTH), lambda i: (0, 0))],
        out_specs=pl.BlockSpec((n_tiles, tm, LANES), lambda i: (0, i, 0)),
        compiler_params=_cparams(("parallel",)), name="in_proj_s5")(h16, w_in16)


def _s5_mixer(xa_tiles, prep, tl):
    w1, pcat, a16, dd = prep
    n_tiles, nb, seq, _ = xa_tiles.shape
    nk = tl // S5_CHUNK
    ts = S5_TILE_STATE
    steps = S5_CHUNK * LANES
    body = functools.partial(_s5_body, nb=nb, tl=tl)
    wspec = lambda shp: pl.BlockSpec((None,) + shp, lambda j, t: (j, 0, 0))
    xspec = pl.BlockSpec((None, nb, tl, LANES), lambda j, t: (j, 0, t, 0))
    return pl.pallas_call(
        body,
        out_shape=jax.ShapeDtypeStruct(xa_tiles.shape, F32),
        grid=(n_tiles, seq // tl),
        in_specs=[xspec,
                  wspec((steps, 2 * ts + steps)),
                  wspec((2 * ts, steps)),
                  wspec((1, 2 * ts)),
                  wspec((1, LANES))],
        out_specs=xspec,
        scratch_shapes=[pltpu.VMEM((nb, 2 * ts), F32),
                        pltpu.VMEM((S5_CHUNK, nb * nk, LANES), F32),
                        pltpu.VMEM((nb * nk, 2 * ts), F32),
                        pltpu.VMEM((nb * nk, steps), F32),
                        pltpu.VMEM((nb * nk, 2 * ts), F32),
                        pltpu.VMEM((S5_CHUNK, nb * nk, LANES), F32),
                        pltpu.VMEM((nb * tl, LANES), F32)],
        compiler_params=_cparams(("parallel", "arbitrary")), name="s5_mixer")(
            xa_tiles, w1, pcat, a16, dd)


def _glu_body(z_ref, w_ref, nw_ref, o_ref):
    z = jnp.concatenate([z_ref[c] for c in range(z_ref.shape[0])], axis=1)
    y = z * _sigmoid(_dot(z.astype(BF16), w_ref[...]))
    ms = jnp.mean(y * y, axis=-1, keepdims=True)
    o_ref[...] = (y * lax.rsqrt(ms + RMS_EPS) * nw_ref[...]).astype(BF16)


def _s5_glu_norm(z_tiles, w_glu, norm_w, tm):
    n_tiles, t_rows, _ = z_tiles.shape
    d = n_tiles * LANES
    return pl.pallas_call(
        _glu_body,
        out_shape=jax.ShapeDtypeStruct((t_rows, d), BF16),
        grid=(t_rows // tm,),
        in_specs=[pl.BlockSpec((n_tiles, tm, LANES), lambda i: (0, i, 0)),
                  pl.BlockSpec((d, d), lambda i: (0, 0)),
                  pl.BlockSpec((1, d), lambda i: (0, 0))],
        out_specs=pl.BlockSpec((tm, d), lambda i: (i, 0)),
        compiler_params=_cparams(("parallel",)), name="s5_glu_norm")(
            z_tiles, w_glu, norm_w.reshape(1, d))


def _sgu_body(u_ref, v_ref, lnw_ref, lnb_ref, w_ref, bs_ref, nw_ref, o_ref,
              vb_ref, x_ref, *, nc):
    v = _gelu_tanh(v_ref[...])
    mu = jnp.mean(v, axis=-1, keepdims=True)
    vc = v - mu
    var = jnp.mean(vc * vc, axis=-1, keepdims=True)
    vb_ref[...] = (vc * lax.rsqrt(var + LN_EPS) * lnw_ref[...] + lnb_ref[...]).astype(BF16)
    hd = SGU_WIDTH // SGU_HEADS
    for h in range(SGU_HEADS):
        cs = slice(h * hd, (h + 1) * hd)
        rhs = jnp.concatenate(
            [vb_ref[c * SGU_CHUNK:(c + 1) * SGU_CHUNK, cs] for c in range(nc)], axis=1)
        zz = _dot(w_ref[h], rhs)
        for c in range(nc):
            rs = slice(c * SGU_CHUNK, (c + 1) * SGU_CHUNK)
            z = zz[:, c * hd:(c + 1) * hd] + bs_ref[:, cs]
            x_ref[rs, cs] = _gelu_tanh(u_ref[rs, cs]) * z
    x = x_ref[...]
    ms = jnp.mean(x * x, axis=-1, keepdims=True)
    o_ref[...] = (x * lax.rsqrt(ms + RMS_EPS) * nw_ref[...]).astype(BF16)


def _sgu_mixer(proj3, ln_w, ln_b, w_s, b_s, norm_w, tl):
    nb, seq, _ = proj3.shape
    nc = tl // SGU_CHUNK
    hd = SGU_WIDTH // SGU_HEADS
    causal = jnp.tril(jnp.ones((SGU_CHUNK, SGU_CHUNK), dtype=bool))
    w_causal = jnp.where(causal[None], w_s, jnp.zeros_like(w_s)).astype(BF16)
    bias = jnp.repeat(b_s.astype(F32).T, hd, axis=1)
    u_blk = 0
    row = lambda a: a.astype(F32).reshape(1, SGU_WIDTH)
    const = lambda shp: pl.BlockSpec(shp, lambda b, t: (0,) * len(shp))
    body = functools.partial(_sgu_body, nc=nc)
    return pl.pallas_call(
        body,
        out_shape=jax.ShapeDtypeStruct((nb, seq, SGU_WIDTH), BF16),
        grid=(nb, seq // tl),
        in_specs=[pl.BlockSpec((None, tl, SGU_WIDTH), lambda b, t: (b, t, u_blk)),
                  pl.BlockSpec((None, tl, SGU_WIDTH), lambda b, t: (b, t, u_blk + 1)),
                  const((1, SGU_WIDTH)), const((1, SGU_WIDTH)),
                  const((SGU_HEADS, SGU_CHUNK, SGU_CHUNK)),
                  const((SGU_CHUNK, SGU_WIDTH)), const((1, SGU_WIDTH))],
        out_specs=pl.BlockSpec((None, tl, SGU_WIDTH), lambda b, t: (b, t, 0)),
        scratch_shapes=[pltpu.VMEM((tl, SGU_WIDTH), BF16),
                        pltpu.VMEM((tl, SGU_WIDTH), F32)],
        compiler_params=_cparams(("parallel", "parallel")), name="sgu_mixer")(
            proj3, proj3, row(ln_w), row(ln_b), w_causal, bias, row(norm_w))


HG_PAIR = 2
HG_PAIR_W = HG_PAIR * HG_HEAD_DIM
HG_FAST_BLOCK = 32
HG_FAST_MIN_LOG_DECAY = -60.0
_NT = (((1,), (1,)), ((), ()))
_TN = (((0,), (0,)), ((), ()))


def _hgrn_tables():
    c = HG_CHUNK
    t = jnp.arange(c)[:, None]
    r = jnp.arange(c)[None, :]
    mats, masks = [], []
    for lev in range(HG_LEVELS):
        m = 1 << lev
        mid = (t // (2 * m)) * (2 * m) + m
        later = t >= mid
        mats.append(jnp.where(later, (r >= mid) & (r <= t), (r > t) & (r < mid)))
        same = (t // (2 * m)) == (r // (2 * m))
        masks.append(same & later & (r < mid))
    tri = r <= t
    mats.append(tri)
    masks.append(t == r)
    fb = HG_FAST_BLOCK
    fast = [masks[5], ((t // fb) == (r // fb)) & tri]
    pair = lambda m: jnp.tile(m.astype(F32), (1, HG_PAIR))
    return dict(
        tri=tri.astype(BF16),
        mst=jnp.concatenate(mats, axis=0).astype(BF16),
        masks=jnp.stack(masks).astype(F32),
        fmasks=jnp.stack([pair(m) for m in fast]))


def _block_diag(x):
    z = jnp.zeros((x.shape[0], HG_HEAD_DIM), x.dtype)
    top = jnp.concatenate([x[:, :HG_HEAD_DIM], z], axis=1)
    bot = jnp.concatenate([z, x[:, HG_HEAD_DIM:]], axis=1)
    return jnp.concatenate([top, bot], axis=0)


def _rows(v, n):
    return jnp.broadcast_to(v, (n, v.shape[1]))


def _hgrn_finish(o, gv, nw, o_ref, rs):
    for h in range(HG_PAIR):
        cs = slice(h * HG_HEAD_DIM, (h + 1) * HG_HEAD_DIM)
        oh = o[:, cs]
        ms = jnp.mean(oh * oh, axis=-1, keepdims=True)
        gg = gv[:, cs]
        o_ref[rs, cs] = (oh * lax.rsqrt(ms + RMS_EPS) * nw * (gg * _sigmoid(gg))).astype(BF16)


def _hgrn_body(q_ref, f_ref, i_ref, g_ref, par_ref, nw_ref, tri_ref, mst_ref, mask_ref,
               fmask_ref, o_ref, st_ref, qs_ref, kk_ref, lf_ref, b_ref, *, nc):
    c = HG_CHUNK
    hd = HG_HEAD_DIM
    wb = HG_PAIR_W
    t = pl.program_id(2)

    @pl.when(t == 0)
    def _():
        st_ref[...] = jnp.zeros_like(st_ref)

    log_lb = par_ref[0:1, :]
    log_1m_lb = par_ref[1:2, :]
    one_m_lb = par_ref[2:3, :]
    nw = nw_ref[...]
    tri = tri_ref[...]
    row = lax.broadcasted_iota(jnp.int32, (c, wb), 0)

    wmin = None
    for ch in range(nc):
        rs = slice(ch * c, (ch + 1) * c)
        fr = f_ref[rs, :]
        e = jnp.exp(-jnp.abs(fr))
        r = 1.0 / (1.0 + e)
        sig_neg = jnp.where(fr >= 0, e * r, r)
        log_sig = jnp.minimum(fr, 0.0) - jnp.log(1.0 + e)
        y = log_1m_lb + log_sig
        log_f = jnp.maximum(log_lb, y) + jnp.log(1.0 + jnp.exp(-jnp.abs(log_lb - y)))
        hi = log_f.astype(BF16)
        lo = (log_f - hi.astype(F32)).astype(BF16)
        b = _dot(tri, hi) + _dot(tri, lo)
        qv = q_ref[rs, :]
        qs_ref[rs, :] = qv * _sigmoid(qv)
        kk_ref[rs, :] = one_m_lb * sig_neg
        lf_ref[rs, :] = log_f
        b_ref[rs, :] = b
        fb = HG_FAST_BLOCK
        starts = jnp.concatenate(
            [jnp.zeros((fb, wb), F32)]
            + [_rows(b[j * fb - 1:j * fb, :], fb) for j in range(1, c // fb)], axis=0)
        w = b - starts
        wmin = w if wmin is None else jnp.minimum(wmin, w)
    fast = jnp.min(wmin) >= HG_FAST_MIN_LOG_DECAY

    @pl.when(fast)
    def _():
        for ch in range(nc):
            rs = slice(ch * c, (ch + 1) * c)
            b = b_ref[rs, :]
            qs = qs_ref[rs, :]
            kk = kk_ref[rs, :]
            vv = i_ref[rs, :].astype(BF16)
            half = c // 2
            b63 = _rows(b[half - 1:half, :], half)
            zero_half = jnp.zeros((half, wb), BF16)
            q6 = jnp.concatenate(
                [zero_half, (qs[half:, :] * jnp.exp(b[half:, :] - b63)).astype(BF16)], axis=0)
            k6 = jnp.concatenate(
                [(kk[:half, :] * jnp.exp(b63 - b[:half, :])).astype(BF16), zero_half], axis=0)
            scores = lax.dot_general(q6, _block_diag(k6), _NT, preferred_element_type=F32)
            bm = jnp.concatenate([_rows(b[31:32, :], 64), _rows(b[95:96, :], 64)], axis=0)
            a5 = jnp.exp(jnp.where((row % 64) >= 32, b - bm, bm - b))
            fb = HG_FAST_BLOCK
            starts = jnp.concatenate(
                [jnp.zeros((fb, wb), F32)]
                + [_rows(b[j * fb - 1:j * fb, :], fb) for j in range(1, c // fb)], axis=0)
            w = b - starts
            for lev, (aq, ak) in enumerate([(a5, a5), (jnp.exp(w), jnp.exp(-w))]):
                sc = lax.dot_general((qs * aq).astype(BF16), _block_diag((kk * ak).astype(BF16)),
                                     _NT, preferred_element_type=F32)
                scores = scores + fmask_ref[lev] * sc
            st = st_ref[...]
            v_bd = _block_diag(vv)
            o = (_dot(scores.astype(BF16), v_bd)
                 + lax.dot_general((qs * jnp.exp(b)).astype(BF16), st.astype(BF16), _NT,
                                   preferred_element_type=F32))
            bend = b[c - 1:c, :]
            kd = (kk * jnp.exp(bend - b)).astype(BF16)
            st_ref[...] = st * jnp.exp(bend) + lax.dot_general(
                v_bd, _block_diag(kd), _TN, preferred_element_type=F32)
            _hgrn_finish(o, g_ref[rs, :], nw, o_ref, rs)

    @pl.when(jnp.logical_not(fast))
    def _():
        mst = mst_ref[...]

        def chunk(ch, carry):
            rs = pl.ds(pl.multiple_of(ch * c, c), c)
            log_f = lf_ref[rs, :]
            hi = log_f.astype(BF16)
            lo = (log_f - hi.astype(F32)).astype(BF16)
            ee = _dot(mst, hi) + _dot(mst, lo)
            qs = qs_ref[rs, :]
            kk = kk_ref[rs, :]
            vv = i_ref[rs, :].astype(BF16)
            bcum = ee[HG_LEVELS * c:(HG_LEVELS + 1) * c, :]
            outs = []
            for h in range(HG_PAIR):
                cs = slice(h * hd, (h + 1) * hd)
                qh = qs[:, cs]
                kh = kk[:, cs]
                vh = vv[:, cs]
                scores = mask_ref[HG_LEVELS] * jnp.sum(qh * kh, axis=-1, keepdims=True)
                for lev in range(HG_LEVELS):
                    a = jnp.exp(ee[lev * c:(lev + 1) * c, cs])
                    sc = lax.dot_general((qh * a).astype(BF16), (kh * a).astype(BF16), _NT,
                                         preferred_element_type=F32)
                    scores = scores + mask_ref[lev] * sc
                bh = bcum[:, cs]
                st = st_ref[cs, cs]
                outs.append(_dot(scores.astype(BF16), vh)
                            + lax.dot_general((qh * jnp.exp(bh)).astype(BF16), st.astype(BF16),
                                              _NT, preferred_element_type=F32))
                bend = bh[c - 1:c, :]
                kd = (kh * jnp.exp(bend - bh)).astype(BF16)
                st_ref[cs, cs] = st * jnp.exp(bend) + lax.dot_general(
                    vh, kd, _TN, preferred_element_type=F32)
            _hgrn_finish(jnp.concatenate(outs, axis=1), g_ref[rs, :], nw, o_ref, rs)
            return carry

        lax.fori_loop(0, nc, chunk, 0)


def _hgrn_mixer(proj3, lb, norm_w, tl):
    nb, seq, _ = proj3.shape
    nc = tl // HG_CHUNK
    wb = HG_PAIR_W
    q0 = (2 * SGU_WIDTH) // wb
    nblk = HG_WIDTH // wb
    lbf = lb.astype(F32)
    par = jnp.stack([jnp.log(lbf), jnp.log1p(-lbf), 1.0 - lbf])
    tb = _hgrn_tables()
    sec = lambda k: pl.BlockSpec((None, tl, wb), functools.partial(
        lambda b, h, t, k: (b, t, q0 + k * nblk + h), k=k))
    const = lambda a: pl.BlockSpec(a.shape, lambda b, h, t: (0,) * a.ndim)
    body = functools.partial(_hgrn_body, nc=nc)
    return pl.pallas_call(
        body,
        out_shape=jax.ShapeDtypeStruct((nb, seq, HG_WIDTH), BF16),
        grid=(nb, nblk, seq // tl),
        in_specs=[sec(0), sec(1), sec(2), sec(3),
                  pl.BlockSpec((3, wb), lambda b, h, t: (0, h)),
                  pl.BlockSpec((1, HG_HEAD_DIM), lambda b, h, t: (0, 0)),
                  const(tb["tri"]), const(tb["mst"]), const(tb["masks"]),
                  const(tb["fmasks"])],
        out_specs=pl.BlockSpec((None, tl, wb), lambda b, h, t: (b, t, h)),
        scratch_shapes=[pltpu.VMEM((wb, wb), F32)] + [pltpu.VMEM((tl, wb), F32)] * 4,
        compiler_params=_cparams(("parallel", "parallel", "arbitrary")),
        name="hgrn2_mixer")(
            proj3, proj3, proj3, proj3, par,
            norm_w.astype(F32).reshape(1, HG_HEAD_DIM),
            tb["tri"], tb["mst"], tb["masks"], tb["fmasks"])


def kernel(x, p, w_in, s5_lam_re, s5_lam_im, s5_log_step, s5_b_re, s5_b_im, s5_c_re, s5_c_im, s5_d, s5_w_glu, sgu_ln_w, sgu_ln_b, sgu_w, sgu_b, hg_lb_logits, hg_norm_w, norm_a_w, norm_b_w, w_out, ln1_w, ln1_b, w_ffn_in, w_ffn_out, ln2_w, ln2_b, w_ple_in, w_ple_gate, ln3_w, ln3_b):
    nb, seq, d_model = x.shape
    depth = w_in.shape[0]
    t_rows = nb * seq
    proj_w = w_in.shape[2]
    d_ff = w_ffn_out.shape[1]
    alpha = (2.0 * depth) ** 0.25

    tm = _pick(t_rows, (1024, 512, 256, 128))
    tm_wide = _pick(t_rows, (2048, 1024, 512, 256, 128))
    tm_half = _pick(t_rows, (512, 256, 128))
    tm_ln = _pick(t_rows, (512, 256, 128))
    tl_s5 = _pick(seq, (1024, 512, 256, 128))
    tl_sgu = _pick(seq, (1024, 512, 256, 128))
    tl_hg = _pick(seq, (2048, 1024, 512, 256, 128))
    tn_in = _pick(proj_w - S5_WIDTH, (1024, 512))

    lbs =jnp.cumsum(jax.nn.softmax(hg_lb_logits.astype(F32), axis=0), axis=0)
    lbs = lbs - lbs[0:1]

    x_rows = x.reshape(t_rows, d_model).astype(F32)
    res = _Residual(alpha, h=x_rows)
    h16 = _layer_to_bf16(x_rows.reshape(1, t_rows, d_model), 0)
    p_rows = p.reshape(depth, t_rows, -1)

    w_in16 = _layer_to_bf16(w_in, 0)
    for l in range(depth):
        if l == 0:
            xa_tiles = _s5_in_proj(h16, w_in16, tm)
        else:
            lw, lb = ln3_w[l - 1], ln3_b[l - 1]
            xa_tiles, mu, rstd, h16 = _ln_fused_mm(
                xres, lw, lb, [(None, w_in16, d_model, 0, 0)], [],
                lambda accs, ex: [accs[0]], [F32], S5_WIDTH, tm, S5_WIDTH // 4, "in_proj_s5",
                emit_h16=True, out_tiles=True)
            res = _Residual(alpha, x=xres, mu=mu, rstd=rstd, w=lw, b=lb)
        xa_tiles = xa_tiles.reshape(-1, nb, seq, LANES)
        proj, wf, wo = _fused_mm(
            [(h16, w_in16, d_model, 0, S5_WIDTH // tn_in)], [],
            lambda accs, ex: [accs[0]], [F32], proj_w - S5_WIDTH, tm, tn_in, "in_proj",
            side_casts=[(w_ffn_in, l), (w_out, l)])
        proj3 = proj.reshape(nb, seq, proj_w - S5_WIDTH)
        prep = _s5_prepare(s5_lam_re[l], s5_lam_im[l], s5_log_step[l], s5_b_re[l],
                           s5_b_im[l], s5_c_re[l], s5_c_im[l], s5_d[l])
        z_tiles = _s5_mixer(xa_tiles, prep, tl_s5).reshape(-1, t_rows, LANES)
        ya = _s5_glu_norm(z_tiles, _layer_to_bf16(s5_w_glu, l), norm_a_w[l], tm_half)
        yb = _sgu_mixer(proj3, sgu_ln_w[l], sgu_ln_b[l], sgu_w[l], sgu_b[l],
                        norm_b_w[l], tl_sgu).reshape(t_rows, SGU_WIDTH)
        yc = _hgrn_mixer(proj3, lbs[l], hg_norm_w[l], tl_hg).reshape(t_rows, HG_WIDTH)
        (xres,) = _fused_mm(
            [(ya, wo, S5_WIDTH, 0, 0), (yb, wo, SGU_WIDTH, 1, 0), (yc, wo, HG_WIDTH, 1, 0)],
            res.extras,
            functools.partial(lambda accs, ex, res: [res(ex) + (accs[0] + accs[1] + accs[2])],
                              res=res),
            [F32], d_model, tm, 1024, "out_proj")

        hid, mu, rstd, wfo, wpg = _ln_fused_mm(
            xres, ln1_w[l], ln1_b[l],
            [(None, wf, d_model, 0, 0), (None, wf, d_model, 0, d_ff // 256)], [],
            lambda accs, ex: [accs[0] * _sigmoid(accs[0]) * accs[1]],
            [BF16], d_ff, tm_wide, 256, "ffn_in",
            side_casts=[(w_ffn_out, l), (w_ple_gate, l)])
        res = _Residual(alpha, x=xres, mu=mu, rstd=rstd, w=ln1_w[l], b=ln1_b[l])
        (xres,) = _fused_mm(
            [(hid, wfo, d_ff, 0, 0)], res.extras,
            functools.partial(lambda accs, ex, res: [res(ex) + accs[0]], res=res),
            [F32], d_model, tm_half, 1024, "ffn_out", cols_outer=True)

        p16 = _layer_to_bf16(p_rows, l)
        xres, _, _, *next_w_in = _ln_fused_mm(
            xres, ln2_w[l], ln2_b[l],
            [(None, wpg, d_model, 0, 0),
             (p16, _layer_to_bf16(w_ple_in, l), p16.shape[1], 0, 0)], [],
            lambda accs, ex, resid: [resid + accs[1] * _sigmoid(accs[0])],
            [F32], d_model, tm, 512, "ple", self_alpha=alpha,
            side_casts=[(w_in, l + 1)] if l + 1 < depth else [])
        if next_w_in:
            w_in16 = next_w_in[0]

    out = _layer_norm_final(xres, ln3_w[depth - 1], ln3_b[depth - 1], tm_ln)
    return out.reshape(nb, seq, d_model).astype(x.dtype)
```

```python
import functools
import math

import jax
import jax.numpy as jnp
from jax import lax
from jax.experimental import pallas as pl
from jax.experimental.pallas import tpu as pltpu

F32 = jnp.float32
BF16 = jnp.bfloat16

V7X_VMEM_LIMIT_BYTES = 56 * 1024 * 1024
LANES = 128

LN_EPS = 1e-5
RMS_EPS = 1e-6

S5_WIDTH = 1024
S5_GROUP_CH = 16
S5_STATE = 64
S5_CHUNK = 8
S5_TILE_GROUPS = LANES // S5_GROUP_CH
S5_TILE_STATE = S5_TILE_GROUPS * S5_STATE

SGU_WIDTH = 1024
SGU_CHUNK = 128
SGU_HEADS = 8

HG_WIDTH = 2048
HG_HEAD_DIM = 128
HG_CHUNK = 128
HG_LEVELS = 7


def _cparams(sem):
    return pltpu.CompilerParams(dimension_semantics=sem,
                                vmem_limit_bytes=V7X_VMEM_LIMIT_BYTES)


def _gelu_tanh(x):
    c = math.sqrt(2.0 / math.pi)
    return x * (0.5 * (1.0 + jnp.tanh(c * (x + 0.044715 * (x * x * x)))))


def _sigmoid(x):
    return 1.0 / (1.0 + jnp.exp(-x))


def _dot(a, b):
    return jnp.dot(a, b, preferred_element_type=F32)


def _pick(n, prefs):
    for p in prefs:
        if n % p == 0:
            return p
    return n


CAST_BLOCK_BYTES = 8 * 1024 * 1024


def _cast_body(w_ref, o_ref):
    o_ref[...] = w_ref[...].astype(BF16)


def _layer_to_bf16(w, layer):
    _, rows, cols = w.shape
    fits = [tr for tr in (8192, 4096, 2048, 1024, 512, 256, 128, 64, 32, 16)
            if rows % tr == 0 and tr * cols * 4 <= CAST_BLOCK_BYTES]
    tr = fits[0]
    return pl.pallas_call(
        _cast_body,
        out_shape=jax.ShapeDtypeStruct((rows, cols), BF16),
        grid=(rows // tr,),
        in_specs=[pl.BlockSpec((None, tr, cols), lambda i: (layer, i, 0))],
        out_specs=pl.BlockSpec((tr, cols), lambda i: (i, 0)),
        compiler_params=_cparams(("parallel",)), name="cast_bf16")(w)


MM_SUB_ROWS = 256


def _mm_body(*refs, a_of_dot, n_a, extra_kinds, n_side, epilogue, sub_rows):
    n_dot = len(a_of_dot)
    n_extra = len(extra_kinds)
    n_in = n_a + n_dot + n_extra
    a_refs = refs[:n_a]
    w_refs = refs[n_a:n_a + n_dot]
    extra = refs[n_a + n_dot:n_in]
    side_in = refs[n_in:n_in + n_side]
    out_refs = refs[n_in + n_side:len(refs) - n_side]
    side_out = refs[len(refs) - n_side:]
    for si, so in zip(side_in, side_out):
        so[...] = si[...].astype(BF16)
    rows = out_refs[0].shape[0]
    sub = min(rows, sub_rows)
    for r in range(rows // sub):
        rs = slice(r * sub, (r + 1) * sub)
        accs = [_dot(a_refs[ai][rs, :], w[...]) for ai, w in zip(a_of_dot, w_refs)]
        outs = epilogue(accs, [e[...] if kind == "row" else e[rs, :]
                               for e, kind in zip(extra, extra_kinds)])
        for o_ref, o in zip(out_refs, outs):
            o_ref[rs, :] = o.astype(o_ref.dtype)


BF16_ROW_TILE = 16


def _fused_mm(dots, extras, epilogue, out_dtypes, n_cols, tm, tn, name,
              sub_rows=MM_SUB_ROWS, side_casts=(), cols_outer=False):
    t_rows = dots[0][0].shape[0]
    ni, nj = t_rows // tm, n_cols // tn
    grid = (nj, ni) if cols_outer else (ni, nj)

    def spec(shape, index_map, **kw):
        if cols_outer:
            return pl.BlockSpec(shape, lambda g0, g1: index_map(g1, g0), **kw)
        return pl.BlockSpec(shape, index_map, **kw)

    w_mode = dict(pipeline_mode=pl.Buffered(1)) if cols_outer else {}
    in_specs, args = [], []
    a_of_dot = []
    for a, _, _, _, _ in dots:
        known = [k for k, seen in enumerate(args) if seen is a]
        if known:
            a_of_dot.append(known[0])
            continue
        a_of_dot.append(len(args))
        in_specs.append(spec((tm, a.shape[1]), lambda i, j: (i, 0)))
        args.append(a)
    n_a = len(args)
    for _, w, rb, ri, co in dots:
        in_specs.append(spec((rb, tn), functools.partial(
            lambda i, j, ri, co: (ri, j + co), ri=ri, co=co), **w_mode))
        args.append(w)
    for arr, kind in extras:
        if kind == "tile":
            in_specs.append(spec((tm, tn), lambda i, j: (i, j)))
        elif kind == "stat":
            in_specs.append(spec((tm, LANES), lambda i, j: (i, 0)))
        else:
            in_specs.append(spec((1, tn), lambda i, j: (0, j)))
        args.append(arr)
    out_shape = [jax.ShapeDtypeStruct((t_rows, n_cols), dt) for dt in out_dtypes]
    out_specs = [spec((tm, tn), lambda i, j: (i, j)) for _ in out_dtypes]
    n_steps = grid[0] * grid[1]
    for stack, layer in side_casts:
        _, rows, cols = stack.shape
        tiles = rows // BF16_ROW_TILE
        n_slabs = max(d for d in range(1, min(tiles, n_steps) + 1) if tiles % d == 0)
        slab = functools.partial(lambda g0, g1, last: jnp.minimum(g0 * grid[1] + g1, last),
                                 last=n_slabs - 1)
        in_specs.append(pl.BlockSpec(
            (None, rows // n_slabs, cols),
            functools.partial(lambda g0, g1, layer, slab: (layer, slab(g0, g1), 0),
                              layer=layer, slab=slab)))
        args.append(stack)
        out_shape.append(jax.ShapeDtypeStruct((rows, cols), BF16))
        out_specs.append(pl.BlockSpec(
            (rows // n_slabs, cols),
            functools.partial(lambda g0, g1, slab: (slab(g0, g1), 0), slab=slab)))
    body = functools.partial(_mm_body, a_of_dot=tuple(a_of_dot), n_a=n_a,
                             extra_kinds=tuple(kind for _, kind in extras),
                             n_side=len(side_casts), epilogue=epilogue, sub_rows=sub_rows)
    return pl.pallas_call(
        body, out_shape=out_shape, grid=grid, in_specs=in_specs, out_specs=out_specs,
        compiler_params=_cparams(("parallel", "arbitrary")), name=name)(*args)


def _ln_rows(x):
    mu = jnp.mean(x, axis=-1, keepdims=True)
    xc = x - mu
    var = jnp.mean(xc * xc, axis=-1, keepdims=True)
    return xc, mu, lax.rsqrt(var + LN_EPS)


LN_ROW_GROUP = 8


def _ln_final_body(x_ref, w_ref, b_ref, h_ref):
    w = w_ref[...]
    b = b_ref[...]
    for g in range(x_ref.shape[0] // LN_ROW_GROUP):
        rs = slice(g * LN_ROW_GROUP, (g + 1) * LN_ROW_GROUP)
        xc, _, rstd = _ln_rows(x_ref[rs, :])
        h_ref[rs, :] = xc * rstd * w + b


def _layer_norm_final(x, w, b, tm):
    t_rows, d = x.shape
    rows = pl.BlockSpec((tm, d), lambda i: (i, 0))
    vec = pl.BlockSpec((1, d), lambda i: (0, 0))
    return pl.pallas_call(
        _ln_final_body, out_shape=jax.ShapeDtypeStruct((t_rows, d), F32),
        grid=(t_rows // tm,), in_specs=[rows, vec, vec], out_specs=rows,
        compiler_params=_cparams(("parallel",)), name="layer_norm")(
            x, w.reshape(1, d), b.reshape(1, d))


def _ln_mm_body(*refs, a_of_dot, n_other_a, extra_kinds, n_side, n_out, emit_h16, out_tiles,
                self_alpha, epilogue, sub_rows, n_slab, slab_rows):
    n_dot = len(a_of_dot)
    n_extra = len(extra_kinds)
    x_ref, lnw_ref, lnb_ref = refs[:3]
    pos = 3
    other_a = refs[pos:pos + n_other_a]
    pos += n_other_a
    w_refs = refs[pos:pos + n_dot]
    pos += n_dot
    extra = refs[pos:pos + n_extra]
    pos += n_extra
    side_in = refs[pos:pos + n_side]
    pos += n_side
    out_refs = refs[pos:pos + n_out]
    pos += n_out
    mu_ref, rs_ref = refs[pos:pos + 2]
    pos += 2
    h16_ref = refs[pos] if emit_h16 else None
    pos += int(emit_h16)
    side_out = refs[pos:pos + n_side]
    pos += n_side
    a_slots, mu_slots, rs_slots = refs[pos:pos + 2], refs[pos + 2:pos + 4], refs[pos + 4:pos + 6]

    g = pl.program_id(0)
    j = pl.program_id(1)
    group = 2 * LN_ROW_GROUP

    def side_jobs(slot):
        a_sc, mu_sc, rs_sc = a_slots[slot], mu_slots[slot], rs_slots[slot]
        for si, so in zip(side_in, side_out):
            so[...] = si[...].astype(BF16)
        row0 = pl.multiple_of(jnp.minimum(j, n_slab - 1) * slab_rows, slab_rows)
        w = lnw_ref[...]
        b = lnb_ref[...]
        for grp in range(slab_rows // group):
            ys = []
            for half in range(2):
                r8 = slice(grp * group + half * LN_ROW_GROUP,
                           grp * group + (half + 1) * LN_ROW_GROUP)
                xc, mu, rstd = _ln_rows(x_ref[r8, :])
                ys.append(xc * rstd * w + b)
                mu_b = jnp.broadcast_to(mu, (LN_ROW_GROUP, LANES))
                rs_b = jnp.broadcast_to(rstd, (LN_ROW_GROUP, LANES))
                mu_ref[r8, :] = mu_b
                rs_ref[r8, :] = rs_b
                dst = pl.ds(row0 + grp * group + half * LN_ROW_GROUP, LN_ROW_GROUP)
                mu_sc[dst, :] = mu_b
                rs_sc[dst, :] = rs_b
            y16 = jnp.concatenate(ys, axis=0).astype(BF16)
            a_sc[pl.ds(row0 + grp * group, group), :] = y16
            if emit_h16:
                h16_ref[grp * group:(grp + 1) * group, :] = y16

    def matmul(slot):
        a_sc, mu_sc, rs_sc = a_slots[slot], mu_slots[slot], rs_slots[slot]
        rows = a_sc.shape[0]
        sub = min(rows, sub_rows)
        for r in range(rows // sub):
            rs = slice(r * sub, (r + 1) * sub)
            a_ln = a_sc[rs, :]
            accs = [_dot(a_ln if ai < 0 else other_a[ai][rs, :], w[...])
                    for ai, w in zip(a_of_dot, w_refs)]
            ex = [e[...] if kind == "row" else e[rs, :] for e, kind in zip(extra, extra_kinds)]
            if self_alpha is not None:
                x_tile, w_row, b_row = ex[:3]
                ex = ex[3:]
                rep = x_tile.shape[1] // LANES
                wide = lambda s: jnp.concatenate([s] * rep, axis=1)
                resid = self_alpha * ((x_tile - wide(mu_sc[rs, :]))
                                      * wide(rs_sc[rs, :]) * w_row + b_row)
                outs = epilogue(accs, ex, resid)
            else:
                outs = epilogue(accs, ex)
            for o_ref, o in zip(out_refs, outs):
                if out_tiles:
                    for c in range(o_ref.shape[0]):
                        o_ref[c, rs, :] = o[:, c * LANES:(c + 1) * LANES].astype(o_ref.dtype)
                else:
                    o_ref[rs, :] = o.astype(o_ref.dtype)

    @pl.when(g == 0)
    def _():
        side_jobs(0)

    for parity in (0, 1):
        @pl.when(jnp.logical_and(g > 0, lax.rem(g, 2) == parity))
        def _():
            side_jobs(parity)
            matmul(1 - parity)


def _ln_fused_mm(x, ln_w, ln_b, dots, extras, epilogue, out_dtypes, n_cols, tm, tn, name,
                 sub_rows=MM_SUB_ROWS, side_casts=(), emit_h16=False, out_tiles=False,
                 self_alpha=None):
    t_rows, d = x.shape
    ni, nj = t_rows // tm, n_cols // tn
    grid = (ni + 1, nj)
    n_slab = max(s for s in range(1, nj + 1)
                 if tm % s == 0 and (tm // s) % (2 * LN_ROW_GROUP) == 0)
    slab_rows = tm // n_slab
    prev = lambda g: jnp.maximum(g - 1, 0)
    first_col = lambda g, j: jnp.where(g > 0, j, 0)
    slab_idx = lambda g, j: (jnp.where(g < ni, g * n_slab + jnp.minimum(j, n_slab - 1),
                                       ni * n_slab - 1), 0)
    vec = lambda a: a.astype(F32).reshape(1, d)
    in_specs = [pl.BlockSpec((slab_rows, d), slab_idx),
                pl.BlockSpec((1, d), lambda g, j: (0, 0)),
                pl.BlockSpec((1, d), lambda g, j: (0, 0))]
    args = [x, vec(ln_w), vec(ln_b)]
    a_of_dot, other = [], []
    for a, _, _, _, _ in dots:
        if a is None:
            a_of_dot.append(-1)
            continue
        a_of_dot.append(len(other))
        other.append(a)
        in_specs.append(pl.BlockSpec((tm, a.shape[1]), lambda g, j: (prev(g), 0)))
        args.append(a)
    for _, w, rb, ri, co in dots:
        in_specs.append(pl.BlockSpec((rb, tn), functools.partial(
            lambda g, j, ri, co: (ri, j + co), ri=ri, co=co)))
        args.append(w)
    extras = list(extras)
    if self_alpha is not None:
        extras = [(x, "tile"), (vec(ln_w), "row"), (vec(ln_b), "row")] + extras
    for arr, kind in extras:
        if kind == "tile":
            in_specs.append(pl.BlockSpec((tm, tn), lambda g, j: (prev(g), j)))
        elif kind == "stat":
            in_specs.append(pl.BlockSpec((tm, LANES), lambda g, j: (prev(g), 0)))
        else:
            in_specs.append(pl.BlockSpec((1, tn), lambda g, j: (0, j)))
        args.append(arr)
    if out_tiles:
        n_tiles = n_cols // LANES
        out_shape = [jax.ShapeDtypeStruct((n_tiles, t_rows, LANES), dt) for dt in out_dtypes]
        out_specs = [pl.BlockSpec((tn // LANES, tm, LANES),
                                  lambda g, j: (first_col(g, j), prev(g), 0))
                     for _ in out_dtypes]
    else:
        out_shape = [jax.ShapeDtypeStruct((t_rows, n_cols), dt) for dt in out_dtypes]
        out_specs = [pl.BlockSpec((tm, tn), lambda g, j: (prev(g), first_col(g, j)))
                     for _ in out_dtypes]
    n_out = len(out_shape)
    for _ in range(2):
        out_shape.append(jax.ShapeDtypeStruct((t_rows, LANES), F32))
        out_specs.append(pl.BlockSpec((slab_rows, LANES), slab_idx))
    if emit_h16:
        out_shape.append(jax.ShapeDtypeStruct((t_rows, d), BF16))
        out_specs.append(pl.BlockSpec((slab_rows, d), slab_idx))
    n_steps = grid[0] * grid[1]
    for stack, layer in side_casts:
        _, rows, cols = stack.shape
        tiles = rows // BF16_ROW_TILE
        n_cast = max(s for s in range(1, min(tiles, n_steps) + 1) if tiles % s == 0)
        slab = functools.partial(lambda g, j, last: jnp.minimum(g * nj + j, last),
                                 last=n_cast - 1)
        in_specs.append(pl.BlockSpec(
            (None, rows // n_cast, cols),
            functools.partial(lambda g, j, layer, slab: (layer, slab(g, j), 0),
                              layer=layer, slab=slab)))
        args.append(stack)
        out_shape.append(jax.ShapeDtypeStruct((rows, cols), BF16))
        out_specs.append(pl.BlockSpec(
            (rows // n_cast, cols),
            functools.partial(lambda g, j, slab: (slab(g, j), 0), slab=slab)))
    body = functools.partial(
        _ln_mm_body, a_of_dot=tuple(a_of_dot), n_other_a=len(other),
        extra_kinds=tuple(kind for _, kind in extras), n_side=len(side_casts), n_out=n_out,
        emit_h16=emit_h16, out_tiles=out_tiles, self_alpha=self_alpha, epilogue=epilogue,
        sub_rows=sub_rows, n_slab=n_slab, slab_rows=slab_rows)
    return pl.pallas_call(
        body, out_shape=out_shape, grid=grid, in_specs=in_specs, out_specs=out_specs,
        scratch_shapes=([pltpu.VMEM((tm, d), BF16)] * 2
                        + [pltpu.VMEM((tm, LANES), F32)] * 4),
        compiler_params=_cparams(("arbitrary", "arbitrary")), name=name)(*args)


class _Residual:
    def __init__(self, alpha, h=None, x=None, mu=None, rstd=None, w=None, b=None):
        self.alpha = alpha
        if h is not None:
            self.extras = [(h, "tile")]
        else:
            self.extras = [(x, "tile"), (mu, "stat"), (rstd, "stat"),
                           (w.astype(F32).reshape(1, -1), "row"),
                           (b.astype(F32).reshape(1, -1), "row")]

    def __call__(self, ex):
        if len(ex) == 1:
            return self.alpha * ex[0]
        x, mu, rstd, w, b = ex
        rep = x.shape[1] // LANES
        wide = lambda s: jnp.concatenate([s] * rep, axis=1)
        return self.alpha * ((x - wide(mu)) * wide(rstd) * w + b)


def _s5_prepare(lam_re, lam_im, log_step, b_re, b_im, c_re, c_im, d):
    hp = lax.Precision.HIGHEST
    n_tiles = S5_WIDTH // LANES
    lr = jnp.minimum(lam_re.astype(F32), -1e-4)
    li = lam_im.astype(F32)
    dt = jnp.exp(log_step.astype(F32))[:, None]
    mag = jnp.exp(lr * dt)
    ab_re = mag * jnp.cos(li * dt)
    ab_im = mag * jnp.sin(li * dt)
    den = lr * lr + li * li
    nr = ab_re - 1.0
    g_re = (nr * lr + ab_im * li) / den
    g_im = (ab_im * lr - nr * li) / den
    br = b_re.astype(F32)
    bi = b_im.astype(F32)
    bb_re = g_re[..., None] * br - g_im[..., None] * bi
    bb_im = g_re[..., None] * bi + g_im[..., None] * br
    pr, pi = [jnp.ones_like(ab_re)], [jnp.zeros_like(ab_re)]
    for _ in range(S5_CHUNK):
        pr_new = pr[-1] * ab_re - pi[-1] * ab_im
        pi_new = pr[-1] * ab_im + pi[-1] * ab_re
        pr.append(pr_new)
        pi.append(pi_new)
    p_re = jnp.stack(pr)
    p_im = jnp.stack(pi)
    pb_re = (p_re[:S5_CHUNK, :, :, None] * bb_re[None]
             - p_im[:S5_CHUNK, :, :, None] * bb_im[None])
    pb_im = (p_re[:S5_CHUNK, :, :, None] * bb_im[None]
             + p_im[:S5_CHUNK, :, :, None] * bb_re[None])
    cr = c_re.astype(F32)
    ci = c_im.astype(F32)
    kern = (jnp.einsum("gon,jgni->jgoi", cr, pb_re, precision=hp)
            - jnp.einsum("gon,jgni->jgoi", ci, pb_im, precision=hp))
    tg, gc, ns = S5_TILE_GROUPS, S5_GROUP_CH, S5_STATE

    def spread(x2d, rep, row_group, col_group):
        rows, width = x2d.shape
        sel = jnp.tile(jnp.eye(width, dtype=BF16), (1, rep))
        out = jnp.dot(x2d.astype(BF16), sel)
        rg = row_group(jnp.arange(rows))[:, None]
        cg = col_group(jnp.arange(width * rep))[None, :]
        return jnp.where(rg == cg, out, jnp.zeros_like(out))

    def spread_steps(x2d, row_group):
        rows, width = x2d.shape
        cols = jnp.arange(S5_CHUNK * LANES)
        src = jnp.arange(width)
        sel = ((src[:, None] // gc == cols[None, :] // LANES)
               & (src[:, None] % gc == cols[None, :] % gc)).astype(BF16)
        out = jnp.dot(x2d.astype(BF16), sel)
        rg = row_group(jnp.arange(rows))[:, None]
        cg = ((cols // gc) % tg)[None, :]
        return jnp.where(rg == cg, out, jnp.zeros_like(out))

    lag = jnp.arange(S5_CHUNK)[None, :] - jnp.arange(S5_CHUNK)[:, None]
    k_lag = jnp.where((lag >= 0)[:, :, None, None, None],
                      kern[jnp.clip(lag, 0, S5_CHUNK - 1)], 0.0)
    k_lag = k_lag.reshape(S5_CHUNK, S5_CHUNK, n_tiles, tg, gc, gc).transpose(2, 0, 3, 5, 1, 4)
    toep = spread_steps(k_lag.reshape(-1, S5_CHUNK * gc), lambda r: (r // gc) % tg).reshape(
        n_tiles, S5_CHUNK * LANES, S5_CHUNK * LANES)

    def w1_half(pb):
        pb5 = pb[::-1].reshape(S5_CHUNK, n_tiles, tg, ns, gc).transpose(1, 0, 2, 4, 3)
        return spread(pb5.reshape(-1, ns), tg, lambda r: (r // gc) % tg,
                      lambda c: c // ns).reshape(n_tiles, S5_CHUNK * LANES, S5_TILE_STATE)
    w1 = jnp.concatenate([w1_half(pb_re), w1_half(pb_im)], axis=-1)

    ca_re = cr[None] * p_re[1:, :, None, :] - ci[None] * p_im[1:, :, None, :]
    ca_im = cr[None] * p_im[1:, :, None, :] + ci[None] * p_re[1:, :, None, :]

    def p_half(ca):
        ca5 = ca.reshape(S5_CHUNK, n_tiles, tg, gc, ns).transpose(1, 2, 4, 0, 3)
        return spread_steps(ca5.reshape(-1, S5_CHUNK * gc), lambda r: (r // ns) % tg).reshape(
            n_tiles, S5_TILE_STATE, S5_CHUNK * LANES)
    pcat = jnp.concatenate([p_half(ca_re), -p_half(ca_im)], axis=1)

    a16 = jnp.concatenate([p_re[S5_CHUNK].reshape(n_tiles, 1, S5_TILE_STATE),
                           p_im[S5_CHUNK].reshape(n_tiles, 1, S5_TILE_STATE)], axis=-1)
    dd = d.astype(F32).reshape(n_tiles, 1, LANES)
    return w1, toep, pcat, a16, dd


def _s5_body(x_ref, w1_ref, toep_ref, pcat_ref, a16_ref, d_ref, z_ref,
             state_ref, xr_ref, q_ref, yi_ref, sin_ref, yc_ref, yint_ref, *, nb, tl):
    t = pl.program_id(1)
    nk = tl // S5_CHUNK
    n = nb * tl
    ts = S5_TILE_STATE

    @pl.when(t == 0)
    def _():
        state_ref[...] = jnp.zeros_like(state_ref)

    for b in range(nb):
        for tp in range(S5_CHUNK):
            xr_ref[tp, pl.ds(b, nk, stride=nb), :] = x_ref[b, pl.ds(tp, nk, stride=S5_CHUNK), :]
    xr = jnp.concatenate([xr_ref[tp].astype(BF16) for tp in range(S5_CHUNK)], axis=1)
    q_ref[...] = _dot(xr, w1_ref[...])
    yi_ref[...] = _dot(xr, toep_ref[...])

    a_re = a16_ref[:, :ts]
    a_im = a16_ref[:, ts:]
    s = state_ref[...]
    for k in range(nk):
        sin_ref[k * nb:(k + 1) * nb, :] = s
        s_re = s[:, :ts]
        s_im = s[:, ts:]
        s = jnp.concatenate([a_re * s_re - a_im * s_im, a_re * s_im + a_im * s_re],
                            axis=1) + q_ref[k * nb:(k + 1) * nb, :]
    state_ref[...] = s

    yc = yi_ref[...] + _dot(sin_ref[...].astype(BF16), pcat_ref[...])
    for tp in range(S5_CHUNK):
        yc_ref[tp] = yc[:, tp * LANES:(tp + 1) * LANES]
    for b in range(nb):
        for tp in range(S5_CHUNK):
            yint_ref[pl.ds(b * tl + tp, nk, stride=S5_CHUNK), :] = (
                yc_ref[tp, pl.ds(b, nk, stride=nb), :])

    x = x_ref[...].reshape(n, LANES)
    z_ref[...] = _gelu_tanh(yint_ref[...] + d_ref[...] * x).reshape(nb, tl, LANES)


def _s5_in_proj_body(a_ref, w_ref, o_ref):
    rows = a_ref.shape[0]
    sub = min(rows, MM_SUB_ROWS)
    for r in range(rows // sub):
        rs = slice(r * sub, (r + 1) * sub)
        acc = _dot(a_ref[rs, :], w_ref[...])
        for c in range(o_ref.shape[0]):
            o_ref[c, rs, :] = acc[:, c * LANES:(c + 1) * LANES]


def _s5_in_proj(h16, w_in16, tm):
    t_rows, d = h16.shape
    n_tiles = S5_WIDTH // LANES
    return pl.pallas_call(
        _s5_in_proj_body,
        out_shape=jax.ShapeDtypeStruct((n_tiles, t_rows, LANES), F32),
        grid=(t_rows // tm,),
        in_specs=[pl.BlockSpec((tm, d), lambda i: (i, 0)),
                  pl.BlockSpec((d, S5_WIDTH), lambda i: (0, 0))],
        out_specs=pl.BlockSpec((n_tiles, tm, LANES), lambda i: (0, i, 0)),
        compiler_params=_cparams(("parallel",)), name="in_proj_s5")(h16, w_in16)


def _s5_mixer(xa_tiles, prep, tl):
    w1, toep, pcat, a16, dd = prep
    n_tiles, nb, seq, _ = xa_tiles.shape
    nk = tl // S5_CHUNK
    ts = S5_TILE_STATE
    steps = S5_CHUNK * LANES
    body = functools.partial(_s5_body, nb=nb, tl=tl)
    wspec = lambda shp: pl.BlockSpec((None,) + shp, lambda j, t: (j, 0, 0))
    xspec = pl.BlockSpec((None, nb, tl, LANES), lambda j, t: (j, 0, t, 0))
    return pl.pallas_call(
        body,
        out_shape=jax.ShapeDtypeStruct(xa_tiles.shape, F32),
        grid=(n_tiles, seq // tl),
        in_specs=[xspec,
                  wspec((steps, 2 * ts)),
                  wspec((steps, steps)),
                  wspec((2 * ts, steps)),
                  wspec((1, 2 * ts)),
                  wspec((1, LANES))],
        out_specs=xspec,
        scratch_shapes=[pltpu.VMEM((nb, 2 * ts), F32),
                        pltpu.VMEM((S5_CHUNK, nb * nk, LANES), F32),
                        pltpu.VMEM((nb * nk, 2 * ts), F32),
                        pltpu.VMEM((nb * nk, steps), F32),
                        pltpu.VMEM((nb * nk, 2 * ts), F32),
                        pltpu.VMEM((S5_CHUNK, nb * nk, LANES), F32),
                        pltpu.VMEM((nb * tl, LANES), F32)],
        compiler_params=_cparams(("parallel", "arbitrary")), name="s5_mixer")(
            xa_tiles, w1, toep, pcat, a16, dd)


def _glu_body(z_ref, w_ref, nw_ref, o_ref):
    z = jnp.concatenate([z_ref[c] for c in range(z_ref.shape[0])], axis=1)
    y = z * _sigmoid(_dot(z.astype(BF16), w_ref[...]))
    ms = jnp.mean(y * y, axis=-1, keepdims=True)
    o_ref[...] = (y * lax.rsqrt(ms + RMS_EPS) * nw_ref[...]).astype(BF16)


def _s5_glu_norm(z_tiles, w_glu, norm_w, tm):
    n_tiles, t_rows, _ = z_tiles.shape
    d = n_tiles * LANES
    return pl.pallas_call(
        _glu_body,
        out_shape=jax.ShapeDtypeStruct((t_rows, d), BF16),
        grid=(t_rows // tm,),
        in_specs=[pl.BlockSpec((n_tiles, tm, LANES), lambda i: (0, i, 0)),
                  pl.BlockSpec((d, d), lambda i: (0, 0)),
                  pl.BlockSpec((1, d), lambda i: (0, 0))],
        out_specs=pl.BlockSpec((tm, d), lambda i: (i, 0)),
        compiler_params=_cparams(("parallel",)), name="s5_glu_norm")(
            z_tiles, w_glu, norm_w.reshape(1, d))


def _sgu_body(u_ref, v_ref, lnw_ref, lnb_ref, w_ref, bs_ref, nw_ref, o_ref,
              vb_ref, x_ref, *, nc):
    v = _gelu_tanh(v_ref[...])
    mu = jnp.mean(v, axis=-1, keepdims=True)
    vc = v - mu
    var = jnp.mean(vc * vc, axis=-1, keepdims=True)
    vb_ref[...] = (vc * lax.rsqrt(var + LN_EPS) * lnw_ref[...] + lnb_ref[...]).astype(BF16)
    hd = SGU_WIDTH // SGU_HEADS
    for h in range(SGU_HEADS):
        cs = slice(h * hd, (h + 1) * hd)
        rhs = jnp.concatenate(
            [vb_ref[c * SGU_CHUNK:(c + 1) * SGU_CHUNK, cs] for c in range(nc)], axis=1)
        zz = _dot(w_ref[h], rhs)
        for c in range(nc):
            rs = slice(c * SGU_CHUNK, (c + 1) * SGU_CHUNK)
            z = zz[:, c * hd:(c + 1) * hd] + bs_ref[:, cs]
            x_ref[rs, cs] = _gelu_tanh(u_ref[rs, cs]) * z
    x = x_ref[...]
    ms = jnp.mean(x * x, axis=-1, keepdims=True)
    o_ref[...] = (x * lax.rsqrt(ms + RMS_EPS) * nw_ref[...]).astype(BF16)


def _sgu_mixer(proj3, ln_w, ln_b, w_s, b_s, norm_w, tl):
    nb, seq, _ = proj3.shape
    nc = tl // SGU_CHUNK
    hd = SGU_WIDTH // SGU_HEADS
    causal = jnp.tril(jnp.ones((SGU_CHUNK, SGU_CHUNK), dtype=bool))
    w_causal = jnp.where(causal[None], w_s, jnp.zeros_like(w_s)).astype(BF16)
    bias = jnp.repeat(b_s.astype(F32).T, hd, axis=1)
    u_blk = 0
    row = lambda a: a.astype(F32).reshape(1, SGU_WIDTH)
    const = lambda shp: pl.BlockSpec(shp, lambda b, t: (0,) * len(shp))
    body = functools.partial(_sgu_body, nc=nc)
    return pl.pallas_call(
        body,
        out_shape=jax.ShapeDtypeStruct((nb, seq, SGU_WIDTH), BF16),
        grid=(nb, seq // tl),
        in_specs=[pl.BlockSpec((None, tl, SGU_WIDTH), lambda b, t: (b, t, u_blk)),
                  pl.BlockSpec((None, tl, SGU_WIDTH), lambda b, t: (b, t, u_blk + 1)),
                  const((1, SGU_WIDTH)), const((1, SGU_WIDTH)),
                  const((SGU_HEADS, SGU_CHUNK, SGU_CHUNK)),
                  const((SGU_CHUNK, SGU_WIDTH)), const((1, SGU_WIDTH))],
        out_specs=pl.BlockSpec((None, tl, SGU_WIDTH), lambda b, t: (b, t, 0)),
        scratch_shapes=[pltpu.VMEM((tl, SGU_WIDTH), BF16),
                        pltpu.VMEM((tl, SGU_WIDTH), F32)],
        compiler_params=_cparams(("parallel", "parallel")), name="sgu_mixer")(
            proj3, proj3, row(ln_w), row(ln_b), w_causal, bias, row(norm_w))


HG_PAIR = 2
HG_PAIR_W = HG_PAIR * HG_HEAD_DIM
HG_FAST_BLOCK = 32
HG_FAST_MIN_LOG_DECAY = -60.0
_NT = (((1,), (1,)), ((), ()))
_TN = (((0,), (0,)), ((), ()))


def _hgrn_tables():
    c = HG_CHUNK
    t = jnp.arange(c)[:, None]
    r = jnp.arange(c)[None, :]
    mats, masks = [], []
    for lev in range(HG_LEVELS):
        m = 1 << lev
        mid = (t // (2 * m)) * (2 * m) + m
        later = t >= mid
        mats.append(jnp.where(later, (r >= mid) & (r <= t), (r > t) & (r < mid)))
        same = (t // (2 * m)) == (r // (2 * m))
        masks.append(same & later & (r < mid))
    tri = r <= t
    mats.append(tri)
    masks.append(t == r)
    fb = HG_FAST_BLOCK
    fast = [masks[5], ((t // fb) == (r // fb)) & tri]
    pair = lambda m: jnp.tile(m.astype(F32), (1, HG_PAIR))
    return dict(
        tri=tri.astype(BF16),
        mst=jnp.concatenate(mats, axis=0).astype(BF16),
        masks=jnp.stack(masks).astype(F32),
        fmasks=jnp.stack([pair(m) for m in fast]))


def _block_diag(x):
    z = jnp.zeros((x.shape[0], HG_HEAD_DIM), x.dtype)
    top = jnp.concatenate([x[:, :HG_HEAD_DIM], z], axis=1)
    bot = jnp.concatenate([z, x[:, HG_HEAD_DIM:]], axis=1)
    return jnp.concatenate([top, bot], axis=0)


def _rows(v, n):
    return jnp.broadcast_to(v, (n, v.shape[1]))


def _hgrn_finish(o, gv, nw, o_ref, rs):
    for h in range(HG_PAIR):
        cs = slice(h * HG_HEAD_DIM, (h + 1) * HG_HEAD_DIM)
        oh = o[:, cs]
        ms = jnp.mean(oh * oh, axis=-1, keepdims=True)
        gg = gv[:, cs]
        o_ref[rs, cs] = (oh * lax.rsqrt(ms + RMS_EPS) * nw * (gg * _sigmoid(gg))).astype(BF16)


def _hgrn_body(q_ref, f_ref, i_ref, g_ref, par_ref, nw_ref, tri_ref, mst_ref, mask_ref,
               fmask_ref, o_ref, st_ref, qs_ref, kk_ref, lf_ref, b_ref, *, nc):
    c = HG_CHUNK
    hd = HG_HEAD_DIM
    wb = HG_PAIR_W
    t = pl.program_id(2)

    @pl.when(t == 0)
    def _():
        st_ref[...] = jnp.zeros_like(st_ref)

    log_lb = par_ref[0:1, :]
    log_1m_lb = par_ref[1:2, :]
    one_m_lb = par_ref[2:3, :]
    nw = nw_ref[...]
    tri = tri_ref[...]
    row = lax.broadcasted_iota(jnp.int32, (c, wb), 0)

    wmin = None
    for ch in range(nc):
        rs = slice(ch * c, (ch + 1) * c)
        fr = f_ref[rs, :]
        e = jnp.exp(-jnp.abs(fr))
        r = 1.0 / (1.0 + e)
        sig_neg = jnp.where(fr >= 0, e * r, r)
        log_sig = jnp.minimum(fr, 0.0) - jnp.log(1.0 + e)
        y = log_1m_lb + log_sig
        log_f = jnp.maximum(log_lb, y) + jnp.log(1.0 + jnp.exp(-jnp.abs(log_lb - y)))
        hi = log_f.astype(BF16)
        lo = (log_f - hi.astype(F32)).astype(BF16)
        b = _dot(tri, hi) + _dot(tri, lo)
        qv = q_ref[rs, :]
        qs_ref[rs, :] = qv * _sigmoid(qv)
        kk_ref[rs, :] = one_m_lb * sig_neg
        lf_ref[rs, :] = log_f
        b_ref[rs, :] = b
        fb = HG_FAST_BLOCK
        starts = jnp.concatenate(
            [jnp.zeros((fb, wb), F32)]
            + [_rows(b[j * fb - 1:j * fb, :], fb) for j in range(1, c // fb)], axis=0)
        w = b - starts
        wmin = w if wmin is None else jnp.minimum(wmin, w)
    fast = jnp.min(wmin) >= HG_FAST_MIN_LOG_DECAY

    @pl.when(fast)
    def _():
        for ch in range(nc):
            rs = slice(ch * c, (ch + 1) * c)
            b = b_ref[rs, :]
            qs = qs_ref[rs, :]
            kk = kk_ref[rs, :]
            vv = i_ref[rs, :].astype(BF16)
            half = c // 2
            b63 = _rows(b[half - 1:half, :], half)
            zero_half = jnp.zeros((half, wb), BF16)
            q6 = jnp.concatenate(
                [zero_half, (qs[half:, :] * jnp.exp(b[half:, :] - b63)).astype(BF16)], axis=0)
            k6 = jnp.concatenate(
                [(kk[:half, :] * jnp.exp(b63 - b[:half, :])).astype(BF16), zero_half], axis=0)
            scores = lax.dot_general(q6, _block_diag(k6), _NT, preferred_element_type=F32)
            bm = jnp.concatenate([_rows(b[31:32, :], 64), _rows(b[95:96, :], 64)], axis=0)
            a5 = jnp.exp(jnp.where((row % 64) >= 32, b - bm, bm - b))
            fb = HG_FAST_BLOCK
            starts = jnp.concatenate(
                [jnp.zeros((fb, wb), F32)]
                + [_rows(b[j * fb - 1:j * fb, :], fb) for j in range(1, c // fb)], axis=0)
            w = b - starts
            for lev, (aq, ak) in enumerate([(a5, a5), (jnp.exp(w), jnp.exp(-w))]):
                sc = lax.dot_general((qs * aq).astype(BF16), _block_diag((kk * ak).astype(BF16)),
                                     _NT, preferred_element_type=F32)
                scores = scores + fmask_ref[lev] * sc
            st = st_ref[...]
            v_bd = _block_diag(vv)
            o = (_dot(scores.astype(BF16), v_bd)
                 + lax.dot_general((qs * jnp.exp(b)).astype(BF16), st.astype(BF16), _NT,
                                   preferred_element_type=F32))
            bend = b[c - 1:c, :]
            kd = (kk * jnp.exp(bend - b)).astype(BF16)
            st_ref[...] = st * jnp.exp(bend) + lax.dot_general(
                v_bd, _block_diag(kd), _TN, preferred_element_type=F32)
            _hgrn_finish(o, g_ref[rs, :], nw, o_ref, rs)

    @pl.when(jnp.logical_not(fast))
    def _():
        mst = mst_ref[...]

        def chunk(ch, carry):
            rs = pl.ds(pl.multiple_of(ch * c, c), c)
            log_f = lf_ref[rs, :]
            hi = log_f.astype(BF16)
            lo = (log_f - hi.astype(F32)).astype(BF16)
            ee = _dot(mst, hi) + _dot(mst, lo)
            qs = qs_ref[rs, :]
            kk = kk_ref[rs, :]
            vv = i_ref[rs, :].astype(BF16)
            bcum = ee[HG_LEVELS * c:(HG_LEVELS + 1) * c, :]
            outs = []
            for h in range(HG_PAIR):
                cs = slice(h * hd, (h + 1) * hd)
                qh = qs[:, cs]
                kh = kk[:, cs]
                vh = vv[:, cs]
                scores = mask_ref[HG_LEVELS] * jnp.sum(qh * kh, axis=-1, keepdims=True)
                for lev in range(HG_LEVELS):
                    a = jnp.exp(ee[lev * c:(lev + 1) * c, cs])
                    sc = lax.dot_general((qh * a).astype(BF16), (kh * a).astype(BF16), _NT,
                                         preferred_element_type=F32)
                    scores = scores + mask_ref[lev] * sc
                bh = bcum[:, cs]
                st = st_ref[cs, cs]
                outs.append(_dot(scores.astype(BF16), vh)
                            + lax.dot_general((qh * jnp.exp(bh)).astype(BF16), st.astype(BF16),
                                              _NT, preferred_element_type=F32))
                bend = bh[c - 1:c, :]
                kd = (kh * jnp.exp(bend - bh)).astype(BF16)
                st_ref[cs, cs] = st * jnp.exp(bend) + lax.dot_general(
                    vh, kd, _TN, preferred_element_type=F32)
            _hgrn_finish(jnp.concatenate(outs, axis=1), g_ref[rs, :], nw, o_ref, rs)
            return carry

        lax.fori_loop(0, nc, chunk, 0)


def _hgrn_mixer(proj3, lb, norm_w, tl):
    nb, seq, _ = proj3.shape
    nc = tl // HG_CHUNK
    wb = HG_PAIR_W
    q0 = (2 * SGU_WIDTH) // wb
    nblk = HG_WIDTH // wb
    lbf = lb.astype(F32)
    par = jnp.stack([jnp.log(lbf), jnp.log1p(-lbf), 1.0 - lbf])
    tb = _hgrn_tables()
    sec = lambda k: pl.BlockSpec((None, tl, wb), functools.partial(
        lambda b, h, t, k: (b, t, q0 + k * nblk + h), k=k))
    const = lambda a: pl.BlockSpec(a.shape, lambda b, h, t: (0,) * a.ndim)
    body = functools.partial(_hgrn_body, nc=nc)
    return pl.pallas_call(
        body,
        out_shape=jax.ShapeDtypeStruct((nb, seq, HG_WIDTH), BF16),
        grid=(nb, nblk, seq // tl),
        in_specs=[sec(0), sec(1), sec(2), sec(3),
                  pl.BlockSpec((3, wb), lambda b, h, t: (0, h)),
                  pl.BlockSpec((1, HG_HEAD_DIM), lambda b, h, t: (0, 0)),
                  const(tb["tri"]), const(tb["mst"]), const(tb["masks"]),
                  const(tb["fmasks"])],
        out_specs=pl.BlockSpec((None, tl, wb), lambda b, h, t: (b, t, h)),
        scratch_shapes=[pltpu.VMEM((wb, wb), F32)] + [pltpu.VMEM((tl, wb), F32)] * 4,
        compiler_params=_cparams(("parallel", "parallel", "arbitrary")),
        name="hgrn2_mixer")(
            proj3, proj3, proj3, proj3, par,
            norm_w.astype(F32).reshape(1, HG_HEAD_DIM),
            tb["tri"], tb["mst"], tb["masks"], tb["fmasks"])


def kernel(x, p, w_in, s5_lam_re, s5_lam_im, s5_log_step, s5_b_re, s5_b_im, s5_c_re, s5_c_im, s5_d, s5_w_glu, sgu_ln_w, sgu_ln_b, sgu_w, sgu_b, hg_lb_logits, hg_norm_w, norm_a_w, norm_b_w, w_out, ln1_w, ln1_b, w_ffn_in, w_ffn_out, ln2_w, ln2_b, w_ple_in, w_ple_gate, ln3_w, ln3_b):
    nb, seq, d_model = x.shape
    depth = w_in.shape[0]
    t_rows = nb * seq
    proj_w = w_in.shape[2]
    d_ff = w_ffn_out.shape[1]
    alpha = (2.0 * depth) ** 0.25

    tm = _pick(t_rows, (1024, 512, 256, 128))
    tm_wide = _pick(t_rows, (2048, 1024, 512, 256, 128))
    tm_half = _pick(t_rows, (512, 256, 128))
    tm_ln = _pick(t_rows, (512, 256, 128))
    tl_s5 = _pick(seq, (1024, 512, 256, 128))
    tl_sgu = _pick(seq, (1024, 512, 256, 128))
    tl_hg = _pick(seq, (2048, 1024, 512, 256, 128))
    tn_in = _pick(proj_w - S5_WIDTH, (1024, 512))

    lbs =jnp.cumsum(jax.nn.softmax(hg_lb_logits.astype(F32), axis=0), axis=0)
    lbs = lbs - lbs[0:1]

    x_rows = x.reshape(t_rows, d_model).astype(F32)
    res = _Residual(alpha, h=x_rows)
    h16 = _layer_to_bf16(x_rows.reshape(1, t_rows, d_model), 0)
    p_rows = p.reshape(depth, t_rows, -1)

    w_in16 = _layer_to_bf16(w_in, 0)
    for l in range(depth):
        if l == 0:
            xa_tiles = _s5_in_proj(h16, w_in16, tm)
        else:
            lw, lb = ln3_w[l - 1], ln3_b[l - 1]
            xa_tiles, mu, rstd, h16 = _ln_fused_mm(
                xres, lw, lb, [(None, w_in16, d_model, 0, 0)], [],
                lambda accs, ex: [accs[0]], [F32], S5_WIDTH, tm, S5_WIDTH // 4, "in_proj_s5",
                emit_h16=True, out_tiles=True)
            res = _Residual(alpha, x=xres, mu=mu, rstd=rstd, w=lw, b=lb)
        xa_tiles = xa_tiles.reshape(-1, nb, seq, LANES)
        proj, wf, wo = _fused_mm(
            [(h16, w_in16, d_model, 0, S5_WIDTH // tn_in)], [],
            lambda accs, ex: [accs[0]], [F32], proj_w - S5_WIDTH, tm, tn_in, "in_proj",
            side_casts=[(w_ffn_in, l), (w_out, l)])
        proj3 = proj.reshape(nb, seq, proj_w - S5_WIDTH)
        prep = _s5_prepare(s5_lam_re[l], s5_lam_im[l], s5_log_step[l], s5_b_re[l],
                           s5_b_im[l], s5_c_re[l], s5_c_im[l], s5_d[l])
        z_tiles = _s5_mixer(xa_tiles, prep, tl_s5).reshape(-1, t_rows, LANES)
        ya = _s5_glu_norm(z_tiles, _layer_to_bf16(s5_w_glu, l), norm_a_w[l], tm_half)
        yb = _sgu_mixer(proj3, sgu_ln_w[l], sgu_ln_b[l], sgu_w[l], sgu_b[l],
                        norm_b_w[l], tl_sgu).reshape(t_rows, SGU_WIDTH)
        yc = _hgrn_mixer(proj3, lbs[l], hg_norm_w[l], tl_hg).reshape(t_rows, HG_WIDTH)
        (xres,) = _fused_mm(
            [(ya, wo, S5_WIDTH, 0, 0), (yb, wo, SGU_WIDTH, 1, 0), (yc, wo, HG_WIDTH, 1, 0)],
            res.extras,
            functools.partial(lambda accs, ex, res: [res(ex) + (accs[0] + accs[1] + accs[2])],
                              res=res),
            [F32], d_model, tm, 1024, "out_proj")

        hid, mu, rstd, wfo, wpg = _ln_fused_mm(
            xres, ln1_w[l], ln1_b[l],
            [(None, wf, d_model, 0, 0), (None, wf, d_model, 0, d_ff // 256)], [],
            lambda accs, ex: [accs[0] * _sigmoid(accs[0]) * accs[1]],
            [BF16], d_ff, tm_wide, 256, "ffn_in",
            side_casts=[(w_ffn_out, l), (w_ple_gate, l)])
        res = _Residual(alpha, x=xres, mu=mu, rstd=rstd, w=ln1_w[l], b=ln1_b[l])
        (xres,) = _fused_mm(
            [(hid, wfo, d_ff, 0, 0)], res.extras,
            functools.partial(lambda accs, ex, res: [res(ex) + accs[0]], res=res),
            [F32], d_model, tm_half, 1024, "ffn_out", cols_outer=True)

        p16 = _layer_to_bf16(p_rows, l)
        xres, _, _, *next_w_in = _ln_fused_mm(
            xres, ln2_w[l], ln2_b[l],
            [(None, wpg, d_model, 0, 0),
             (p16, _layer_to_bf16(w_ple_in, l), p16.shape[1], 0, 0)], [],
            lambda accs, ex, resid: [resid + accs[1] * _sigmoid(accs[0])],
            [F32], d_model, tm, 512, "ple", self_alpha=alpha,
            side_casts=[(w_in, l + 1)] if l + 1 < depth else [])
        if next_w_in:
            w_in16 = next_w_in[0]

    out = _layer_norm_final(xres, ln3_w[depth - 1], ln3_b[depth - 1], tm_ln)
    return out.reshape(nb, seq, d_model).astype(x.dtype)
```

```python
import functools
import math

import jax
import jax.numpy as jnp
from jax import lax
from jax.experimental import pallas as pl
from jax.experimental.pallas import tpu as pltpu

F32 = jnp.float32
BF16 = jnp.bfloat16

V7X_VMEM_LIMIT_BYTES = 56 * 1024 * 1024
LANES = 128

LN_EPS = 1e-5
RMS_EPS = 1e-6

S5_WIDTH = 1024
S5_GROUP_CH = 16
S5_STATE = 64
S5_CHUNK = 8
S5_TILE_GROUPS = LANES // S5_GROUP_CH
S5_TILE_STATE = S5_TILE_GROUPS * S5_STATE

SGU_WIDTH = 1024
SGU_CHUNK = 128
SGU_HEADS = 8

HG_WIDTH = 2048
HG_HEAD_DIM = 128
HG_CHUNK = 128
HG_LEVELS = 7


def _cparams(sem):
    return pltpu.CompilerParams(dimension_semantics=sem,
                                vmem_limit_bytes=V7X_VMEM_LIMIT_BYTES)


def _gelu_tanh(x):
    c = math.sqrt(2.0 / math.pi)
    return x * (0.5 * (1.0 + jnp.tanh(c * (x + 0.044715 * (x * x * x)))))


def _sigmoid(x):
    return 1.0 / (1.0 + jnp.exp(-x))


def _dot(a, b):
    return jnp.dot(a, b, preferred_element_type=F32)


def _pick(n, prefs):
    for p in prefs:
        if n % p == 0:
            return p
    return n


CAST_BLOCK_BYTES = 8 * 1024 * 1024


def _cast_body(w_ref, o_ref):
    o_ref[...] = w_ref[...].astype(BF16)


def _layer_to_bf16(w, layer):
    _, rows, cols = w.shape
    fits = [tr for tr in (8192, 4096, 2048, 1024, 512, 256, 128, 64, 32, 16)
            if rows % tr == 0 and tr * cols * 4 <= CAST_BLOCK_BYTES]
    tr = fits[0]
    return pl.pallas_call(
        _cast_body,
        out_shape=jax.ShapeDtypeStruct((rows, cols), BF16),
        grid=(rows // tr,),
        in_specs=[pl.BlockSpec((None, tr, cols), lambda i: (layer, i, 0))],
        out_specs=pl.BlockSpec((tr, cols), lambda i: (i, 0)),
        compiler_params=_cparams(("parallel",)), name="cast_bf16")(w)


MM_SUB_ROWS = 256


def _mm_body(*refs, a_of_dot, n_a, extra_kinds, n_side, epilogue, sub_rows):
    n_dot = len(a_of_dot)
    n_extra = len(extra_kinds)
    n_in = n_a + n_dot + n_extra
    a_refs = refs[:n_a]
    w_refs = refs[n_a:n_a + n_dot]
    extra = refs[n_a + n_dot:n_in]
    side_in = refs[n_in:n_in + n_side]
    out_refs = refs[n_in + n_side:len(refs) - n_side]
    side_out = refs[len(refs) - n_side:]
    for si, so in zip(side_in, side_out):
        so[...] = si[...].astype(BF16)
    rows = out_refs[0].shape[0]
    sub = min(rows, sub_rows)
    for r in range(rows // sub):
        rs = slice(r * sub, (r + 1) * sub)
        accs = [_dot(a_refs[ai][rs, :], w[...]) for ai, w in zip(a_of_dot, w_refs)]
        outs = epilogue(accs, [e[...] if kind == "row" else e[rs, :]
                               for e, kind in zip(extra, extra_kinds)])
        for o_ref, o in zip(out_refs, outs):
            o_ref[rs, :] = o.astype(o_ref.dtype)


BF16_ROW_TILE = 16


def _fused_mm(dots, extras, epilogue, out_dtypes, n_cols, tm, tn, name,
              sub_rows=MM_SUB_ROWS, side_casts=(), cols_outer=False):
    t_rows = dots[0][0].shape[0]
    ni, nj = t_rows // tm, n_cols // tn
    grid = (nj, ni) if cols_outer else (ni, nj)

    def spec(shape, index_map, **kw):
        if cols_outer:
            return pl.BlockSpec(shape, lambda g0, g1: index_map(g1, g0), **kw)
        return pl.BlockSpec(shape, index_map, **kw)

    w_mode = dict(pipeline_mode=pl.Buffered(1)) if cols_outer else {}
    in_specs, args = [], []
    a_of_dot = []
    for a, _, _, _, _ in dots:
        known = [k for k, seen in enumerate(args) if seen is a]
        if known:
            a_of_dot.append(known[0])
            continue
        a_of_dot.append(len(args))
        in_specs.append(spec((tm, a.shape[1]), lambda i, j: (i, 0)))
        args.append(a)
    n_a = len(args)
    for _, w, rb, ri, co in dots:
        in_specs.append(spec((rb, tn), functools.partial(
            lambda i, j, ri, co: (ri, j + co), ri=ri, co=co), **w_mode))
        args.append(w)
    for arr, kind in extras:
        if kind == "tile":
            in_specs.append(spec((tm, tn), lambda i, j: (i, j)))
        elif kind == "stat":
            in_specs.append(spec((tm, LANES), lambda i, j: (i, 0)))
        else:
            in_specs.append(spec((1, tn), lambda i, j: (0, j)))
        args.append(arr)
    out_shape = [jax.ShapeDtypeStruct((t_rows, n_cols), dt) for dt in out_dtypes]
    out_specs = [spec((tm, tn), lambda i, j: (i, j)) for _ in out_dtypes]
    n_steps = grid[0] * grid[1]
    for stack, layer in side_casts:
        _, rows, cols = stack.shape
        tiles = rows // BF16_ROW_TILE
        n_slabs = max(d for d in range(1, min(tiles, n_steps) + 1) if tiles % d == 0)
        slab = functools.partial(lambda g0, g1, last: jnp.minimum(g0 * grid[1] + g1, last),
                                 last=n_slabs - 1)
        in_specs.append(pl.BlockSpec(
            (None, rows // n_slabs, cols),
            functools.partial(lambda g0, g1, layer, slab: (layer, slab(g0, g1), 0),
                              layer=layer, slab=slab)))
        args.append(stack)
        out_shape.append(jax.ShapeDtypeStruct((rows, cols), BF16))
        out_specs.append(pl.BlockSpec(
            (rows // n_slabs, cols),
            functools.partial(lambda g0, g1, slab: (slab(g0, g1), 0), slab=slab)))
    body = functools.partial(_mm_body, a_of_dot=tuple(a_of_dot), n_a=n_a,
                             extra_kinds=tuple(kind for _, kind in extras),
                             n_side=len(side_casts), epilogue=epilogue, sub_rows=sub_rows)
    return pl.pallas_call(
        body, out_shape=out_shape, grid=grid, in_specs=in_specs, out_specs=out_specs,
        compiler_params=_cparams(("parallel", "arbitrary")), name=name)(*args)


def _ln_rows(x):
    mu = jnp.mean(x, axis=-1, keepdims=True)
    xc = x - mu
    var = jnp.mean(xc * xc, axis=-1, keepdims=True)
    return xc, mu, lax.rsqrt(var + LN_EPS)


LN_ROW_GROUP = 8


def _ln_final_body(x_ref, w_ref, b_ref, h_ref):
    w = w_ref[...]
    b = b_ref[...]
    for g in range(x_ref.shape[0] // LN_ROW_GROUP):
        rs = slice(g * LN_ROW_GROUP, (g + 1) * LN_ROW_GROUP)
        xc, _, rstd = _ln_rows(x_ref[rs, :])
        h_ref[rs, :] = xc * rstd * w + b


def _layer_norm_final(x, w, b, tm):
    t_rows, d = x.shape
    rows = pl.BlockSpec((tm, d), lambda i: (i, 0))
    vec = pl.BlockSpec((1, d), lambda i: (0, 0))
    return pl.pallas_call(
        _ln_final_body, out_shape=jax.ShapeDtypeStruct((t_rows, d), F32),
        grid=(t_rows // tm,), in_specs=[rows, vec, vec], out_specs=rows,
        compiler_params=_cparams(("parallel",)), name="layer_norm")(
            x, w.reshape(1, d), b.reshape(1, d))


def _ln_mm_body(*refs, a_of_dot, n_other_a, extra_kinds, n_side, n_out, emit_h16, out_tiles,
                self_alpha, epilogue, sub_rows, n_slab, slab_rows):
    n_dot = len(a_of_dot)
    n_extra = len(extra_kinds)
    x_ref, lnw_ref, lnb_ref = refs[:3]
    pos = 3
    other_a = refs[pos:pos + n_other_a]
    pos += n_other_a
    w_refs = refs[pos:pos + n_dot]
    pos += n_dot
    extra = refs[pos:pos + n_extra]
    pos += n_extra
    side_in = refs[pos:pos + n_side]
    pos += n_side
    out_refs = refs[pos:pos + n_out]
    pos += n_out
    mu_ref, rs_ref = refs[pos:pos + 2]
    pos += 2
    h16_ref = refs[pos] if emit_h16 else None
    pos += int(emit_h16)
    side_out = refs[pos:pos + n_side]
    pos += n_side
    a_slots, mu_slots, rs_slots = refs[pos:pos + 2], refs[pos + 2:pos + 4], refs[pos + 4:pos + 6]

    g = pl.program_id(0)
    j = pl.program_id(1)
    group = 2 * LN_ROW_GROUP

    def side_jobs(slot):
        a_sc, mu_sc, rs_sc = a_slots[slot], mu_slots[slot], rs_slots[slot]
        for si, so in zip(side_in, side_out):
            so[...] = si[...].astype(BF16)
        row0 = pl.multiple_of(jnp.minimum(j, n_slab - 1) * slab_rows, slab_rows)
        w = lnw_ref[...]
        b = lnb_ref[...]
        for grp in range(slab_rows // group):
            ys = []
            for half in range(2):
                r8 = slice(grp * group + half * LN_ROW_GROUP,
                           grp * group + (half + 1) * LN_ROW_GROUP)
                xc, mu, rstd = _ln_rows(x_ref[r8, :])
                ys.append(xc * rstd * w + b)
                mu_b = jnp.broadcast_to(mu, (LN_ROW_GROUP, LANES))
                rs_b = jnp.broadcast_to(rstd, (LN_ROW_GROUP, LANES))
                mu_ref[r8, :] = mu_b
                rs_ref[r8, :] = rs_b
                dst = pl.ds(row0 + grp * group + half * LN_ROW_GROUP, LN_ROW_GROUP)
                mu_sc[dst, :] = mu_b
                rs_sc[dst, :] = rs_b
            y16 = jnp.concatenate(ys, axis=0).astype(BF16)
            a_sc[pl.ds(row0 + grp * group, group), :] = y16
            if emit_h16:
                h16_ref[grp * group:(grp + 1) * group, :] = y16

    def matmul(slot):
        a_sc, mu_sc, rs_sc = a_slots[slot], mu_slots[slot], rs_slots[slot]
        rows = a_sc.shape[0]
        sub = min(rows, sub_rows)
        for r in range(rows // sub):
            rs = slice(r * sub, (r + 1) * sub)
            a_ln = a_sc[rs, :]
            accs = [_dot(a_ln if ai < 0 else other_a[ai][rs, :], w[...])
                    for ai, w in zip(a_of_dot, w_refs)]
            ex = [e[...] if kind == "row" else e[rs, :] for e, kind in zip(extra, extra_kinds)]
            if self_alpha is not None:
                x_tile, w_row, b_row = ex[:3]
                ex = ex[3:]
                rep = x_tile.shape[1] // LANES
                wide = lambda s: jnp.concatenate([s] * rep, axis=1)
                resid = self_alpha * ((x_tile - wide(mu_sc[rs, :]))
                                      * wide(rs_sc[rs, :]) * w_row + b_row)
                outs = epilogue(accs, ex, resid)
            else:
                outs = epilogue(accs, ex)
            for o_ref, o in zip(out_refs, outs):
                if out_tiles:
                    for c in range(o_ref.shape[0]):
                        o_ref[c, rs, :] = o[:, c * LANES:(c + 1) * LANES].astype(o_ref.dtype)
                else:
                    o_ref[rs, :] = o.astype(o_ref.dtype)

    @pl.when(g == 0)
    def _():
        side_jobs(0)

    for parity in (0, 1):
        @pl.when(jnp.logical_and(g > 0, lax.rem(g, 2) == parity))
        def _():
            side_jobs(parity)
            matmul(1 - parity)


def _ln_fused_mm(x, ln_w, ln_b, dots, extras, epilogue, out_dtypes, n_cols, tm, tn, name,
                 sub_rows=MM_SUB_ROWS, side_casts=(), emit_h16=False, out_tiles=False,
                 self_alpha=None):
    t_rows, d = x.shape
    ni, nj = t_rows // tm, n_cols // tn
    grid = (ni + 1, nj)
    n_slab = max(s for s in range(1, nj + 1)
                 if tm % s == 0 and (tm // s) % (2 * LN_ROW_GROUP) == 0)
    slab_rows = tm // n_slab
    prev = lambda g: jnp.maximum(g - 1, 0)
    first_col = lambda g, j: jnp.where(g > 0, j, 0)
    slab_idx = lambda g, j: (jnp.where(g < ni, g * n_slab + jnp.minimum(j, n_slab - 1),
                                       ni * n_slab - 1), 0)
    vec = lambda a: a.astype(F32).reshape(1, d)
    in_specs = [pl.BlockSpec((slab_rows, d), slab_idx),
                pl.BlockSpec((1, d), lambda g, j: (0, 0)),
                pl.BlockSpec((1, d), lambda g, j: (0, 0))]
    args = [x, vec(ln_w), vec(ln_b)]
    a_of_dot, other = [], []
    for a, _, _, _, _ in dots:
        if a is None:
            a_of_dot.append(-1)
            continue
        a_of_dot.append(len(other))
        other.append(a)
        in_specs.append(pl.BlockSpec((tm, a.shape[1]), lambda g, j: (prev(g), 0)))
        args.append(a)
    for _, w, rb, ri, co in dots:
        in_specs.append(pl.BlockSpec((rb, tn), functools.partial(
            lambda g, j, ri, co: (ri, j + co), ri=ri, co=co)))
        args.append(w)
    extras = list(extras)
    if self_alpha is not None:
        extras = [(x, "tile"), (vec(ln_w), "row"), (vec(ln_b), "row")] + extras
    for arr, kind in extras:
        if kind == "tile":
            in_specs.append(pl.BlockSpec((tm, tn), lambda g, j: (prev(g), j)))
        elif kind == "stat":
            in_specs.append(pl.BlockSpec((tm, LANES), lambda g, j: (prev(g), 0)))
        else:
            in_specs.append(pl.BlockSpec((1, tn), lambda g, j: (0, j)))
        args.append(arr)
    if out_tiles:
        n_tiles = n_cols // LANES
        out_shape = [jax.ShapeDtypeStruct((n_tiles, t_rows, LANES), dt) for dt in out_dtypes]
        out_specs = [pl.BlockSpec((tn // LANES, tm, LANES),
                                  lambda g, j: (first_col(g, j), prev(g), 0))
                     for _ in out_dtypes]
    else:
        out_shape = [jax.ShapeDtypeStruct((t_rows, n_cols), dt) for dt in out_dtypes]
        out_specs = [pl.BlockSpec((tm, tn), lambda g, j: (prev(g), first_col(g, j)))
                     for _ in out_dtypes]
    n_out = len(out_shape)
    for _ in range(2):
        out_shape.append(jax.ShapeDtypeStruct((t_rows, LANES), F32))
        out_specs.append(pl.BlockSpec((slab_rows, LANES), slab_idx))
    if emit_h16:
        out_shape.append(jax.ShapeDtypeStruct((t_rows, d), BF16))
        out_specs.append(pl.BlockSpec((slab_rows, d), slab_idx))
    n_steps = grid[0] * grid[1]
    for stack, layer in side_casts:
        _, rows, cols = stack.shape
        tiles = rows // BF16_ROW_TILE
        n_cast = max(s for s in range(1, min(tiles, n_steps) + 1) if tiles % s == 0)
        slab = functools.partial(lambda g, j, last: jnp.minimum(g * nj + j, last),
                                 last=n_cast - 1)
        in_specs.append(pl.BlockSpec(
            (None, rows // n_cast, cols),
            functools.partial(lambda g, j, layer, slab: (layer, slab(g, j), 0),
                              layer=layer, slab=slab)))
        args.append(stack)
        out_shape.append(jax.ShapeDtypeStruct((rows, cols), BF16))
        out_specs.append(pl.BlockSpec(
            (rows // n_cast, cols),
            functools.partial(lambda g, j, slab: (slab(g, j), 0), slab=slab)))
    body = functools.partial(
        _ln_mm_body, a_of_dot=tuple(a_of_dot), n_other_a=len(other),
        extra_kinds=tuple(kind for _, kind in extras), n_side=len(side_casts), n_out=n_out,
        emit_h16=emit_h16, out_tiles=out_tiles, self_alpha=self_alpha, epilogue=epilogue,
        sub_rows=sub_rows, n_slab=n_slab, slab_rows=slab_rows)
    return pl.pallas_call(
        body, out_shape=out_shape, grid=grid, in_specs=in_specs, out_specs=out_specs,
        scratch_shapes=([pltpu.VMEM((tm, d), BF16)] * 2
                        + [pltpu.VMEM((tm, LANES), F32)] * 4),
        compiler_params=_cparams(("arbitrary", "arbitrary")), name=name)(*args)


class _Residual:
    def __init__(self, alpha, h=None, x=None, mu=None, rstd=None, w=None, b=None):
        self.alpha = alpha
        if h is not None:
            self.extras = [(h, "tile")]
        else:
            self.extras = [(x, "tile"), (mu, "stat"), (rstd, "stat"),
                           (w.astype(F32).reshape(1, -1), "row"),
                           (b.astype(F32).reshape(1, -1), "row")]

    def __call__(self, ex):
        if len(ex) == 1:
            return self.alpha * ex[0]
        x, mu, rstd, w, b = ex
        rep = x.shape[1] // LANES
        wide = lambda s: jnp.concatenate([s] * rep, axis=1)
        return self.alpha * ((x - wide(mu)) * wide(rstd) * w + b)


def _s5_prepare(lam_re, lam_im, log_step, b_re, b_im, c_re, c_im, d):
    hp = lax.Precision.HIGHEST
    n_tiles = S5_WIDTH // LANES
    lr = jnp.minimum(lam_re.astype(F32), -1e-4)
    li = lam_im.astype(F32)
    dt = jnp.exp(log_step.astype(F32))[:, None]
    mag = jnp.exp(lr * dt)
    ab_re = mag * jnp.cos(li * dt)
    ab_im = mag * jnp.sin(li * dt)
    den = lr * lr + li * li
    nr = ab_re - 1.0
    g_re = (nr * lr + ab_im * li) / den
    g_im = (ab_im * lr - nr * li) / den
    br = b_re.astype(F32)
    bi = b_im.astype(F32)
    bb_re = g_re[..., None] * br - g_im[..., None] * bi
    bb_im = g_re[..., None] * bi + g_im[..., None] * br
    pr, pi = [jnp.ones_like(ab_re)], [jnp.zeros_like(ab_re)]
    for _ in range(S5_CHUNK):
        pr_new = pr[-1] * ab_re - pi[-1] * ab_im
        pi_new = pr[-1] * ab_im + pi[-1] * ab_re
        pr.append(pr_new)
        pi.append(pi_new)
    p_re = jnp.stack(pr)
    p_im = jnp.stack(pi)
    pb_re = (p_re[:S5_CHUNK, :, :, None] * bb_re[None]
             - p_im[:S5_CHUNK, :, :, None] * bb_im[None])
    pb_im = (p_re[:S5_CHUNK, :, :, None] * bb_im[None]
             + p_im[:S5_CHUNK, :, :, None] * bb_re[None])
    cr = c_re.astype(F32)
    ci = c_im.astype(F32)
    kern = (jnp.einsum("gon,jgni->jgoi", cr, pb_re, precision=hp)
            - jnp.einsum("gon,jgni->jgoi", ci, pb_im, precision=hp))
    tg, gc, ns = S5_TILE_GROUPS, S5_GROUP_CH, S5_STATE

    def spread(x2d, rep, row_group, col_group):
        rows, width = x2d.shape
        sel = jnp.tile(jnp.eye(width, dtype=BF16), (1, rep))
        out = jnp.dot(x2d.astype(BF16), sel)
        rg = row_group(jnp.arange(rows))[:, None]
        cg = col_group(jnp.arange(width * rep))[None, :]
        return jnp.where(rg == cg, out, jnp.zeros_like(out))

    def spread_steps(x2d, row_group):
        rows, width = x2d.shape
        cols = jnp.arange(S5_CHUNK * LANES)
        src = jnp.arange(width)
        sel = ((src[:, None] // gc == cols[None, :] // LANES)
               & (src[:, None] % gc == cols[None, :] % gc)).astype(BF16)
        out = jnp.dot(x2d.astype(BF16), sel)
        rg = row_group(jnp.arange(rows))[:, None]
        cg = ((cols // gc) % tg)[None, :]
        return jnp.where(rg == cg, out, jnp.zeros_like(out))

    k5 = kern.reshape(S5_CHUNK, n_tiles, tg, gc, gc).transpose(1, 0, 2, 4, 3)
    kblk = spread(k5.reshape(-1, gc), tg, lambda r: (r // gc) % tg,
                  lambda c: c // gc).reshape(n_tiles, S5_CHUNK, LANES, LANES)
    zero = jnp.zeros((n_tiles, LANES, LANES), BF16)
    toep = jnp.concatenate(
        [jnp.concatenate([zero] * t + [kblk[:, j] for j in range(S5_CHUNK - t)], axis=-1)
         for t in range(S5_CHUNK)], axis=-2)

    def w1_half(pb):
        pb5 = pb[::-1].reshape(S5_CHUNK, n_tiles, tg, ns, gc).transpose(1, 0, 2, 4, 3)
        return spread(pb5.reshape(-1, ns), tg, lambda r: (r // gc) % tg,
                      lambda c: c // ns).reshape(n_tiles, S5_CHUNK * LANES, S5_TILE_STATE)
    w1 = jnp.concatenate([w1_half(pb_re), w1_half(pb_im)], axis=-1)

    ca_re = cr[None] * p_re[1:, :, None, :] - ci[None] * p_im[1:, :, None, :]
    ca_im = cr[None] * p_im[1:, :, None, :] + ci[None] * p_re[1:, :, None, :]

    def p_half(ca):
        ca5 = ca.reshape(S5_CHUNK, n_tiles, tg, gc, ns).transpose(1, 2, 4, 0, 3)
        return spread_steps(ca5.reshape(-1, S5_CHUNK * gc), lambda r: (r // ns) % tg).reshape(
            n_tiles, S5_TILE_STATE, S5_CHUNK * LANES)
    pcat = jnp.concatenate([p_half(ca_re), -p_half(ca_im)], axis=1)

    a16 = jnp.concatenate([p_re[S5_CHUNK].reshape(n_tiles, 1, S5_TILE_STATE),
                           p_im[S5_CHUNK].reshape(n_tiles, 1, S5_TILE_STATE)], axis=-1)
    dd = d.astype(F32).reshape(n_tiles, 1, LANES)
    return w1, toep, pcat, a16, dd


def _s5_body(x_ref, w1_ref, toep_ref, pcat_ref, a16_ref, d_ref, z_ref,
             state_ref, xr_ref, q_ref, yi_ref, sin_ref, yc_ref, yint_ref, *, nb, tl):
    t = pl.program_id(1)
    nk = tl // S5_CHUNK
    n = nb * tl
    ts = S5_TILE_STATE

    @pl.when(t == 0)
    def _():
        state_ref[...] = jnp.zeros_like(state_ref)

    for b in range(nb):
        for tp in range(S5_CHUNK):
            xr_ref[tp, pl.ds(b, nk, stride=nb), :] = x_ref[b, pl.ds(tp, nk, stride=S5_CHUNK), :]
    xr = jnp.concatenate([xr_ref[tp].astype(BF16) for tp in range(S5_CHUNK)], axis=1)
    q_ref[...] = _dot(xr, w1_ref[...])
    yi_ref[...] = _dot(xr, toep_ref[...])

    a_re = a16_ref[:, :ts]
    a_im = a16_ref[:, ts:]
    s = state_ref[...]
    for k in range(nk):
        sin_ref[k * nb:(k + 1) * nb, :] = s
        s_re = s[:, :ts]
        s_im = s[:, ts:]
        s = jnp.concatenate([a_re * s_re - a_im * s_im, a_re * s_im + a_im * s_re],
                            axis=1) + q_ref[k * nb:(k + 1) * nb, :]
    state_ref[...] = s

    yc = yi_ref[...] + _dot(sin_ref[...].astype(BF16), pcat_ref[...])
    for tp in range(S5_CHUNK):
        yc_ref[tp] = yc[:, tp * LANES:(tp + 1) * LANES]
    for b in range(nb):
        for tp in range(S5_CHUNK):
            yint_ref[pl.ds(b * tl + tp, nk, stride=S5_CHUNK), :] = (
                yc_ref[tp, pl.ds(b, nk, stride=nb), :])

    x = x_ref[...].reshape(n, LANES)
    z_ref[...] = _gelu_tanh(yint_ref[...] + d_ref[...] * x).reshape(nb, tl, LANES)


def _s5_in_proj_body(a_ref, w_ref, o_ref):
    rows = a_ref.shape[0]
    sub = min(rows, MM_SUB_ROWS)
    for r in range(rows // sub):
        rs = slice(r * sub, (r + 1) * sub)
        acc = _dot(a_ref[rs, :], w_ref[...])
        for c in range(o_ref.shape[0]):
            o_ref[c, rs, :] = acc[:, c * LANES:(c + 1) * LANES]


def _s5_in_proj(h16, w_in16, tm):
    t_rows, d = h16.shape
    n_tiles = S5_WIDTH // LANES
    return pl.pallas_call(
        _s5_in_proj_body,
        out_shape=jax.ShapeDtypeStruct((n_tiles, t_rows, LANES), F32),
        grid=(t_rows // tm,),
        in_specs=[pl.BlockSpec((tm, d), lambda i: (i, 0)),
                  pl.BlockSpec((d, S5_WIDTH), lambda i: (0, 0))],
        out_specs=pl.BlockSpec((n_tiles, tm, LANES), lambda i: (0, i, 0)),
        compiler_params=_cparams(("parallel",)), name="in_proj_s5")(h16, w_in16)


def _s5_mixer(xa_tiles, prep, tl):
    w1, toep, pcat, a16, dd = prep
    n_tiles, nb, seq, _ = xa_tiles.shape
    nk = tl // S5_CHUNK
    ts = S5_TILE_STATE
    steps = S5_CHUNK * LANES
    body = functools.partial(_s5_body, nb=nb, tl=tl)
    wspec = lambda shp: pl.BlockSpec((None,) + shp, lambda j, t: (j, 0, 0))
    xspec = pl.BlockSpec((None, nb, tl, LANES), lambda j, t: (j, 0, t, 0))
    return pl.pallas_call(
        body,
        out_shape=jax.ShapeDtypeStruct(xa_tiles.shape, F32),
        grid=(n_tiles, seq // tl),
        in_specs=[xspec,
                  wspec((steps, 2 * ts)),
                  wspec((steps, steps)),
                  wspec((2 * ts, steps)),
                  wspec((1, 2 * ts)),
                  wspec((1, LANES))],
        out_specs=xspec,
        scratch_shapes=[pltpu.VMEM((nb, 2 * ts), F32),
                        pltpu.VMEM((S5_CHUNK, nb * nk, LANES), F32),
                        pltpu.VMEM((nb * nk, 2 * ts), F32),
                        pltpu.VMEM((nb * nk, steps), F32),
                        pltpu.VMEM((nb * nk, 2 * ts), F32),
                        pltpu.VMEM((S5_CHUNK, nb * nk, LANES), F32),
                        pltpu.VMEM((nb * tl, LANES), F32)],
        compiler_params=_cparams(("parallel", "arbitrary")), name="s5_mixer")(
            xa_tiles, w1, toep, pcat, a16, dd)


def _glu_body(z_ref, w_ref, nw_ref, o_ref):
    z = jnp.concatenate([z_ref[c] for c in range(z_ref.shape[0])], axis=1)
    y = z * _sigmoid(_dot(z.astype(BF16), w_ref[...]))
    ms = jnp.mean(y * y, axis=-1, keepdims=True)
    o_ref[...] = (y * lax.rsqrt(ms + RMS_EPS) * nw_ref[...]).astype(BF16)


def _s5_glu_norm(z_tiles, w_glu, norm_w, tm):
    n_tiles, t_rows, _ = z_tiles.shape
    d = n_tiles * LANES
    return pl.pallas_call(
        _glu_body,
        out_shape=jax.ShapeDtypeStruct((t_rows, d), BF16),
        grid=(t_rows // tm,),
        in_specs=[pl.BlockSpec((n_tiles, tm, LANES), lambda i: (0, i, 0)),
                  pl.BlockSpec((d, d), lambda i: (0, 0)),
                  pl.BlockSpec((1, d), lambda i: (0, 0))],
        out_specs=pl.BlockSpec((tm, d), lambda i: (i, 0)),
        compiler_params=_cparams(("parallel",)), name="s5_glu_norm")(
            z_tiles, w_glu, norm_w.reshape(1, d))


def _sgu_body(u_ref, v_ref, lnw_ref, lnb_ref, w_ref, bs_ref, nw_ref, o_ref,
              vb_ref, x_ref, *, nc):
    v = _gelu_tanh(v_ref[...])
    mu = jnp.mean(v, axis=-1, keepdims=True)
    vc = v - mu
    var = jnp.mean(vc * vc, axis=-1, keepdims=True)
    vb_ref[...] = (vc * lax.rsqrt(var + LN_EPS) * lnw_ref[...] + lnb_ref[...]).astype(BF16)
    hd = SGU_WIDTH // SGU_HEADS
    for h in range(SGU_HEADS):
        cs = slice(h * hd, (h + 1) * hd)
        rhs = jnp.concatenate(
            [vb_ref[c * SGU_CHUNK:(c + 1) * SGU_CHUNK, cs] for c in range(nc)], axis=1)
        zz = _dot(w_ref[h], rhs)
        for c in range(nc):
            rs = slice(c * SGU_CHUNK, (c + 1) * SGU_CHUNK)
            z = zz[:, c * hd:(c + 1) * hd] + bs_ref[:, cs]
            x_ref[rs, cs] = _gelu_tanh(u_ref[rs, cs]) * z
    x = x_ref[...]
    ms = jnp.mean(x * x, axis=-1, keepdims=True)
    o_ref[...] = (x * lax.rsqrt(ms + RMS_EPS) * nw_ref[...]).astype(BF16)


def _sgu_mixer(proj3, ln_w, ln_b, w_s, b_s, norm_w, tl):
    nb, seq, _ = proj3.shape
    nc = tl // SGU_CHUNK
    hd = SGU_WIDTH // SGU_HEADS
    causal = jnp.tril(jnp.ones((SGU_CHUNK, SGU_CHUNK), dtype=bool))
    w_causal = jnp.where(causal[None], w_s, jnp.zeros_like(w_s)).astype(BF16)
    bias = jnp.repeat(b_s.astype(F32).T, hd, axis=1)
    u_blk = 0
    row = lambda a: a.astype(F32).reshape(1, SGU_WIDTH)
    const = lambda shp: pl.BlockSpec(shp, lambda b, t: (0,) * len(shp))
    body = functools.partial(_sgu_body, nc=nc)
    return pl.pallas_call(
        body,
        out_shape=jax.ShapeDtypeStruct((nb, seq, SGU_WIDTH), BF16),
        grid=(nb, seq // tl),
        in_specs=[pl.BlockSpec((None, tl, SGU_WIDTH), lambda b, t: (b, t, u_blk)),
                  pl.BlockSpec((None, tl, SGU_WIDTH), lambda b, t: (b, t, u_blk + 1)),
                  const((1, SGU_WIDTH)), const((1, SGU_WIDTH)),
                  const((SGU_HEADS, SGU_CHUNK, SGU_CHUNK)),
                  const((SGU_CHUNK, SGU_WIDTH)), const((1, SGU_WIDTH))],
        out_specs=pl.BlockSpec((None, tl, SGU_WIDTH), lambda b, t: (b, t, 0)),
        scratch_shapes=[pltpu.VMEM((tl, SGU_WIDTH), BF16),
                        pltpu.VMEM((tl, SGU_WIDTH), F32)],
        compiler_params=_cparams(("parallel", "parallel")), name="sgu_mixer")(
            proj3, proj3, row(ln_w), row(ln_b), w_causal, bias, row(norm_w))


HG_PAIR = 2
HG_PAIR_W = HG_PAIR * HG_HEAD_DIM
HG_FAST_BLOCK = 32
HG_FAST_MIN_LOG_DECAY = -60.0
_NT = (((1,), (1,)), ((), ()))
_TN = (((0,), (0,)), ((), ()))


def _hgrn_tables():
    c = HG_CHUNK
    t = jnp.arange(c)[:, None]
    r = jnp.arange(c)[None, :]
    mats, masks = [], []
    for lev in range(HG_LEVELS):
        m = 1 << lev
        mid = (t // (2 * m)) * (2 * m) + m
        later = t >= mid
        mats.append(jnp.where(later, (r >= mid) & (r <= t), (r > t) & (r < mid)))
        same = (t // (2 * m)) == (r // (2 * m))
        masks.append(same & later & (r < mid))
    tri = r <= t
    mats.append(tri)
    masks.append(t == r)
    fb = HG_FAST_BLOCK
    fast = [masks[5], ((t // fb) == (r // fb)) & tri]
    pair = lambda m: jnp.tile(m.astype(F32), (1, HG_PAIR))
    return dict(
        tri=tri.astype(BF16),
        mst=jnp.concatenate(mats, axis=0).astype(BF16),
        masks=jnp.stack(masks).astype(F32),
        fmasks=jnp.stack([pair(m) for m in fast]))


def _block_diag(x):
    z = jnp.zeros((x.shape[0], HG_HEAD_DIM), x.dtype)
    top = jnp.concatenate([x[:, :HG_HEAD_DIM], z], axis=1)
    bot = jnp.concatenate([z, x[:, HG_HEAD_DIM:]], axis=1)
    return jnp.concatenate([top, bot], axis=0)


def _rows(v, n):
    return jnp.broadcast_to(v, (n, v.shape[1]))


def _hgrn_finish(o, gv, nw, o_ref, rs):
    for h in range(HG_PAIR):
        cs = slice(h * HG_HEAD_DIM, (h + 1) * HG_HEAD_DIM)
        oh = o[:, cs]
        ms = jnp.mean(oh * oh, axis=-1, keepdims=True)
        gg = gv[:, cs]
        o_ref[rs, cs] = (oh * lax.rsqrt(ms + RMS_EPS) * nw * (gg * _sigmoid(gg))).astype(BF16)


def _hgrn_body(q_ref, f_ref, i_ref, g_ref, par_ref, nw_ref, tri_ref, mst_ref, mask_ref,
               fmask_ref, o_ref, st_ref, qs_ref, kk_ref, lf_ref, b_ref, *, nc):
    c = HG_CHUNK
    hd = HG_HEAD_DIM
    wb = HG_PAIR_W
    t = pl.program_id(2)

    @pl.when(t == 0)
    def _():
        st_ref[...] = jnp.zeros_like(st_ref)

    log_lb = par_ref[0:1, :]
    log_1m_lb = par_ref[1:2, :]
    one_m_lb = par_ref[2:3, :]
    nw = nw_ref[...]
    tri = tri_ref[...]
    row = lax.broadcasted_iota(jnp.int32, (c, wb), 0)

    wmin = None
    for ch in range(nc):
        rs = slice(ch * c, (ch + 1) * c)
        fr = f_ref[rs, :]
        e = jnp.exp(-jnp.abs(fr))
        r = 1.0 / (1.0 + e)
        sig_neg = jnp.where(fr >= 0, e * r, r)
        log_sig = jnp.minimum(fr, 0.0) - jnp.log(1.0 + e)
        y = log_1m_lb + log_sig
        log_f = jnp.maximum(log_lb, y) + jnp.log(1.0 + jnp.exp(-jnp.abs(log_lb - y)))
        hi = log_f.astype(BF16)
        lo = (log_f - hi.astype(F32)).astype(BF16)
        b = _dot(tri, hi) + _dot(tri, lo)
        qv = q_ref[rs, :]
        qs_ref[rs, :] = qv * _sigmoid(qv)
        kk_ref[rs, :] = one_m_lb * sig_neg
        lf_ref[rs, :] = log_f
        b_ref[rs, :] = b
        fb = HG_FAST_BLOCK
        starts = jnp.concatenate(
            [jnp.zeros((fb, wb), F32)]
            + [_rows(b[j * fb - 1:j * fb, :], fb) for j in range(1, c // fb)], axis=0)
        w = b - starts
        wmin = w if wmin is None else jnp.minimum(wmin, w)
    fast = jnp.min(wmin) >= HG_FAST_MIN_LOG_DECAY

    @pl.when(fast)
    def _():
        for ch in range(nc):
            rs = slice(ch * c, (ch + 1) * c)
            b = b_ref[rs, :]
            qs = qs_ref[rs, :]
            kk = kk_ref[rs, :]
            vv = i_ref[rs, :].astype(BF16)
            half = c // 2
            b63 = _rows(b[half - 1:half, :], half)
            zero_half = jnp.zeros((half, wb), BF16)
            q6 = jnp.concatenate(
                [zero_half, (qs[half:, :] * jnp.exp(b[half:, :] - b63)).astype(BF16)], axis=0)
            k6 = jnp.concatenate(
                [(kk[:half, :] * jnp.exp(b63 - b[:half, :])).astype(BF16), zero_half], axis=0)
            scores = lax.dot_general(q6, _block_diag(k6), _NT, preferred_element_type=F32)
            bm = jnp.concatenate([_rows(b[31:32, :], 64), _rows(b[95:96, :], 64)], axis=0)
            a5 = jnp.exp(jnp.where((row % 64) >= 32, b - bm, bm - b))
            fb = HG_FAST_BLOCK
            starts = jnp.concatenate(
                [jnp.zeros((fb, wb), F32)]
                + [_rows(b[j * fb - 1:j * fb, :], fb) for j in range(1, c // fb)], axis=0)
            w = b - starts
            for lev, (aq, ak) in enumerate([(a5, a5), (jnp.exp(w), jnp.exp(-w))]):
                sc = lax.dot_general((qs * aq).astype(BF16), _block_diag((kk * ak).astype(BF16)),
                                     _NT, preferred_element_type=F32)
                scores = scores + fmask_ref[lev] * sc
            st = st_ref[...]
            v_bd = _block_diag(vv)
            o = (_dot(scores.astype(BF16), v_bd)
                 + lax.dot_general((qs * jnp.exp(b)).astype(BF16), st.astype(BF16), _NT,
                                   preferred_element_type=F32))
            bend = b[c - 1:c, :]
            kd = (kk * jnp.exp(bend - b)).astype(BF16)
            st_ref[...] = st * jnp.exp(bend) + lax.dot_general(
                v_bd, _block_diag(kd), _TN, preferred_element_type=F32)
            _hgrn_finish(o, g_ref[rs, :], nw, o_ref, rs)

    @pl.when(jnp.logical_not(fast))
    def _():
        mst = mst_ref[...]

        def chunk(ch, carry):
            rs = pl.ds(pl.multiple_of(ch * c, c), c)
            log_f = lf_ref[rs, :]
            hi = log_f.astype(BF16)
            lo = (log_f - hi.astype(F32)).astype(BF16)
            ee = _dot(mst, hi) + _dot(mst, lo)
            qs = qs_ref[rs, :]
            kk = kk_ref[rs, :]
            vv = i_ref[rs, :].astype(BF16)
            bcum = ee[HG_LEVELS * c:(HG_LEVELS + 1) * c, :]
            outs = []
            for h in range(HG_PAIR):
                cs = slice(h * hd, (h + 1) * hd)
                qh = qs[:, cs]
                kh = kk[:, cs]
                vh = vv[:, cs]
                scores = mask_ref[HG_LEVELS] * jnp.sum(qh * kh, axis=-1, keepdims=True)
                for lev in range(HG_LEVELS):
                    a = jnp.exp(ee[lev * c:(lev + 1) * c, cs])
                    sc = lax.dot_general((qh * a).astype(BF16), (kh * a).astype(BF16), _NT,
                                         preferred_element_type=F32)
                    scores = scores + mask_ref[lev] * sc
                bh = bcum[:, cs]
                st = st_ref[cs, cs]
                outs.append(_dot(scores.astype(BF16), vh)
                            + lax.dot_general((qh * jnp.exp(bh)).astype(BF16), st.astype(BF16),
                                              _NT, preferred_element_type=F32))
                bend = bh[c - 1:c, :]
                kd = (kh * jnp.exp(bend - bh)).astype(BF16)
                st_ref[cs, cs] = st * jnp.exp(bend) + lax.dot_general(
                    vh, kd, _TN, preferred_element_type=F32)
            _hgrn_finish(jnp.concatenate(outs, axis=1), g_ref[rs, :], nw, o_ref, rs)
            return carry

        lax.fori_loop(0, nc, chunk, 0)


def _hgrn_mixer(proj3, lb, norm_w, tl):
    nb, seq, _ = proj3.shape
    nc = tl // HG_CHUNK
    wb = HG_PAIR_W
    q0 = (2 * SGU_WIDTH) // wb
    nblk = HG_WIDTH // wb
    lbf = lb.astype(F32)
    par = jnp.stack([jnp.log(lbf), jnp.log1p(-lbf), 1.0 - lbf])
    tb = _hgrn_tables()
    sec = lambda k: pl.BlockSpec((None, tl, wb), functools.partial(
        lambda b, h, t, k: (b, t, q0 + k * nblk + h), k=k))
    const = lambda a: pl.BlockSpec(a.shape, lambda b, h, t: (0,) * a.ndim)
    body = functools.partial(_hgrn_body, nc=nc)
    return pl.pallas_call(
        body,
        out_shape=jax.ShapeDtypeStruct((nb, seq, HG_WIDTH), BF16),
        grid=(nb, nblk, seq // tl),
        in_specs=[sec(0), sec(1), sec(2), sec(3),
                  pl.BlockSpec((3, wb), lambda b, h, t: (0, h)),
                  pl.BlockSpec((1, HG_HEAD_DIM), lambda b, h, t: (0, 0)),
                  const(tb["tri"]), const(tb["mst"]), const(tb["masks"]),
                  const(tb["fmasks"])],
        out_specs=pl.BlockSpec((None, tl, wb), lambda b, h, t: (b, t, h)),
        scratch_shapes=[pltpu.VMEM((wb, wb), F32)] + [pltpu.VMEM((tl, wb), F32)] * 4,
        compiler_params=_cparams(("parallel", "parallel", "arbitrary")),
        name="hgrn2_mixer")(
            proj3, proj3, proj3, proj3, par,
            norm_w.astype(F32).reshape(1, HG_HEAD_DIM),
            tb["tri"], tb["mst"], tb["masks"], tb["fmasks"])


def kernel(x, p, w_in, s5_lam_re, s5_lam_im, s5_log_step, s5_b_re, s5_b_im, s5_c_re, s5_c_im, s5_d, s5_w_glu, sgu_ln_w, sgu_ln_b, sgu_w, sgu_b, hg_lb_logits, hg_norm_w, norm_a_w, norm_b_w, w_out, ln1_w, ln1_b, w_ffn_in, w_ffn_out, ln2_w, ln2_b, w_ple_in, w_ple_gate, ln3_w, ln3_b):
    nb, seq, d_model = x.shape
    depth = w_in.shape[0]
    t_rows = nb * seq
    proj_w = w_in.shape[2]
    d_ff = w_ffn_out.shape[1]
    alpha = (2.0 * depth) ** 0.25

    tm = _pick(t_rows, (1024, 512, 256, 128))
    tm_wide = _pick(t_rows, (2048, 1024, 512, 256, 128))
    tm_half = _pick(t_rows, (512, 256, 128))
    tm_ln = _pick(t_rows, (512, 256, 128))
    tl_s5 = _pick(seq, (1024, 512, 256, 128))
    tl_sgu = _pick(seq, (1024, 512, 256, 128))
    tl_hg = _pick(seq, (2048, 1024, 512, 256, 128))
    tn_in = _pick(proj_w - S5_WIDTH, (1024, 512))

    lbs =jnp.cumsum(jax.nn.softmax(hg_lb_logits.astype(F32), axis=0), axis=0)
    lbs = lbs - lbs[0:1]

    x_rows = x.reshape(t_rows, d_model).astype(F32)
    res = _Residual(alpha, h=x_rows)
    h16 = _layer_to_bf16(x_rows.reshape(1, t_rows, d_model), 0)
    p_rows = p.reshape(depth, t_rows, -1)

    w_in16 = _layer_to_bf16(w_in, 0)
    for l in range(depth):
        if l == 0:
            xa_tiles = _s5_in_proj(h16, w_in16, tm)
        else:
            lw, lb = ln3_w[l - 1], ln3_b[l - 1]
            xa_tiles, mu, rstd, h16 = _ln_fused_mm(
                xres, lw, lb, [(None, w_in16, d_model, 0, 0)], [],
                lambda accs, ex: [accs[0]], [F32], S5_WIDTH, tm, S5_WIDTH // 4, "in_proj_s5",
                emit_h16=True, out_tiles=True)
            res = _Residual(alpha, x=xres, mu=mu, rstd=rstd, w=lw, b=lb)
        xa_tiles = xa_tiles.reshape(-1, nb, seq, LANES)
        proj, wf, wo = _fused_mm(
            [(h16, w_in16, d_model, 0, S5_WIDTH // tn_in)], [],
            lambda accs, ex: [accs[0]], [F32], proj_w - S5_WIDTH, tm, tn_in, "in_proj",
            side_casts=[(w_ffn_in, l), (w_out, l)])
        proj3 = proj.reshape(nb, seq, proj_w - S5_WIDTH)
        prep = _s5_prepare(s5_lam_re[l], s5_lam_im[l], s5_log_step[l], s5_b_re[l],
                           s5_b_im[l], s5_c_re[l], s5_c_im[l], s5_d[l])
        z_tiles = _s5_mixer(xa_tiles, prep, tl_s5).reshape(-1, t_rows, LANES)
        ya = _s5_glu_norm(z_tiles, _layer_to_bf16(s5_w_glu, l), norm_a_w[l], tm_half)
        yb = _sgu_mixer(proj3, sgu_ln_w[l], sgu_ln_b[l], sgu_w[l], sgu_b[l],
                        norm_b_w[l], tl_sgu).reshape(t_rows, SGU_WIDTH)
        yc = _hgrn_mixer(proj3, lbs[l], hg_norm_w[l], tl_hg).reshape(t_rows, HG_WIDTH)
        (xres,) = _fused_mm(
            [(ya, wo, S5_WIDTH, 0, 0), (yb, wo, SGU_WIDTH, 1, 0), (yc, wo, HG_WIDTH, 1, 0)],
            res.extras,
            functools.partial(lambda accs, ex, res: [res(ex) + (accs[0] + accs[1] + accs[2])],
                              res=res),
            [F32], d_model, tm, 1024, "out_proj")

        hid, mu, rstd, wfo, wpg = _ln_fused_mm(
            xres, ln1_w[l], ln1_b[l],
            [(None, wf, d_model, 0, 0), (None, wf, d_model, 0, d_ff // 256)], [],
            lambda accs, ex: [accs[0] * _sigmoid(accs[0]) * accs[1]],
            [BF16], d_ff, tm_wide, 256, "ffn_in",
            side_casts=[(w_ffn_out, l), (w_ple_gate, l)])
        res = _Residual(alpha, x=xres, mu=mu, rstd=rstd, w=ln1_w[l], b=ln1_b[l])
        (xres,) = _fused_mm(
            [(hid, wfo, d_ff, 0, 0)], res.extras,
            functools.partial(lambda accs, ex, res: [res(ex) + accs[0]], res=res),
            [F32], d_model, tm_half, 1024, "ffn_out", cols_outer=True)

        p16 = _layer_to_bf16(p_rows, l)
        xres, _, _, *next_w_in = _ln_fused_mm(
            xres, ln2_w[l], ln2_b[l],
            [(None, wpg, d_model, 0, 0),
             (p16, _layer_to_bf16(w_ple_in, l), p16.shape[1], 0, 0)], [],
            lambda accs, ex, resid: [resid + accs[1] * _sigmoid(accs[0])],
            [F32], d_model, tm, 512, "ple", self_alpha=alpha,
            side_casts=[(w_in, l + 1)] if l + 1 < depth else [])
        if next_w_in:
            w_in16 = next_w_in[0]

    out = _layer_norm_final(xres, ln3_w[depth - 1], ln3_b[depth - 1], tm_ln)
    return out.reshape(nb, seq, d_model).astype(x.dtype)
```

```python
import functools
import math

import jax
import jax.numpy as jnp
from jax import lax
from jax.experimental import pallas as pl
from jax.experimental.pallas import tpu as pltpu

F32 = jnp.float32
BF16 = jnp.bfloat16

V7X_VMEM_LIMIT_BYTES = 56 * 1024 * 1024
LANES = 128

LN_EPS = 1e-5
RMS_EPS = 1e-6

S5_WIDTH = 1024
S5_GROUP_CH = 16
S5_STATE = 64
S5_CHUNK = 8
S5_TILE_GROUPS = LANES // S5_GROUP_CH
S5_TILE_STATE = S5_TILE_GROUPS * S5_STATE

SGU_WIDTH = 1024
SGU_CHUNK = 128
SGU_HEADS = 8

HG_WIDTH = 2048
HG_HEAD_DIM = 128
HG_CHUNK = 128
HG_LEVELS = 7


def _cparams(sem):
    return pltpu.CompilerParams(dimension_semantics=sem,
                                vmem_limit_bytes=V7X_VMEM_LIMIT_BYTES)


def _gelu_tanh(x):
    c = math.sqrt(2.0 / math.pi)
    return x * (0.5 + 0.5 * jnp.tanh(x * (c + (c * 0.044715) * (x * x))))


def _sigmoid(x):
    return 1.0 / (1.0 + jnp.exp(-x))


def _dot(a, b):
    return jnp.dot(a, b, preferred_element_type=F32)


def _pick(n, prefs):
    for p in prefs:
        if n % p == 0:
            return p
    return n


CAST_BLOCK_BYTES = 8 * 1024 * 1024


def _cast_body(w_ref, o_ref):
    o_ref[...] = w_ref[...].astype(BF16)


def _layer_to_bf16(w, layer):
    _, rows, cols = w.shape
    fits = [tr for tr in (8192, 4096, 2048, 1024, 512, 256, 128, 64, 32, 16)
            if rows % tr == 0 and tr * cols * 4 <= CAST_BLOCK_BYTES]
    tr = fits[0]
    return pl.pallas_call(
        _cast_body,
        out_shape=jax.ShapeDtypeStruct((rows, cols), BF16),
        grid=(rows // tr,),
        in_specs=[pl.BlockSpec((None, tr, cols), lambda i: (layer, i, 0))],
        out_specs=pl.BlockSpec((tr, cols), lambda i: (i, 0)),
        compiler_params=_cparams(("parallel",)), name="cast_bf16")(w)


MM_SUB_ROWS = 256


def _mm_body(*refs, a_of_dot, n_a, extra_kinds, n_side, epilogue, sub_rows):
    n_dot = len(a_of_dot)
    n_extra = len(extra_kinds)
    n_in = n_a + n_dot + n_extra
    a_refs = refs[:n_a]
    w_refs = refs[n_a:n_a + n_dot]
    extra = refs[n_a + n_dot:n_in]
    side_in = refs[n_in:n_in + n_side]
    out_refs = refs[n_in + n_side:len(refs) - n_side]
    side_out = refs[len(refs) - n_side:]
    for si, so in zip(side_in, side_out):
        so[...] = si[...].astype(BF16)
    rows = out_refs[0].shape[0]
    sub = min(rows, sub_rows)
    for r in range(rows // sub):
        rs = slice(r * sub, (r + 1) * sub)
        accs = [_dot(a_refs[ai][rs, :], w[...]) for ai, w in zip(a_of_dot, w_refs)]
        outs = epilogue(accs, [e[...] if kind == "row" else e[rs, :]
                               for e, kind in zip(extra, extra_kinds)])
        for o_ref, o in zip(out_refs, outs):
            o_ref[rs, :] = o.astype(o_ref.dtype)


BF16_ROW_TILE = 16


def _fused_mm(dots, extras, epilogue, out_dtypes, n_cols, tm, tn, name,
              sub_rows=MM_SUB_ROWS, side_casts=(), cols_outer=False):
    t_rows = dots[0][0].shape[0]
    ni, nj = t_rows // tm, n_cols // tn
    grid = (nj, ni) if cols_outer else (ni, nj)

    def spec(shape, index_map, **kw):
        if cols_outer:
            return pl.BlockSpec(shape, lambda g0, g1: index_map(g1, g0), **kw)
        return pl.BlockSpec(shape, index_map, **kw)

    w_mode = dict(pipeline_mode=pl.Buffered(1)) if cols_outer else {}
    in_specs, args = [], []
    a_of_dot = []
    for a, _, _, _, _ in dots:
        known = [k for k, seen in enumerate(args) if seen is a]
        if known:
            a_of_dot.append(known[0])
            continue
        a_of_dot.append(len(args))
        in_specs.append(spec((tm, a.shape[1]), lambda i, j: (i, 0)))
        args.append(a)
    n_a = len(args)
    for _, w, rb, ri, co in dots:
        in_specs.append(spec((rb, tn), functools.partial(
            lambda i, j, ri, co: (ri, j + co), ri=ri, co=co), **w_mode))
        args.append(w)
    for arr, kind in extras:
        if kind == "tile":
            in_specs.append(spec((tm, tn), lambda i, j: (i, j)))
        elif kind == "stat":
            in_specs.append(spec((tm, LANES), lambda i, j: (i, 0)))
        else:
            in_specs.append(spec((1, tn), lambda i, j: (0, j)))
        args.append(arr)
    out_shape = [jax.ShapeDtypeStruct((t_rows, n_cols), dt) for dt in out_dtypes]
    out_specs = [spec((tm, tn), lambda i, j: (i, j)) for _ in out_dtypes]
    n_steps = grid[0] * grid[1]
    for stack, layer in side_casts:
        _, rows, cols = stack.shape
        tiles = rows // BF16_ROW_TILE
        n_slabs = max(d for d in range(1, min(tiles, n_steps) + 1) if tiles % d == 0)
        slab = functools.partial(lambda g0, g1, last: jnp.minimum(g0 * grid[1] + g1, last),
                                 last=n_slabs - 1)
        in_specs.append(pl.BlockSpec(
            (None, rows // n_slabs, cols),
            functools.partial(lambda g0, g1, layer, slab: (layer, slab(g0, g1), 0),
                              layer=layer, slab=slab)))
        args.append(stack)
        out_shape.append(jax.ShapeDtypeStruct((rows, cols), BF16))
        out_specs.append(pl.BlockSpec(
            (rows // n_slabs, cols),
            functools.partial(lambda g0, g1, slab: (slab(g0, g1), 0), slab=slab)))
    body = functools.partial(_mm_body, a_of_dot=tuple(a_of_dot), n_a=n_a,
                             extra_kinds=tuple(kind for _, kind in extras),
                             n_side=len(side_casts), epilogue=epilogue, sub_rows=sub_rows)
    return pl.pallas_call(
        body, out_shape=out_shape, grid=grid, in_specs=in_specs, out_specs=out_specs,
        compiler_params=_cparams(("parallel", "arbitrary")), name=name)(*args)


def _ln_rows(x):
    mu = jnp.mean(x, axis=-1, keepdims=True)
    xc = x - mu
    var = jnp.mean(xc * xc, axis=-1, keepdims=True)
    return xc, mu, lax.rsqrt(var + LN_EPS)


LN_ROW_GROUP = 8


def _ln_final_body(x_ref, w_ref, b_ref, h_ref):
    w = w_ref[...]
    b = b_ref[...]
    for g in range(x_ref.shape[0] // LN_ROW_GROUP):
        rs = slice(g * LN_ROW_GROUP, (g + 1) * LN_ROW_GROUP)
        xc, _, rstd = _ln_rows(x_ref[rs, :])
        h_ref[rs, :] = xc * rstd * w + b


def _layer_norm_final(x, w, b, tm):
    t_rows, d = x.shape
    rows = pl.BlockSpec((tm, d), lambda i: (i, 0))
    vec = pl.BlockSpec((1, d), lambda i: (0, 0))
    return pl.pallas_call(
        _ln_final_body, out_shape=jax.ShapeDtypeStruct((t_rows, d), F32),
        grid=(t_rows // tm,), in_specs=[rows, vec, vec], out_specs=rows,
        compiler_params=_cparams(("parallel",)), name="layer_norm")(
            x, w.reshape(1, d), b.reshape(1, d))


def _ln_mm_body(*refs, a_of_dot, n_other_a, extra_kinds, n_side, n_out, emit_h16, out_tiles,
                self_alpha, epilogue, sub_rows, n_slab, slab_rows):
    n_dot = len(a_of_dot)
    n_extra = len(extra_kinds)
    x_ref, lnw_ref, lnb_ref = refs[:3]
    pos = 3
    other_a = refs[pos:pos + n_other_a]
    pos += n_other_a
    w_refs = refs[pos:pos + n_dot]
    pos += n_dot
    extra = refs[pos:pos + n_extra]
    pos += n_extra
    side_in = refs[pos:pos + n_side]
    pos += n_side
    out_refs = refs[pos:pos + n_out]
    pos += n_out
    mu_ref, rs_ref = refs[pos:pos + 2]
    pos += 2
    h16_ref = refs[pos] if emit_h16 else None
    pos += int(emit_h16)
    side_out = refs[pos:pos + n_side]
    pos += n_side
    a_slots, mu_slots, rs_slots = refs[pos:pos + 2], refs[pos + 2:pos + 4], refs[pos + 4:pos + 6]

    g = pl.program_id(0)
    j = pl.program_id(1)
    group = 2 * LN_ROW_GROUP

    def side_jobs(slot):
        a_sc, mu_sc, rs_sc = a_slots[slot], mu_slots[slot], rs_slots[slot]
        for si, so in zip(side_in, side_out):
            so[...] = si[...].astype(BF16)
        row0 = pl.multiple_of(jnp.minimum(j, n_slab - 1) * slab_rows, slab_rows)
        w = lnw_ref[...]
        b = lnb_ref[...]
        for grp in range(slab_rows // group):
            ys = []
            for half in range(2):
                r8 = slice(grp * group + half * LN_ROW_GROUP,
                           grp * group + (half + 1) * LN_ROW_GROUP)
                xc, mu, rstd = _ln_rows(x_ref[r8, :])
                ys.append(xc * rstd * w + b)
                mu_b = jnp.broadcast_to(mu, (LN_ROW_GROUP, LANES))
                rs_b = jnp.broadcast_to(rstd, (LN_ROW_GROUP, LANES))
                mu_ref[r8, :] = mu_b
                rs_ref[r8, :] = rs_b
                dst = pl.ds(row0 + grp * group + half * LN_ROW_GROUP, LN_ROW_GROUP)
                mu_sc[dst, :] = mu_b
                rs_sc[dst, :] = rs_b
            y16 = jnp.concatenate(ys, axis=0).astype(BF16)
            a_sc[pl.ds(row0 + grp * group, group), :] = y16
            if emit_h16:
                h16_ref[grp * group:(grp + 1) * group, :] = y16

    def matmul(slot):
        a_sc, mu_sc, rs_sc = a_slots[slot], mu_slots[slot], rs_slots[slot]
        rows = a_sc.shape[0]
        sub = min(rows, sub_rows)
        for r in range(rows // sub):
            rs = slice(r * sub, (r + 1) * sub)
            a_ln = a_sc[rs, :]
            accs = [_dot(a_ln if ai < 0 else other_a[ai][rs, :], w[...])
                    for ai, w in zip(a_of_dot, w_refs)]
            ex = [e[...] if kind == "row" else e[rs, :] for e, kind in zip(extra, extra_kinds)]
            if self_alpha is not None:
                x_tile, w_row, b_row = ex[:3]
                ex = ex[3:]
                rep = x_tile.shape[1] // LANES
                wide = lambda s: jnp.concatenate([s] * rep, axis=1)
                resid = self_alpha * ((x_tile - wide(mu_sc[rs, :]))
                                      * wide(rs_sc[rs, :]) * w_row + b_row)
                outs = epilogue(accs, ex, resid)
            else:
                outs = epilogue(accs, ex)
            for o_ref, o in zip(out_refs, outs):
                if out_tiles:
                    for c in range(o_ref.shape[0]):
                        o_ref[c, rs, :] = o[:, c * LANES:(c + 1) * LANES].astype(o_ref.dtype)
                else:
                    o_ref[rs, :] = o.astype(o_ref.dtype)

    @pl.when(g == 0)
    def _():
        side_jobs(0)

    for parity in (0, 1):
        @pl.when(jnp.logical_and(g > 0, lax.rem(g, 2) == parity))
        def _():
            side_jobs(parity)
            matmul(1 - parity)


def _ln_fused_mm(x, ln_w, ln_b, dots, extras, epilogue, out_dtypes, n_cols, tm, tn, name,
                 sub_rows=MM_SUB_ROWS, side_casts=(), emit_h16=False, out_tiles=False,
                 self_alpha=None):
    t_rows, d = x.shape
    ni, nj = t_rows // tm, n_cols // tn
    grid = (ni + 1, nj)
    n_slab = max(s for s in range(1, nj + 1)
                 if tm % s == 0 and (tm // s) % (2 * LN_ROW_GROUP) == 0)
    slab_rows = tm // n_slab
    prev = lambda g: jnp.maximum(g - 1, 0)
    first_col = lambda g, j: jnp.where(g > 0, j, 0)
    slab_idx = lambda g, j: (jnp.where(g < ni, g * n_slab + jnp.minimum(j, n_slab - 1),
                                       ni * n_slab - 1), 0)
    vec = lambda a: a.astype(F32).reshape(1, d)
    in_specs = [pl.BlockSpec((slab_rows, d), slab_idx),
                pl.BlockSpec((1, d), lambda g, j: (0, 0)),
                pl.BlockSpec((1, d), lambda g, j: (0, 0))]
    args = [x, vec(ln_w), vec(ln_b)]
    a_of_dot, other = [], []
    for a, _, _, _, _ in dots:
        if a is None:
            a_of_dot.append(-1)
            continue
        a_of_dot.append(len(other))
        other.append(a)
        in_specs.append(pl.BlockSpec((tm, a.shape[1]), lambda g, j: (prev(g), 0)))
        args.append(a)
    for _, w, rb, ri, co in dots:
        in_specs.append(pl.BlockSpec((rb, tn), functools.partial(
            lambda g, j, ri, co: (ri, j + co), ri=ri, co=co)))
        args.append(w)
    extras = list(extras)
    if self_alpha is not None:
        extras = [(x, "tile"), (vec(ln_w), "row"), (vec(ln_b), "row")] + extras
    for arr, kind in extras:
        if kind == "tile":
            in_specs.append(pl.BlockSpec((tm, tn), lambda g, j: (prev(g), j)))
        elif kind == "stat":
            in_specs.append(pl.BlockSpec((tm, LANES), lambda g, j: (prev(g), 0)))
        else:
            in_specs.append(pl.BlockSpec((1, tn), lambda g, j: (0, j)))
        args.append(arr)
    if out_tiles:
        n_tiles = n_cols // LANES
        out_shape = [jax.ShapeDtypeStruct((n_tiles, t_rows, LANES), dt) for dt in out_dtypes]
        out_specs = [pl.BlockSpec((tn // LANES, tm, LANES),
                                  lambda g, j: (first_col(g, j), prev(g), 0))
                     for _ in out_dtypes]
    else:
        out_shape = [jax.ShapeDtypeStruct((t_rows, n_cols), dt) for dt in out_dtypes]
        out_specs = [pl.BlockSpec((tm, tn), lambda g, j: (prev(g), first_col(g, j)))
                     for _ in out_dtypes]
    n_out = len(out_shape)
    for _ in range(2):
        out_shape.append(jax.ShapeDtypeStruct((t_rows, LANES), F32))
        out_specs.append(pl.BlockSpec((slab_rows, LANES), slab_idx))
    if emit_h16:
        out_shape.append(jax.ShapeDtypeStruct((t_rows, d), BF16))
        out_specs.append(pl.BlockSpec((slab_rows, d), slab_idx))
    n_steps = grid[0] * grid[1]
    for stack, layer in side_casts:
        _, rows, cols = stack.shape
        tiles = rows // BF16_ROW_TILE
        n_cast = max(s for s in range(1, min(tiles, n_steps) + 1) if tiles % s == 0)
        slab = functools.partial(lambda g, j, last: jnp.minimum(g * nj + j, last),
                                 last=n_cast - 1)
        in_specs.append(pl.BlockSpec(
            (None, rows // n_cast, cols),
            functools.partial(lambda g, j, layer, slab: (layer, slab(g, j), 0),
                              layer=layer, slab=slab)))
        args.append(stack)
        out_shape.append(jax.ShapeDtypeStruct((rows, cols), BF16))
        out_specs.append(pl.BlockSpec(
            (rows // n_cast, cols),
            functools.partial(lambda g, j, slab: (slab(g, j), 0), slab=slab)))
    body = functools.partial(
        _ln_mm_body, a_of_dot=tuple(a_of_dot), n_other_a=len(other),
        extra_kinds=tuple(kind for _, kind in extras), n_side=len(side_casts), n_out=n_out,
        emit_h16=emit_h16, out_tiles=out_tiles, self_alpha=self_alpha, epilogue=epilogue,
        sub_rows=sub_rows, n_slab=n_slab, slab_rows=slab_rows)
    return pl.pallas_call(
        body, out_shape=out_shape, grid=grid, in_specs=in_specs, out_specs=out_specs,
        scratch_shapes=([pltpu.VMEM((tm, d), BF16)] * 2
                        + [pltpu.VMEM((tm, LANES), F32)] * 4),
        compiler_params=_cparams(("arbitrary", "arbitrary")), name=name)(*args)


class _Residual:
    def __init__(self, alpha, h=None, x=None, mu=None, rstd=None, w=None, b=None):
        self.alpha = alpha
        if h is not None:
            self.extras = [(h, "tile")]
        else:
            self.extras = [(x, "tile"), (mu, "stat"), (rstd, "stat"),
                           (w.astype(F32).reshape(1, -1), "row"),
                           (b.astype(F32).reshape(1, -1), "row")]

    def __call__(self, ex):
        if len(ex) == 1:
            return self.alpha * ex[0]
        x, mu, rstd, w, b = ex
        rep = x.shape[1] // LANES
        wide = lambda s: jnp.concatenate([s] * rep, axis=1)
        return self.alpha * ((x - wide(mu)) * wide(rstd) * w + b)


def _s5_prepare(lam_re, lam_im, log_step, b_re, b_im, c_re, c_im, d):
    hp = lax.Precision.HIGHEST
    n_tiles = S5_WIDTH // LANES
    lr = jnp.minimum(lam_re.astype(F32), -1e-4)
    li = lam_im.astype(F32)
    dt = jnp.exp(log_step.astype(F32))[:, None]
    mag = jnp.exp(lr * dt)
    ab_re = mag * jnp.cos(li * dt)
    ab_im = mag * jnp.sin(li * dt)
    den = lr * lr + li * li
    nr = ab_re - 1.0
    g_re = (nr * lr + ab_im * li) / den
    g_im = (ab_im * lr - nr * li) / den
    br = b_re.astype(F32)
    bi = b_im.astype(F32)
    bb_re = g_re[..., None] * br - g_im[..., None] * bi
    bb_im = g_re[..., None] * bi + g_im[..., None] * br
    pr, pi = [jnp.ones_like(ab_re)], [jnp.zeros_like(ab_re)]
    for _ in range(S5_CHUNK):
        pr_new = pr[-1] * ab_re - pi[-1] * ab_im
        pi_new = pr[-1] * ab_im + pi[-1] * ab_re
        pr.append(pr_new)
        pi.append(pi_new)
    p_re = jnp.stack(pr)
    p_im = jnp.stack(pi)
    pb_re = (p_re[:S5_CHUNK, :, :, None] * bb_re[None]
             - p_im[:S5_CHUNK, :, :, None] * bb_im[None])
    pb_im = (p_re[:S5_CHUNK, :, :, None] * bb_im[None]
             + p_im[:S5_CHUNK, :, :, None] * bb_re[None])
    cr = c_re.astype(F32)
    ci = c_im.astype(F32)
    kern = (jnp.einsum("gon,jgni->jgoi", cr, pb_re, precision=hp)
            - jnp.einsum("gon,jgni->jgoi", ci, pb_im, precision=hp))
    tg, gc, ns = S5_TILE_GROUPS, S5_GROUP_CH, S5_STATE

    def spread(x2d, rep, row_group, col_group):
        rows, width = x2d.shape
        sel = jnp.tile(jnp.eye(width, dtype=BF16), (1, rep))
        out = jnp.dot(x2d.astype(BF16), sel)
        rg = row_group(jnp.arange(rows))[:, None]
        cg = col_group(jnp.arange(width * rep))[None, :]
        return jnp.where(rg == cg, out, jnp.zeros_like(out))

    def spread_steps(x2d, row_group):
        rows, width = x2d.shape
        cols = jnp.arange(S5_CHUNK * LANES)
        src = jnp.arange(width)
        sel = ((src[:, None] // gc == cols[None, :] // LANES)
               & (src[:, None] % gc == cols[None, :] % gc)).astype(BF16)
        out = jnp.dot(x2d.astype(BF16), sel)
        rg = row_group(jnp.arange(rows))[:, None]
        cg = ((cols // gc) % tg)[None, :]
        return jnp.where(rg == cg, out, jnp.zeros_like(out))

    k5 = kern.reshape(S5_CHUNK, n_tiles, tg, gc, gc).transpose(1, 0, 2, 4, 3)
    kblk = spread(k5.reshape(-1, gc), tg, lambda r: (r // gc) % tg,
                  lambda c: c // gc).reshape(n_tiles, S5_CHUNK, LANES, LANES)
    zero = jnp.zeros((n_tiles, LANES, LANES), BF16)
    toep = jnp.concatenate(
        [jnp.concatenate([zero] * t + [kblk[:, j] for j in range(S5_CHUNK - t)], axis=-1)
         for t in range(S5_CHUNK)], axis=-2)

    def w1_half(pb):
        pb5 = pb[::-1].reshape(S5_CHUNK, n_tiles, tg, ns, gc).transpose(1, 0, 2, 4, 3)
        return spread(pb5.reshape(-1, ns), tg, lambda r: (r // gc) % tg,
                      lambda c: c // ns).reshape(n_tiles, S5_CHUNK * LANES, S5_TILE_STATE)
    w1 = jnp.concatenate([w1_half(pb_re), w1_half(pb_im)], axis=-1)

    ca_re = cr[None] * p_re[1:, :, None, :] - ci[None] * p_im[1:, :, None, :]
    ca_im = cr[None] * p_im[1:, :, None, :] + ci[None] * p_re[1:, :, None, :]

    def p_half(ca):
        ca5 = ca.reshape(S5_CHUNK, n_tiles, tg, gc, ns).transpose(1, 2, 4, 0, 3)
        return spread_steps(ca5.reshape(-1, S5_CHUNK * gc), lambda r: (r // ns) % tg).reshape(
            n_tiles, S5_TILE_STATE, S5_CHUNK * LANES)
    pcat = jnp.concatenate([p_half(ca_re), -p_half(ca_im)], axis=1)

    a16 = jnp.concatenate([p_re[S5_CHUNK].reshape(n_tiles, 1, S5_TILE_STATE),
                           p_im[S5_CHUNK].reshape(n_tiles, 1, S5_TILE_STATE)], axis=-1)
    dd = d.astype(F32).reshape(n_tiles, 1, LANES)
    return w1, toep, pcat, a16, dd


def _s5_body(x_ref, w1_ref, toep_ref, pcat_ref, a16_ref, d_ref, z_ref,
             state_ref, xr_ref, q_ref, yi_ref, sin_ref, yc_ref, yint_ref, *, nb, tl):
    t = pl.program_id(1)
    nk = tl // S5_CHUNK
    n = nb * tl
    ts = S5_TILE_STATE

    @pl.when(t == 0)
    def _():
        state_ref[...] = jnp.zeros_like(state_ref)

    for b in range(nb):
        for tp in range(S5_CHUNK):
            xr_ref[tp, pl.ds(b, nk, stride=nb), :] = x_ref[b, pl.ds(tp, nk, stride=S5_CHUNK), :]
    xr = jnp.concatenate([xr_ref[tp].astype(BF16) for tp in range(S5_CHUNK)], axis=1)
    q_ref[...] = _dot(xr, w1_ref[...])
    yi_ref[...] = _dot(xr, toep_ref[...])

    a_re = a16_ref[:, :ts]
    a_im = a16_ref[:, ts:]
    s = state_ref[...]
    for k in range(nk):
        sin_ref[k * nb:(k + 1) * nb, :] = s
        s_re = s[:, :ts]
        s_im = s[:, ts:]
        s = jnp.concatenate([a_re * s_re - a_im * s_im, a_re * s_im + a_im * s_re],
                            axis=1) + q_ref[k * nb:(k + 1) * nb, :]
    state_ref[...] = s

    yc = yi_ref[...] + _dot(sin_ref[...].astype(BF16), pcat_ref[...])
    for tp in range(S5_CHUNK):
        yc_ref[tp] = yc[:, tp * LANES:(tp + 1) * LANES]
    for b in range(nb):
        for tp in range(S5_CHUNK):
            yint_ref[pl.ds(b * tl + tp, nk, stride=S5_CHUNK), :] = (
                yc_ref[tp, pl.ds(b, nk, stride=nb), :])

    x = x_ref[...].reshape(n, LANES)
    z_ref[...] = _gelu_tanh(yint_ref[...] + d_ref[...] * x).reshape(nb, tl, LANES)


def _s5_in_proj_body(a_ref, w_ref, o_ref, a16_ref):
    rows = a_ref.shape[0]
    sub = min(rows, MM_SUB_ROWS)
    for r in range(rows // sub):
        rs = slice(r * sub, (r + 1) * sub)
        a16 = a_ref[rs, :].astype(BF16)
        a16_ref[rs, :] = a16
        acc = _dot(a16, w_ref[...])
        for c in range(o_ref.shape[0]):
            o_ref[c, rs, :] = acc[:, c * LANES:(c + 1) * LANES]


def _s5_in_proj(h, w_in16, tm):
    t_rows, d = h.shape
    n_tiles = S5_WIDTH // LANES
    return pl.pallas_call(
        _s5_in_proj_body,
        out_shape=[jax.ShapeDtypeStruct((n_tiles, t_rows, LANES), F32),
                   jax.ShapeDtypeStruct((t_rows, d), BF16)],
        grid=(t_rows // tm,),
        in_specs=[pl.BlockSpec((tm, d), lambda i: (i, 0)),
                  pl.BlockSpec((d, S5_WIDTH), lambda i: (0, 0))],
        out_specs=[pl.BlockSpec((n_tiles, tm, LANES), lambda i: (0, i, 0)),
                   pl.BlockSpec((tm, d), lambda i: (i, 0))],
        compiler_params=_cparams(("parallel",)), name="in_proj_s5")(h, w_in16)


def _s5_mixer(xa_tiles, prep, tl):
    w1, toep, pcat, a16, dd = prep
    n_tiles, nb, seq, _ = xa_tiles.shape
    nk = tl // S5_CHUNK
    ts = S5_TILE_STATE
    steps = S5_CHUNK * LANES
    body = functools.partial(_s5_body, nb=nb, tl=tl)
    wspec = lambda shp: pl.BlockSpec((None,) + shp, lambda j, t: (j, 0, 0))
    xspec = pl.BlockSpec((None, nb, tl, LANES), lambda j, t: (j, 0, t, 0))
    return pl.pallas_call(
        body,
        out_shape=jax.ShapeDtypeStruct(xa_tiles.shape, F32),
        grid=(n_tiles, seq // tl),
        in_specs=[xspec,
                  wspec((steps, 2 * ts)),
                  wspec((steps, steps)),
                  wspec((2 * ts, steps)),
                  wspec((1, 2 * ts)),
                  wspec((1, LANES))],
        out_specs=xspec,
        scratch_shapes=[pltpu.VMEM((nb, 2 * ts), F32),
                        pltpu.VMEM((S5_CHUNK, nb * nk, LANES), F32),
                        pltpu.VMEM((nb * nk, 2 * ts), F32),
                        pltpu.VMEM((nb * nk, steps), F32),
                        pltpu.VMEM((nb * nk, 2 * ts), F32),
                        pltpu.VMEM((S5_CHUNK, nb * nk, LANES), F32),
                        pltpu.VMEM((nb * tl, LANES), F32)],
        compiler_params=_cparams(("parallel", "arbitrary")), name="s5_mixer")(
            xa_tiles, w1, toep, pcat, a16, dd)


def _glu_body(z_ref, w_ref, nw_ref, o_ref):
    z = jnp.concatenate([z_ref[c] for c in range(z_ref.shape[0])], axis=1)
    y = z * _sigmoid(_dot(z.astype(BF16), w_ref[...]))
    ms = jnp.mean(y * y, axis=-1, keepdims=True)
    o_ref[...] = (y * lax.rsqrt(ms + RMS_EPS) * nw_ref[...]).astype(BF16)


def _s5_glu_norm(z_tiles, w_glu, norm_w, tm):
    n_tiles, t_rows, _ = z_tiles.shape
    d = n_tiles * LANES
    return pl.pallas_call(
        _glu_body,
        out_shape=jax.ShapeDtypeStruct((t_rows, d), BF16),
        grid=(t_rows // tm,),
        in_specs=[pl.BlockSpec((n_tiles, tm, LANES), lambda i: (0, i, 0)),
                  pl.BlockSpec((d, d), lambda i: (0, 0)),
                  pl.BlockSpec((1, d), lambda i: (0, 0))],
        out_specs=pl.BlockSpec((tm, d), lambda i: (i, 0)),
        compiler_params=_cparams(("parallel",)), name="s5_glu_norm")(
            z_tiles, w_glu, norm_w.reshape(1, d))


def _sgu_body(u_ref, v_ref, lnw_ref, lnb_ref, w_ref, bs_ref, nw_ref, o_ref,
              vb_ref, x_ref, *, nc):
    v = _gelu_tanh(v_ref[...])
    mu = jnp.mean(v, axis=-1, keepdims=True)
    vc = v - mu
    var = jnp.mean(vc * vc, axis=-1, keepdims=True)
    vb_ref[...] = (vc * lax.rsqrt(var + LN_EPS) * lnw_ref[...] + lnb_ref[...]).astype(BF16)
    hd = SGU_WIDTH // SGU_HEADS
    for h in range(SGU_HEADS):
        cs = slice(h * hd, (h + 1) * hd)
        rhs = jnp.concatenate(
            [vb_ref[c * SGU_CHUNK:(c + 1) * SGU_CHUNK, cs] for c in range(nc)], axis=1)
        zz = _dot(w_ref[h], rhs)
        for c in range(nc):
            rs = slice(c * SGU_CHUNK, (c + 1) * SGU_CHUNK)
            z = zz[:, c * hd:(c + 1) * hd] + bs_ref[:, cs]
            x_ref[rs, cs] = _gelu_tanh(u_ref[rs, cs]) * z
    x = x_ref[...]
    ms = jnp.mean(x * x, axis=-1, keepdims=True)
    o_ref[...] = (x * lax.rsqrt(ms + RMS_EPS) * nw_ref[...]).astype(BF16)


def _sgu_mixer(proj3, ln_w, ln_b, w_s, b_s, norm_w, tl):
    nb, seq, _ = proj3.shape
    nc = tl // SGU_CHUNK
    hd = SGU_WIDTH // SGU_HEADS
    causal = jnp.tril(jnp.ones((SGU_CHUNK, SGU_CHUNK), dtype=bool))
    w_causal = jnp.where(causal[None], w_s, jnp.zeros_like(w_s)).astype(BF16)
    bias = jnp.repeat(b_s.astype(F32).T, hd, axis=1)
    u_blk = 0
    row = lambda a: a.astype(F32).reshape(1, SGU_WIDTH)
    const = lambda shp: pl.BlockSpec(shp, lambda b, t: (0,) * len(shp))
    body = functools.partial(_sgu_body, nc=nc)
    return pl.pallas_call(
        body,
        out_shape=jax.ShapeDtypeStruct((nb, seq, SGU_WIDTH), BF16),
        grid=(nb, seq // tl),
        in_specs=[pl.BlockSpec((None, tl, SGU_WIDTH), lambda b, t: (b, t, u_blk)),
                  pl.BlockSpec((None, tl, SGU_WIDTH), lambda b, t: (b, t, u_blk + 1)),
                  const((1, SGU_WIDTH)), const((1, SGU_WIDTH)),
                  const((SGU_HEADS, SGU_CHUNK, SGU_CHUNK)),
                  const((SGU_CHUNK, SGU_WIDTH)), const((1, SGU_WIDTH))],
        out_specs=pl.BlockSpec((None, tl, SGU_WIDTH), lambda b, t: (b, t, 0)),
        scratch_shapes=[pltpu.VMEM((tl, SGU_WIDTH), BF16),
                        pltpu.VMEM((tl, SGU_WIDTH), F32)],
        compiler_params=_cparams(("parallel", "parallel")), name="sgu_mixer")(
            proj3, proj3, row(ln_w), row(ln_b), w_causal, bias, row(norm_w))


HG_PAIR = 2
HG_PAIR_W = HG_PAIR * HG_HEAD_DIM
HG_FAST_BLOCK = 32
assert HG_CHUNK == 4 * HG_FAST_BLOCK
HG_FAST_MIN_LOG_DECAY = -60.0
_NT = (((1,), (1,)), ((), ()))
_TN = (((0,), (0,)), ((), ()))


def _hgrn_tables():
    c = HG_CHUNK
    t = jnp.arange(c)[:, None]
    r = jnp.arange(c)[None, :]
    mats, masks = [], []
    for lev in range(HG_LEVELS):
        m = 1 << lev
        mid = (t // (2 * m)) * (2 * m) + m
        later = t >= mid
        mats.append(jnp.where(later, (r >= mid) & (r <= t), (r > t) & (r < mid)))
        same = (t // (2 * m)) == (r // (2 * m))
        masks.append(same & later & (r < mid))
    tri = r <= t
    mats.append(tri)
    masks.append(t == r)
    fb = HG_FAST_BLOCK
    fast = [masks[fb.bit_length() - 1], ((t // fb) == (r // fb)) & tri]
    pair = lambda m: jnp.tile(m.astype(F32), (1, HG_PAIR))
    return dict(
        tri=tri.astype(BF16),
        mst=jnp.concatenate(mats, axis=0).astype(BF16),
        masks=jnp.stack(masks).astype(F32),
        fmasks=jnp.stack([pair(m) for m in fast]))


def _block_diag(x):
    z = jnp.zeros((x.shape[0], HG_HEAD_DIM), x.dtype)
    top = jnp.concatenate([x[:, :HG_HEAD_DIM], z], axis=1)
    bot = jnp.concatenate([z, x[:, HG_HEAD_DIM:]], axis=1)
    return jnp.concatenate([top, bot], axis=0)


def _rows(v, n):
    return jnp.broadcast_to(v, (n, v.shape[1]))


def _block_starts(b):
    fb = HG_FAST_BLOCK
    return jnp.concatenate(
        [jnp.zeros((fb, b.shape[1]), F32)]
        + [_rows(b[j * fb - 1:j * fb, :], fb) for j in range(1, b.shape[0] // fb)], axis=0)


def _hgrn_finish(o, gv, nw, o_ref, rs):
    for h in range(HG_PAIR):
        cs = slice(h * HG_HEAD_DIM, (h + 1) * HG_HEAD_DIM)
        oh = o[:, cs]
        ms = jnp.mean(oh * oh, axis=-1, keepdims=True)
        gg = gv[:, cs]
        o_ref[rs, cs] = (oh * lax.rsqrt(ms + RMS_EPS) * nw * (gg * _sigmoid(gg))).astype(BF16)


def _hgrn_body(q_ref, f_ref, i_ref, g_ref, par_ref, nw_ref, tri_ref, mst_ref, mask_ref,
               fmask_ref, o_ref, st_ref, qs_ref, kk_ref, lf_ref, b_ref, *, nc):
    c = HG_CHUNK
    hd = HG_HEAD_DIM
    wb = HG_PAIR_W
    t = pl.program_id(2)

    @pl.when(t == 0)
    def _():
        st_ref[...] = jnp.zeros_like(st_ref)

    log_lb = par_ref[0:1, :]
    log_1m_lb = par_ref[1:2, :]
    one_m_lb = par_ref[2:3, :]
    nw = nw_ref[...]
    tri = tri_ref[...]
    row = lax.broadcasted_iota(jnp.int32, (c, wb), 0)

    wmin = None
    for ch in range(nc):
        rs = slice(ch * c, (ch + 1) * c)
        fr = f_ref[rs, :]
        e = jnp.exp(-jnp.abs(fr))
        r = 1.0 / (1.0 + e)
        sig_neg = jnp.where(fr >= 0, e * r, r)
        log_sig = jnp.minimum(fr, 0.0) - jnp.log(1.0 + e)
        y = log_1m_lb + log_sig
        log_f = jnp.maximum(log_lb, y) + jnp.log(1.0 + jnp.exp(-jnp.abs(log_lb - y)))
        hi = log_f.astype(BF16)
        lo = (log_f - hi.astype(F32)).astype(BF16)
        b = _dot(tri, hi) + _dot(tri, lo)
        qv = q_ref[rs, :]
        qs_ref[rs, :] = qv * _sigmoid(qv)
        kk_ref[rs, :] = one_m_lb * sig_neg
        lf_ref[rs, :] = log_f
        b_ref[rs, :] = b
        w = b - _block_starts(b)
        wmin = w if wmin is None else jnp.minimum(wmin, w)
    fast = jnp.min(wmin) >= HG_FAST_MIN_LOG_DECAY

    @pl.when(fast)
    def _():
        for ch in range(nc):
            rs = slice(ch * c, (ch + 1) * c)
            b = b_ref[rs, :]
            qs = qs_ref[rs, :]
            kk = kk_ref[rs, :]
            vv = i_ref[rs, :].astype(BF16)
            half = c // 2
            b63 = _rows(b[half - 1:half, :], half)
            zero_half = jnp.zeros((half, wb), BF16)
            q6 = jnp.concatenate(
                [zero_half, (qs[half:, :] * jnp.exp(b[half:, :] - b63)).astype(BF16)], axis=0)
            k6 = jnp.concatenate(
                [(kk[:half, :] * jnp.exp(b63 - b[:half, :])).astype(BF16), zero_half], axis=0)
            scores = lax.dot_general(q6, _block_diag(k6), _NT, preferred_element_type=F32)
            fb = HG_FAST_BLOCK
            bm = jnp.concatenate([_rows(b[j * half + fb - 1:j * half + fb, :], half)
                                  for j in range(c // half)], axis=0)
            a5 = jnp.exp(jnp.where((row % half) >= fb, b - bm, bm - b))
            w = b - _block_starts(b)
            for lev, (aq, ak) in enumerate([(a5, a5), (jnp.exp(w), jnp.exp(-w))]):
                sc = lax.dot_general((qs * aq).astype(BF16), _block_diag((kk * ak).astype(BF16)),
                                     _NT, preferred_element_type=F32)
                scores = scores + fmask_ref[lev] * sc
            st = st_ref[...]
            v_bd = _block_diag(vv)
            o = (_dot(scores.astype(BF16), v_bd)
                 + lax.dot_general((qs * jnp.exp(b)).astype(BF16), st.astype(BF16), _NT,
                                   preferred_element_type=F32))
            bend = b[c - 1:c, :]
            kd = (kk * jnp.exp(bend - b)).astype(BF16)
            st_ref[...] = st * jnp.exp(bend) + lax.dot_general(
                v_bd, _block_diag(kd), _TN, preferred_element_type=F32)
            _hgrn_finish(o, g_ref[rs, :], nw, o_ref, rs)

    @pl.when(jnp.logical_not(fast))
    def _():
        mst = mst_ref[...]

        def chunk(ch, carry):
            rs = pl.ds(pl.multiple_of(ch * c, c), c)
            log_f = lf_ref[rs, :]
            hi = log_f.astype(BF16)
            lo = (log_f - hi.astype(F32)).astype(BF16)
            ee = _dot(mst, hi) + _dot(mst, lo)
            qs = qs_ref[rs, :]
            kk = kk_ref[rs, :]
            vv = i_ref[rs, :].astype(BF16)
            bcum = ee[HG_LEVELS * c:(HG_LEVELS + 1) * c, :]
            outs = []
            for h in range(HG_PAIR):
                cs = slice(h * hd, (h + 1) * hd)
                qh = qs[:, cs]
                kh = kk[:, cs]
                vh = vv[:, cs]
                scores = mask_ref[HG_LEVELS] * jnp.sum(qh * kh, axis=-1, keepdims=True)
                for lev in range(HG_LEVELS):
                    a = jnp.exp(ee[lev * c:(lev + 1) * c, cs])
                    sc = lax.dot_general((qh * a).astype(BF16), (kh * a).astype(BF16), _NT,
                                         preferred_element_type=F32)
                    scores = scores + mask_ref[lev] * sc
                bh = bcum[:, cs]
                st = st_ref[cs, cs]
                outs.append(_dot(scores.astype(BF16), vh)
                            + lax.dot_general((qh * jnp.exp(bh)).astype(BF16), st.astype(BF16),
                                              _NT, preferred_element_type=F32))
                bend = bh[c - 1:c, :]
                kd = (kh * jnp.exp(bend - bh)).astype(BF16)
                st_ref[cs, cs] = st * jnp.exp(bend) + lax.dot_general(
                    vh, kd, _TN, preferred_element_type=F32)
            _hgrn_finish(jnp.concatenate(outs, axis=1), g_ref[rs, :], nw, o_ref, rs)
            return carry

        lax.fori_loop(0, nc, chunk, 0)


def _hgrn_mixer(proj3, lb, norm_w, tl):
    nb, seq, _ = proj3.shape
    nc = tl // HG_CHUNK
    wb = HG_PAIR_W
    q0 = (2 * SGU_WIDTH) // wb
    nblk = HG_WIDTH // wb
    lbf = lb.astype(F32)
    par = jnp.stack([jnp.log(lbf), jnp.log1p(-lbf), 1.0 - lbf])
    tb = _hgrn_tables()
    sec = lambda k: pl.BlockSpec((None, tl, wb), functools.partial(
        lambda b, h, t, k: (b, t, q0 + k * nblk + h), k=k))
    const = lambda a: pl.BlockSpec(a.shape, lambda b, h, t: (0,) * a.ndim)
    body = functools.partial(_hgrn_body, nc=nc)
    return pl.pallas_call(
        body,
        out_shape=jax.ShapeDtypeStruct((nb, seq, HG_WIDTH), BF16),
        grid=(nb, nblk, seq // tl),
        in_specs=[sec(0), sec(1), sec(2), sec(3),
                  pl.BlockSpec((3, wb), lambda b, h, t: (0, h)),
                  pl.BlockSpec((1, HG_HEAD_DIM), lambda b, h, t: (0, 0)),
                  const(tb["tri"]), const(tb["mst"]), const(tb["masks"]),
                  const(tb["fmasks"])],
        out_specs=pl.BlockSpec((None, tl, wb), lambda b, h, t: (b, t, h)),
        scratch_shapes=[pltpu.VMEM((wb, wb), F32)] + [pltpu.VMEM((tl, wb), F32)] * 4,
        compiler_params=_cparams(("parallel", "parallel", "arbitrary")),
        name="hgrn2_mixer")(
            proj3, proj3, proj3, proj3, par,
            norm_w.astype(F32).reshape(1, HG_HEAD_DIM),
            tb["tri"], tb["mst"], tb["masks"], tb["fmasks"])


def kernel(x, p, w_in, s5_lam_re, s5_lam_im, s5_log_step, s5_b_re, s5_b_im, s5_c_re, s5_c_im, s5_d, s5_w_glu, sgu_ln_w, sgu_ln_b, sgu_w, sgu_b, hg_lb_logits, hg_norm_w, norm_a_w, norm_b_w, w_out, ln1_w, ln1_b, w_ffn_in, w_ffn_out, ln2_w, ln2_b, w_ple_in, w_ple_gate, ln3_w, ln3_b):
    nb, seq, d_model = x.shape
    depth = w_in.shape[0]
    t_rows = nb * seq
    proj_w = w_in.shape[2]
    d_ff = w_ffn_out.shape[1]
    alpha = (2.0 * depth) ** 0.25

    tm = _pick(t_rows, (1024, 512, 256, 128))
    tm_wide = _pick(t_rows, (2048, 1024, 512, 256, 128))
    tm_half = _pick(t_rows, (512, 256, 128))
    tm_ln = _pick(t_rows, (512, 256, 128))
    tl_s5 = _pick(seq, (1024, 512, 256, 128))
    tl_sgu = _pick(seq, (1024, 512, 256, 128))
    tl_hg = _pick(seq, (2048, 1024, 512, 256, 128))
    tn_in = _pick(proj_w - S5_WIDTH, (1024, 512))

    lbs =jnp.cumsum(jax.nn.softmax(hg_lb_logits.astype(F32), axis=0), axis=0)
    lbs = lbs - lbs[0:1]

    x_rows = x.reshape(t_rows, d_model).astype(F32)
    res = _Residual(alpha, h=x_rows)
    p_rows = p.reshape(depth, t_rows, -1)

    w_in16 = _layer_to_bf16(w_in, 0)
    for l in range(depth):
        if l == 0:
            xa_tiles, h16 = _s5_in_proj(x_rows, w_in16, tm_half)
        else:
            lw, lb = ln3_w[l - 1], ln3_b[l - 1]
            xa_tiles, mu, rstd, h16 = _ln_fused_mm(
                xres, lw, lb, [(None, w_in16, d_model, 0, 0)], [],
                lambda accs, ex: [accs[0]], [F32], S5_WIDTH, tm, S5_WIDTH // 4, "in_proj_s5",
                emit_h16=True, out_tiles=True)
            res = _Residual(alpha, x=xres, mu=mu, rstd=rstd, w=lw, b=lb)
        xa_tiles = xa_tiles.reshape(-1, nb, seq, LANES)
        proj, wf, wo = _fused_mm(
            [(h16, w_in16, d_model, 0, S5_WIDTH // tn_in)], [],
            lambda accs, ex: [accs[0]], [F32], proj_w - S5_WIDTH, tm, tn_in, "in_proj",
            side_casts=[(w_ffn_in, l), (w_out, l)])
        proj3 = proj.reshape(nb, seq, proj_w - S5_WIDTH)
        prep = _s5_prepare(s5_lam_re[l], s5_lam_im[l], s5_log_step[l], s5_b_re[l],
                           s5_b_im[l], s5_c_re[l], s5_c_im[l], s5_d[l])
        z_tiles = _s5_mixer(xa_tiles, prep, tl_s5).reshape(-1, t_rows, LANES)
        ya = _s5_glu_norm(z_tiles, _layer_to_bf16(s5_w_glu, l), norm_a_w[l], tm_half)
        yb = _sgu_mixer(proj3, sgu_ln_w[l], sgu_ln_b[l], sgu_w[l], sgu_b[l],
                        norm_b_w[l], tl_sgu).reshape(t_rows, SGU_WIDTH)
        yc = _hgrn_mixer(proj3, lbs[l], hg_norm_w[l], tl_hg).reshape(t_rows, HG_WIDTH)
        (xres,) = _fused_mm(
            [(ya, wo, S5_WIDTH, 0, 0), (yb, wo, SGU_WIDTH, 1, 0), (yc, wo, HG_WIDTH, 1, 0)],
            res.extras,
            functools.partial(lambda accs, ex, res: [res(ex) + (accs[0] + accs[1] + accs[2])],
                              res=res),
            [F32], d_model, tm, 1024, "out_proj")

        hid, mu, rstd, wfo, wpg = _ln_fused_mm(
            xres, ln1_w[l], ln1_b[l],
            [(None, wf, d_model, 0, 0), (None, wf, d_model, 0, d_ff // 256)], [],
            lambda accs, ex: [accs[0] * _sigmoid(accs[0]) * accs[1]],
            [BF16], d_ff, tm_wide, 256, "ffn_in",
            side_casts=[(w_ffn_out, l), (w_ple_gate, l)])
        res = _Residual(alpha, x=xres, mu=mu, rstd=rstd, w=ln1_w[l], b=ln1_b[l])
        (xres,) = _fused_mm(
            [(hid, wfo, d_ff, 0, 0)], res.extras,
            functools.partial(lambda accs, ex, res: [res(ex) + accs[0]], res=res),
            [F32], d_model, tm_half, 1024, "ffn_out", cols_outer=True)

        p16 = _layer_to_bf16(p_rows, l)
        xres, _, _, *next_w_in = _ln_fused_mm(
            xres, ln2_w[l], ln2_b[l],
            [(None, wpg, d_model, 0, 0),
             (p16, _layer_to_bf16(w_ple_in, l), p16.shape[1], 0, 0)], [],
            lambda accs, ex, resid: [resid + accs[1] * _sigmoid(accs[0])],
            [F32], d_model, tm, 512, "ple", self_alpha=alpha,
            side_casts=[(w_in, l + 1)] if l + 1 < depth else [])
        if next_w_in:
            w_in16 = next_w_in[0]

    out = _layer_norm_final(xres, ln3_w[depth - 1], ln3_b[depth - 1], tm_ln)
    return out.reshape(nb, seq, d_model).astype(x.dtype)
```

```python
import functools
import math

import jax
import jax.numpy as jnp
from jax import lax
from jax.experimental import pallas as pl
from jax.experimental.pallas import tpu as pltpu

F32 = jnp.float32
BF16 = jnp.bfloat16

V7X_VMEM_LIMIT_BYTES = 56 * 1024 * 1024
LANES = 128

LN_EPS = 1e-5
RMS_EPS = 1e-6

S5_WIDTH = 1024
S5_GROUP_CH = 16
S5_STATE = 64
S5_CHUNK = 8
S5_TILE_GROUPS = LANES // S5_GROUP_CH
S5_TILE_STATE = S5_TILE_GROUPS * S5_STATE

SGU_WIDTH = 1024
SGU_CHUNK = 128
SGU_HEADS = 8

HG_WIDTH = 2048
HG_HEAD_DIM = 128
HG_CHUNK = 128
HG_LEVELS = 7


def _cparams(sem):
    return pltpu.CompilerParams(dimension_semantics=sem,
                                vmem_limit_bytes=V7X_VMEM_LIMIT_BYTES)


def _gelu_tanh(x):
    c = math.sqrt(2.0 / math.pi)
    return x * (0.5 + 0.5 * jnp.tanh(x * (c + (c * 0.044715) * (x * x))))


def _sigmoid(x):
    return 1.0 / (1.0 + jnp.exp(-x))


def _dot(a, b):
    return jnp.dot(a, b, preferred_element_type=F32)


def _pick(n, prefs):
    for p in prefs:
        if n % p == 0:
            return p
    return n


CAST_BLOCK_BYTES = 8 * 1024 * 1024


def _cast_body(w_ref, o_ref):
    o_ref[...] = w_ref[...].astype(BF16)


def _layer_to_bf16(w, layer):
    _, rows, cols = w.shape
    fits = [tr for tr in (8192, 4096, 2048, 1024, 512, 256, 128, 64, 32, 16)
            if rows % tr == 0 and tr * cols * 4 <= CAST_BLOCK_BYTES]
    tr = fits[0]
    return pl.pallas_call(
        _cast_body,
        out_shape=jax.ShapeDtypeStruct((rows, cols), BF16),
        grid=(rows // tr,),
        in_specs=[pl.BlockSpec((None, tr, cols), lambda i: (layer, i, 0))],
        out_specs=pl.BlockSpec((tr, cols), lambda i: (i, 0)),
        compiler_params=_cparams(("parallel",)), name="cast_bf16")(w)


MM_SUB_ROWS = 256


def _mm_body(*refs, a_of_dot, n_a, extra_kinds, n_side, epilogue, sub_rows):
    n_dot = len(a_of_dot)
    n_extra = len(extra_kinds)
    n_in = n_a + n_dot + n_extra
    a_refs = refs[:n_a]
    w_refs = refs[n_a:n_a + n_dot]
    extra = refs[n_a + n_dot:n_in]
    side_in = refs[n_in:n_in + n_side]
    out_refs = refs[n_in + n_side:len(refs) - n_side]
    side_out = refs[len(refs) - n_side:]
    for si, so in zip(side_in, side_out):
        so[...] = si[...].astype(BF16)
    rows = out_refs[0].shape[0]
    sub = min(rows, sub_rows)
    for r in range(rows // sub):
        rs = slice(r * sub, (r + 1) * sub)
        accs = [_dot(a_refs[ai][rs, :], w[...]) for ai, w in zip(a_of_dot, w_refs)]
        outs = epilogue(accs, [e[...] if kind == "row" else e[rs, :]
                               for e, kind in zip(extra, extra_kinds)])
        for o_ref, o in zip(out_refs, outs):
            o_ref[rs, :] = o.astype(o_ref.dtype)


BF16_ROW_TILE = 16


def _fused_mm(dots, extras, epilogue, out_dtypes, n_cols, tm, tn, name,
              sub_rows=MM_SUB_ROWS, side_casts=(), cols_outer=False):
    t_rows = dots[0][0].shape[0]
    ni, nj = t_rows // tm, n_cols // tn
    grid = (nj, ni) if cols_outer else (ni, nj)

    def spec(shape, index_map, **kw):
        if cols_outer:
            return pl.BlockSpec(shape, lambda g0, g1: index_map(g1, g0), **kw)
        return pl.BlockSpec(shape, index_map, **kw)

    w_mode = dict(pipeline_mode=pl.Buffered(1)) if cols_outer else {}
    in_specs, args = [], []
    a_of_dot = []
    for a, _, _, _, _ in dots:
        known = [k for k, seen in enumerate(args) if seen is a]
        if known:
            a_of_dot.append(known[0])
            continue
        a_of_dot.append(len(args))
        in_specs.append(spec((tm, a.shape[1]), lambda i, j: (i, 0)))
        args.append(a)
    n_a = len(args)
    for _, w, rb, ri, co in dots:
        in_specs.append(spec((rb, tn), functools.partial(
            lambda i, j, ri, co: (ri, j + co), ri=ri, co=co), **w_mode))
        args.append(w)
    for arr, kind in extras:
        if kind == "tile":
            in_specs.append(spec((tm, tn), lambda i, j: (i, j)))
        elif kind == "stat":
            in_specs.append(spec((tm, LANES), lambda i, j: (i, 0)))
        else:
            in_specs.append(spec((1, tn), lambda i, j: (0, j)))
        args.append(arr)
    out_shape = [jax.ShapeDtypeStruct((t_rows, n_cols), dt) for dt in out_dtypes]
    out_specs = [spec((tm, tn), lambda i, j: (i, j)) for _ in out_dtypes]
    n_steps = grid[0] * grid[1]
    for stack, layer in side_casts:
        _, rows, cols = stack.shape
        tiles = rows // BF16_ROW_TILE
        n_slabs = max(d for d in range(1, min(tiles, n_steps) + 1) if tiles % d == 0)
        slab = functools.partial(lambda g0, g1, last: jnp.minimum(g0 * grid[1] + g1, last),
                                 last=n_slabs - 1)
        in_specs.append(pl.BlockSpec(
            (None, rows // n_slabs, cols),
            functools.partial(lambda g0, g1, layer, slab: (layer, slab(g0, g1), 0),
                              layer=layer, slab=slab)))
        args.append(stack)
        out_shape.append(jax.ShapeDtypeStruct((rows, cols), BF16))
        out_specs.append(pl.BlockSpec(
            (rows // n_slabs, cols),
            functools.partial(lambda g0, g1, slab: (slab(g0, g1), 0), slab=slab)))
    body = functools.partial(_mm_body, a_of_dot=tuple(a_of_dot), n_a=n_a,
                             extra_kinds=tuple(kind for _, kind in extras),
                             n_side=len(side_casts), epilogue=epilogue, sub_rows=sub_rows)
    return pl.pallas_call(
        body, out_shape=out_shape, grid=grid, in_specs=in_specs, out_specs=out_specs,
        compiler_params=_cparams(("parallel", "arbitrary")), name=name)(*args)


def _ln_rows(x):
    mu = jnp.mean(x, axis=-1, keepdims=True)
    xc = x - mu
    var = jnp.mean(xc * xc, axis=-1, keepdims=True)
    return xc, mu, lax.rsqrt(var + LN_EPS)


LN_ROW_GROUP = 8


def _ln_final_body(x_ref, w_ref, b_ref, h_ref):
    w = w_ref[...]
    b = b_ref[...]
    for g in range(x_ref.shape[0] // LN_ROW_GROUP):
        rs = slice(g * LN_ROW_GROUP, (g + 1) * LN_ROW_GROUP)
        xc, _, rstd = _ln_rows(x_ref[rs, :])
        h_ref[rs, :] = xc * rstd * w + b


def _layer_norm_final(x, w, b, tm):
    t_rows, d = x.shape
    rows = pl.BlockSpec((tm, d), lambda i: (i, 0))
    vec = pl.BlockSpec((1, d), lambda i: (0, 0))
    return pl.pallas_call(
        _ln_final_body, out_shape=jax.ShapeDtypeStruct((t_rows, d), F32),
        grid=(t_rows // tm,), in_specs=[rows, vec, vec], out_specs=rows,
        compiler_params=_cparams(("parallel",)), name="layer_norm")(
            x, w.reshape(1, d), b.reshape(1, d))


def _ln_mm_body(*refs, a_of_dot, n_other_a, extra_kinds, n_side, n_out, emit_h16, out_tiles,
                self_alpha, epilogue, sub_rows, n_slab, slab_rows):
    n_dot = len(a_of_dot)
    n_extra = len(extra_kinds)
    x_ref, lnw_ref, lnb_ref = refs[:3]
    pos = 3
    other_a = refs[pos:pos + n_other_a]
    pos += n_other_a
    w_refs = refs[pos:pos + n_dot]
    pos += n_dot
    extra = refs[pos:pos + n_extra]
    pos += n_extra
    side_in = refs[pos:pos + n_side]
    pos += n_side
    out_refs = refs[pos:pos + n_out]
    pos += n_out
    mu_ref, rs_ref = refs[pos:pos + 2]
    pos += 2
    h16_ref = refs[pos] if emit_h16 else None
    pos += int(emit_h16)
    side_out = refs[pos:pos + n_side]
    pos += n_side
    a_slots, mu_slots, rs_slots = refs[pos:pos + 2], refs[pos + 2:pos + 4], refs[pos + 4:pos + 6]

    g = pl.program_id(0)
    j = pl.program_id(1)
    group = 2 * LN_ROW_GROUP

    def side_jobs(slot):
        a_sc, mu_sc, rs_sc = a_slots[slot], mu_slots[slot], rs_slots[slot]
        for si, so in zip(side_in, side_out):
            so[...] = si[...].astype(BF16)
        row0 = pl.multiple_of(jnp.minimum(j, n_slab - 1) * slab_rows, slab_rows)
        w = lnw_ref[...]
        b = lnb_ref[...]
        for grp in range(slab_rows // group):
            ys = []
            for half in range(2):
                r8 = slice(grp * group + half * LN_ROW_GROUP,
                           grp * group + (half + 1) * LN_ROW_GROUP)
                xc, mu, rstd = _ln_rows(x_ref[r8, :])
                ys.append(xc * rstd * w + b)
                mu_b = jnp.broadcast_to(mu, (LN_ROW_GROUP, LANES))
                rs_b = jnp.broadcast_to(rstd, (LN_ROW_GROUP, LANES))
                mu_ref[r8, :] = mu_b
                rs_ref[r8, :] = rs_b
                dst = pl.ds(row0 + grp * group + half * LN_ROW_GROUP, LN_ROW_GROUP)
                mu_sc[dst, :] = mu_b
                rs_sc[dst, :] = rs_b
            y16 = jnp.concatenate(ys, axis=0).astype(BF16)
            a_sc[pl.ds(row0 + grp * group, group), :] = y16
            if emit_h16:
                h16_ref[grp * group:(grp + 1) * group, :] = y16

    def matmul(slot):
        a_sc, mu_sc, rs_sc = a_slots[slot], mu_slots[slot], rs_slots[slot]
        rows = a_sc.shape[0]
        sub = min(rows, sub_rows)
        for r in range(rows // sub):
            rs = slice(r * sub, (r + 1) * sub)
            a_ln = a_sc[rs, :]
            accs = [_dot(a_ln if ai < 0 else other_a[ai][rs, :], w[...])
                    for ai, w in zip(a_of_dot, w_refs)]
            ex = [e[...] if kind == "row" else e[rs, :] for e, kind in zip(extra, extra_kinds)]
            if self_alpha is not None:
                x_tile, w_row, b_row = ex[:3]
                ex = ex[3:]
                rep = x_tile.shape[1] // LANES
                wide = lambda s: jnp.concatenate([s] * rep, axis=1)
                resid = self_alpha * ((x_tile - wide(mu_sc[rs, :]))
                                      * wide(rs_sc[rs, :]) * w_row + b_row)
                outs = epilogue(accs, ex, resid)
            else:
                outs = epilogue(accs, ex)
            for o_ref, o in zip(out_refs, outs):
                if out_tiles:
                    for c in range(o_ref.shape[0]):
                        o_ref[c, rs, :] = o[:, c * LANES:(c + 1) * LANES].astype(o_ref.dtype)
                else:
                    o_ref[rs, :] = o.astype(o_ref.dtype)

    @pl.when(g == 0)
    def _():
        side_jobs(0)

    for parity in (0, 1):
        @pl.when(jnp.logical_and(g > 0, lax.rem(g, 2) == parity))
        def _():
            side_jobs(parity)
            matmul(1 - parity)


def _ln_fused_mm(x, ln_w, ln_b, dots, extras, epilogue, out_dtypes, n_cols, tm, tn, name,
                 sub_rows=MM_SUB_ROWS, side_casts=(), emit_h16=False, out_tiles=False,
                 self_alpha=None):
    t_rows, d = x.shape
    ni, nj = t_rows // tm, n_cols // tn
    grid = (ni + 1, nj)
    n_slab = max(s for s in range(1, nj + 1)
                 if tm % s == 0 and (tm // s) % (2 * LN_ROW_GROUP) == 0)
    slab_rows = tm // n_slab
    prev = lambda g: jnp.maximum(g - 1, 0)
    first_col = lambda g, j: jnp.where(g > 0, j, 0)
    slab_idx = lambda g, j: (jnp.where(g < ni, g * n_slab + jnp.minimum(j, n_slab - 1),
                                       ni * n_slab - 1), 0)
    vec = lambda a: a.astype(F32).reshape(1, d)
    in_specs = [pl.BlockSpec((slab_rows, d), slab_idx),
                pl.BlockSpec((1, d), lambda g, j: (0, 0)),
                pl.BlockSpec((1, d), lambda g, j: (0, 0))]
    args = [x, vec(ln_w), vec(ln_b)]
    a_of_dot, other = [], []
    for a, _, _, _, _ in dots:
        if a is None:
            a_of_dot.append(-1)
            continue
        a_of_dot.append(len(other))
        other.append(a)
        in_specs.append(pl.BlockSpec((tm, a.shape[1]), lambda g, j: (prev(g), 0)))
        args.append(a)
    for _, w, rb, ri, co in dots:
        in_specs.append(pl.BlockSpec((rb, tn), functools.partial(
            lambda g, j, ri, co: (ri, first_col(g, j) + co), ri=ri, co=co)))
        args.append(w)
    extras = list(extras)
    if self_alpha is not None:
        extras = [(x, "tile"), (vec(ln_w), "row"), (vec(ln_b), "row")] + extras
    for arr, kind in extras:
        if kind == "tile":
            in_specs.append(pl.BlockSpec((tm, tn), lambda g, j: (prev(g), first_col(g, j))))
        elif kind == "stat":
            in_specs.append(pl.BlockSpec((tm, LANES), lambda g, j: (prev(g), 0)))
        else:
            in_specs.append(pl.BlockSpec((1, tn), lambda g, j: (0, first_col(g, j))))
        args.append(arr)
    if out_tiles:
        n_tiles = n_cols // LANES
        out_shape = [jax.ShapeDtypeStruct((n_tiles, t_rows, LANES), dt) for dt in out_dtypes]
        out_specs = [pl.BlockSpec((tn // LANES, tm, LANES),
                                  lambda g, j: (first_col(g, j), prev(g), 0))
                     for _ in out_dtypes]
    else:
        out_shape = [jax.ShapeDtypeStruct((t_rows, n_cols), dt) for dt in out_dtypes]
        out_specs = [pl.BlockSpec((tm, tn), lambda g, j: (prev(g), first_col(g, j)))
                     for _ in out_dtypes]
    n_out = len(out_shape)
    for _ in range(2):
        out_shape.append(jax.ShapeDtypeStruct((t_rows, LANES), F32))
        out_specs.append(pl.BlockSpec((slab_rows, LANES), slab_idx))
    if emit_h16:
        out_shape.append(jax.ShapeDtypeStruct((t_rows, d), BF16))
        out_specs.append(pl.BlockSpec((slab_rows, d), slab_idx))
    n_steps = grid[0] * grid[1]
    for stack, layer in side_casts:
        _, rows, cols = stack.shape
        tiles = rows // BF16_ROW_TILE
        n_cast = max(s for s in range(1, min(tiles, n_steps) + 1) if tiles % s == 0)
        slab = functools.partial(lambda g, j, last: jnp.minimum(g * nj + j, last),
                                 last=n_cast - 1)
        in_specs.append(pl.BlockSpec(
            (None, rows // n_cast, cols),
            functools.partial(lambda g, j, layer, slab: (layer, slab(g, j), 0),
                              layer=layer, slab=slab)))
        args.append(stack)
        out_shape.append(jax.ShapeDtypeStruct((rows, cols), BF16))
        out_specs.append(pl.BlockSpec(
            (rows // n_cast, cols),
            functools.partial(lambda g, j, slab: (slab(g, j), 0), slab=slab)))
    body = functools.partial(
        _ln_mm_body, a_of_dot=tuple(a_of_dot), n_other_a=len(other),
        extra_kinds=tuple(kind for _, kind in extras), n_side=len(side_casts), n_out=n_out,
        emit_h16=emit_h16, out_tiles=out_tiles, self_alpha=self_alpha, epilogue=epilogue,
        sub_rows=sub_rows, n_slab=n_slab, slab_rows=slab_rows)
    return pl.pallas_call(
        body, out_shape=out_shape, grid=grid, in_specs=in_specs, out_specs=out_specs,
        scratch_shapes=([pltpu.VMEM((tm, d), BF16)] * 2
                        + [pltpu.VMEM((tm, LANES), F32)] * 4),
        compiler_params=_cparams(("arbitrary", "arbitrary")), name=name)(*args)


class _Residual:
    def __init__(self, alpha, h=None, x=None, mu=None, rstd=None, w=None, b=None):
        self.alpha = alpha
        if h is not None:
            self.extras = [(h, "tile")]
        else:
            self.extras = [(x, "tile"), (mu, "stat"), (rstd, "stat"),
                           (w.astype(F32).reshape(1, -1), "row"),
                           (b.astype(F32).reshape(1, -1), "row")]

    def __call__(self, ex):
        if len(ex) == 1:
            return self.alpha * ex[0]
        x, mu, rstd, w, b = ex
        rep = x.shape[1] // LANES
        wide = lambda s: jnp.concatenate([s] * rep, axis=1)
        return self.alpha * ((x - wide(mu)) * wide(rstd) * w + b)


def _s5_prepare(lam_re, lam_im, log_step, b_re, b_im, c_re, c_im, d):
    hp = lax.Precision.HIGHEST
    n_tiles = S5_WIDTH // LANES
    lr = jnp.minimum(lam_re.astype(F32), -1e-4)
    li = lam_im.astype(F32)
    dt = jnp.exp(log_step.astype(F32))[:, None]
    mag = jnp.exp(lr * dt)
    ab_re = mag * jnp.cos(li * dt)
    ab_im = mag * jnp.sin(li * dt)
    den = lr * lr + li * li
    nr = ab_re - 1.0
    g_re = (nr * lr + ab_im * li) / den
    g_im = (ab_im * lr - nr * li) / den
    br = b_re.astype(F32)
    bi = b_im.astype(F32)
    bb_re = g_re[..., None] * br - g_im[..., None] * bi
    bb_im = g_re[..., None] * bi + g_im[..., None] * br
    pr, pi = [jnp.ones_like(ab_re)], [jnp.zeros_like(ab_re)]
    for _ in range(S5_CHUNK):
        pr_new = pr[-1] * ab_re - pi[-1] * ab_im
        pi_new = pr[-1] * ab_im + pi[-1] * ab_re
        pr.append(pr_new)
        pi.append(pi_new)
    p_re = jnp.stack(pr)
    p_im = jnp.stack(pi)
    pb_re = (p_re[:S5_CHUNK, :, :, None] * bb_re[None]
             - p_im[:S5_CHUNK, :, :, None] * bb_im[None])
    pb_im = (p_re[:S5_CHUNK, :, :, None] * bb_im[None]
             + p_im[:S5_CHUNK, :, :, None] * bb_re[None])
    cr = c_re.astype(F32)
    ci = c_im.astype(F32)
    kern = (jnp.einsum("gon,jgni->jgoi", cr, pb_re, precision=hp)
            - jnp.einsum("gon,jgni->jgoi", ci, pb_im, precision=hp))
    tg, gc, ns = S5_TILE_GROUPS, S5_GROUP_CH, S5_STATE

    def spread(x2d, rep, row_group, col_group):
        rows, width = x2d.shape
        sel = jnp.tile(jnp.eye(width, dtype=BF16), (1, rep))
        out = jnp.dot(x2d.astype(BF16), sel)
        rg = row_group(jnp.arange(rows))[:, None]
        cg = col_group(jnp.arange(width * rep))[None, :]
        return jnp.where(rg == cg, out, jnp.zeros_like(out))

    def spread_steps(x2d, row_group):
        rows, width = x2d.shape
        cols = jnp.arange(S5_CHUNK * LANES)
        src = jnp.arange(width)
        sel = ((src[:, None] // gc == cols[None, :] // LANES)
               & (src[:, None] % gc == cols[None, :] % gc)).astype(BF16)
        out = jnp.dot(x2d.astype(BF16), sel)
        rg = row_group(jnp.arange(rows))[:, None]
        cg = ((cols // gc) % tg)[None, :]
        return jnp.where(rg == cg, out, jnp.zeros_like(out))

    k5 = kern.reshape(S5_CHUNK, n_tiles, tg, gc, gc).transpose(1, 0, 2, 4, 3)
    kblk = spread(k5.reshape(-1, gc), tg, lambda r: (r // gc) % tg,
                  lambda c: c // gc).reshape(n_tiles, S5_CHUNK, LANES, LANES)
    zero = jnp.zeros((n_tiles, LANES, LANES), BF16)
    toep = jnp.concatenate(
        [jnp.concatenate([zero] * t + [kblk[:, j] for j in range(S5_CHUNK - t)], axis=-1)
         for t in range(S5_CHUNK)], axis=-2)

    def w1_half(pb):
        pb5 = pb[::-1].reshape(S5_CHUNK, n_tiles, tg, ns, gc).transpose(1, 0, 2, 4, 3)
        return spread(pb5.reshape(-1, ns), tg, lambda r: (r // gc) % tg,
                      lambda c: c // ns).reshape(n_tiles, S5_CHUNK * LANES, S5_TILE_STATE)
    w1 = jnp.concatenate([w1_half(pb_re), w1_half(pb_im)], axis=-1)

    ca_re = cr[None] * p_re[1:, :, None, :] - ci[None] * p_im[1:, :, None, :]
    ca_im = cr[None] * p_im[1:, :, None, :] + ci[None] * p_re[1:, :, None, :]

    def p_half(ca):
        ca5 = ca.reshape(S5_CHUNK, n_tiles, tg, gc, ns).transpose(1, 2, 4, 0, 3)
        return spread_steps(ca5.reshape(-1, S5_CHUNK * gc), lambda r: (r // ns) % tg).reshape(
            n_tiles, S5_TILE_STATE, S5_CHUNK * LANES)
    pcat = jnp.concatenate([p_half(ca_re), -p_half(ca_im)], axis=1)

    a16 = jnp.concatenate([p_re[S5_CHUNK].reshape(n_tiles, 1, S5_TILE_STATE),
                           p_im[S5_CHUNK].reshape(n_tiles, 1, S5_TILE_STATE)], axis=-1)
    dd = d.astype(F32).reshape(n_tiles, 1, LANES)
    return w1, toep, pcat, a16, dd


def _s5_body(x_ref, w1_ref, toep_ref, pcat_ref, a16_ref, d_ref, z_ref,
             state_ref, xr_ref, q_ref, yi_ref, sin_ref, yc_ref, yint_ref, *, nb, tl):
    t = pl.program_id(1)
    nk = tl // S5_CHUNK
    n = nb * tl
    ts = S5_TILE_STATE

    @pl.when(t == 0)
    def _():
        state_ref[...] = jnp.zeros_like(state_ref)

    for b in range(nb):
        for tp in range(S5_CHUNK):
            xr_ref[tp, pl.ds(b, nk, stride=nb), :] = x_ref[b, pl.ds(tp, nk, stride=S5_CHUNK), :]
    xr = jnp.concatenate([xr_ref[tp].astype(BF16) for tp in range(S5_CHUNK)], axis=1)
    q_ref[...] = _dot(xr, w1_ref[...])
    yi_ref[...] = _dot(xr, toep_ref[...])

    a_re = a16_ref[:, :ts]
    a_im = a16_ref[:, ts:]
    s = state_ref[...]
    for k in range(nk):
        sin_ref[k * nb:(k + 1) * nb, :] = s
        s_re = s[:, :ts]
        s_im = s[:, ts:]
        s = jnp.concatenate([a_re * s_re - a_im * s_im, a_re * s_im + a_im * s_re],
                            axis=1) + q_ref[k * nb:(k + 1) * nb, :]
    state_ref[...] = s

    yc = yi_ref[...] + _dot(sin_ref[...].astype(BF16), pcat_ref[...])
    for tp in range(S5_CHUNK):
        yc_ref[tp] = yc[:, tp * LANES:(tp + 1) * LANES]
    for b in range(nb):
        for tp in range(S5_CHUNK):
            yint_ref[pl.ds(b * tl + tp, nk, stride=S5_CHUNK), :] = (
                yc_ref[tp, pl.ds(b, nk, stride=nb), :])

    x = x_ref[...].reshape(n, LANES)
    z_ref[...] = _gelu_tanh(yint_ref[...] + d_ref[...] * x).reshape(nb, tl, LANES)


def _s5_in_proj_body(a_ref, w_ref, o_ref, a16_ref):
    rows = a_ref.shape[0]
    sub = min(rows, MM_SUB_ROWS)
    for r in range(rows // sub):
        rs = slice(r * sub, (r + 1) * sub)
        a16 = a_ref[rs, :].astype(BF16)
        a16_ref[rs, :] = a16
        acc = _dot(a16, w_ref[...])
        for c in range(o_ref.shape[0]):
            o_ref[c, rs, :] = acc[:, c * LANES:(c + 1) * LANES]


def _s5_in_proj(h, w_in16, tm):
    t_rows, d = h.shape
    n_tiles = S5_WIDTH // LANES
    return pl.pallas_call(
        _s5_in_proj_body,
        out_shape=[jax.ShapeDtypeStruct((n_tiles, t_rows, LANES), F32),
                   jax.ShapeDtypeStruct((t_rows, d), BF16)],
        grid=(t_rows // tm,),
        in_specs=[pl.BlockSpec((tm, d), lambda i: (i, 0)),
                  pl.BlockSpec((d, S5_WIDTH), lambda i: (0, 0))],
        out_specs=[pl.BlockSpec((n_tiles, tm, LANES), lambda i: (0, i, 0)),
                   pl.BlockSpec((tm, d), lambda i: (i, 0))],
        compiler_params=_cparams(("parallel",)), name="in_proj_s5")(h, w_in16)


def _s5_mixer(xa_tiles, prep, tl):
    w1, toep, pcat, a16, dd = prep
    n_tiles, nb, seq, _ = xa_tiles.shape
    nk = tl // S5_CHUNK
    ts = S5_TILE_STATE
    steps = S5_CHUNK * LANES
    body = functools.partial(_s5_body, nb=nb, tl=tl)
    wspec = lambda shp: pl.BlockSpec((None,) + shp, lambda j, t: (j, 0, 0))
    xspec = pl.BlockSpec((None, nb, tl, LANES), lambda j, t: (j, 0, t, 0))
    return pl.pallas_call(
        body,
        out_shape=jax.ShapeDtypeStruct(xa_tiles.shape, F32),
        grid=(n_tiles, seq // tl),
        in_specs=[xspec,
                  wspec((steps, 2 * ts)),
                  wspec((steps, steps)),
                  wspec((2 * ts, steps)),
                  wspec((1, 2 * ts)),
                  wspec((1, LANES))],
        out_specs=xspec,
        scratch_shapes=[pltpu.VMEM((nb, 2 * ts), F32),
                        pltpu.VMEM((S5_CHUNK, nb * nk, LANES), F32),
                        pltpu.VMEM((nb * nk, 2 * ts), F32),
                        pltpu.VMEM((nb * nk, steps), F32),
                        pltpu.VMEM((nb * nk, 2 * ts), F32),
                        pltpu.VMEM((S5_CHUNK, nb * nk, LANES), F32),
                        pltpu.VMEM((nb * tl, LANES), F32)],
        compiler_params=_cparams(("parallel", "arbitrary")), name="s5_mixer")(
            xa_tiles, w1, toep, pcat, a16, dd)


def _glu_body(z_ref, w_ref, nw_ref, o_ref):
    z = jnp.concatenate([z_ref[c] for c in range(z_ref.shape[0])], axis=1)
    y = z * _sigmoid(_dot(z.astype(BF16), w_ref[...]))
    ms = jnp.mean(y * y, axis=-1, keepdims=True)
    o_ref[...] = (y * lax.rsqrt(ms + RMS_EPS) * nw_ref[...]).astype(BF16)


def _s5_glu_norm(z_tiles, w_glu, norm_w, tm):
    n_tiles, t_rows, _ = z_tiles.shape
    d = n_tiles * LANES
    return pl.pallas_call(
        _glu_body,
        out_shape=jax.ShapeDtypeStruct((t_rows, d), BF16),
        grid=(t_rows // tm,),
        in_specs=[pl.BlockSpec((n_tiles, tm, LANES), lambda i: (0, i, 0)),
                  pl.BlockSpec((d, d), lambda i: (0, 0)),
                  pl.BlockSpec((1, d), lambda i: (0, 0))],
        out_specs=pl.BlockSpec((tm, d), lambda i: (i, 0)),
        compiler_params=_cparams(("parallel",)), name="s5_glu_norm")(
            z_tiles, w_glu, norm_w.reshape(1, d))


def _sgu_body(u_ref, v_ref, lnw_ref, lnb_ref, w_ref, bs_ref, nw_ref, o_ref,
              vb_ref, x_ref, *, nc):
    v = _gelu_tanh(v_ref[...])
    mu = jnp.mean(v, axis=-1, keepdims=True)
    vc = v - mu
    var = jnp.mean(vc * vc, axis=-1, keepdims=True)
    vb_ref[...] = (vc * lax.rsqrt(var + LN_EPS) * lnw_ref[...] + lnb_ref[...]).astype(BF16)
    hd = SGU_WIDTH // SGU_HEADS
    for h in range(SGU_HEADS):
        cs = slice(h * hd, (h + 1) * hd)
        rhs = jnp.concatenate(
            [vb_ref[c * SGU_CHUNK:(c + 1) * SGU_CHUNK, cs] for c in range(nc)], axis=1)
        zz = _dot(w_ref[h], rhs)
        for c in range(nc):
            rs = slice(c * SGU_CHUNK, (c + 1) * SGU_CHUNK)
            z = zz[:, c * hd:(c + 1) * hd] + bs_ref[:, cs]
            x_ref[rs, cs] = _gelu_tanh(u_ref[rs, cs]) * z
    x = x_ref[...]
    ms = jnp.mean(x * x, axis=-1, keepdims=True)
    o_ref[...] = (x * lax.rsqrt(ms + RMS_EPS) * nw_ref[...]).astype(BF16)


def _sgu_mixer(proj3, ln_w, ln_b, w_s, b_s, norm_w, tl):
    nb, seq, _ = proj3.shape
    nc = tl // SGU_CHUNK
    hd = SGU_WIDTH // SGU_HEADS
    causal = jnp.tril(jnp.ones((SGU_CHUNK, SGU_CHUNK), dtype=bool))
    w_causal = jnp.where(causal[None], w_s, jnp.zeros_like(w_s)).astype(BF16)
    bias = jnp.repeat(b_s.astype(F32).T, hd, axis=1)
    u_blk = 0
    row = lambda a: a.astype(F32).reshape(1, SGU_WIDTH)
    const = lambda shp: pl.BlockSpec(shp, lambda b, t: (0,) * len(shp))
    body = functools.partial(_sgu_body, nc=nc)
    return pl.pallas_call(
        body,
        out_shape=jax.ShapeDtypeStruct((nb, seq, SGU_WIDTH), BF16),
        grid=(nb, seq // tl),
        in_specs=[pl.BlockSpec((None, tl, SGU_WIDTH), lambda b, t: (b, t, u_blk)),
                  pl.BlockSpec((None, tl, SGU_WIDTH), lambda b, t: (b, t, u_blk + 1)),
                  const((1, SGU_WIDTH)), const((1, SGU_WIDTH)),
                  const((SGU_HEADS, SGU_CHUNK, SGU_CHUNK)),
                  const((SGU_CHUNK, SGU_WIDTH)), const((1, SGU_WIDTH))],
        out_specs=pl.BlockSpec((None, tl, SGU_WIDTH), lambda b, t: (b, t, 0)),
        scratch_shapes=[pltpu.VMEM((tl, SGU_WIDTH), BF16),
                        pltpu.VMEM((tl, SGU_WIDTH), F32)],
        compiler_params=_cparams(("parallel", "parallel")), name="sgu_mixer")(
            proj3, proj3, row(ln_w), row(ln_b), w_causal, bias, row(norm_w))


HG_PAIR = 2
HG_PAIR_W = HG_PAIR * HG_HEAD_DIM
HG_FAST_BLOCK = 32
HG_FAST_MIN_LOG_DECAY = -60.0
_NT = (((1,), (1,)), ((), ()))
_TN = (((0,), (0,)), ((), ()))


def _hgrn_tables():
    c = HG_CHUNK
    t = jnp.arange(c)[:, None]
    r = jnp.arange(c)[None, :]
    mats, masks = [], []
    for lev in range(HG_LEVELS):
        m = 1 << lev
        mid = (t // (2 * m)) * (2 * m) + m
        later = t >= mid
        mats.append(jnp.where(later, (r >= mid) & (r <= t), (r > t) & (r < mid)))
        same = (t // (2 * m)) == (r // (2 * m))
        masks.append(same & later & (r < mid))
    tri = r <= t
    mats.append(tri)
    masks.append(t == r)
    fb = HG_FAST_BLOCK
    fast = [masks[5], ((t // fb) == (r // fb)) & tri]
    pair = lambda m: jnp.tile(m.astype(F32), (1, HG_PAIR))
    return dict(
        tri=tri.astype(BF16),
        mst=jnp.concatenate(mats, axis=0).astype(BF16),
        masks=jnp.stack(masks).astype(F32),
        fmasks=jnp.stack([pair(m) for m in fast]))


def _block_diag(x):
    z = jnp.zeros((x.shape[0], HG_HEAD_DIM), x.dtype)
    top = jnp.concatenate([x[:, :HG_HEAD_DIM], z], axis=1)
    bot = jnp.concatenate([z, x[:, HG_HEAD_DIM:]], axis=1)
    return jnp.concatenate([top, bot], axis=0)


def _rows(v, n):
    return jnp.broadcast_to(v, (n, v.shape[1]))


def _hgrn_finish(o, gv, nw, o_ref, rs):
    for h in range(HG_PAIR):
        cs = slice(h * HG_HEAD_DIM, (h + 1) * HG_HEAD_DIM)
        oh = o[:, cs]
        ms = jnp.mean(oh * oh, axis=-1, keepdims=True)
        gg = gv[:, cs]
        o_ref[rs, cs] = (oh * lax.rsqrt(ms + RMS_EPS) * nw * (gg * _sigmoid(gg))).astype(BF16)


def _hgrn_body(q_ref, f_ref, i_ref, g_ref, par_ref, nw_ref, tri_ref, mst_ref, mask_ref,
               fmask_ref, o_ref, st_ref, qs_ref, kk_ref, lf_ref, b_ref, *, nc):
    c = HG_CHUNK
    hd = HG_HEAD_DIM
    wb = HG_PAIR_W
    t = pl.program_id(2)

    @pl.when(t == 0)
    def _():
        st_ref[...] = jnp.zeros_like(st_ref)

    log_lb = par_ref[0:1, :]
    log_1m_lb = par_ref[1:2, :]
    one_m_lb = par_ref[2:3, :]
    nw = nw_ref[...]
    tri = tri_ref[...]
    row = lax.broadcasted_iota(jnp.int32, (c, wb), 0)

    wmin = None
    for ch in range(nc):
        rs = slice(ch * c, (ch + 1) * c)
        fr = f_ref[rs, :]
        e = jnp.exp(-jnp.abs(fr))
        r = 1.0 / (1.0 + e)
        sig_neg = jnp.where(fr >= 0, e * r, r)
        log_sig = jnp.minimum(fr, 0.0) - jnp.log(1.0 + e)
        y = log_1m_lb + log_sig
        log_f = jnp.maximum(log_lb, y) + jnp.log(1.0 + jnp.exp(-jnp.abs(log_lb - y)))
        hi = log_f.astype(BF16)
        lo = (log_f - hi.astype(F32)).astype(BF16)
        b = _dot(tri, hi) + _dot(tri, lo)
        qv = q_ref[rs, :]
        qs_ref[rs, :] = qv * _sigmoid(qv)
        kk_ref[rs, :] = one_m_lb * sig_neg
        lf_ref[rs, :] = log_f
        b_ref[rs, :] = b
        fb = HG_FAST_BLOCK
        starts = jnp.concatenate(
            [jnp.zeros((fb, wb), F32)]
            + [_rows(b[j * fb - 1:j * fb, :], fb) for j in range(1, c // fb)], axis=0)
        w = b - starts
        wmin = w if wmin is None else jnp.minimum(wmin, w)
    fast = jnp.min(wmin) >= HG_FAST_MIN_LOG_DECAY

    @pl.when(fast)
    def _():
        for ch in range(nc):
            rs = slice(ch * c, (ch + 1) * c)
            b = b_ref[rs, :]
            qs = qs_ref[rs, :]
            kk = kk_ref[rs, :]
            vv = i_ref[rs, :].astype(BF16)
            half = c // 2
            b63 = _rows(b[half - 1:half, :], half)
            zero_half = jnp.zeros((half, wb), BF16)
            q6 = jnp.concatenate(
                [zero_half, (qs[half:, :] * jnp.exp(b[half:, :] - b63)).astype(BF16)], axis=0)
            k6 = jnp.concatenate(
                [(kk[:half, :] * jnp.exp(b63 - b[:half, :])).astype(BF16), zero_half], axis=0)
            scores = lax.dot_general(q6, _block_diag(k6), _NT, preferred_element_type=F32)
            bm = jnp.concatenate([_rows(b[31:32, :], 64), _rows(b[95:96, :], 64)], axis=0)
            a5 = jnp.exp(jnp.where((row % 64) >= 32, b - bm, bm - b))
            fb = HG_FAST_BLOCK
            starts = jnp.concatenate(
                [jnp.zeros((fb, wb), F32)]
                + [_rows(b[j * fb - 1:j * fb, :], fb) for j in range(1, c // fb)], axis=0)
            w = b - starts
            for lev, (aq, ak) in enumerate([(a5, a5), (jnp.exp(w), jnp.exp(-w))]):
                sc = lax.dot_general((qs * aq).astype(BF16), _block_diag((kk * ak).astype(BF16)),
                                     _NT, preferred_element_type=F32)
                scores = scores + fmask_ref[lev] * sc
            st = st_ref[...]
            v_bd = _block_diag(vv)
            o = (_dot(scores.astype(BF16), v_bd)
                 + lax.dot_general((qs * jnp.exp(b)).astype(BF16), st.astype(BF16), _NT,
                                   preferred_element_type=F32))
            bend = b[c - 1:c, :]
            kd = (kk * jnp.exp(bend - b)).astype(BF16)
            st_ref[...] = st * jnp.exp(bend) + lax.dot_general(
                v_bd, _block_diag(kd), _TN, preferred_element_type=F32)
            _hgrn_finish(o, g_ref[rs, :], nw, o_ref, rs)

    @pl.when(jnp.logical_not(fast))
    def _():
        mst = mst_ref[...]

        def chunk(ch, carry):
            rs = pl.ds(pl.multiple_of(ch * c, c), c)
            log_f = lf_ref[rs, :]
            hi = log_f.astype(BF16)
            lo = (log_f - hi.astype(F32)).astype(BF16)
            ee = _dot(mst, hi) + _dot(mst, lo)
            qs = qs_ref[rs, :]
            kk = kk_ref[rs, :]
            vv = i_ref[rs, :].astype(BF16)
            bcum = ee[HG_LEVELS * c:(HG_LEVELS + 1) * c, :]
            outs = []
            for h in range(HG_PAIR):
                cs = slice(h * hd, (h + 1) * hd)
                qh = qs[:, cs]
                kh = kk[:, cs]
                vh = vv[:, cs]
                scores = mask_ref[HG_LEVELS] * jnp.sum(qh * kh, axis=-1, keepdims=True)
                for lev in range(HG_LEVELS):
                    a = jnp.exp(ee[lev * c:(lev + 1) * c, cs])
                    sc = lax.dot_general((qh * a).astype(BF16), (kh * a).astype(BF16), _NT,
                                         preferred_element_type=F32)
                    scores = scores + mask_ref[lev] * sc
                bh = bcum[:, cs]
                st = st_ref[cs, cs]
                outs.append(_dot(scores.astype(BF16), vh)
                            + lax.dot_general((qh * jnp.exp(bh)).astype(BF16), st.astype(BF16),
                                              _NT, preferred_element_type=F32))
                bend = bh[c - 1:c, :]
                kd = (kh * jnp.exp(bend - bh)).astype(BF16)
                st_ref[cs, cs] = st * jnp.exp(bend) + lax.dot_general(
                    vh, kd, _TN, preferred_element_type=F32)
            _hgrn_finish(jnp.concatenate(outs, axis=1), g_ref[rs, :], nw, o_ref, rs)
            return carry

        lax.fori_loop(0, nc, chunk, 0)


def _hgrn_mixer(proj3, lb, norm_w, tl):
    nb, seq, _ = proj3.shape
    nc = tl // HG_CHUNK
    wb = HG_PAIR_W
    q0 = (2 * SGU_WIDTH) // wb
    nblk = HG_WIDTH // wb
    lbf = lb.astype(F32)
    par = jnp.stack([jnp.log(lbf), jnp.log1p(-lbf), 1.0 - lbf])
    tb = _hgrn_tables()
    sec = lambda k: pl.BlockSpec((None, tl, wb), functools.partial(
        lambda b, h, t, k: (b, t, q0 + k * nblk + h), k=k))
    const = lambda a: pl.BlockSpec(a.shape, lambda b, h, t: (0,) * a.ndim)
    body = functools.partial(_hgrn_body, nc=nc)
    return pl.pallas_call(
        body,
        out_shape=jax.ShapeDtypeStruct((nb, seq, HG_WIDTH), BF16),
        grid=(nb, nblk, seq // tl),
        in_specs=[sec(0), sec(1), sec(2), sec(3),
                  pl.BlockSpec((3, wb), lambda b, h, t: (0, h)),
                  pl.BlockSpec((1, HG_HEAD_DIM), lambda b, h, t: (0, 0)),
                  const(tb["tri"]), const(tb["mst"]), const(tb["masks"]),
                  const(tb["fmasks"])],
        out_specs=pl.BlockSpec((None, tl, wb), lambda b, h, t: (b, t, h)),
        scratch_shapes=[pltpu.VMEM((wb, wb), F32)] + [pltpu.VMEM((tl, wb), F32)] * 4,
        compiler_params=_cparams(("parallel", "parallel", "arbitrary")),
        name="hgrn2_mixer")(
            proj3, proj3, proj3, proj3, par,
            norm_w.astype(F32).reshape(1, HG_HEAD_DIM),
            tb["tri"], tb["mst"], tb["masks"], tb["fmasks"])


def kernel(x, p, w_in, s5_lam_re, s5_lam_im, s5_log_step, s5_b_re, s5_b_im, s5_c_re, s5_c_im, s5_d, s5_w_glu, sgu_ln_w, sgu_ln_b, sgu_w, sgu_b, hg_lb_logits, hg_norm_w, norm_a_w, norm_b_w, w_out, ln1_w, ln1_b, w_ffn_in, w_ffn_out, ln2_w, ln2_b, w_ple_in, w_ple_gate, ln3_w, ln3_b):
    nb, seq, d_model = x.shape
    depth = w_in.shape[0]
    t_rows = nb * seq
    proj_w = w_in.shape[2]
    d_ff = w_ffn_out.shape[1]
    alpha = (2.0 * depth) ** 0.25

    tm = _pick(t_rows, (1024, 512, 256, 128))
    tm_wide = _pick(t_rows, (2048, 1024, 512, 256, 128))
    tm_half = _pick(t_rows, (512, 256, 128))
    tm_ln = _pick(t_rows, (512, 256, 128))
    tl_s5 = _pick(seq, (1024, 512, 256, 128))
    tl_sgu = _pick(seq, (1024, 512, 256, 128))
    tl_hg = _pick(seq, (2048, 1024, 512, 256, 128))
    tn_in = _pick(proj_w - S5_WIDTH, (1024, 512))

    lbs =jnp.cumsum(jax.nn.softmax(hg_lb_logits.astype(F32), axis=0), axis=0)
    lbs = lbs - lbs[0:1]

    x_rows = x.reshape(t_rows, d_model).astype(F32)
    res = _Residual(alpha, h=x_rows)
    p_rows = p.reshape(depth, t_rows, -1)

    w_in16 = _layer_to_bf16(w_in, 0)
    for l in range(depth):
        if l == 0:
            xa_tiles, h16 = _s5_in_proj(x_rows, w_in16, tm_half)
        else:
            lw, lb = ln3_w[l - 1], ln3_b[l - 1]
            xa_tiles, mu, rstd, h16 = _ln_fused_mm(
                xres, lw, lb, [(None, w_in16, d_model, 0, 0)], [],
                lambda accs, ex: [accs[0]], [F32], S5_WIDTH, tm, S5_WIDTH // 4, "in_proj_s5",
                emit_h16=True, out_tiles=True)
            res = _Residual(alpha, x=xres, mu=mu, rstd=rstd, w=lw, b=lb)
        xa_tiles = xa_tiles.reshape(-1, nb, seq, LANES)
        proj, wf, wo = _fused_mm(
            [(h16, w_in16, d_model, 0, S5_WIDTH // tn_in)], [],
            lambda accs, ex: [accs[0]], [F32], proj_w - S5_WIDTH, tm, tn_in, "in_proj",
            side_casts=[(w_ffn_in, l), (w_out, l)])
        proj3 = proj.reshape(nb, seq, proj_w - S5_WIDTH)
        prep = _s5_prepare(s5_lam_re[l], s5_lam_im[l], s5_log_step[l], s5_b_re[l],
                           s5_b_im[l], s5_c_re[l], s5_c_im[l], s5_d[l])
        z_tiles = _s5_mixer(xa_tiles, prep, tl_s5).reshape(-1, t_rows, LANES)
        ya = _s5_glu_norm(z_tiles, _layer_to_bf16(s5_w_glu, l), norm_a_w[l], tm_half)
        yb = _sgu_mixer(proj3, sgu_ln_w[l], sgu_ln_b[l], sgu_w[l], sgu_b[l],
                        norm_b_w[l], tl_sgu).reshape(t_rows, SGU_WIDTH)
        yc = _hgrn_mixer(proj3, lbs[l], hg_norm_w[l], tl_hg).reshape(t_rows, HG_WIDTH)
        (xres,) = _fused_mm(
            [(ya, wo, S5_WIDTH, 0, 0), (yb, wo, SGU_WIDTH, 1, 0), (yc, wo, HG_WIDTH, 1, 0)],
            res.extras,
            functools.partial(lambda accs, ex, res: [res(ex) + (accs[0] + accs[1] + accs[2])],
                              res=res),
            [F32], d_model, tm, 1024, "out_proj")

        hid, mu, rstd, wfo, wpg = _ln_fused_mm(
            xres, ln1_w[l], ln1_b[l],
            [(None, wf, d_model, 0, 0), (None, wf, d_model, 0, d_ff // 256)], [],
            lambda accs, ex: [accs[0] * _sigmoid(accs[0]) * accs[1]],
            [BF16], d_ff, tm_wide, 256, "ffn_in",
            side_casts=[(w_ffn_out, l), (w_ple_gate, l)])
        res = _Residual(alpha, x=xres, mu=mu, rstd=rstd, w=ln1_w[l], b=ln1_b[l])
        (xres,) = _fused_mm(
            [(hid, wfo, d_ff, 0, 0)], res.extras,
            functools.partial(lambda accs, ex, res: [res(ex) + accs[0]], res=res),
            [F32], d_model, tm_half, 1024, "ffn_out", cols_outer=True)

        p16 = _layer_to_bf16(p_rows, l)
        xres, _, _, *next_w_in = _ln_fused_mm(
            xres, ln2_w[l], ln2_b[l],
            [(None, wpg, d_model, 0, 0),
             (p16, _layer_to_bf16(w_ple_in, l), p16.shape[1], 0, 0)], [],
            lambda accs, ex, resid: [resid + accs[1] * _sigmoid(accs[0])],
            [F32], d_model, tm, 512, "ple", self_alpha=alpha,
            side_casts=[(w_in, l + 1)] if l + 1 < depth else [])
        if next_w_in:
            w_in16 = next_w_in[0]

    out = _layer_norm_final(xres, ln3_w[depth - 1], ln3_b[depth - 1], tm_ln)
    return out.reshape(nb, seq, d_model).astype(x.dtype)
```

```python
import functools
import math

import jax
import jax.numpy as jnp
from jax import lax
from jax.experimental import pallas as pl
from jax.experimental.pallas import tpu as pltpu

F32 = jnp.float32
BF16 = jnp.bfloat16

V7X_VMEM_LIMIT_BYTES = 56 * 1024 * 1024
LANES = 128

LN_EPS = 1e-5
RMS_EPS = 1e-6

S5_WIDTH = 1024
S5_GROUP_CH = 16
S5_STATE = 64
S5_CHUNK = 8
S5_TILE_GROUPS = LANES // S5_GROUP_CH
S5_TILE_STATE = S5_TILE_GROUPS * S5_STATE

SGU_WIDTH = 1024
SGU_CHUNK = 128
SGU_HEADS = 8

HG_WIDTH = 2048
HG_HEAD_DIM = 128
HG_CHUNK = 128
HG_LEVELS = 7


def _cparams(sem):
    return pltpu.CompilerParams(dimension_semantics=sem,
                                vmem_limit_bytes=V7X_VMEM_LIMIT_BYTES)


def _gelu_tanh(x):
    c = math.sqrt(2.0 / math.pi)
    return x * (0.5 + 0.5 * jnp.tanh(x * (c + (c * 0.044715) * (x * x))))


def _sigmoid(x):
    return 1.0 / (1.0 + jnp.exp(-x))


def _dot(a, b):
    return jnp.dot(a, b, preferred_element_type=F32)


def _pick(n, prefs):
    for p in prefs:
        if n % p == 0:
            return p
    return n


CAST_BLOCK_BYTES = 8 * 1024 * 1024


def _cast_body(w_ref, o_ref):
    o_ref[...] = w_ref[...].astype(BF16)


def _layer_to_bf16(w, layer):
    _, rows, cols = w.shape
    fits = [tr for tr in (8192, 4096, 2048, 1024, 512, 256, 128, 64, 32, 16)
            if rows % tr == 0 and tr * cols * 4 <= CAST_BLOCK_BYTES]
    tr = fits[0]
    return pl.pallas_call(
        _cast_body,
        out_shape=jax.ShapeDtypeStruct((rows, cols), BF16),
        grid=(rows // tr,),
        in_specs=[pl.BlockSpec((None, tr, cols), lambda i: (layer, i, 0))],
        out_specs=pl.BlockSpec((tr, cols), lambda i: (i, 0)),
        compiler_params=_cparams(("parallel",)), name="cast_bf16")(w)


MM_SUB_ROWS = 256


def _mm_body(*refs, a_of_dot, n_a, extra_kinds, n_side, epilogue, sub_rows):
    n_dot = len(a_of_dot)
    n_extra = len(extra_kinds)
    n_in = n_a + n_dot + n_extra
    a_refs = refs[:n_a]
    w_refs = refs[n_a:n_a + n_dot]
    extra = refs[n_a + n_dot:n_in]
    side_in = refs[n_in:n_in + n_side]
    out_refs = refs[n_in + n_side:len(refs) - n_side]
    side_out = refs[len(refs) - n_side:]
    for si, so in zip(side_in, side_out):
        so[...] = si[...].astype(BF16)
    rows = out_refs[0].shape[0]
    sub = min(rows, sub_rows)
    for r in range(rows // sub):
        rs = slice(r * sub, (r + 1) * sub)
        lhs = [jnp.concatenate([a_refs[k][rs, :] for k in ai], axis=1) for ai in a_of_dot]
        accs = [_dot(a, w[...]) for a, w in zip(lhs, w_refs)]
        outs = epilogue(accs, [e[...] if kind == "row" else e[rs, :]
                               for e, kind in zip(extra, extra_kinds)])
        for o_ref, o in zip(out_refs, outs):
            o_ref[rs, :] = o.astype(o_ref.dtype)


BF16_ROW_TILE = 16


def _fused_mm(dots, extras, epilogue, out_dtypes, n_cols, tm, tn, name,
              sub_rows=MM_SUB_ROWS, side_casts=(), cols_outer=False):
    as_parts = lambda a: a if isinstance(a, tuple) else (a,)
    t_rows = as_parts(dots[0][0])[0].shape[0]
    ni, nj = t_rows // tm, n_cols // tn
    grid = (nj, ni) if cols_outer else (ni, nj)

    def spec(shape, index_map, **kw):
        if cols_outer:
            return pl.BlockSpec(shape, lambda g0, g1: index_map(g1, g0), **kw)
        return pl.BlockSpec(shape, index_map, **kw)

    w_mode = dict(pipeline_mode=pl.Buffered(1)) if cols_outer else {}
    in_specs, args = [], []
    a_of_dot = []
    for a, _, _, _, _ in dots:
        idx = []
        for part in as_parts(a):
            known = [k for k, seen in enumerate(args) if seen is part]
            if known:
                idx.append(known[0])
                continue
            idx.append(len(args))
            in_specs.append(spec((tm, part.shape[1]), lambda i, j: (i, 0)))
            args.append(part)
        a_of_dot.append(tuple(idx))
    n_a = len(args)
    for _, w, rb, ri, co in dots:
        in_specs.append(spec((rb, tn), functools.partial(
            lambda i, j, ri, co: (ri, j + co), ri=ri, co=co), **w_mode))
        args.append(w)
    for arr, kind in extras:
        if kind == "tile":
            in_specs.append(spec((tm, tn), lambda i, j: (i, j)))
        elif kind == "stat":
            in_specs.append(spec((tm, LANES), lambda i, j: (i, 0)))
        else:
            in_specs.append(spec((1, tn), lambda i, j: (0, j)))
        args.append(arr)
    out_shape = [jax.ShapeDtypeStruct((t_rows, n_cols), dt) for dt in out_dtypes]
    out_specs = [spec((tm, tn), lambda i, j: (i, j)) for _ in out_dtypes]
    n_steps = grid[0] * grid[1]
    for stack, layer in side_casts:
        _, rows, cols = stack.shape
        tiles = rows // BF16_ROW_TILE
        n_slabs = max(d for d in range(1, min(tiles, n_steps) + 1) if tiles % d == 0)
        slab = functools.partial(lambda g0, g1, last: jnp.minimum(g0 * grid[1] + g1, last),
                                 last=n_slabs - 1)
        in_specs.append(pl.BlockSpec(
            (None, rows // n_slabs, cols),
            functools.partial(lambda g0, g1, layer, slab: (layer, slab(g0, g1), 0),
                              layer=layer, slab=slab)))
        args.append(stack)
        out_shape.append(jax.ShapeDtypeStruct((rows, cols), BF16))
        out_specs.append(pl.BlockSpec(
            (rows // n_slabs, cols),
            functools.partial(lambda g0, g1, slab: (slab(g0, g1), 0), slab=slab)))
    body = functools.partial(_mm_body, a_of_dot=tuple(a_of_dot), n_a=n_a,
                             extra_kinds=tuple(kind for _, kind in extras),
                             n_side=len(side_casts), epilogue=epilogue, sub_rows=sub_rows)
    return pl.pallas_call(
        body, out_shape=out_shape, grid=grid, in_specs=in_specs, out_specs=out_specs,
        compiler_params=_cparams(("parallel", "arbitrary")), name=name)(*args)


def _ln_rows(x):
    mu = jnp.mean(x, axis=-1, keepdims=True)
    xc = x - mu
    var = jnp.mean(xc * xc, axis=-1, keepdims=True)
    return xc, mu, lax.rsqrt(var + LN_EPS)


LN_ROW_GROUP = 8


def _ln_final_body(x_ref, w_ref, b_ref, h_ref):
    w = w_ref[...]
    b = b_ref[...]
    for g in range(x_ref.shape[0] // LN_ROW_GROUP):
        rs = slice(g * LN_ROW_GROUP, (g + 1) * LN_ROW_GROUP)
        xc, _, rstd = _ln_rows(x_ref[rs, :])
        h_ref[rs, :] = xc * rstd * w + b


def _layer_norm_final(x, w, b, tm):
    t_rows, d = x.shape
    rows = pl.BlockSpec((tm, d), lambda i: (i, 0))
    vec = pl.BlockSpec((1, d), lambda i: (0, 0))
    return pl.pallas_call(
        _ln_final_body, out_shape=jax.ShapeDtypeStruct((t_rows, d), F32),
        grid=(t_rows // tm,), in_specs=[rows, vec, vec], out_specs=rows,
        compiler_params=_cparams(("parallel",)), name="layer_norm")(
            x, w.reshape(1, d), b.reshape(1, d))


def _ln_mm_body(*refs, a_of_dot, n_other_a, extra_kinds, n_side, n_out, emit_h16, out_tiles,
                self_alpha, epilogue, sub_rows, n_slab, slab_rows):
    n_dot = len(a_of_dot)
    n_extra = len(extra_kinds)
    x_ref, lnw_ref, lnb_ref = refs[:3]
    pos = 3
    other_a = refs[pos:pos + n_other_a]
    pos += n_other_a
    w_refs = refs[pos:pos + n_dot]
    pos += n_dot
    extra = refs[pos:pos + n_extra]
    pos += n_extra
    side_in = refs[pos:pos + n_side]
    pos += n_side
    out_refs = refs[pos:pos + n_out]
    pos += n_out
    mu_ref, rs_ref = refs[pos:pos + 2]
    pos += 2
    h16_ref = refs[pos] if emit_h16 else None
    pos += int(emit_h16)
    side_out = refs[pos:pos + n_side]
    pos += n_side
    a_slots, mu_slots, rs_slots = refs[pos:pos + 2], refs[pos + 2:pos + 4], refs[pos + 4:pos + 6]

    g = pl.program_id(0)
    j = pl.program_id(1)
    group = 2 * LN_ROW_GROUP

    def side_jobs(slot):
        a_sc, mu_sc, rs_sc = a_slots[slot], mu_slots[slot], rs_slots[slot]
        for si, so in zip(side_in, side_out):
            so[...] = si[...].astype(BF16)
        row0 = pl.multiple_of(jnp.minimum(j, n_slab - 1) * slab_rows, slab_rows)
        w = lnw_ref[...]
        b = lnb_ref[...]
        for grp in range(slab_rows // group):
            ys = []
            for half in range(2):
                r8 = slice(grp * group + half * LN_ROW_GROUP,
                           grp * group + (half + 1) * LN_ROW_GROUP)
                xc, mu, rstd = _ln_rows(x_ref[r8, :])
                ys.append(xc * rstd * w + b)
                mu_b = jnp.broadcast_to(mu, (LN_ROW_GROUP, LANES))
                rs_b = jnp.broadcast_to(rstd, (LN_ROW_GROUP, LANES))
                mu_ref[r8, :] = mu_b
                rs_ref[r8, :] = rs_b
                dst = pl.ds(row0 + grp * group + half * LN_ROW_GROUP, LN_ROW_GROUP)
                mu_sc[dst, :] = mu_b
                rs_sc[dst, :] = rs_b
            y16 = jnp.concatenate(ys, axis=0).astype(BF16)
            a_sc[pl.ds(row0 + grp * group, group), :] = y16
            if emit_h16:
                h16_ref[grp * group:(grp + 1) * group, :] = y16

    def matmul(slot):
        a_sc, mu_sc, rs_sc = a_slots[slot], mu_slots[slot], rs_slots[slot]
        rows = a_sc.shape[0]
        sub = min(rows, sub_rows)
        for r in range(rows // sub):
            rs = slice(r * sub, (r + 1) * sub)
            a_ln = a_sc[rs, :]
            accs = [_dot(a_ln if ai < 0 else other_a[ai][rs, :], w[...])
                    for ai, w in zip(a_of_dot, w_refs)]
            ex = [e[...] if kind == "row" else e[rs, :] for e, kind in zip(extra, extra_kinds)]
            if self_alpha is not None:
                x_tile, w_row, b_row = ex[:3]
                ex = ex[3:]
                rep = x_tile.shape[1] // LANES
                wide = lambda s: jnp.concatenate([s] * rep, axis=1)
                resid = self_alpha * ((x_tile - wide(mu_sc[rs, :]))
                                      * wide(rs_sc[rs, :]) * w_row + b_row)
                outs = epilogue(accs, ex, resid)
            else:
                outs = epilogue(accs, ex)
            for o_ref, o in zip(out_refs, outs):
                if out_tiles:
                    for c in range(o_ref.shape[0]):
                        o_ref[c, rs, :] = o[:, c * LANES:(c + 1) * LANES].astype(o_ref.dtype)
                else:
                    o_ref[rs, :] = o.astype(o_ref.dtype)

    @pl.when(g == 0)
    def _():
        side_jobs(0)

    for parity in (0, 1):
        @pl.when(jnp.logical_and(g > 0, lax.rem(g, 2) == parity))
        def _():
            side_jobs(parity)
            matmul(1 - parity)


def _ln_fused_mm(x, ln_w, ln_b, dots, extras, epilogue, out_dtypes, n_cols, tm, tn, name,
                 sub_rows=MM_SUB_ROWS, side_casts=(), emit_h16=False, out_tiles=False,
                 self_alpha=None):
    t_rows, d = x.shape
    ni, nj = t_rows // tm, n_cols // tn
    grid = (ni + 1, nj)
    n_slab = max(s for s in range(1, nj + 1)
                 if tm % s == 0 and (tm // s) % (2 * LN_ROW_GROUP) == 0)
    slab_rows = tm // n_slab
    prev = lambda g: jnp.maximum(g - 1, 0)
    first_col = lambda g, j: jnp.where(g > 0, j, 0)
    slab_idx = lambda g, j: (jnp.where(g < ni, g * n_slab + jnp.minimum(j, n_slab - 1),
                                       ni * n_slab - 1), 0)
    vec = lambda a: a.astype(F32).reshape(1, d)
    in_specs = [pl.BlockSpec((slab_rows, d), slab_idx),
                pl.BlockSpec((1, d), lambda g, j: (0, 0)),
                pl.BlockSpec((1, d), lambda g, j: (0, 0))]
    args = [x, vec(ln_w), vec(ln_b)]
    a_of_dot, other = [], []
    for a, _, _, _, _ in dots:
        if a is None:
            a_of_dot.append(-1)
            continue
        a_of_dot.append(len(other))
        other.append(a)
        in_specs.append(pl.BlockSpec((tm, a.shape[1]), lambda g, j: (prev(g), 0)))
        args.append(a)
    for _, w, rb, ri, co in dots:
        in_specs.append(pl.BlockSpec((rb, tn), functools.partial(
            lambda g, j, ri, co: (ri, first_col(g, j) + co), ri=ri, co=co)))
        args.append(w)
    extras = list(extras)
    if self_alpha is not None:
        extras = [(x, "tile"), (vec(ln_w), "row"), (vec(ln_b), "row")] + extras
    for arr, kind in extras:
        if kind == "tile":
            in_specs.append(pl.BlockSpec((tm, tn), lambda g, j: (prev(g), first_col(g, j))))
        elif kind == "stat":
            in_specs.append(pl.BlockSpec((tm, LANES), lambda g, j: (prev(g), 0)))
        else:
            in_specs.append(pl.BlockSpec((1, tn), lambda g, j: (0, first_col(g, j))))
        args.append(arr)
    if out_tiles:
        n_tiles = n_cols // LANES
        out_shape = [jax.ShapeDtypeStruct((n_tiles, t_rows, LANES), dt) for dt in out_dtypes]
        out_specs = [pl.BlockSpec((tn // LANES, tm, LANES),
                                  lambda g, j: (first_col(g, j), prev(g), 0))
                     for _ in out_dtypes]
    else:
        out_shape = [jax.ShapeDtypeStruct((t_rows, n_cols), dt) for dt in out_dtypes]
        out_specs = [pl.BlockSpec((tm, tn), lambda g, j: (prev(g), first_col(g, j)))
                     for _ in out_dtypes]
    n_out = len(out_shape)
    for _ in range(2):
        out_shape.append(jax.ShapeDtypeStruct((t_rows, LANES), F32))
        out_specs.append(pl.BlockSpec((slab_rows, LANES), slab_idx))
    if emit_h16:
        out_shape.append(jax.ShapeDtypeStruct((t_rows, d), BF16))
        out_specs.append(pl.BlockSpec((slab_rows, d), slab_idx))
    n_steps = grid[0] * grid[1]
    for stack, layer in side_casts:
        _, rows, cols = stack.shape
        tiles = rows // BF16_ROW_TILE
        n_cast = max(s for s in range(1, min(tiles, n_steps) + 1) if tiles % s == 0)
        slab = functools.partial(lambda g, j, last: jnp.minimum(g * nj + j, last),
                                 last=n_cast - 1)
        in_specs.append(pl.BlockSpec(
            (None, rows // n_cast, cols),
            functools.partial(lambda g, j, layer, slab: (layer, slab(g, j), 0),
                              layer=layer, slab=slab)))
        args.append(stack)
        out_shape.append(jax.ShapeDtypeStruct((rows, cols), BF16))
        out_specs.append(pl.BlockSpec(
            (rows // n_cast, cols),
            functools.partial(lambda g, j, slab: (slab(g, j), 0), slab=slab)))
    body = functools.partial(
        _ln_mm_body, a_of_dot=tuple(a_of_dot), n_other_a=len(other),
        extra_kinds=tuple(kind for _, kind in extras), n_side=len(side_casts), n_out=n_out,
        emit_h16=emit_h16, out_tiles=out_tiles, self_alpha=self_alpha, epilogue=epilogue,
        sub_rows=sub_rows, n_slab=n_slab, slab_rows=slab_rows)
    return pl.pallas_call(
        body, out_shape=out_shape, grid=grid, in_specs=in_specs, out_specs=out_specs,
        scratch_shapes=([pltpu.VMEM((tm, d), BF16)] * 2
                        + [pltpu.VMEM((tm, LANES), F32)] * 4),
        compiler_params=_cparams(("arbitrary", "arbitrary")), name=name)(*args)


class _Residual:
    def __init__(self, alpha, h=None, x=None, mu=None, rstd=None, w=None, b=None):
        self.alpha = alpha
        if h is not None:
            self.extras = [(h, "tile")]
        else:
            self.extras = [(x, "tile"), (mu, "stat"), (rstd, "stat"),
                           (w.astype(F32).reshape(1, -1), "row"),
                           (b.astype(F32).reshape(1, -1), "row")]

    def __call__(self, ex):
        if len(ex) == 1:
            return self.alpha * ex[0]
        x, mu, rstd, w, b = ex
        rep = x.shape[1] // LANES
        wide = lambda s: jnp.concatenate([s] * rep, axis=1)
        return self.alpha * ((x - wide(mu)) * wide(rstd) * w + b)


def _s5_prepare(lam_re, lam_im, log_step, b_re, b_im, c_re, c_im, d):
    hp = lax.Precision.HIGHEST
    n_tiles = S5_WIDTH // LANES
    lr = jnp.minimum(lam_re.astype(F32), -1e-4)
    li = lam_im.astype(F32)
    dt = jnp.exp(log_step.astype(F32))[:, None]
    mag = jnp.exp(lr * dt)
    ab_re = mag * jnp.cos(li * dt)
    ab_im = mag * jnp.sin(li * dt)
    den = lr * lr + li * li
    nr = ab_re - 1.0
    g_re = (nr * lr + ab_im * li) / den
    g_im = (ab_im * lr - nr * li) / den
    br = b_re.astype(F32)
    bi = b_im.astype(F32)
    bb_re = g_re[..., None] * br - g_im[..., None] * bi
    bb_im = g_re[..., None] * bi + g_im[..., None] * br
    pr, pi = [jnp.ones_like(ab_re)], [jnp.zeros_like(ab_re)]
    for _ in range(S5_CHUNK):
        pr_new = pr[-1] * ab_re - pi[-1] * ab_im
        pi_new = pr[-1] * ab_im + pi[-1] * ab_re
        pr.append(pr_new)
        pi.append(pi_new)
    p_re = jnp.stack(pr)
    p_im = jnp.stack(pi)
    pb_re = (p_re[:S5_CHUNK, :, :, None] * bb_re[None]
             - p_im[:S5_CHUNK, :, :, None] * bb_im[None])
    pb_im = (p_re[:S5_CHUNK, :, :, None] * bb_im[None]
             + p_im[:S5_CHUNK, :, :, None] * bb_re[None])
    cr = c_re.astype(F32)
    ci = c_im.astype(F32)
    kern = (jnp.einsum("gon,jgni->jgoi", cr, pb_re, precision=hp)
            - jnp.einsum("gon,jgni->jgoi", ci, pb_im, precision=hp))
    tg, gc, ns = S5_TILE_GROUPS, S5_GROUP_CH, S5_STATE

    def spread(x2d, rep, row_group, col_group):
        rows, width = x2d.shape
        sel = jnp.tile(jnp.eye(width, dtype=BF16), (1, rep))
        out = jnp.dot(x2d.astype(BF16), sel)
        rg = row_group(jnp.arange(rows))[:, None]
        cg = col_group(jnp.arange(width * rep))[None, :]
        return jnp.where(rg == cg, out, jnp.zeros_like(out))

    def spread_steps(x2d, row_group):
        rows, width = x2d.shape
        cols = jnp.arange(S5_CHUNK * LANES)
        src = jnp.arange(width)
        sel = ((src[:, None] // gc == cols[None, :] // LANES)
               & (src[:, None] % gc == cols[None, :] % gc)).astype(BF16)
        out = jnp.dot(x2d.astype(BF16), sel)
        rg = row_group(jnp.arange(rows))[:, None]
        cg = ((cols // gc) % tg)[None, :]
        return jnp.where(rg == cg, out, jnp.zeros_like(out))

    k5 = kern.reshape(S5_CHUNK, n_tiles, tg, gc, gc).transpose(1, 0, 2, 4, 3)
    kblk = spread(k5.reshape(-1, gc), tg, lambda r: (r // gc) % tg,
                  lambda c: c // gc).reshape(n_tiles, S5_CHUNK, LANES, LANES)
    zero = jnp.zeros((n_tiles, LANES, LANES), BF16)
    toep = jnp.concatenate(
        [jnp.concatenate([zero] * t + [kblk[:, j] for j in range(S5_CHUNK - t)], axis=-1)
         for t in range(S5_CHUNK)], axis=-2)

    def w1_half(pb):
        pb5 = pb[::-1].reshape(S5_CHUNK, n_tiles, tg, ns, gc).transpose(1, 0, 2, 4, 3)
        return spread(pb5.reshape(-1, ns), tg, lambda r: (r // gc) % tg,
                      lambda c: c // ns).reshape(n_tiles, S5_CHUNK * LANES, S5_TILE_STATE)
    w1 = jnp.concatenate([w1_half(pb_re), w1_half(pb_im)], axis=-1)

    ca_re = cr[None] * p_re[1:, :, None, :] - ci[None] * p_im[1:, :, None, :]
    ca_im = cr[None] * p_im[1:, :, None, :] + ci[None] * p_re[1:, :, None, :]

    def p_half(ca):
        ca5 = ca.reshape(S5_CHUNK, n_tiles, tg, gc, ns).transpose(1, 2, 4, 0, 3)
        return spread_steps(ca5.reshape(-1, S5_CHUNK * gc), lambda r: (r // ns) % tg).reshape(
            n_tiles, S5_TILE_STATE, S5_CHUNK * LANES)
    pcat = jnp.concatenate([p_half(ca_re), -p_half(ca_im)], axis=1)

    a16 = jnp.concatenate([p_re[S5_CHUNK].reshape(n_tiles, 1, S5_TILE_STATE),
                           p_im[S5_CHUNK].reshape(n_tiles, 1, S5_TILE_STATE)], axis=-1)
    dd = d.astype(F32).reshape(n_tiles, 1, LANES)
    return w1, toep, pcat, a16, dd


def _s5_body(x_ref, w1_ref, toep_ref, pcat_ref, a16_ref, d_ref, z_ref,
             state_ref, xr_ref, q_ref, yi_ref, sin_ref, yc_ref, yint_ref, *, nb, tl):
    t = pl.program_id(1)
    nk = tl // S5_CHUNK
    n = nb * tl
    ts = S5_TILE_STATE

    @pl.when(t == 0)
    def _():
        state_ref[...] = jnp.zeros_like(state_ref)

    for b in range(nb):
        for tp in range(S5_CHUNK):
            xr_ref[tp, pl.ds(b, nk, stride=nb), :] = x_ref[b, pl.ds(tp, nk, stride=S5_CHUNK), :]
    xr = jnp.concatenate([xr_ref[tp].astype(BF16) for tp in range(S5_CHUNK)], axis=1)
    q_ref[...] = _dot(xr, w1_ref[...])
    yi_ref[...] = _dot(xr, toep_ref[...])

    a_re = a16_ref[:, :ts]
    a_im = a16_ref[:, ts:]
    s = state_ref[...]
    for k in range(nk):
        sin_ref[k * nb:(k + 1) * nb, :] = s
        s_re = s[:, :ts]
        s_im = s[:, ts:]
        s = jnp.concatenate([a_re * s_re - a_im * s_im, a_re * s_im + a_im * s_re],
                            axis=1) + q_ref[k * nb:(k + 1) * nb, :]
    state_ref[...] = s

    yc = yi_ref[...] + _dot(sin_ref[...].astype(BF16), pcat_ref[...])
    for tp in range(S5_CHUNK):
        yc_ref[tp] = yc[:, tp * LANES:(tp + 1) * LANES]
    for b in range(nb):
        for tp in range(S5_CHUNK):
            yint_ref[pl.ds(b * tl + tp, nk, stride=S5_CHUNK), :] = (
                yc_ref[tp, pl.ds(b, nk, stride=nb), :])

    x = x_ref[...].reshape(n, LANES)
    z_ref[...] = _gelu_tanh(yint_ref[...] + d_ref[...] * x).reshape(nb, tl, LANES)


def _s5_in_proj_body(a_ref, w_ref, o_ref, a16_ref):
    rows = a_ref.shape[0]
    sub = min(rows, MM_SUB_ROWS)
    for r in range(rows // sub):
        rs = slice(r * sub, (r + 1) * sub)
        a16 = a_ref[rs, :].astype(BF16)
        a16_ref[rs, :] = a16
        acc = _dot(a16, w_ref[...])
        for c in range(o_ref.shape[0]):
            o_ref[c, rs, :] = acc[:, c * LANES:(c + 1) * LANES]


def _s5_in_proj(h, w_in16, tm):
    t_rows, d = h.shape
    n_tiles = S5_WIDTH // LANES
    return pl.pallas_call(
        _s5_in_proj_body,
        out_shape=[jax.ShapeDtypeStruct((n_tiles, t_rows, LANES), F32),
                   jax.ShapeDtypeStruct((t_rows, d), BF16)],
        grid=(t_rows // tm,),
        in_specs=[pl.BlockSpec((tm, d), lambda i: (i, 0)),
                  pl.BlockSpec((d, S5_WIDTH), lambda i: (0, 0))],
        out_specs=[pl.BlockSpec((n_tiles, tm, LANES), lambda i: (0, i, 0)),
                   pl.BlockSpec((tm, d), lambda i: (i, 0))],
        compiler_params=_cparams(("parallel",)), name="in_proj_s5")(h, w_in16)


def _s5_mixer(xa_tiles, prep, tl):
    w1, toep, pcat, a16, dd = prep
    n_tiles, nb, seq, _ = xa_tiles.shape
    nk = tl // S5_CHUNK
    ts = S5_TILE_STATE
    steps = S5_CHUNK * LANES
    body = functools.partial(_s5_body, nb=nb, tl=tl)
    wspec = lambda shp: pl.BlockSpec((None,) + shp, lambda j, t: (j, 0, 0))
    xspec = pl.BlockSpec((None, nb, tl, LANES), lambda j, t: (j, 0, t, 0))
    return pl.pallas_call(
        body,
        out_shape=jax.ShapeDtypeStruct(xa_tiles.shape, F32),
        grid=(n_tiles, seq // tl),
        in_specs=[xspec,
                  wspec((steps, 2 * ts)),
                  wspec((steps, steps)),
                  wspec((2 * ts, steps)),
                  wspec((1, 2 * ts)),
                  wspec((1, LANES))],
        out_specs=xspec,
        scratch_shapes=[pltpu.VMEM((nb, 2 * ts), F32),
                        pltpu.VMEM((S5_CHUNK, nb * nk, LANES), F32),
                        pltpu.VMEM((nb * nk, 2 * ts), F32),
                        pltpu.VMEM((nb * nk, steps), F32),
                        pltpu.VMEM((nb * nk, 2 * ts), F32),
                        pltpu.VMEM((S5_CHUNK, nb * nk, LANES), F32),
                        pltpu.VMEM((nb * tl, LANES), F32)],
        compiler_params=_cparams(("parallel", "arbitrary")), name="s5_mixer")(
            xa_tiles, w1, toep, pcat, a16, dd)


def _glu_body(z_ref, w_ref, nw_ref, o_ref):
    z = jnp.concatenate([z_ref[c] for c in range(z_ref.shape[0])], axis=1)
    y = z * _sigmoid(_dot(z.astype(BF16), w_ref[...]))
    ms = jnp.mean(y * y, axis=-1, keepdims=True)
    o_ref[...] = (y * lax.rsqrt(ms + RMS_EPS) * nw_ref[...]).astype(BF16)


def _s5_glu_norm(z_tiles, w_glu, norm_w, tm):
    n_tiles, t_rows, _ = z_tiles.shape
    d = n_tiles * LANES
    return pl.pallas_call(
        _glu_body,
        out_shape=jax.ShapeDtypeStruct((t_rows, d), BF16),
        grid=(t_rows // tm,),
        in_specs=[pl.BlockSpec((n_tiles, tm, LANES), lambda i: (0, i, 0)),
                  pl.BlockSpec((d, d), lambda i: (0, 0)),
                  pl.BlockSpec((1, d), lambda i: (0, 0))],
        out_specs=pl.BlockSpec((tm, d), lambda i: (i, 0)),
        compiler_params=_cparams(("parallel",)), name="s5_glu_norm")(
            z_tiles, w_glu, norm_w.reshape(1, d))


def _sgu_body(u_ref, v_ref, lnw_ref, lnb_ref, w_ref, bs_ref, nw_ref, o_ref,
              vb_ref, x_ref, *, nc):
    v = _gelu_tanh(v_ref[...])
    mu = jnp.mean(v, axis=-1, keepdims=True)
    vc = v - mu
    var = jnp.mean(vc * vc, axis=-1, keepdims=True)
    vb_ref[...] = (vc * lax.rsqrt(var + LN_EPS) * lnw_ref[...] + lnb_ref[...]).astype(BF16)
    hd = SGU_WIDTH // SGU_HEADS
    for h in range(SGU_HEADS):
        cs = slice(h * hd, (h + 1) * hd)
        rhs = jnp.concatenate(
            [vb_ref[c * SGU_CHUNK:(c + 1) * SGU_CHUNK, cs] for c in range(nc)], axis=1)
        zz = _dot(w_ref[h], rhs)
        for c in range(nc):
            rs = slice(c * SGU_CHUNK, (c + 1) * SGU_CHUNK)
            z = zz[:, c * hd:(c + 1) * hd] + bs_ref[:, cs]
            x_ref[rs, cs] = _gelu_tanh(u_ref[rs, cs]) * z
    x = x_ref[...]
    ms = jnp.mean(x * x, axis=-1, keepdims=True)
    o_ref[...] = (x * lax.rsqrt(ms + RMS_EPS) * nw_ref[...]).astype(BF16)


def _sgu_mixer(proj3, ln_w, ln_b, w_s, b_s, norm_w, tl):
    nb, seq, _ = proj3.shape
    nc = tl // SGU_CHUNK
    hd = SGU_WIDTH // SGU_HEADS
    causal = jnp.tril(jnp.ones((SGU_CHUNK, SGU_CHUNK), dtype=bool))
    w_causal = jnp.where(causal[None], w_s, jnp.zeros_like(w_s)).astype(BF16)
    bias = jnp.repeat(b_s.astype(F32).T, hd, axis=1)
    u_blk = 0
    row = lambda a: a.astype(F32).reshape(1, SGU_WIDTH)
    const = lambda shp: pl.BlockSpec(shp, lambda b, t: (0,) * len(shp))
    body = functools.partial(_sgu_body, nc=nc)
    return pl.pallas_call(
        body,
        out_shape=jax.ShapeDtypeStruct((nb, seq, SGU_WIDTH), BF16),
        grid=(nb, seq // tl),
        in_specs=[pl.BlockSpec((None, tl, SGU_WIDTH), lambda b, t: (b, t, u_blk)),
                  pl.BlockSpec((None, tl, SGU_WIDTH), lambda b, t: (b, t, u_blk + 1)),
                  const((1, SGU_WIDTH)), const((1, SGU_WIDTH)),
                  const((SGU_HEADS, SGU_CHUNK, SGU_CHUNK)),
                  const((SGU_CHUNK, SGU_WIDTH)), const((1, SGU_WIDTH))],
        out_specs=pl.BlockSpec((None, tl, SGU_WIDTH), lambda b, t: (b, t, 0)),
        scratch_shapes=[pltpu.VMEM((tl, SGU_WIDTH), BF16),
                        pltpu.VMEM((tl, SGU_WIDTH), F32)],
        compiler_params=_cparams(("parallel", "parallel")), name="sgu_mixer")(
            proj3, proj3, row(ln_w), row(ln_b), w_causal, bias, row(norm_w))


HG_PAIR = 2
HG_PAIR_W = HG_PAIR * HG_HEAD_DIM
HG_FAST_BLOCK = 32
HG_FAST_MIN_LOG_DECAY = -60.0
_NT = (((1,), (1,)), ((), ()))
_TN = (((0,), (0,)), ((), ()))


def _hgrn_tables():
    c = HG_CHUNK
    t = jnp.arange(c)[:, None]
    r = jnp.arange(c)[None, :]
    mats, masks = [], []
    for lev in range(HG_LEVELS):
        m = 1 << lev
        mid = (t // (2 * m)) * (2 * m) + m
        later = t >= mid
        mats.append(jnp.where(later, (r >= mid) & (r <= t), (r > t) & (r < mid)))
        same = (t // (2 * m)) == (r // (2 * m))
        masks.append(same & later & (r < mid))
    tri = r <= t
    mats.append(tri)
    masks.append(t == r)
    fb = HG_FAST_BLOCK
    fast = [masks[5], ((t // fb) == (r // fb)) & tri]
    pair = lambda m: jnp.tile(m.astype(F32), (1, HG_PAIR))
    return dict(
        tri=tri.astype(BF16),
        mst=jnp.concatenate(mats, axis=0).astype(BF16),
        masks=jnp.stack(masks).astype(F32),
        fmasks=jnp.stack([pair(m) for m in fast]))


def _block_diag(x):
    z = jnp.zeros((x.shape[0], HG_HEAD_DIM), x.dtype)
    top = jnp.concatenate([x[:, :HG_HEAD_DIM], z], axis=1)
    bot = jnp.concatenate([z, x[:, HG_HEAD_DIM:]], axis=1)
    return jnp.concatenate([top, bot], axis=0)


def _rows(v, n):
    return jnp.broadcast_to(v, (n, v.shape[1]))


def _hgrn_finish(o, gv, nw, o_ref, rs):
    for h in range(HG_PAIR):
        cs = slice(h * HG_HEAD_DIM, (h + 1) * HG_HEAD_DIM)
        oh = o[:, cs]
        ms = jnp.mean(oh * oh, axis=-1, keepdims=True)
        gg = gv[:, cs]
        o_ref[rs, cs] = (oh * lax.rsqrt(ms + RMS_EPS) * nw * (gg * _sigmoid(gg))).astype(BF16)


def _hgrn_body(q_ref, f_ref, i_ref, g_ref, par_ref, nw_ref, tri_ref, mst_ref, mask_ref,
               fmask_ref, o_ref, st_ref, qs_ref, kk_ref, lf_ref, b_ref, *, nc):
    c = HG_CHUNK
    hd = HG_HEAD_DIM
    wb = HG_PAIR_W
    t = pl.program_id(2)

    @pl.when(t == 0)
    def _():
        st_ref[...] = jnp.zeros_like(st_ref)

    log_lb = par_ref[0:1, :]
    log_1m_lb = par_ref[1:2, :]
    one_m_lb = par_ref[2:3, :]
    nw = nw_ref[...]
    tri = tri_ref[...]
    row = lax.broadcasted_iota(jnp.int32, (c, wb), 0)

    wmin = None
    for ch in range(nc):
        rs = slice(ch * c, (ch + 1) * c)
        fr = f_ref[rs, :]
        e = jnp.exp(-jnp.abs(fr))
        r = 1.0 / (1.0 + e)
        sig_neg = jnp.where(fr >= 0, e * r, r)
        log_sig = jnp.minimum(fr, 0.0) - jnp.log(1.0 + e)
        y = log_1m_lb + log_sig
        log_f = jnp.maximum(log_lb, y) + jnp.log(1.0 + jnp.exp(-jnp.abs(log_lb - y)))
        hi = log_f.astype(BF16)
        lo = (log_f - hi.astype(F32)).astype(BF16)
        b = _dot(tri, hi) + _dot(tri, lo)
        qv = q_ref[rs, :]
        qs_ref[rs, :] = qv * _sigmoid(qv)
        kk_ref[rs, :] = one_m_lb * sig_neg
        lf_ref[rs, :] = log_f
        b_ref[rs, :] = b
        fb = HG_FAST_BLOCK
        starts = jnp.concatenate(
            [jnp.zeros((fb, wb), F32)]
            + [_rows(b[j * fb - 1:j * fb, :], fb) for j in range(1, c // fb)], axis=0)
        w = b - starts
        wmin = w if wmin is None else jnp.minimum(wmin, w)
    fast = jnp.min(wmin) >= HG_FAST_MIN_LOG_DECAY

    @pl.when(fast)
    def _():
        for ch in range(nc):
            rs = slice(ch * c, (ch + 1) * c)
            b = b_ref[rs, :]
            qs = qs_ref[rs, :]
            kk = kk_ref[rs, :]
            vv = i_ref[rs, :].astype(BF16)
            half = c // 2
            b63 = _rows(b[half - 1:half, :], half)
            zero_half = jnp.zeros((half, wb), BF16)
            q6 = jnp.concatenate(
                [zero_half, (qs[half:, :] * jnp.exp(b[half:, :] - b63)).astype(BF16)], axis=0)
            k6 = jnp.concatenate(
                [(kk[:half, :] * jnp.exp(b63 - b[:half, :])).astype(BF16), zero_half], axis=0)
            scores = lax.dot_general(q6, _block_diag(k6), _NT, preferred_element_type=F32)
            bm = jnp.concatenate([_rows(b[31:32, :], 64), _rows(b[95:96, :], 64)], axis=0)
            a5 = jnp.exp(jnp.where((row % 64) >= 32, b - bm, bm - b))
            fb = HG_FAST_BLOCK
            starts = jnp.concatenate(
                [jnp.zeros((fb, wb), F32)]
                + [_rows(b[j * fb - 1:j * fb, :], fb) for j in range(1, c // fb)], axis=0)
            w = b - starts
            for lev, (aq, ak) in enumerate([(a5, a5), (jnp.exp(w), jnp.exp(-w))]):
                sc = lax.dot_general((qs * aq).astype(BF16), _block_diag((kk * ak).astype(BF16)),
                                     _NT, preferred_element_type=F32)
                scores = scores + fmask_ref[lev] * sc
            st = st_ref[...]
            v_bd = _block_diag(vv)
            o = (_dot(scores.astype(BF16), v_bd)
                 + lax.dot_general((qs * jnp.exp(b)).astype(BF16), st.astype(BF16), _NT,
                                   preferred_element_type=F32))
            bend = b[c - 1:c, :]
            kd = (kk * jnp.exp(bend - b)).astype(BF16)
            st_ref[...] = st * jnp.exp(bend) + lax.dot_general(
                v_bd, _block_diag(kd), _TN, preferred_element_type=F32)
            _hgrn_finish(o, g_ref[rs, :], nw, o_ref, rs)

    @pl.when(jnp.logical_not(fast))
    def _():
        mst = mst_ref[...]

        def chunk(ch, carry):
            rs = pl.ds(pl.multiple_of(ch * c, c), c)
            log_f = lf_ref[rs, :]
            hi = log_f.astype(BF16)
            lo = (log_f - hi.astype(F32)).astype(BF16)
            ee = _dot(mst, hi) + _dot(mst, lo)
            qs = qs_ref[rs, :]
            kk = kk_ref[rs, :]
            vv = i_ref[rs, :].astype(BF16)
            bcum = ee[HG_LEVELS * c:(HG_LEVELS + 1) * c, :]
            outs = []
            for h in range(HG_PAIR):
                cs = slice(h * hd, (h + 1) * hd)
                qh = qs[:, cs]
                kh = kk[:, cs]
                vh = vv[:, cs]
                scores = mask_ref[HG_LEVELS] * jnp.sum(qh * kh, axis=-1, keepdims=True)
                for lev in range(HG_LEVELS):
                    a = jnp.exp(ee[lev * c:(lev + 1) * c, cs])
                    sc = lax.dot_general((qh * a).astype(BF16), (kh * a).astype(BF16), _NT,
                                         preferred_element_type=F32)
                    scores = scores + mask_ref[lev] * sc
                bh = bcum[:, cs]
                st = st_ref[cs, cs]
                outs.append(_dot(scores.astype(BF16), vh)
                            + lax.dot_general((qh * jnp.exp(bh)).astype(BF16), st.astype(BF16),
                                              _NT, preferred_element_type=F32))
                bend = bh[c - 1:c, :]
                kd = (kh * jnp.exp(bend - bh)).astype(BF16)
                st_ref[cs, cs] = st * jnp.exp(bend) + lax.dot_general(
                    vh, kd, _TN, preferred_element_type=F32)
            _hgrn_finish(jnp.concatenate(outs, axis=1), g_ref[rs, :], nw, o_ref, rs)
            return carry

        lax.fori_loop(0, nc, chunk, 0)


def _hgrn_mixer(proj3, lb, norm_w, tl):
    nb, seq, _ = proj3.shape
    nc = tl // HG_CHUNK
    wb = HG_PAIR_W
    q0 = (2 * SGU_WIDTH) // wb
    nblk = HG_WIDTH // wb
    lbf = lb.astype(F32)
    par = jnp.stack([jnp.log(lbf), jnp.log1p(-lbf), 1.0 - lbf])
    tb = _hgrn_tables()
    sec = lambda k: pl.BlockSpec((None, tl, wb), functools.partial(
        lambda b, h, t, k: (b, t, q0 + k * nblk + h), k=k))
    const = lambda a: pl.BlockSpec(a.shape, lambda b, h, t: (0,) * a.ndim)
    body = functools.partial(_hgrn_body, nc=nc)
    return pl.pallas_call(
        body,
        out_shape=jax.ShapeDtypeStruct((nb, seq, HG_WIDTH), BF16),
        grid=(nb, nblk, seq // tl),
        in_specs=[sec(0), sec(1), sec(2), sec(3),
                  pl.BlockSpec((3, wb), lambda b, h, t: (0, h)),
                  pl.BlockSpec((1, HG_HEAD_DIM), lambda b, h, t: (0, 0)),
                  const(tb["tri"]), const(tb["mst"]), const(tb["masks"]),
                  const(tb["fmasks"])],
        out_specs=pl.BlockSpec((None, tl, wb), lambda b, h, t: (b, t, h)),
        scratch_shapes=[pltpu.VMEM((wb, wb), F32)] + [pltpu.VMEM((tl, wb), F32)] * 4,
        compiler_params=_cparams(("parallel", "parallel", "arbitrary")),
        name="hgrn2_mixer")(
            proj3, proj3, proj3, proj3, par,
            norm_w.astype(F32).reshape(1, HG_HEAD_DIM),
            tb["tri"], tb["mst"], tb["masks"], tb["fmasks"])


def kernel(x, p, w_in, s5_lam_re, s5_lam_im, s5_log_step, s5_b_re, s5_b_im, s5_c_re, s5_c_im, s5_d, s5_w_glu, sgu_ln_w, sgu_ln_b, sgu_w, sgu_b, hg_lb_logits, hg_norm_w, norm_a_w, norm_b_w, w_out, ln1_w, ln1_b, w_ffn_in, w_ffn_out, ln2_w, ln2_b, w_ple_in, w_ple_gate, ln3_w, ln3_b):
    nb, seq, d_model = x.shape
    depth = w_in.shape[0]
    t_rows = nb * seq
    proj_w = w_in.shape[2]
    d_ff = w_ffn_out.shape[1]
    alpha = (2.0 * depth) ** 0.25

    tm = _pick(t_rows, (1024, 512, 256, 128))
    tm_wide = _pick(t_rows, (2048, 1024, 512, 256, 128))
    tm_half = _pick(t_rows, (512, 256, 128))
    tm_ln = _pick(t_rows, (512, 256, 128))
    tl_s5 = _pick(seq, (1024, 512, 256, 128))
    tl_sgu = _pick(seq, (1024, 512, 256, 128))
    tl_hg = _pick(seq, (2048, 1024, 512, 256, 128))
    tn_in = _pick(proj_w - S5_WIDTH, (1024, 512))

    lbs =jnp.cumsum(jax.nn.softmax(hg_lb_logits.astype(F32), axis=0), axis=0)
    lbs = lbs - lbs[0:1]

    x_rows = x.reshape(t_rows, d_model).astype(F32)
    res = _Residual(alpha, h=x_rows)
    p_rows = p.reshape(depth, t_rows, -1)

    w_in16 = _layer_to_bf16(w_in, 0)
    for l in range(depth):
        if l == 0:
            xa_tiles, h16 = _s5_in_proj(x_rows, w_in16, tm_half)
        else:
            lw, lb = ln3_w[l - 1], ln3_b[l - 1]
            xa_tiles, mu, rstd, h16 = _ln_fused_mm(
                xres, lw, lb, [(None, w_in16, d_model, 0, 0)], [],
                lambda accs, ex: [accs[0]], [F32], S5_WIDTH, tm, S5_WIDTH // 4, "in_proj_s5",
                emit_h16=True, out_tiles=True)
            res = _Residual(alpha, x=xres, mu=mu, rstd=rstd, w=lw, b=lb)
        xa_tiles = xa_tiles.reshape(-1, nb, seq, LANES)
        proj, wf, wo = _fused_mm(
            [(h16, w_in16, d_model, 0, S5_WIDTH // tn_in)], [],
            lambda accs, ex: [accs[0]], [F32], proj_w - S5_WIDTH, tm, tn_in, "in_proj",
            side_casts=[(w_ffn_in, l), (w_out, l)])
        proj3 = proj.reshape(nb, seq, proj_w - S5_WIDTH)
        prep = _s5_prepare(s5_lam_re[l], s5_lam_im[l], s5_log_step[l], s5_b_re[l],
                           s5_b_im[l], s5_c_re[l], s5_c_im[l], s5_d[l])
        z_tiles = _s5_mixer(xa_tiles, prep, tl_s5).reshape(-1, t_rows, LANES)
        ya = _s5_glu_norm(z_tiles, _layer_to_bf16(s5_w_glu, l), norm_a_w[l], tm_half)
        yb = _sgu_mixer(proj3, sgu_ln_w[l], sgu_ln_b[l], sgu_w[l], sgu_b[l],
                        norm_b_w[l], tl_sgu).reshape(t_rows, SGU_WIDTH)
        yc = _hgrn_mixer(proj3, lbs[l], hg_norm_w[l], tl_hg).reshape(t_rows, HG_WIDTH)
        (xres,) = _fused_mm(
            [((ya, yb, yc), wo, d_model, 0, 0)], res.extras,
            functools.partial(lambda accs, ex, res: [res(ex) + accs[0]], res=res),
            [F32], d_model, tm, 1024, "out_proj")

        hid, mu, rstd, wfo, wpg = _ln_fused_mm(
            xres, ln1_w[l], ln1_b[l],
            [(None, wf, d_model, 0, 0), (None, wf, d_model, 0, d_ff // 256)], [],
            lambda accs, ex: [accs[0] * _sigmoid(accs[0]) * accs[1]],
            [BF16], d_ff, tm_wide, 256, "ffn_in",
            side_casts=[(w_ffn_out, l), (w_ple_gate, l)])
        res = _Residual(alpha, x=xres, mu=mu, rstd=rstd, w=ln1_w[l], b=ln1_b[l])
        (xres,) = _fused_mm(
            [(hid, wfo, d_ff, 0, 0)], res.extras,
            functools.partial(lambda accs, ex, res: [res(ex) + accs[0]], res=res),
            [F32], d_model, tm_half, 1024, "ffn_out", cols_outer=True)

        p16 = _layer_to_bf16(p_rows, l)
        xres, _, _, *next_w_in = _ln_fused_mm(
            xres, ln2_w[l], ln2_b[l],
            [(None, wpg, d_model, 0, 0),
             (p16, _layer_to_bf16(w_ple_in, l), p16.shape[1], 0, 0)], [],
            lambda accs, ex, resid: [resid + accs[1] * _sigmoid(accs[0])],
            [F32], d_model, tm, 512, "ple", self_alpha=alpha,
            side_casts=[(w_in, l + 1)] if l + 1 < depth else [])
        if next_w_in:
            w_in16 = next_w_in[0]

    out = _layer_norm_final(xres, ln3_w[depth - 1], ln3_b[depth - 1], tm_ln)
    return out.reshape(nb, seq, d_model).astype(x.dtype)
```
